```python
import math
import jax, jax.numpy as jnp
from jax import lax
import numpy as np

D_MODEL = 1024
BATCH = 2
SEQ = 8192
DEPTH = 2

GRID_W = 64
CTX_LEN = 256
HEAD_DIM = 64
ROPE_THETA = 10000.0
RMS_EPS = 1e-6
NEG_INF = -1e30
BLOCK = 128
A_HQ = 8
A_HKV = 2
A_GROUP = A_HQ // A_HKV
WINDOW = 128
B_H = 4
B_HD = HEAD_DIM
B_VD = 2 * HEAD_DIM
C_H = 4
C_DK = 128
C_DV = 128
CHUNK = 64
N_BRANCH = 3
BRANCH_W = 512
A_Q_W = A_HQ * HEAD_DIM
A_KV_W = A_HKV * HEAD_DIM
B_QK_W = B_H * 2 * B_HD
B_V_W = B_H * B_VD
C_K_W = C_H * C_DK
C_V_W = C_H * C_DV
GATE_W = N_BRANCH * D_MODEL
SPLITS = (A_Q_W, A_KV_W, A_KV_W, B_QK_W, B_QK_W, B_V_W, C_K_W, C_V_W, C_K_W, C_K_W, C_V_W, GATE_W)
IN_W = sum(SPLITS)
SPLIT_IDX = tuple(int(v) for v in np.cumsum(SPLITS)[:-1])
N_EXPERTS = 32
TOP_K = 4
EXPERT_FF = D_MODEL
SWIGLU_LIMIT = 7.0
SWIGLU_ALPHA = 1.702
MOE_BLOCK = 128

kernel_name = "hybrid_diffusion_gqa_diff_hgrn2_moe"


def rms_norm(x, g):
    xf = x.astype(jnp.float32)
    y = xf * lax.rsqrt(jnp.mean(xf * xf, axis=-1, keepdims=True) + RMS_EPS)
    return (y * g.astype(jnp.float32)).astype(x.dtype)


def modulate(x, g, shift, scale):
    return rms_norm(x, g) * (1 + scale[:, None, :]) + shift[:, None, :]


def axial_rope_tables(rows, dtype):
    row = jnp.repeat(jnp.arange(rows, dtype=jnp.float32), GRID_W)
    col = jnp.tile(jnp.arange(GRID_W, dtype=jnp.float32), rows)
    n_freq = HEAD_DIM // 4
    inv = ROPE_THETA ** (-jnp.arange(n_freq, dtype=jnp.float32) / n_freq)
    ang_r = row[:, None] * inv
    ang_c = col[:, None] * inv
    return (jnp.cos(ang_r).astype(dtype), jnp.sin(ang_r).astype(dtype),
            jnp.cos(ang_c).astype(dtype), jnp.sin(ang_c).astype(dtype))


def apply_axial_rope(x, tabs):
    cos_r, sin_r, cos_c, sin_c = tabs
    n = HEAD_DIM // 4
    half = HEAD_DIM // 2
    shape = (1, x.shape[1]) + (1,) * (x.ndim - 3) + (n,)

    def rot(xa, cos, sin):
        cos = cos.reshape(shape)
        sin = sin.reshape(shape)
        x1, x2 = xa[..., :n], xa[..., n:]
        return jnp.concatenate([x1 * cos - x2 * sin, x2 * cos + x1 * sin], axis=-1)

    return jnp.concatenate([rot(x[..., :half], cos_r, sin_r), rot(x[..., half:], cos_c, sin_c)], axis=-1)


def window_gqa_latent(q, k, v, kc, vc, sink):
    B, S = q.shape[:2]
    L = kc.shape[1]
    nb = S // BLOCK
    scale = HEAD_DIM ** -0.5
    qb = q.reshape(B, nb, BLOCK, A_HKV, A_GROUP, HEAD_DIM)

    def band(t):
        tp = jnp.pad(t, ((0, 0), (BLOCK, BLOCK), (0, 0), (0, 0))).reshape(B, nb + 2, BLOCK, A_HKV, HEAD_DIM)
        return jnp.concatenate([tp[:, :-2], tp[:, 1:-1], tp[:, 2:]], axis=2)

    kw, vw = band(k), band(v)
    s_loc = jnp.einsum('bnqhgd,bnkhd->bnhgqk', qb, kw).astype(jnp.float32) * scale
    s_ctx = jnp.einsum('bnqhgd,blhd->bnhgql', qb, kc).astype(jnp.float32) * scale
    qi = jnp.arange(BLOCK)[:, None]
    kj = jnp.arange(3 * BLOCK)[None, :]
    rel = kj - BLOCK - qi
    kpos = jnp.arange(nb)[:, None, None] * BLOCK + kj[None] - BLOCK
    valid = (jnp.abs(rel) <= WINDOW)[None] & (kpos >= 0) & (kpos < S)
    s_loc = jnp.where(valid[None, :, None, None], s_loc, NEG_INF)
    s_sink = jnp.broadcast_to(sink.astype(jnp.float32).reshape(1, 1, A_HKV, A_GROUP, 1, 1), s_loc.shape[:-1] + (1,))
    p = jax.nn.softmax(jnp.concatenate([s_loc, s_ctx, s_sink], axis=-1), axis=-1).astype(v.dtype)
    o = (jnp.einsum('bnhgqk,bnkhd->bnqhgd', p[..., :3 * BLOCK], vw)
         + jnp.einsum('bnhgql,blhd->bnqhgd', p[..., 3 * BLOCK:3 * BLOCK + L], vc))
    return o.reshape(B, S, A_Q_W)


def sink_attn_context(qc, kc, vc, sink):
    B, L = qc.shape[:2]
    qg = qc.reshape(B, L, A_HKV, A_GROUP, HEAD_DIM)
    s = jnp.einsum('blhgd,bmhd->bhglm', qg, kc).astype(jnp.float32) * HEAD_DIM ** -0.5
    s_sink = jnp.broadcast_to(sink.astype(jnp.float32).reshape(1, A_HKV, A_GROUP, 1, 1), s.shape[:-1] + (1,))
    p = jax.nn.softmax(jnp.concatenate([s, s_sink], axis=-1), axis=-1)[..., :L].astype(vc.dtype)
    return jnp.einsum('bhglm,bmhd->blhgd', p, vc).reshape(B, L, A_Q_W)


def diff_attend(q, keys, vals, lam):
    s = jnp.einsum('bqhcd,bkhcd->bhcqk', q, keys).astype(jnp.float32) * B_HD ** -0.5
    p = jax.nn.softmax(s, axis=-1)
    a = (p[:, :, 0] - lam * p[:, :, 1]).astype(vals.dtype)
    return jnp.einsum('bhqk,bkhe->bqhe', a, vals)


def diff_attn_latent(q, k, v, kc, vc, lam):
    B, S = q.shape[:2]
    nb = S // BLOCK
    keys = jnp.concatenate([k, kc], axis=1)
    vals = jnp.concatenate([v, vc], axis=1)
    qb = jnp.moveaxis(q.reshape(B, nb, BLOCK, B_H, 2, B_HD), 1, 0)
    o = lax.map(lambda qq: diff_attend(qq, keys, vals, lam), qb)
    return jnp.moveaxis(o, 0, 1).reshape(B, S, B_H, B_VD)


def diff_head_out(o, w, lam_init):
    B, T = o.shape[:2]
    return (rms_norm(o, w) * (1 - lam_init)).reshape(B, T, B_V_W)


def hgrn2_log_forget(z, lb):
    lbf = lb.astype(jnp.float32)
    return jnp.logaddexp(jnp.log(lbf), jnp.log1p(-lbf) + jax.nn.log_sigmoid(z.astype(jnp.float32)))


def hgrn2_scan(q, k, logf, v, state):
    B, T = q.shape[:2]
    nc = T // CHUNK

    def chunks(t):
        return t.astype(jnp.float32).reshape(B, nc, CHUNK, C_H, t.shape[-1]).transpose(1, 0, 3, 2, 4)

    tri = jnp.tril(jnp.ones((CHUNK, CHUNK), dtype=bool))

    def step(S_prev, inp):
        qc, kc, gc, vc = inp
        b = jnp.cumsum(gc, axis=2)
        o_inter = jnp.einsum('bhtd,bhde->bhte', qc * jnp.exp(b), S_prev)
        diff = jnp.where(tri[:, :, None], b[:, :, :, None, :] - b[:, :, None, :, :], NEG_INF)
        att = jnp.einsum('bhtsd,bhsd->bhts', qc[:, :, :, None, :] * jnp.exp(diff), kc)
        o_intra = jnp.einsum('bhts,bhse->bhte', att, vc)
        b_last = b[:, :, -1:, :]
        S_new = (jnp.exp(b_last[:, :, 0, :])[..., None] * S_prev
                 + jnp.einsum('bhsd,bhse->bhde', kc * jnp.exp(b_last - b), vc))
        return S_new, o_inter + o_intra

    S_fin, o = lax.scan(step, state, (chunks(q), chunks(k), chunks(logf), chunks(v)))
    return S_fin, o.transpose(1, 0, 3, 2, 4).reshape(B, T, C_H, C_DV)


def hgrn2_prep(cq, ci, cff, cfb, lb):
    B, T, _ = cq.shape
    shp = (B, T, C_H, C_DK)
    q = jax.nn.silu(cq).reshape(shp)
    i = ci.reshape(B, T, C_H, C_DV)
    lf_f = hgrn2_log_forget(cff, lb[0]).reshape(shp)
    lf_b = hgrn2_log_forget(cfb, lb[1]).reshape(shp)
    return q, i, lf_f, lf_b


def hgrn2_bidir(q, i, lf_f, lf_b, s_f, s_b):
    k_f = -jnp.expm1(lf_f)
    k_b = -jnp.expm1(lf_b)
    sf, o_f = hgrn2_scan(q, k_f, lf_f, i, s_f)
    rev = lambda t: jnp.flip(t, axis=1)
    sb, o_b = hgrn2_scan(rev(q), rev(k_b), rev(lf_b), rev(i), s_b)
    return sf, sb, o_f + rev(o_b)


def hgrn2_out(o, g, w):
    B, T = o.shape[:2]
    y = rms_norm(o, w) * jax.nn.silu(g.reshape(o.shape).astype(jnp.float32))
    return y.reshape(B, T, C_V_W).astype(g.dtype)


def merge_branches(o_a, o_b, o_c, gates, w_br, w_o):
    g = gates.reshape(gates.shape[:-1] + (N_BRANCH, D_MODEL))
    y = jax.nn.sigmoid(g[..., 0, :]) * (o_a @ w_br[0])
    y = y + jax.nn.sigmoid(g[..., 1, :]) * (o_b @ w_br[1])
    y = y + jax.nn.sigmoid(g[..., 2, :]) * (o_c @ w_br[2])
    return y @ w_o


def hybrid_mixer(h, hc, tabs, lb, w_in, sink, lam_par, lam_init, diff_w, hgrn_w, w_br, w_o, with_ctx):
    B, S, _ = h.shape
    L = hc.shape[1]
    (aq, ak, av, bq, bk, bv, cq, ci, cff, cfb, cg, gt) = jnp.split(h @ w_in, SPLIT_IDX, axis=-1)
    (aqc, akc, avc, bqc, bkc, bvc, cqc, cic, cffc, cfbc, cgc, gtc) = jnp.split(hc @ w_in, SPLIT_IDX, axis=-1)

    ka_c = akc.reshape(B, L, A_HKV, HEAD_DIM)
    va_c = avc.reshape(B, L, A_HKV, HEAD_DIM)
    o_a = window_gqa_latent(apply_axial_rope(aq.reshape(B, S, A_HQ, HEAD_DIM), tabs),
                            apply_axial_rope(ak.reshape(B, S, A_HKV, HEAD_DIM), tabs),
                            av.reshape(B, S, A_HKV, HEAD_DIM), ka_c, va_c, sink)

    lp = lam_par.astype(jnp.float32)
    lam = jnp.exp(jnp.sum(lp[0] * lp[1])) - jnp.exp(jnp.sum(lp[2] * lp[3])) + lam_init
    kb_c = bkc.reshape(B, L, B_H, 2, B_HD)
    vb_c = bvc.reshape(B, L, B_H, B_VD)
    o_b = diff_attn_latent(apply_axial_rope(bq.reshape(B, S, B_H, 2, B_HD), tabs),
                           apply_axial_rope(bk.reshape(B, S, B_H, 2, B_HD), tabs),
                           bv.reshape(B, S, B_H, B_VD), kb_c, vb_c, lam)
    o_b = diff_head_out(o_b, diff_w, lam_init)

    zero = jnp.zeros((B, C_H, C_DK, C_DV), jnp.float32)
    qcc, icc, lfc_f, lfc_b = hgrn2_prep(cqc, cic, cffc, cfbc, lb)
    s_f, s_b, oc_c = hgrn2_bidir(qcc, icc, lfc_f, lfc_b, zero, zero)
    q_c, i_c, lf_f, lf_b = hgrn2_prep(cq, ci, cff, cfb, lb)
    _, _, o_c = hgrn2_bidir(q_c, i_c, lf_f, lf_b, s_f, s_b)
    o_c = hgrn2_out(o_c, cg, hgrn_w)

    y = merge_branches(o_a, o_b, o_c, gt, w_br, w_o)
    if not with_ctx:
        return y, None
    oa_c = sink_attn_context(aqc.reshape(B, L, A_HQ, HEAD_DIM), ka_c, va_c, sink)
    ob_c = diff_head_out(diff_attend(bqc.reshape(B, L, B_H, 2, B_HD), kb_c, vb_c, lam), diff_w, lam_init)
    oc_c = hgrn2_out(oc_c, cgc, hgrn_w)
    yc = merge_branches(oa_c, ob_c, oc_c, gtc, w_br, w_o)
    return y, yc


def moe_ffn(h, w_router, b_router, w_gu, b_gu, w_dn, b_dn):
    N, D = h.shape
    NK = N * TOP_K
    logits = (h @ w_router + b_router).astype(jnp.float32)
    top_v, top_e = lax.top_k(logits, TOP_K)
    gates = jax.nn.softmax(top_v, axis=-1)
    flat_e = top_e.reshape(-1).astype(jnp.int32)
    order = jnp.argsort(flat_e)
    sorted_e = flat_e[order]
    counts = jnp.bincount(flat_e, length=N_EXPERTS).astype(jnp.int32)
    padded = (counts + MOE_BLOCK - 1) // MOE_BLOCK * MOE_BLOCK
    pad_end = jnp.cumsum(padded)
    pad_start = pad_end - padded
    grp_start = jnp.cumsum(counts) - counts
    rank = jnp.arange(NK, dtype=jnp.int32) - grp_start[sorted_e]
    dest = jnp.zeros((NK,), jnp.int32).at[order].set(pad_start[sorted_e] + rank)
    n_blk = (NK + MOE_BLOCK - 1) // MOE_BLOCK + N_EXPERTS
    R = n_blk * MOE_BLOCK
    buf = jnp.zeros((R, D), h.dtype).at[dest].set(jnp.repeat(h, TOP_K, axis=0))
    blk_e = jnp.clip(jnp.searchsorted(pad_end, jnp.arange(n_blk, dtype=jnp.int32) * MOE_BLOCK, side='right'),
                     0, N_EXPERTS - 1)

    def expert_block(args):
        xb, e = args
        gu = xb @ w_gu[e] + b_gu[e]
        g, u = gu[:, :EXPERT_FF], gu[:, EXPERT_FF:]
        g = jnp.minimum(g, SWIGLU_LIMIT)
        u = jnp.clip(u, -SWIGLU_LIMIT, SWIGLU_LIMIT)
        return ((u + 1) * (g * jax.nn.sigmoid(SWIGLU_ALPHA * g))) @ w_dn[e] + b_dn[e]

    y = lax.map(expert_block, (buf.reshape(n_blk, MOE_BLOCK, D), blk_e)).reshape(R, D)
    return jnp.sum(y[dest].reshape(N, TOP_K, D) * gates[..., None].astype(y.dtype), axis=1)


def setup_inputs(seed: int = 0) -> dict:
    key = jax.random.key(seed)
    ks = jax.random.split(key, 21)
    D = D_MODEL

    def nrm(k, shape, scale):
        return jax.random.normal(k, shape, jnp.float32) * scale

    return {
        "x": nrm(ks[0], (BATCH, SEQ, D), 1.0),
        "c": nrm(ks[1], (BATCH, D), 1.0),
        "ctx": nrm(ks[2], (BATCH, CTX_LEN, D), 1.0),
        "c_ctx": nrm(ks[3], (D,), 1.0),
        "w_mod": nrm(ks[4], (DEPTH, D, 6 * D), 0.3 * D ** -0.5),
        "b_mod": nrm(ks[5], (DEPTH, 6 * D), 0.02),
        "norm_g": 1.0 + nrm(ks[6], (DEPTH, 4, D), 0.02),
        "w_in": nrm(ks[7], (DEPTH, D, IN_W), D ** -0.5),
        "attn_sink": nrm(ks[8], (DEPTH, A_HQ), 0.5),
        "diff_lambda": nrm(ks[9], (DEPTH, 4, B_HD), 0.1),
        "diff_norm_w": 1.0 + nrm(ks[10], (DEPTH, B_VD), 0.02),
        "hgrn_lb_logits": nrm(ks[11], (DEPTH, 2, C_H * C_DK), 0.5),
        "hgrn_norm_w": 1.0 + nrm(ks[12], (DEPTH, C_DV), 0.02),
        "w_branch": nrm(ks[13], (DEPTH, N_BRANCH, BRANCH_W, D), BRANCH_W ** -0.5),
        "w_out": nrm(ks[14], (DEPTH, D, D), D ** -0.5),
        "w_router": nrm(ks[15], (DEPTH, D, N_EXPERTS), D ** -0.5),
        "b_router": nrm(ks[16], (DEPTH, N_EXPERTS), 0.01),
        "w_gate_up": nrm(ks[17], (DEPTH, N_EXPERTS, D, 2 * EXPERT_FF), D ** -0.5),
        "b_gate_up": nrm(ks[18], (DEPTH, N_EXPERTS, 2 * EXPERT_FF), 0.01),
        "w_down": nrm(ks[19], (DEPTH, N_EXPERTS, EXPERT_FF, D), EXPERT_FF ** -0.5),
        "b_down": nrm(ks[20], (DEPTH, N_EXPERTS, D), 0.01),
    }


def reference(x, c, ctx, c_ctx, w_mod, b_mod, norm_g, w_in, attn_sink, diff_lambda, diff_norm_w,
              hgrn_lb_logits, hgrn_norm_w, w_branch, w_out, w_router, b_router, w_gate_up, b_gate_up,
              w_down, b_down):
    B, S, D = x.shape
    L = ctx.shape[1]
    ROWS = S // GRID_W
    tabs = axial_rope_tables(ROWS, x.dtype)
    lb_cum = jnp.cumsum(jax.nn.softmax(hgrn_lb_logits.astype(jnp.float32), axis=0), axis=0)
    lower_bounds = lb_cum - lb_cum[0:1]
    xc = ctx
    for l in range(DEPTH):
        last = l == DEPTH - 1
        lam_init = 0.8 - 0.6 * math.exp(-0.3 * l)
        mod = (jax.nn.silu(c) @ w_mod[l] + b_mod[l]).reshape(B, 6, D)
        mod_c = (jax.nn.silu(c_ctx)[None] @ w_mod[l] + b_mod[l]).reshape(1, 6, D)
        h = modulate(x, norm_g[l, 0], mod[:, 0], mod[:, 1])
        hc = modulate(xc, norm_g[l, 0], mod_c[:, 0], mod_c[:, 1])
        y, yc = hybrid_mixer(h, hc, tabs, lower_bounds[l], w_in[l], attn_sink[l], diff_lambda[l], lam_init,
                             diff_norm_w[l], hgrn_norm_w[l], w_branch[l], w_out[l], not last)
        x = x + mod[:, 2][:, None, :] * rms_norm(y, norm_g[l, 1])
        h = modulate(x, norm_g[l, 2], mod[:, 3], mod[:, 4])
        if last:
            f = moe_ffn(h.reshape(B * S, D), w_router[l], b_router[l], w_gate_up[l], b_gate_up[l],
                        w_down[l], b_down[l]).reshape(B, S, D)
        else:
            xc = xc + mod_c[:, 2][:, None, :] * rms_norm(yc, norm_g[l, 1])
            hc = modulate(xc, norm_g[l, 2], mod_c[:, 3], mod_c[:, 4])
            tok = jnp.concatenate([h.reshape(B * S, D), hc.reshape(B * L, D)], axis=0)
            ft = moe_ffn(tok, w_router[l], b_router[l], w_gate_up[l], b_gate_up[l], w_down[l], b_down[l])
            f = ft[:B * S].reshape(B, S, D)
            fc = ft[B * S:].reshape(B, L, D)
            xc = xc + mod_c[:, 5][:, None, :] * rms_norm(fc, norm_g[l, 3])
        x = x + mod[:, 5][:, None, :] * rms_norm(f, norm_g[l, 3])
    return x
```

```python
import functools
import math

import jax
import jax.numpy as jnp
from jax import lax
from jax.experimental import pallas as pl
from jax.experimental.pallas import tpu as pltpu

D_MODEL = 1024
DEPTH = 2
GRID_W = 64
HEAD_DIM = 64
ROPE_THETA = 10000.0
RMS_EPS = 1e-6
NEG_INF = -1e30
BLOCK = 128
A_HQ = 8
A_HKV = 2
A_GROUP = A_HQ // A_HKV
WINDOW = 128
B_H = 4
B_HD = HEAD_DIM
B_VD = 2 * HEAD_DIM
C_H = 4
C_DK = 128
C_DV = 128
N_BRANCH = 3
BRANCH_W = 512
A_Q_W = A_HQ * HEAD_DIM
A_KV_W = A_HKV * HEAD_DIM
B_QK_W = B_H * 2 * B_HD
B_V_W = B_H * B_VD
C_K_W = C_H * C_DK
C_V_W = C_H * C_DV
GATE_W = N_BRANCH * D_MODEL
MIX_W = A_Q_W + 2 * A_KV_W + 2 * B_QK_W + B_V_W + 3 * C_K_W + 2 * C_V_W
N_EXPERTS = 32
TOP_K = 4
EXPERT_FF = D_MODEL
SWIGLU_LIMIT = 7.0
SWIGLU_ALPHA = 1.702

OFF_AQ = 0
OFF_AK = OFF_AQ + A_Q_W
OFF_AV = OFF_AK + A_KV_W
OFF_BQ = OFF_AV + A_KV_W
OFF_BK = OFF_BQ + B_QK_W
OFF_BV = OFF_BK + B_QK_W
OFF_CQ = OFF_BV + B_V_W
OFF_CI = OFF_CQ + C_K_W
OFF_CFF = OFF_CI + C_V_W
OFF_CFB = OFF_CFF + C_K_W
OFF_CG = OFF_CFB + C_K_W

V7X_VMEM_LIMIT_BYTES = 56 * 1024 * 1024
LANES = 128

MM_TM = 512
FLASH_TQ = 256
FLASH_TK = 768
HGRN_TB = 256
HGRN_SUB = 16
MOE_TM = 512


def _cparams(sem):
    return pltpu.CompilerParams(dimension_semantics=sem, vmem_limit_bytes=V7X_VMEM_LIMIT_BYTES)


def _mm_kernel(x_ref, w_ref, o_ref, *, precision):
    o_ref[...] = jnp.dot(x_ref[...], w_ref[...], preferred_element_type=jnp.float32,
                         precision=precision).astype(o_ref.dtype)


def _matmul(x, w, *, tm, tn, out_dtype, precision=None, name):
    m, k = x.shape
    n = w.shape[1]
    assert m % tm == 0 and n % tn == 0, (m, tm, n, tn)
    return pl.pallas_call(
        functools.partial(_mm_kernel, precision=precision),
        grid=(n // tn, m // tm),
        in_specs=[pl.BlockSpec((tm, k), lambda j, i: (i, 0)),
                  pl.BlockSpec((k, tn), lambda j, i: (0, j))],
        out_specs=pl.BlockSpec((tm, tn), lambda j, i: (i, j)),
        out_shape=jax.ShapeDtypeStruct((m, n), out_dtype),
        compiler_params=_cparams(("arbitrary", "arbitrary")),
        name=name,
    )(x, w)


def _sink_attend(qh, kh, vh, mask, sk):
    s = lax.dot_general(qh, kh, (((1,), (1,)), ((), ())), preferred_element_type=jnp.float32)
    if mask is not None:
        s = jnp.where(mask, s, NEG_INF)
    m = jnp.maximum(jnp.max(s, axis=-1, keepdims=True), sk)
    p = jnp.exp(s - m)
    l = jnp.sum(p, axis=-1, keepdims=True) + jnp.exp(sk - m)
    o = jnp.dot(p.astype(vh.dtype), vh, preferred_element_type=jnp.float32)
    return o / l


def _attn_a_kernel(sink_ref, q_ref, kp_ref, kc_ref, kn_ref, vp_ref, vc_ref, vn_ref, kx_ref, vx_ref,
                   o_ref, *, seq):
    n = pl.program_id(1)
    q = q_ref[0]
    k_all = jnp.concatenate([kp_ref[0], kc_ref[0], kn_ref[0], kx_ref[0]], axis=0)
    v_all = jnp.concatenate([vp_ref[0], vc_ref[0], vn_ref[0], vx_ref[0]], axis=0)
    n_keys = k_all.shape[0]
    qi = lax.broadcasted_iota(jnp.int32, (BLOCK, n_keys), 0)
    kj = lax.broadcasted_iota(jnp.int32, (BLOCK, n_keys), 1)
    kpos = n * BLOCK + kj - BLOCK
    local_ok = (jnp.abs(kj - BLOCK - qi) <= WINDOW) & (kpos >= 0) & (kpos < seq)
    mask = local_ok | (kj >= 3 * BLOCK)
    outs = []
    for hq in range(A_HQ):
        hk = hq // A_GROUP
        outs.append(_sink_attend(q[:, hq * HEAD_DIM:(hq + 1) * HEAD_DIM],
                                 k_all[:, hk * HEAD_DIM:(hk + 1) * HEAD_DIM],
                                 v_all[:, hk * HEAD_DIM:(hk + 1) * HEAD_DIM], mask, sink_ref[hq]))
    o_ref[0] = jnp.concatenate(outs, axis=1).astype(o_ref.dtype)


def _attn_a_ctx_kernel(sink_ref, q_ref, kx_ref, vx_ref, o_ref):
    q = q_ref[0]
    k_all = kx_ref[0]
    v_all = vx_ref[0]
    outs = []
    for hq in range(A_HQ):
        hk = hq // A_GROUP
        outs.append(_sink_attend(q[:, hq * HEAD_DIM:(hq + 1) * HEAD_DIM],
                                 k_all[:, hk * HEAD_DIM:(hk + 1) * HEAD_DIM],
                                 v_all[:, hk * HEAD_DIM:(hk + 1) * HEAD_DIM], None, sink_ref[hq]))
    o_ref[0] = jnp.concatenate(outs, axis=1).astype(o_ref.dtype)


def _window_attention(q, k, v, kx, vx, sink):
    b, s, _ = q.shape
    l = kx.shape[1]
    nb = s // BLOCK
    prev = lambda bi, n, *_: (bi, jnp.maximum(n - 1, 0), 0)
    cur = lambda bi, n, *_: (bi, n, 0)
    nxt = lambda bi, n, *_: (bi, jnp.minimum(n + 1, nb - 1), 0)
    whole = lambda bi, n, *_: (bi, 0, 0)
    kv_blk = (1, BLOCK, A_KV_W)
    return pl.pallas_call(
        functools.partial(_attn_a_kernel, seq=s),
        grid_spec=pltpu.PrefetchScalarGridSpec(
            num_scalar_prefetch=1, grid=(b, nb),
            in_specs=[pl.BlockSpec((1, BLOCK, A_Q_W), cur),
                      pl.BlockSpec(kv_blk, prev), pl.BlockSpec(kv_blk, cur), pl.BlockSpec(kv_blk, nxt),
                      pl.BlockSpec(kv_blk, prev), pl.BlockSpec(kv_blk, cur), pl.BlockSpec(kv_blk, nxt),
                      pl.BlockSpec((1, l, A_KV_W), whole), pl.BlockSpec((1, l, A_KV_W), whole)],
            out_specs=pl.BlockSpec((1, BLOCK, A_Q_W), cur)),
        out_shape=jax.ShapeDtypeStruct((b, s, A_Q_W), jnp.bfloat16),
        compiler_params=_cparams(("arbitrary", "arbitrary")),
        name="window_gqa",
    )(sink, q, k, k, k, v, v, v, kx, vx)


def _context_attention(q, kx, vx, sink):
    b, l, _ = q.shape
    whole = lambda bi, *_: (bi, 0, 0)
    return pl.pallas_call(
        _attn_a_ctx_kernel,
        grid_spec=pltpu.PrefetchScalarGridSpec(
            num_scalar_prefetch=1, grid=(b,),
            in_specs=[pl.BlockSpec((1, l, A_Q_W), whole),
                      pl.BlockSpec((1, l, A_KV_W), whole), pl.BlockSpec((1, l, A_KV_W), whole)],
            out_specs=pl.BlockSpec((1, l, A_Q_W), whole)),
        out_shape=jax.ShapeDtypeStruct((b, l, A_Q_W), jnp.bfloat16),
        compiler_params=_cparams(("arbitrary",)),
        name="context_gqa",
    )(sink, q, kx, vx)


def _diff_flash_kernel(lam_ref, q_ref, k_ref, v_ref, o_ref, q2_sc, m_sc, l_sc, acc_sc, *, tq, nk):
    j = pl.program_id(3)

    @pl.when(j == 0)
    def _():
        q = q_ref[0]
        lane = lax.broadcasted_iota(jnp.int32, q.shape, 1)
        q2_sc[:tq, :] = jnp.where(lane < B_HD, q, jnp.zeros_like(q))
        q2_sc[tq:, :] = jnp.where(lane >= B_HD, q, jnp.zeros_like(q))
        m_sc[...] = jnp.full_like(m_sc, NEG_INF)
        l_sc[...] = jnp.zeros_like(l_sc)
        acc_sc[...] = jnp.zeros_like(acc_sc)

    s = lax.dot_general(q2_sc[...], k_ref[0, 0], (((1,), (1,)), ((), ())),
                        preferred_element_type=jnp.float32)
    m_prev = m_sc[...]
    m_new = jnp.maximum(m_prev, jnp.max(s, axis=-1, keepdims=True))
    alpha = jnp.exp(m_prev - m_new)
    p = jnp.exp(s - m_new)
    l_sc[...] = alpha * l_sc[...] + jnp.sum(p, axis=-1, keepdims=True)
    acc_sc[...] = alpha * acc_sc[...] + jnp.dot(p.astype(jnp.bfloat16), v_ref[0, 0],
                                                preferred_element_type=jnp.float32)
    m_sc[...] = m_new

    @pl.when(j == nk - 1)
    def _():
        o = acc_sc[...] / l_sc[...]
        o_ref[0] = (o[:tq, :] - lam_ref[0] * o[tq:, :]).astype(o_ref.dtype)


def _diff_attention(lam, q, keys, vals, *, tq, tk, k_block_offset, nk):
    b, t_q, _ = q.shape
    return pl.pallas_call(
        functools.partial(_diff_flash_kernel, tq=tq, nk=nk),
        grid_spec=pltpu.PrefetchScalarGridSpec(
            num_scalar_prefetch=1, grid=(b, B_H, t_q // tq, nk),
            in_specs=[pl.BlockSpec((1, tq, 2 * B_HD), lambda bi, h, i, j, *_: (bi, i, h)),
                      pl.BlockSpec((1, 1, tk, 2 * B_HD), lambda bi, h, i, j, *_: (bi, h, j + k_block_offset, 0)),
                      pl.BlockSpec((1, 1, tk, B_VD), lambda bi, h, i, j, *_: (bi, h, j + k_block_offset, 0))],
            out_specs=pl.BlockSpec((1, tq, B_VD), lambda bi, h, i, j, *_: (bi, i, h)),
            scratch_shapes=[pltpu.VMEM((2 * tq, 2 * B_HD), jnp.bfloat16),
                            pltpu.VMEM((2 * tq, 1), jnp.float32),
                            pltpu.VMEM((2 * tq, 1), jnp.float32),
                            pltpu.VMEM((2 * tq, B_VD), jnp.float32)]),
        out_shape=jax.ShapeDtypeStruct((b, t_q, B_V_W), jnp.float32),
        compiler_params=_cparams(("arbitrary", "arbitrary", "arbitrary", "arbitrary")),
        name="diff_flash",
    )(lam, q, keys, vals)


def _hgrn_kernel(q_ref, v_ref, lf_ref, o_ref, st_ref, *, tb):
    d = pl.program_id(1)
    t = pl.program_id(2)
    nsub = tb // HGRN_SUB

    @pl.when(t == 0)
    def _():
        st_ref[...] = jnp.zeros_like(st_ref)

    sgn = 1 - 2 * d
    rr = lax.broadcasted_iota(jnp.int32, (HGRN_SUB, HGRN_SUB), 0)
    cc = lax.broadcasted_iota(jnp.int32, (HGRN_SUB, HGRN_SUB), 1)
    tri = (((cc - rr) * sgn) <= 0).astype(jnp.bfloat16)
    srow = lax.broadcasted_iota(jnp.int32, (HGRN_SUB, C_DK), 0)

    def body(i, carry):
        sc = jnp.where(d == 0, i, nsub - 1 - i)
        r0 = pl.multiple_of(sc * HGRN_SUB, HGRN_SUB)
        lf = lf_ref[0, 0, pl.ds(r0, HGRN_SUB), :]
        lf_a = lf.astype(jnp.bfloat16)
        rem = lf - lf_a.astype(jnp.float32)
        lf_b = rem.astype(jnp.bfloat16)
        lf_c = (rem - lf_b.astype(jnp.float32)).astype(jnp.bfloat16)
        bl = (jnp.dot(tri, lf_a, preferred_element_type=jnp.float32)
              + jnp.dot(tri, lf_b, preferred_element_type=jnp.float32)
              + jnp.dot(tri, lf_c, preferred_element_type=jnp.float32))
        q = q_ref[0, pl.ds(r0, HGRN_SUB), :]
        v = v_ref[0, pl.ds(r0, HGRN_SUB), :]
        for hh in range(C_H):
            sl = slice(hh * C_DK, (hh + 1) * C_DK)
            lfh, blh, qh, vh = lf[:, sl], bl[:, sl], q[:, sl], v[:, sl]
            tot = jnp.sum(lfh, axis=0, keepdims=True)
            kh = 1.0 - jnp.exp(lfh)
            qt = qh * jnp.exp(blh)
            kt = kh * jnp.exp(tot - blh)
            st = st_ref[hh]
            o_sub = lax.dot_general(qt.astype(jnp.bfloat16), st.astype(jnp.bfloat16),
                                    (((1,), (1,)), ((), ())), preferred_element_type=jnp.float32)
            for tt in range(HGRN_SUB):
                ok = ((srow - tt) * sgn) <= 0
                e = jnp.exp(jnp.where(ok, blh[tt:tt + 1, :] - blh, NEG_INF))
                att = jnp.sum((qh[tt:tt + 1, :] * kh) * e, axis=1, keepdims=True)
                row = jnp.sum(att * vh, axis=0, keepdims=True)
                o_sub = o_sub + jnp.where(srow == tt, row, 0.0)
            o_ref[0, 0, pl.ds(r0, HGRN_SUB), sl] = o_sub
            upd = lax.dot_general(vh.astype(jnp.bfloat16), kt.astype(jnp.bfloat16),
                                  (((0,), (0,)), ((), ())), preferred_element_type=jnp.float32)
            st_ref[hh] = jnp.exp(tot) * st + upd
        return carry

    lax.fori_loop(0, nsub, body, 0)


def _hgrn_scan(q, v, lf, *, n_ctx_blocks):
    b, t, w = q.shape
    tb = HGRN_TB
    nblk = t // tb
    nlat = nblk - n_ctx_blocks

    def row_block(di, ti):
        ctx_blk = jnp.where(di == 0, nlat + ti, nblk - 1 - ti)
        lat_blk = jnp.where(di == 0, ti - n_ctx_blocks, nblk - 1 - ti)
        return jnp.where(ti < n_ctx_blocks, ctx_blk, lat_blk)

    return pl.pallas_call(
        functools.partial(_hgrn_kernel, tb=tb),
        grid=(b, 2, nblk),
        in_specs=[pl.BlockSpec((1, tb, w), lambda bi, di, ti: (bi, row_block(di, ti), 0)),
                  pl.BlockSpec((1, tb, w), lambda bi, di, ti: (bi, row_block(di, ti), 0)),
                  pl.BlockSpec((1, 1, tb, w), lambda bi, di, ti: (di, bi, row_block(di, ti), 0))],
        out_specs=pl.BlockSpec((1, 1, tb, w), lambda bi, di, ti: (di, bi, row_block(di, ti), 0)),
        out_shape=jax.ShapeDtypeStruct((2, b, t, w), jnp.float32),
        scratch_shapes=[pltpu.VMEM((C_H, C_DV, C_DK), jnp.float32)],
        compiler_params=_cparams(("arbitrary", "arbitrary", "arbitrary")),
        name="hgrn2_scan",
    )(q, v, lf)


def _merge_kernel(oa_ref, ob_ref, oc_ref, g_ref, wbr_ref, wo_ref, y_ref):
    acc = None
    for i, o_ref in enumerate((oa_ref, ob_ref, oc_ref)):
        z = jnp.dot(o_ref[...], wbr_ref[i], preferred_element_type=jnp.float32)
        g = jax.nn.sigmoid(g_ref[:, i * D_MODEL:(i + 1) * D_MODEL].astype(jnp.float32))
        acc = g * z if acc is None else acc + g * z
    y_ref[...] = jnp.dot(acc.astype(jnp.bfloat16), wo_ref[...], preferred_element_type=jnp.float32)


def _merge(oa, ob, oc, gates, w_br, w_o):
    m = oa.shape[0]
    tm = MM_TM
    row = lambda i: (i, 0)
    return pl.pallas_call(
        _merge_kernel,
        grid=(m // tm,),
        in_specs=[pl.BlockSpec((tm, BRANCH_W), row), pl.BlockSpec((tm, BRANCH_W), row),
                  pl.BlockSpec((tm, BRANCH_W), row), pl.BlockSpec((tm, GATE_W), row),
                  pl.BlockSpec((N_BRANCH, BRANCH_W, D_MODEL), lambda i: (0, 0, 0)),
                  pl.BlockSpec((D_MODEL, D_MODEL), lambda i: (0, 0))],
        out_specs=pl.BlockSpec((tm, D_MODEL), row),
        out_shape=jax.ShapeDtypeStruct((m, D_MODEL), jnp.float32),
        compiler_params=_cparams(("arbitrary",)),
        name="branch_merge",
    )(oa, ob, oc, gates, w_br, w_o)


def _moe_kernel(blk_e_ref, n_used_ref, x_ref, wgu_ref, bgu_ref, wdn_ref, bdn_ref, y_ref, wgu_sc, wdn_sc):
    i = pl.program_id(0)
    e = blk_e_ref[i]
    e_prev = blk_e_ref[jnp.maximum(i - 1, 0)]

    @pl.when((i == 0) | (e != e_prev))
    def _():
        wgu_sc[...] = wgu_ref[0].astype(jnp.bfloat16)
        wdn_sc[...] = wdn_ref[0].astype(jnp.bfloat16)

    @pl.when(i < n_used_ref[0])
    def _():
        gu = jnp.dot(x_ref[...], wgu_sc[...], preferred_element_type=jnp.float32) + bgu_ref[0]
        g = jnp.minimum(gu[:, :EXPERT_FF], SWIGLU_LIMIT)
        u = jnp.clip(gu[:, EXPERT_FF:], -SWIGLU_LIMIT, SWIGLU_LIMIT)
        act = (u + 1.0) * (g * jax.nn.sigmoid(SWIGLU_ALPHA * g))
        y = jnp.dot(act.astype(jnp.bfloat16), wdn_sc[...], preferred_element_type=jnp.float32) + bdn_ref[0]
        y_ref[...] = y.astype(y_ref.dtype)

    @pl.when(i >= n_used_ref[0])
    def _():
        y_ref[...] = jnp.zeros_like(y_ref)


def _moe_experts(blk_e, n_used, xs, w_gu, b_gu, w_dn, b_dn):
    r, dm = xs.shape
    tm = MOE_TM
    return pl.pallas_call(
        _moe_kernel,
        grid_spec=pltpu.PrefetchScalarGridSpec(
            num_scalar_prefetch=2, grid=(r // tm,),
            in_specs=[pl.BlockSpec((tm, dm), lambda i, be, nu: (i, 0)),
                      pl.BlockSpec((1, dm, 2 * EXPERT_FF), lambda i, be, nu: (be[i], 0, 0)),
                      pl.BlockSpec((1, 1, 2 * EXPERT_FF), lambda i, be, nu: (be[i], 0, 0)),
                      pl.BlockSpec((1, EXPERT_FF, dm), lambda i, be, nu: (be[i], 0, 0)),
                      pl.BlockSpec((1, 1, dm), lambda i, be, nu: (be[i], 0, 0))],
            out_specs=pl.BlockSpec((tm, dm), lambda i, be, nu: (i, 0)),
            scratch_shapes=[pltpu.VMEM((dm, 2 * EXPERT_FF), jnp.bfloat16),
                            pltpu.VMEM((EXPERT_FF, dm), jnp.bfloat16)]),
        out_shape=jax.ShapeDtypeStruct((r, dm), jnp.bfloat16),
        compiler_params=_cparams(("arbitrary",)),
        name="moe_experts",
    )(blk_e, n_used, xs, w_gu, b_gu.reshape(N_EXPERTS, 1, -1), w_dn, b_dn.reshape(N_EXPERTS, 1, -1))


def _moe_ffn(h, w_router, b_router, w_gu, b_gu, w_dn, b_dn):
    n, dm = h.shape
    nk = n * TOP_K
    w_r = jnp.zeros((dm, LANES), jnp.float32).at[:, :N_EXPERTS].set(w_router)
    logits = _matmul(h, w_r, tm=MM_TM, tn=LANES, out_dtype=jnp.float32,
                     precision=lax.Precision.HIGHEST, name="router")[:, :N_EXPERTS] + b_router
    top_v, top_e = lax.top_k(logits, TOP_K)
    gates = jax.nn.softmax(top_v, axis=-1)
    flat_e = top_e.reshape(-1).astype(jnp.int32)
    order = jnp.argsort(flat_e)
    sorted_e = flat_e[order]
    counts = jnp.bincount(flat_e, length=N_EXPERTS).astype(jnp.int32)
    padded = (counts + MOE_TM - 1) // MOE_TM * MOE_TM
    pad_end = jnp.cumsum(padded)
    pad_start = pad_end - padded
    grp_start = jnp.cumsum(counts) - counts
    rank = jnp.arange(nk, dtype=jnp.int32) - grp_start[sorted_e]
    dest = jnp.zeros((nk,), jnp.int32).at[order].set(pad_start[sorted_e] + rank)
    n_blk = (nk + MOE_TM - 1) // MOE_TM + N_EXPERTS
    rows = n_blk * MOE_TM
    row_tok = jnp.zeros((rows,), jnp.int32).at[dest].set(jnp.arange(nk, dtype=jnp.int32) // TOP_K)
    xs = h.astype(jnp.bfloat16)[row_tok]
    blk_e = jnp.clip(jnp.searchsorted(pad_end, jnp.arange(n_blk, dtype=jnp.int32) * MOE_TM, side='right'),
                     0, N_EXPERTS - 1).astype(jnp.int32)
    n_used = (pad_end[-1] // MOE_TM).astype(jnp.int32).reshape(1)
    y = _moe_experts(blk_e, n_used, xs, w_gu, b_gu, w_dn, b_dn)
    yk = y[dest].reshape(n, TOP_K, dm).astype(jnp.float32)
    return jnp.sum(yk * gates[..., None], axis=1)


def _rms_norm(x, g):
    xf = x.astype(jnp.float32)
    y = xf * lax.rsqrt(jnp.mean(xf * xf, axis=-1, keepdims=True) + RMS_EPS)
    return y * g.astype(jnp.float32)


def _modulate(x, g, shift, scale):
    return _rms_norm(x, g) * (1 + scale[:, None, :]) + shift[:, None, :]


def _rope_tables(rows):
    row = jnp.repeat(jnp.arange(rows, dtype=jnp.float32), GRID_W)
    col = jnp.tile(jnp.arange(GRID_W, dtype=jnp.float32), rows)
    n_freq = HEAD_DIM // 4
    inv = ROPE_THETA ** (-jnp.arange(n_freq, dtype=jnp.float32) / n_freq)
    ang_r = row[:, None] * inv
    ang_c = col[:, None] * inv
    return jnp.cos(ang_r), jnp.sin(ang_r), jnp.cos(ang_c), jnp.sin(ang_c)


def _rope(x, tabs):
    cos_r, sin_r, cos_c, sin_c = tabs
    n = HEAD_DIM // 4
    half = HEAD_DIM // 2
    shape = (1, x.shape[1]) + (1,) * (x.ndim - 3) + (n,)

    def rot(xa, cos, sin):
        cos = cos.reshape(shape)
        sin = sin.reshape(shape)
        x1, x2 = xa[..., :n], xa[..., n:]
        return jnp.concatenate([x1 * cos - x2 * sin, x2 * cos + x1 * sin], axis=-1)

    return jnp.concatenate([rot(x[..., :half], cos_r, sin_r), rot(x[..., half:], cos_c, sin_c)], axis=-1)


def _log_forget(z, lb):
    lbf = lb.astype(jnp.float32)
    return jnp.logaddexp(jnp.log(lbf), jnp.log1p(-lbf) + jax.nn.log_sigmoid(z.astype(jnp.float32)))


def _hybrid_mixer(tok, seq, tabs, lb, w_in, sink, lam_par, lam_init, diff_w, hgrn_w, w_br, w_o):
    b, t, dm = tok.shape
    l = t - seq
    f32 = jnp.float32
    bf16 = jnp.bfloat16
    flat = tok.reshape(b * t, dm)
    w_in_bf = w_in.astype(bf16)
    proj = _matmul(flat, w_in_bf[:, :MIX_W], tm=MM_TM, tn=MIX_W // 2, out_dtype=bf16,
                   name="in_proj").reshape(b, t, MIX_W)
    gates = _matmul(flat, w_in_bf[:, MIX_W:], tm=MM_TM, tn=GATE_W, out_dtype=bf16, name="gate_proj")

    def cols(off, width, rows=slice(None)):
        return proj[:, rows, off:off + width].astype(f32)

    lat = slice(0, seq)
    cx = slice(seq, t)
    scale = HEAD_DIM ** -0.5

    aq = (_rope(cols(OFF_AQ, A_Q_W, lat).reshape(b, seq, A_HQ, HEAD_DIM), tabs) * scale)
    aq = aq.reshape(b, seq, A_Q_W).astype(bf16)
    ak = _rope(cols(OFF_AK, A_KV_W, lat).reshape(b, seq, A_HKV, HEAD_DIM), tabs).reshape(b, seq, A_KV_W).astype(bf16)
    av = proj[:, lat, OFF_AV:OFF_AV + A_KV_W]
    akx = proj[:, cx, OFF_AK:OFF_AK + A_KV_W]
    avx = proj[:, cx, OFF_AV:OFF_AV + A_KV_W]
    sink32 = sink.astype(f32)
    o_a = _window_attention(aq, ak, av, akx, avx, sink32)
    aqx = (cols(OFF_AQ, A_Q_W, cx) * scale).astype(bf16)
    o_ax = _context_attention(aqx, akx, avx, sink32)
    o_a = jnp.concatenate([o_a, o_ax], axis=1)

    lp = lam_par.astype(f32)
    lam = jnp.exp(jnp.sum(lp[0] * lp[1])) - jnp.exp(jnp.sum(lp[2] * lp[3])) + lam_init
    lam = lam.reshape(1).astype(f32)
    bq = (_rope(cols(OFF_BQ, B_QK_W, lat).reshape(b, seq, B_H, 2, B_HD), tabs) * scale)
    bq = bq.reshape(b, seq, B_QK_W).astype(bf16)
    bk = _rope(cols(OFF_BK, B_QK_W, lat).reshape(b, seq, B_H, 2, B_HD), tabs).reshape(b, seq, B_QK_W).astype(bf16)
    keys = jnp.concatenate([bk, proj[:, cx, OFF_BK:OFF_BK + B_QK_W]], axis=1)
    keys = keys.reshape(b, t, B_H, 2 * B_HD).transpose(0, 2, 1, 3)
    vals = proj[:, :, OFF_BV:OFF_BV + B_V_W].reshape(b, t, B_H, B_VD).transpose(0, 2, 1, 3)
    o_b = _diff_attention(lam, bq, keys, vals, tq=FLASH_TQ, tk=FLASH_TK, k_block_offset=0, nk=t // FLASH_TK)
    bqx = (cols(OFF_BQ, B_QK_W, cx) * scale).astype(bf16)
    o_bx = _diff_attention(lam, bqx, keys, vals, tq=l, tk=l, k_block_offset=seq // l, nk=1)
    o_b = jnp.concatenate([o_b, o_bx], axis=1).reshape(b, t, B_H, B_VD)
    o_b = (_rms_norm(o_b, diff_w) * (1 - lam_init)).reshape(b, t, B_V_W).astype(bf16)

    qc = jax.nn.silu(cols(OFF_CQ, C_K_W))
    ic = cols(OFF_CI, C_V_W)
    lf = jnp.stack([_log_forget(cols(OFF_CFF, C_K_W), lb[0]), _log_forget(cols(OFF_CFB, C_K_W), lb[1])])
    o2 = _hgrn_scan(qc, ic, lf, n_ctx_blocks=l // HGRN_TB)
    o_c = (o2[0] + o2[1]).reshape(b, t, C_H, C_DV)
    o_c = _rms_norm(o_c, hgrn_w) * jax.nn.silu(cols(OFF_CG, C_V_W).reshape(b, t, C_H, C_DV))
    o_c = o_c.reshape(b, t, C_V_W).astype(bf16)

    y = _merge(o_a.reshape(b * t, A_Q_W), o_b.reshape(b * t, B_V_W), o_c.reshape(b * t, C_V_W), gates,
               w_br.astype(bf16), w_o.astype(bf16))
    return y.reshape(b, t, dm)


def kernel(x, c, ctx, c_ctx, w_mod, b_mod, norm_g, w_in, attn_sink, diff_lambda, diff_norm_w,
           hgrn_lb_logits, hgrn_norm_w, w_branch, w_out, w_router, b_router, w_gate_up, b_gate_up,
           w_down, b_down):
    b, s, dm = x.shape
    l = ctx.shape[1]
    tabs = _rope_tables(s // GRID_W)
    lb_cum = jnp.cumsum(jax.nn.softmax(hgrn_lb_logits.astype(jnp.float32), axis=0), axis=0)
    lower_bounds = lb_cum - lb_cum[0:1]
    cond_rows = 16
    cond = jnp.zeros((cond_rows, dm), jnp.float32).at[:b].set(jax.nn.silu(c)).at[b].set(jax.nn.silu(c_ctx))
    xc = ctx
    for layer in range(DEPTH):
        last = layer == DEPTH - 1
        lam_init = 0.8 - 0.6 * math.exp(-0.3 * layer)
        mod_all = _matmul(cond, w_mod[layer], tm=cond_rows, tn=3 * dm, out_dtype=jnp.float32,
                          precision=lax.Precision.HIGHEST, name="adaln") + b_mod[layer]
        mod = mod_all[:b].reshape(b, 6, dm)
        mod_c = mod_all[b:b + 1].reshape(1, 6, dm)
        g = norm_g[layer]
        h = _modulate(x, g[0], mod[:, 0], mod[:, 1])
        hc = _modulate(xc, g[0], mod_c[:, 0], mod_c[:, 1])
        tok = jnp.concatenate([h, hc], axis=1).astype(jnp.bfloat16)
        y = _hybrid_mixer(tok, s, tabs, lower_bounds[layer], w_in[layer], attn_sink[layer], diff_lambda[layer],
                          lam_init, diff_norm_w[layer], hgrn_norm_w[layer], w_branch[layer], w_out[layer])
        x = x + mod[:, 2][:, None, :] * _rms_norm(y[:, :s], g[1])
        h = _modulate(x, g[2], mod[:, 3], mod[:, 4])
        moe_w = (w_router[layer], b_router[layer], w_gate_up[layer], b_gate_up[layer], w_down[layer], b_down[layer])
        if last:
            f = _moe_ffn(h.reshape(b * s, dm), *moe_w).reshape(b, s, dm)
        else:
            xc = xc + mod_c[:, 2][:, None, :] * _rms_norm(y[:, s:], g[1])
            hc = _modulate(xc, g[2], mod_c[:, 3], mod_c[:, 4])
            rows = jnp.concatenate([h.reshape(b * s, dm), hc.reshape(b * l, dm)], axis=0)
            ft = _moe_ffn(rows, *moe_w)
            f = ft[:b * s].reshape(b, s, dm)
            fc = ft[b * s:].reshape(b, l, dm)
            xc = xc + mod_c[:, 5][:, None, :] * _rms_norm(fc, g[3])
        x = x + mod[:, 5][:, None, :] * _rms_norm(f, g[3])
    return x
```

```python
import functools
import math

import jax
import jax.numpy as jnp
from jax import lax
from jax.experimental import pallas as pl
from jax.experimental.pallas import tpu as pltpu

D_MODEL = 1024
DEPTH = 2
GRID_W = 64
HEAD_DIM = 64
ROPE_THETA = 10000.0
RMS_EPS = 1e-6
NEG_INF = -1e30
BLOCK = 128
A_HQ = 8
A_HKV = 2
A_GROUP = A_HQ // A_HKV
WINDOW = 128
B_H = 4
B_HD = HEAD_DIM
B_VD = 2 * HEAD_DIM
C_H = 4
C_DK = 128
C_DV = 128
N_BRANCH = 3
BRANCH_W = 512
A_Q_W = A_HQ * HEAD_DIM
A_KV_W = A_HKV * HEAD_DIM
B_QK_W = B_H * 2 * B_HD
B_V_W = B_H * B_VD
C_K_W = C_H * C_DK
C_V_W = C_H * C_DV
GATE_W = N_BRANCH * D_MODEL
MIX_W = A_Q_W + 2 * A_KV_W + 2 * B_QK_W + B_V_W + 3 * C_K_W + 2 * C_V_W
N_EXPERTS = 32
TOP_K = 4
EXPERT_FF = D_MODEL
SWIGLU_LIMIT = 7.0
SWIGLU_ALPHA = 1.702

OFF_AQ = 0
OFF_AK = OFF_AQ + A_Q_W
OFF_AV = OFF_AK + A_KV_W
OFF_BQ = OFF_AV + A_KV_W
OFF_BK = OFF_BQ + B_QK_W
OFF_BV = OFF_BK + B_QK_W
OFF_CQ = OFF_BV + B_V_W
OFF_CI = OFF_CQ + C_K_W
OFF_CFF = OFF_CI + C_V_W
OFF_CFB = OFF_CFF + C_K_W
OFF_CG = OFF_CFB + C_K_W

V7X_VMEM_LIMIT_BYTES = 56 * 1024 * 1024
LANES = 128

MM_TM = 512
FLASH_TQ = 512
FLASH_TK = 768
HGRN_TB = 256
HGRN_SUB = 16
MOE_TM = 512


def _cparams(sem):
    return pltpu.CompilerParams(dimension_semantics=sem, vmem_limit_bytes=V7X_VMEM_LIMIT_BYTES)


def _mm_kernel(x_ref, w_ref, o_ref, *, precision):
    o_ref[...] = jnp.dot(x_ref[...], w_ref[...], preferred_element_type=jnp.float32,
                         precision=precision).astype(o_ref.dtype)


def _matmul(x, w, *, tm, tn, out_dtype, precision=None, name):
    m, k = x.shape
    n = w.shape[1]
    assert m % tm == 0 and n % tn == 0, (m, tm, n, tn)
    return pl.pallas_call(
        functools.partial(_mm_kernel, precision=precision),
        grid=(n // tn, m // tm),
        in_specs=[pl.BlockSpec((tm, k), lambda j, i: (i, 0)),
                  pl.BlockSpec((k, tn), lambda j, i: (0, j))],
        out_specs=pl.BlockSpec((tm, tn), lambda j, i: (i, j)),
        out_shape=jax.ShapeDtypeStruct((m, n), out_dtype),
        compiler_params=_cparams(("arbitrary", "arbitrary")),
        name=name,
    )(x, w)


def _sink_attend(qh, kh, vh, mask, sk):
    s = lax.dot_general(qh, kh, (((1,), (1,)), ((), ())), preferred_element_type=jnp.float32)
    if mask is not None:
        s = jnp.where(mask, s, NEG_INF)
    m = jnp.maximum(jnp.max(s, axis=-1, keepdims=True), sk)
    p = jnp.exp(s - m)
    l = jnp.sum(p, axis=-1, keepdims=True) + jnp.exp(sk - m)
    o = jnp.dot(p.astype(vh.dtype), vh, preferred_element_type=jnp.float32)
    return o / l


def _attn_a_kernel(sink_ref, q_ref, kp_ref, kc_ref, kn_ref, vp_ref, vc_ref, vn_ref, kx_ref, vx_ref,
                   o_ref, *, seq):
    n = pl.program_id(1)
    q = q_ref[0]
    k_all = jnp.concatenate([kp_ref[0], kc_ref[0], kn_ref[0], kx_ref[0]], axis=0)
    v_all = jnp.concatenate([vp_ref[0], vc_ref[0], vn_ref[0], vx_ref[0]], axis=0)
    n_keys = k_all.shape[0]
    qi = lax.broadcasted_iota(jnp.int32, (BLOCK, n_keys), 0)
    kj = lax.broadcasted_iota(jnp.int32, (BLOCK, n_keys), 1)
    kpos = n * BLOCK + kj - BLOCK
    local_ok = (jnp.abs(kj - BLOCK - qi) <= WINDOW) & (kpos >= 0) & (kpos < seq)
    mask = local_ok | (kj >= 3 * BLOCK)
    outs = []
    for hq in range(A_HQ):
        hk = hq // A_GROUP
        outs.append(_sink_attend(q[:, hq * HEAD_DIM:(hq + 1) * HEAD_DIM],
                                 k_all[:, hk * HEAD_DIM:(hk + 1) * HEAD_DIM],
                                 v_all[:, hk * HEAD_DIM:(hk + 1) * HEAD_DIM], mask, sink_ref[hq]))
    o_ref[0] = jnp.concatenate(outs, axis=1).astype(o_ref.dtype)


def _attn_a_ctx_kernel(sink_ref, q_ref, kx_ref, vx_ref, o_ref):
    q = q_ref[0]
    k_all = kx_ref[0]
    v_all = vx_ref[0]
    outs = []
    for hq in range(A_HQ):
        hk = hq // A_GROUP
        outs.append(_sink_attend(q[:, hq * HEAD_DIM:(hq + 1) * HEAD_DIM],
                                 k_all[:, hk * HEAD_DIM:(hk + 1) * HEAD_DIM],
                                 v_all[:, hk * HEAD_DIM:(hk + 1) * HEAD_DIM], None, sink_ref[hq]))
    o_ref[0] = jnp.concatenate(outs, axis=1).astype(o_ref.dtype)


def _window_attention(q, k, v, kx, vx, sink):
    b, s, _ = q.shape
    l = kx.shape[1]
    nb = s // BLOCK
    prev = lambda bi, n, *_: (bi, jnp.maximum(n - 1, 0), 0)
    cur = lambda bi, n, *_: (bi, n, 0)
    nxt = lambda bi, n, *_: (bi, jnp.minimum(n + 1, nb - 1), 0)
    whole = lambda bi, n, *_: (bi, 0, 0)
    kv_blk = (1, BLOCK, A_KV_W)
    return pl.pallas_call(
        functools.partial(_attn_a_kernel, seq=s),
        grid_spec=pltpu.PrefetchScalarGridSpec(
            num_scalar_prefetch=1, grid=(b, nb),
            in_specs=[pl.BlockSpec((1, BLOCK, A_Q_W), cur),
                      pl.BlockSpec(kv_blk, prev), pl.BlockSpec(kv_blk, cur), pl.BlockSpec(kv_blk, nxt),
                      pl.BlockSpec(kv_blk, prev), pl.BlockSpec(kv_blk, cur), pl.BlockSpec(kv_blk, nxt),
                      pl.BlockSpec((1, l, A_KV_W), whole), pl.BlockSpec((1, l, A_KV_W), whole)],
            out_specs=pl.BlockSpec((1, BLOCK, A_Q_W), cur)),
        out_shape=jax.ShapeDtypeStruct((b, s, A_Q_W), jnp.bfloat16),
        compiler_params=_cparams(("arbitrary", "arbitrary")),
        name="window_gqa",
    )(sink, q, k, k, k, v, v, v, kx, vx)


def _context_attention(q, kx, vx, sink):
    b, l, _ = q.shape
    whole = lambda bi, *_: (bi, 0, 0)
    return pl.pallas_call(
        _attn_a_ctx_kernel,
        grid_spec=pltpu.PrefetchScalarGridSpec(
            num_scalar_prefetch=1, grid=(b,),
            in_specs=[pl.BlockSpec((1, l, A_Q_W), whole),
                      pl.BlockSpec((1, l, A_KV_W), whole), pl.BlockSpec((1, l, A_KV_W), whole)],
            out_specs=pl.BlockSpec((1, l, A_Q_W), whole)),
        out_shape=jax.ShapeDtypeStruct((b, l, A_Q_W), jnp.bfloat16),
        compiler_params=_cparams(("arbitrary",)),
        name="context_gqa",
    )(sink, q, kx, vx)


def _diff_flash_kernel(lam_ref, q_ref, k_ref, v_ref, o_ref, q2_sc, sa_sc, sb_sc, m_sc, acc_sc, *, tq, tk, nk):
    j = pl.program_id(3)

    @pl.when(j == 0)
    def _():
        q = q_ref[0]
        lane = lax.broadcasted_iota(jnp.int32, q.shape, 1)
        q2_sc[:tq, :] = jnp.where(lane < B_HD, q, jnp.zeros_like(q))
        q2_sc[tq:, :] = jnp.where(lane >= B_HD, q, jnp.zeros_like(q))
        m_sc[...] = jnp.full_like(m_sc, NEG_INF)
        acc_sc[...] = jnp.zeros_like(acc_sc)
        sb_sc[...] = jnp.full_like(sb_sc, NEG_INF)

    def step(s_new_ref, s_old_ref):
        m_prev = m_sc[...]
        m_new = jnp.maximum(m_prev, jnp.max(s_old_ref[...], axis=-1, keepdims=True))
        alpha = jnp.exp2(m_prev - m_new)
        p = jnp.exp2(s_old_ref[...] - jnp.tile(m_new, (1, tk // LANES))).astype(jnp.bfloat16)
        acc_sc[...] = jnp.tile(alpha, (1, 2)) * acc_sc[...] + jnp.dot(
            p, v_ref[0, 0], preferred_element_type=jnp.float32)
        m_sc[...] = m_new
        s_new_ref[...] = lax.dot_general(q2_sc[...], k_ref[0, 0], (((1,), (1,)), ((), ())),
                                         preferred_element_type=jnp.float32)

    @pl.when(j % 2 == 0)
    def _():
        step(sa_sc, sb_sc)

    @pl.when(j % 2 == 1)
    def _():
        step(sb_sc, sa_sc)

    @pl.when(j == nk)
    def _():
        acc = acc_sc[...]
        o = acc[:, :B_VD] / acc[:, B_VD:]
        o_ref[0] = (o[:tq, :] - lam_ref[0] * o[tq:, :]).astype(o_ref.dtype)


def _diff_attention(lam, q, keys, vals1, *, tq, tk, k_block_offset, nk):
    b, t_q, _ = q.shape
    return pl.pallas_call(
        functools.partial(_diff_flash_kernel, tq=tq, tk=tk, nk=nk),
        grid_spec=pltpu.PrefetchScalarGridSpec(
            num_scalar_prefetch=1, grid=(b, B_H, t_q // tq, nk + 1),
            in_specs=[pl.BlockSpec((1, tq, 2 * B_HD), lambda bi, h, i, j, *_: (bi, i, h)),
                      pl.BlockSpec((1, 1, tk, 2 * B_HD),
                                   lambda bi, h, i, j, *_: (bi, h, jnp.minimum(j, nk - 1) + k_block_offset, 0)),
                      pl.BlockSpec((1, 1, tk, 2 * B_VD),
                                   lambda bi, h, i, j, *_: (bi, h, jnp.maximum(j - 1, 0) + k_block_offset, 0))],
            out_specs=pl.BlockSpec((1, tq, B_VD), lambda bi, h, i, j, *_: (bi, i, h)),
            scratch_shapes=[pltpu.VMEM((2 * tq, 2 * B_HD), jnp.bfloat16),
                            pltpu.VMEM((2 * tq, tk), jnp.float32),
                            pltpu.VMEM((2 * tq, tk), jnp.float32),
                            pltpu.VMEM((2 * tq, LANES), jnp.float32),
                            pltpu.VMEM((2 * tq, 2 * B_VD), jnp.float32)]),
        out_shape=jax.ShapeDtypeStruct((b, t_q, B_V_W), jnp.float32),
        compiler_params=_cparams(("arbitrary", "arbitrary", "arbitrary", "arbitrary")),
        name="diff_flash",
    )(lam, q, keys, vals1)


def _hgrn_kernel(q_ref, v_ref, lf_ref, o_ref, st_ref, *, tb):
    d = pl.program_id(1)
    t = pl.program_id(2)
    nsub = tb // HGRN_SUB

    @pl.when(t == 0)
    def _():
        st_ref[...] = jnp.zeros_like(st_ref)

    sgn = 1 - 2 * d
    rr = lax.broadcasted_iota(jnp.int32, (HGRN_SUB, HGRN_SUB), 0)
    cc = lax.broadcasted_iota(jnp.int32, (HGRN_SUB, HGRN_SUB), 1)
    tri = (((cc - rr) * sgn) <= 0).astype(jnp.bfloat16)
    srow = lax.broadcasted_iota(jnp.int32, (HGRN_SUB, C_DK), 0)

    def body(i, carry):
        sc = jnp.where(d == 0, i, nsub - 1 - i)
        r0 = pl.multiple_of(sc * HGRN_SUB, HGRN_SUB)
        lf = lf_ref[0, 0, pl.ds(r0, HGRN_SUB), :]
        lf_a = lf.astype(jnp.bfloat16)
        rem = lf - lf_a.astype(jnp.float32)
        lf_b = rem.astype(jnp.bfloat16)
        lf_c = (rem - lf_b.astype(jnp.float32)).astype(jnp.bfloat16)
        bl = (jnp.dot(tri, lf_a, preferred_element_type=jnp.float32)
              + jnp.dot(tri, lf_b, preferred_element_type=jnp.float32)
              + jnp.dot(tri, lf_c, preferred_element_type=jnp.float32))
        q = q_ref[0, pl.ds(r0, HGRN_SUB), :]
        v = v_ref[0, pl.ds(r0, HGRN_SUB), :]
        for hh in range(C_H):
            sl = slice(hh * C_DK, (hh + 1) * C_DK)
            lfh, blh, qh, vh = lf[:, sl], bl[:, sl], q[:, sl], v[:, sl]
            tot = jnp.sum(lfh, axis=0, keepdims=True)
            kh = 1.0 - jnp.exp(lfh)
            qt = qh * jnp.exp(blh)
            kt = kh * jnp.exp(tot - blh)
            st = st_ref[hh]
            o_sub = lax.dot_general(qt.astype(jnp.bfloat16), st.astype(jnp.bfloat16),
                                    (((1,), (1,)), ((), ())), preferred_element_type=jnp.float32)
            for tt in range(HGRN_SUB):
                ok = ((srow - tt) * sgn) <= 0
                e = jnp.exp(jnp.where(ok, blh[tt:tt + 1, :] - blh, NEG_INF))
                att = jnp.sum((qh[tt:tt + 1, :] * kh) * e, axis=1, keepdims=True)
                row = jnp.sum(att * vh, axis=0, keepdims=True)
                o_sub = o_sub + jnp.where(srow == tt, row, 0.0)
            o_ref[0, 0, pl.ds(r0, HGRN_SUB), sl] = o_sub
            upd = lax.dot_general(vh.astype(jnp.bfloat16), kt.astype(jnp.bfloat16),
                                  (((0,), (0,)), ((), ())), preferred_element_type=jnp.float32)
            st_ref[hh] = jnp.exp(tot) * st + upd
        return carry

    lax.fori_loop(0, nsub, body, 0)


def _hgrn_scan(q, v, lf, *, n_ctx_blocks):
    b, t, w = q.shape
    tb = HGRN_TB
    nblk = t // tb
    nlat = nblk - n_ctx_blocks

    def row_block(di, ti):
        ctx_blk = jnp.where(di == 0, nlat + ti, nblk - 1 - ti)
        lat_blk = jnp.where(di == 0, ti - n_ctx_blocks, nblk - 1 - ti)
        return jnp.where(ti < n_ctx_blocks, ctx_blk, lat_blk)

    return pl.pallas_call(
        functools.partial(_hgrn_kernel, tb=tb),
        grid=(b, 2, nblk),
        in_specs=[pl.BlockSpec((1, tb, w), lambda bi, di, ti: (bi, row_block(di, ti), 0)),
                  pl.BlockSpec((1, tb, w), lambda bi, di, ti: (bi, row_block(di, ti), 0)),
                  pl.BlockSpec((1, 1, tb, w), lambda bi, di, ti: (di, bi, row_block(di, ti), 0))],
        out_specs=pl.BlockSpec((1, 1, tb, w), lambda bi, di, ti: (di, bi, row_block(di, ti), 0)),
        out_shape=jax.ShapeDtypeStruct((2, b, t, w), jnp.float32),
        scratch_shapes=[pltpu.VMEM((C_H, C_DV, C_DK), jnp.float32)],
        compiler_params=_cparams(("arbitrary", "arbitrary", "arbitrary")),
        name="hgrn2_scan",
    )(q, v, lf)


def _merge_kernel(oa_ref, ob_ref, oc_ref, g_ref, wbr_ref, wo_ref, y_ref):
    acc = None
    for i, o_ref in enumerate((oa_ref, ob_ref, oc_ref)):
        z = jnp.dot(o_ref[...], wbr_ref[i], preferred_element_type=jnp.float32)
        g = jax.nn.sigmoid(g_ref[:, i * D_MODEL:(i + 1) * D_MODEL].astype(jnp.float32))
        acc = g * z if acc is None else acc + g * z
    y_ref[...] = jnp.dot(acc.astype(jnp.bfloat16), wo_ref[...], preferred_element_type=jnp.float32)


def _merge(oa, ob, oc, gates, w_br, w_o):
    m = oa.shape[0]
    tm = MM_TM
    row = lambda i: (i, 0)
    return pl.pallas_call(
        _merge_kernel,
        grid=(m // tm,),
        in_specs=[pl.BlockSpec((tm, BRANCH_W), row), pl.BlockSpec((tm, BRANCH_W), row),
                  pl.BlockSpec((tm, BRANCH_W), row), pl.BlockSpec((tm, GATE_W), row),
                  pl.BlockSpec((N_BRANCH, BRANCH_W, D_MODEL), lambda i: (0, 0, 0)),
                  pl.BlockSpec((D_MODEL, D_MODEL), lambda i: (0, 0))],
        out_specs=pl.BlockSpec((tm, D_MODEL), row),
        out_shape=jax.ShapeDtypeStruct((m, D_MODEL), jnp.float32),
        compiler_params=_cparams(("arbitrary",)),
        name="branch_merge",
    )(oa, ob, oc, gates, w_br, w_o)


def _moe_kernel(blk_e_ref, n_used_ref, x_ref, wgu_ref, bgu_ref, wdn_ref, bdn_ref, y_ref, wgu_sc, wdn_sc):
    i = pl.program_id(0)
    e = blk_e_ref[i]
    e_prev = blk_e_ref[jnp.maximum(i - 1, 0)]

    @pl.when((i == 0) | (e != e_prev))
    def _():
        wgu_sc[...] = wgu_ref[0].astype(jnp.bfloat16)
        wdn_sc[...] = wdn_ref[0].astype(jnp.bfloat16)

    @pl.when(i < n_used_ref[0])
    def _():
        gu = jnp.dot(x_ref[...], wgu_sc[...], preferred_element_type=jnp.float32) + bgu_ref[0]
        g = jnp.minimum(gu[:, :EXPERT_FF], SWIGLU_LIMIT)
        u = jnp.clip(gu[:, EXPERT_FF:], -SWIGLU_LIMIT, SWIGLU_LIMIT)
        act = (u + 1.0) * (g * jax.nn.sigmoid(SWIGLU_ALPHA * g))
        y = jnp.dot(act.astype(jnp.bfloat16), wdn_sc[...], preferred_element_type=jnp.float32) + bdn_ref[0]
        y_ref[...] = y.astype(y_ref.dtype)

    @pl.when(i >= n_used_ref[0])
    def _():
        y_ref[...] = jnp.zeros_like(y_ref)


def _moe_experts(blk_e, n_used, xs, w_gu, b_gu, w_dn, b_dn):
    r, dm = xs.shape
    tm = MOE_TM
    return pl.pallas_call(
        _moe_kernel,
        grid_spec=pltpu.PrefetchScalarGridSpec(
            num_scalar_prefetch=2, grid=(r // tm,),
            in_specs=[pl.BlockSpec((tm, dm), lambda i, be, nu: (i, 0)),
                      pl.BlockSpec((1, dm, 2 * EXPERT_FF), lambda i, be, nu: (be[i], 0, 0)),
                      pl.BlockSpec((1, 1, 2 * EXPERT_FF), lambda i, be, nu: (be[i], 0, 0)),
                      pl.BlockSpec((1, EXPERT_FF, dm), lambda i, be, nu: (be[i], 0, 0)),
                      pl.BlockSpec((1, 1, dm), lambda i, be, nu: (be[i], 0, 0))],
            out_specs=pl.BlockSpec((tm, dm), lambda i, be, nu: (i, 0)),
            scratch_shapes=[pltpu.VMEM((dm, 2 * EXPERT_FF), jnp.bfloat16),
                            pltpu.VMEM((EXPERT_FF, dm), jnp.bfloat16)]),
        out_shape=jax.ShapeDtypeStruct((r, dm), jnp.bfloat16),
        compiler_params=_cparams(("arbitrary",)),
        name="moe_experts",
    )(blk_e, n_used, xs, w_gu, b_gu.reshape(N_EXPERTS, 1, -1), w_dn, b_dn.reshape(N_EXPERTS, 1, -1))


def _moe_ffn(h, w_router, b_router, w_gu, b_gu, w_dn, b_dn):
    n, dm = h.shape
    nk = n * TOP_K
    w_r = jnp.zeros((dm, LANES), jnp.float32).at[:, :N_EXPERTS].set(w_router)
    logits = _matmul(h, w_r, tm=MM_TM, tn=LANES, out_dtype=jnp.float32,
                     precision=lax.Precision.HIGHEST, name="router")[:, :N_EXPERTS] + b_router
    top_v, top_e = lax.top_k(logits, TOP_K)
    gates = jax.nn.softmax(top_v, axis=-1)
    flat_e = top_e.reshape(-1).astype(jnp.int32)
    onehot = (flat_e[:, None] == jnp.arange(N_EXPERTS, dtype=jnp.int32)[None, :]).astype(jnp.int32)
    incl = jnp.cumsum(onehot, axis=0)
    rank = jnp.sum(incl * onehot, axis=1) - 1
    counts = incl[-1]
    padded = (counts + MOE_TM - 1) // MOE_TM * MOE_TM
    pad_end = jnp.cumsum(padded)
    pad_start = pad_end - padded
    grp_start = jnp.cumsum(counts) - counts
    dest = jnp.sum(onehot * pad_start[None, :], axis=1) + rank
    n_blk = (nk + MOE_TM - 1) // MOE_TM + N_EXPERTS
    rows = n_blk * MOE_TM
    blk_e = jnp.clip(jnp.searchsorted(pad_end, jnp.arange(n_blk, dtype=jnp.int32) * MOE_TM, side='right'),
                     0, N_EXPERTS - 1).astype(jnp.int32)
    n_used = (pad_end[-1] // MOE_TM).astype(jnp.int32).reshape(1)
    order = jnp.argsort(flat_e)
    row_e = jnp.repeat(blk_e, MOE_TM)
    sorted_pos = jnp.arange(rows, dtype=jnp.int32) - pad_start[row_e] + grp_start[row_e]
    row_tok = order[jnp.clip(sorted_pos, 0, nk - 1)] // TOP_K
    xs = lax.optimization_barrier(h.astype(jnp.bfloat16))[row_tok]
    xs = lax.optimization_barrier(xs)
    y = _moe_experts(blk_e, n_used, xs, w_gu, b_gu, w_dn, b_dn)
    yk = lax.optimization_barrier(y[dest]).reshape(n, TOP_K, dm).astype(jnp.float32)
    return jnp.sum(yk * gates[..., None], axis=1)


def _rms_norm(x, g):
    xf = x.astype(jnp.float32)
    y = xf * lax.rsqrt(jnp.mean(xf * xf, axis=-1, keepdims=True) + RMS_EPS)
    return y * g.astype(jnp.float32)


def _modulate(x, g, shift, scale):
    return _rms_norm(x, g) * (1 + scale[:, None, :]) + shift[:, None, :]


def _rope_tables(rows):
    row = jnp.repeat(jnp.arange(rows, dtype=jnp.float32), GRID_W)
    col = jnp.tile(jnp.arange(GRID_W, dtype=jnp.float32), rows)
    n_freq = HEAD_DIM // 4
    inv = ROPE_THETA ** (-jnp.arange(n_freq, dtype=jnp.float32) / n_freq)
    ang_r = row[:, None] * inv
    ang_c = col[:, None] * inv
    return jnp.cos(ang_r), jnp.sin(ang_r), jnp.cos(ang_c), jnp.sin(ang_c)


def _rope(x, tabs):
    cos_r, sin_r, cos_c, sin_c = tabs
    n = HEAD_DIM // 4
    half = HEAD_DIM // 2
    shape = (1, x.shape[1]) + (1,) * (x.ndim - 3) + (n,)

    def rot(xa, cos, sin):
        cos = cos.reshape(shape)
        sin = sin.reshape(shape)
        x1, x2 = xa[..., :n], xa[..., n:]
        return jnp.concatenate([x1 * cos - x2 * sin, x2 * cos + x1 * sin], axis=-1)

    return jnp.concatenate([rot(x[..., :half], cos_r, sin_r), rot(x[..., half:], cos_c, sin_c)], axis=-1)


def _log_forget(z, lb):
    lbf = lb.astype(jnp.float32)
    return jnp.logaddexp(jnp.log(lbf), jnp.log1p(-lbf) + jax.nn.log_sigmoid(z.astype(jnp.float32)))


def _hybrid_mixer(tok, seq, tabs, lb, w_in, sink, lam_par, lam_init, diff_w, hgrn_w, w_br, w_o):
    b, t, dm = tok.shape
    l = t - seq
    f32 = jnp.float32
    bf16 = jnp.bfloat16
    flat = tok.reshape(b * t, dm)
    w_in_bf = w_in.astype(bf16)
    proj = _matmul(flat, w_in_bf[:, :MIX_W], tm=MM_TM, tn=MIX_W // 2, out_dtype=bf16,
                   name="in_proj").reshape(b, t, MIX_W)
    gates = _matmul(flat, w_in_bf[:, MIX_W:], tm=MM_TM, tn=GATE_W, out_dtype=bf16, name="gate_proj")

    def cols(off, width, rows=slice(None)):
        return proj[:, rows, off:off + width].astype(f32)

    lat = slice(0, seq)
    cx = slice(seq, t)
    scale = HEAD_DIM ** -0.5

    aq = (_rope(cols(OFF_AQ, A_Q_W, lat).reshape(b, seq, A_HQ, HEAD_DIM), tabs) * scale)
    aq = aq.reshape(b, seq, A_Q_W).astype(bf16)
    ak = _rope(cols(OFF_AK, A_KV_W, lat).reshape(b, seq, A_HKV, HEAD_DIM), tabs).reshape(b, seq, A_KV_W).astype(bf16)
    av = proj[:, lat, OFF_AV:OFF_AV + A_KV_W]
    akx = proj[:, cx, OFF_AK:OFF_AK + A_KV_W]
    avx = proj[:, cx, OFF_AV:OFF_AV + A_KV_W]
    sink32 = sink.astype(f32)
    o_a = _window_attention(aq, ak, av, akx, avx, sink32)
    aqx = (cols(OFF_AQ, A_Q_W, cx) * scale).astype(bf16)
    o_ax = _context_attention(aqx, akx, avx, sink32)
    o_a = jnp.concatenate([o_a, o_ax], axis=1)

    lp = lam_par.astype(f32)
    lam = jnp.exp(jnp.sum(lp[0] * lp[1])) - jnp.exp(jnp.sum(lp[2] * lp[3])) + lam_init
    lam = lam.reshape(1).astype(f32)
    scale2 = scale * math.log2(math.e)
    bq = (_rope(cols(OFF_BQ, B_QK_W, lat).reshape(b, seq, B_H, 2, B_HD), tabs) * scale2)
    bq = bq.reshape(b, seq, B_QK_W).astype(bf16)
    bk = _rope(cols(OFF_BK, B_QK_W, lat).reshape(b, seq, B_H, 2, B_HD), tabs).reshape(b, seq, B_QK_W).astype(bf16)
    keys = jnp.concatenate([bk, proj[:, cx, OFF_BK:OFF_BK + B_QK_W]], axis=1)
    keys = keys.reshape(b, t, B_H, 2 * B_HD).transpose(0, 2, 1, 3)
    vals = proj[:, :, OFF_BV:OFF_BV + B_V_W].reshape(b, t, B_H, B_VD).transpose(0, 2, 1, 3)
    vals1 = jnp.concatenate([vals, jnp.ones_like(vals)], axis=-1)
    o_b = _diff_attention(lam, bq, keys, vals1, tq=FLASH_TQ, tk=FLASH_TK, k_block_offset=0, nk=t // FLASH_TK)
    bqx = (cols(OFF_BQ, B_QK_W, cx) * scale2).astype(bf16)
    o_bx = _diff_attention(lam, bqx, keys, vals1, tq=l, tk=l, k_block_offset=seq // l, nk=1)
    o_b = jnp.concatenate([o_b, o_bx], axis=1).reshape(b, t, B_H, B_VD)
    o_b = (_rms_norm(o_b, diff_w) * (1 - lam_init)).reshape(b, t, B_V_W).astype(bf16)

    qc = jax.nn.silu(cols(OFF_CQ, C_K_W))
    ic = cols(OFF_CI, C_V_W)
    lf = jnp.stack([_log_forget(cols(OFF_CFF, C_K_W), lb[0]), _log_forget(cols(OFF_CFB, C_K_W), lb[1])])
    o2 = _hgrn_scan(qc, ic, lf, n_ctx_blocks=l // HGRN_TB)
    o_c = (o2[0] + o2[1]).reshape(b, t, C_H, C_DV)
    o_c = _rms_norm(o_c, hgrn_w) * jax.nn.silu(cols(OFF_CG, C_V_W).reshape(b, t, C_H, C_DV))
    o_c = o_c.reshape(b, t, C_V_W).astype(bf16)

    y = _merge(o_a.reshape(b * t, A_Q_W), o_b.reshape(b * t, B_V_W), o_c.reshape(b * t, C_V_W), gates,
               w_br.astype(bf16), w_o.astype(bf16))
    return y.reshape(b, t, dm)


def kernel(x, c, ctx, c_ctx, w_mod, b_mod, norm_g, w_in, attn_sink, diff_lambda, diff_norm_w,
           hgrn_lb_logits, hgrn_norm_w, w_branch, w_out, w_router, b_router, w_gate_up, b_gate_up,
           w_down, b_down):
    b, s, dm = x.shape
    l = ctx.shape[1]
    tabs = _rope_tables(s // GRID_W)
    lb_cum = jnp.cumsum(jax.nn.softmax(hgrn_lb_logits.astype(jnp.float32), axis=0), axis=0)
    lower_bounds = lb_cum - lb_cum[0:1]
    cond_rows = 16
    cond = jnp.zeros((cond_rows, dm), jnp.float32).at[:b].set(jax.nn.silu(c)).at[b].set(jax.nn.silu(c_ctx))
    xc = ctx
    for layer in range(DEPTH):
        last = layer == DEPTH - 1
        lam_init = 0.8 - 0.6 * math.exp(-0.3 * layer)
        mod_all = _matmul(cond, w_mod[layer], tm=cond_rows, tn=3 * dm, out_dtype=jnp.float32,
                          precision=lax.Precision.HIGHEST, name="adaln") + b_mod[layer]
        mod = mod_all[:b].reshape(b, 6, dm)
        mod_c = mod_all[b:b + 1].reshape(1, 6, dm)
        g = norm_g[layer]
        h = _modulate(x, g[0], mod[:, 0], mod[:, 1])
        hc = _modulate(xc, g[0], mod_c[:, 0], mod_c[:, 1])
        tok = jnp.concatenate([h, hc], axis=1).astype(jnp.bfloat16)
        y = _hybrid_mixer(tok, s, tabs, lower_bounds[layer], w_in[layer], attn_sink[layer], diff_lambda[layer],
                          lam_init, diff_norm_w[layer], hgrn_norm_w[layer], w_branch[layer], w_out[layer])
        x = x + mod[:, 2][:, None, :] * _rms_norm(y[:, :s], g[1])
        h = _modulate(x, g[2], mod[:, 3], mod[:, 4])
        moe_w = (w_router[layer], b_router[layer], w_gate_up[layer], b_gate_up[layer], w_down[layer], b_down[layer])
        if last:
            f = _moe_ffn(h.reshape(b * s, dm), *moe_w).reshape(b, s, dm)
        else:
            xc = xc + mod_c[:, 2][:, None, :] * _rms_norm(y[:, s:], g[1])
            hc = _modulate(xc, g[2], mod_c[:, 3], mod_c[:, 4])
            rows = jnp.concatenate([h.reshape(b * s, dm), hc.reshape(b * l, dm)], axis=0)
            ft = _moe_ffn(rows, *moe_w)
            f = ft[:b * s].reshape(b, s, dm)
            fc = ft[b * s:].reshape(b, l, dm)
            xc = xc + mod_c[:, 5][:, None, :] * _rms_norm(fc, g[3])
        x = x + mod[:, 5][:, None, :] * _rms_norm(f, g[3])
    return x
```

```python
import functools
import math

import jax
import jax.numpy as jnp
from jax import lax
from jax.experimental import pallas as pl
from jax.experimental.pallas import tpu as pltpu

D_MODEL = 1024
DEPTH = 2
GRID_W = 64
HEAD_DIM = 64
ROPE_THETA = 10000.0
RMS_EPS = 1e-6
NEG_INF = -1e30
BLOCK = 128
A_HQ = 8
A_HKV = 2
A_GROUP = A_HQ // A_HKV
WINDOW = 128
B_H = 4
B_HD = HEAD_DIM
B_VD = 2 * HEAD_DIM
C_H = 4
C_DK = 128
C_DV = 128
N_BRANCH = 3
BRANCH_W = 512
A_Q_W = A_HQ * HEAD_DIM
A_KV_W = A_HKV * HEAD_DIM
B_QK_W = B_H * 2 * B_HD
B_V_W = B_H * B_VD
C_K_W = C_H * C_DK
C_V_W = C_H * C_DV
GATE_W = N_BRANCH * D_MODEL
MIX_W = A_Q_W + 2 * A_KV_W + 2 * B_QK_W + B_V_W + 3 * C_K_W + 2 * C_V_W
N_EXPERTS = 32
TOP_K = 4
EXPERT_FF = D_MODEL
SWIGLU_LIMIT = 7.0
SWIGLU_ALPHA = 1.702

OFF_AQ = 0
OFF_AK = OFF_AQ + A_Q_W
OFF_AV = OFF_AK + A_KV_W
OFF_BQ = OFF_AV + A_KV_W
OFF_BK = OFF_BQ + B_QK_W
OFF_BV = OFF_BK + B_QK_W
OFF_CQ = OFF_BV + B_V_W
OFF_CI = OFF_CQ + C_K_W
OFF_CFF = OFF_CI + C_V_W
OFF_CFB = OFF_CFF + C_K_W
OFF_CG = OFF_CFB + C_K_W

V7X_VMEM_LIMIT_BYTES = 56 * 1024 * 1024
LANES = 128
LOG2_E = math.log2(math.e)

MM_TM = 512
FLASH_TQ = 512
FLASH_TK = 768
HGRN_TB = 256
HGRN_SUB = 16
HGRN_UNROLL = 4
MOE_TM = 512


def _cparams(sem):
    return pltpu.CompilerParams(dimension_semantics=sem, vmem_limit_bytes=V7X_VMEM_LIMIT_BYTES)


def _mm_kernel(x_ref, w_ref, o_ref, *, precision):
    o_ref[...] = jnp.dot(x_ref[...], w_ref[...], preferred_element_type=jnp.float32,
                         precision=precision).astype(o_ref.dtype)


def _matmul(x, w, *, tm, tn, out_dtype, precision=None, name):
    m, k = x.shape
    n = w.shape[1]
    assert m % tm == 0 and n % tn == 0, (m, tm, n, tn)
    return pl.pallas_call(
        functools.partial(_mm_kernel, precision=precision),
        grid=(n // tn, m // tm),
        in_specs=[pl.BlockSpec((tm, k), lambda j, i: (i, 0)),
                  pl.BlockSpec((k, tn), lambda j, i: (0, j))],
        out_specs=pl.BlockSpec((tm, tn), lambda j, i: (i, j)),
        out_shape=jax.ShapeDtypeStruct((m, n), out_dtype),
        compiler_params=_cparams(("arbitrary", "arbitrary")),
        name=name,
    )(x, w)


def _sink_attend(qh, kh, vh, mask, sk):
    s = lax.dot_general(qh, kh, (((1,), (1,)), ((), ())), preferred_element_type=jnp.float32)
    if mask is not None:
        s = jnp.where(mask, s, NEG_INF)
    m = jnp.maximum(jnp.max(s, axis=-1, keepdims=True), sk)
    p = jnp.exp(s - m)
    l = jnp.sum(p, axis=-1, keepdims=True) + jnp.exp(sk - m)
    o = jnp.dot(p.astype(vh.dtype), vh, preferred_element_type=jnp.float32)
    return o / l


def _attn_a_kernel(sink_ref, q_ref, kp_ref, kc_ref, kn_ref, vp_ref, vc_ref, vn_ref, kx_ref, vx_ref,
                   o_ref, *, seq):
    n = pl.program_id(1)
    q = q_ref[0]
    k_all = jnp.concatenate([kp_ref[0], kc_ref[0], kn_ref[0], kx_ref[0]], axis=0)
    v_all = jnp.concatenate([vp_ref[0], vc_ref[0], vn_ref[0], vx_ref[0]], axis=0)
    n_keys = k_all.shape[0]
    qi = lax.broadcasted_iota(jnp.int32, (BLOCK, n_keys), 0)
    kj = lax.broadcasted_iota(jnp.int32, (BLOCK, n_keys), 1)
    kpos = n * BLOCK + kj - BLOCK
    local_ok = (jnp.abs(kj - BLOCK - qi) <= WINDOW) & (kpos >= 0) & (kpos < seq)
    mask = local_ok | (kj >= 3 * BLOCK)
    outs = []
    for hq in range(A_HQ):
        hk = hq // A_GROUP
        outs.append(_sink_attend(q[:, hq * HEAD_DIM:(hq + 1) * HEAD_DIM],
                                 k_all[:, hk * HEAD_DIM:(hk + 1) * HEAD_DIM],
                                 v_all[:, hk * HEAD_DIM:(hk + 1) * HEAD_DIM], mask, sink_ref[hq]))
    o_ref[0] = jnp.concatenate(outs, axis=1).astype(o_ref.dtype)


def _attn_a_ctx_kernel(sink_ref, q_ref, kx_ref, vx_ref, o_ref):
    q = q_ref[0]
    k_all = kx_ref[0]
    v_all = vx_ref[0]
    outs = []
    for hq in range(A_HQ):
        hk = hq // A_GROUP
        outs.append(_sink_attend(q[:, hq * HEAD_DIM:(hq + 1) * HEAD_DIM],
                                 k_all[:, hk * HEAD_DIM:(hk + 1) * HEAD_DIM],
                                 v_all[:, hk * HEAD_DIM:(hk + 1) * HEAD_DIM], None, sink_ref[hq]))
    o_ref[0] = jnp.concatenate(outs, axis=1).astype(o_ref.dtype)


def _window_attention(q, k, v, kx, vx, sink):
    b, s, _ = q.shape
    l = kx.shape[1]
    nb = s // BLOCK
    prev = lambda bi, n, *_: (bi, jnp.maximum(n - 1, 0), 0)
    cur = lambda bi, n, *_: (bi, n, 0)
    nxt = lambda bi, n, *_: (bi, jnp.minimum(n + 1, nb - 1), 0)
    whole = lambda bi, n, *_: (bi, 0, 0)
    kv_blk = (1, BLOCK, A_KV_W)
    return pl.pallas_call(
        functools.partial(_attn_a_kernel, seq=s),
        grid_spec=pltpu.PrefetchScalarGridSpec(
            num_scalar_prefetch=1, grid=(b, nb),
            in_specs=[pl.BlockSpec((1, BLOCK, A_Q_W), cur),
                      pl.BlockSpec(kv_blk, prev), pl.BlockSpec(kv_blk, cur), pl.BlockSpec(kv_blk, nxt),
                      pl.BlockSpec(kv_blk, prev), pl.BlockSpec(kv_blk, cur), pl.BlockSpec(kv_blk, nxt),
                      pl.BlockSpec((1, l, A_KV_W), whole), pl.BlockSpec((1, l, A_KV_W), whole)],
            out_specs=pl.BlockSpec((1, BLOCK, A_Q_W), cur)),
        out_shape=jax.ShapeDtypeStruct((b, s, A_Q_W), jnp.bfloat16),
        compiler_params=_cparams(("arbitrary", "arbitrary")),
        name="window_gqa",
    )(sink, q, k, k, k, v, v, v, kx, vx)


def _context_attention(q, kx, vx, sink):
    b, l, _ = q.shape
    whole = lambda bi, *_: (bi, 0, 0)
    return pl.pallas_call(
        _attn_a_ctx_kernel,
        grid_spec=pltpu.PrefetchScalarGridSpec(
            num_scalar_prefetch=1, grid=(b,),
            in_specs=[pl.BlockSpec((1, l, A_Q_W), whole),
                      pl.BlockSpec((1, l, A_KV_W), whole), pl.BlockSpec((1, l, A_KV_W), whole)],
            out_specs=pl.BlockSpec((1, l, A_Q_W), whole)),
        out_shape=jax.ShapeDtypeStruct((b, l, A_Q_W), jnp.bfloat16),
        compiler_params=_cparams(("arbitrary",)),
        name="context_gqa",
    )(sink, q, kx, vx)


def _diff_flash_kernel(lam_ref, q_ref, k_ref, v_ref, o_ref, q2_sc, sa_sc, sb_sc, m_sc, acc_sc, *, tq, tk, nk):
    j = pl.program_id(3)

    @pl.when(j == 0)
    def _():
        q = q_ref[0]
        lane = lax.broadcasted_iota(jnp.int32, q.shape, 1)
        q2_sc[:tq, :] = jnp.where(lane < B_HD, q, jnp.zeros_like(q))
        q2_sc[tq:, :] = jnp.where(lane >= B_HD, q, jnp.zeros_like(q))
        m_sc[...] = jnp.full_like(m_sc, NEG_INF)
        acc_sc[...] = jnp.zeros_like(acc_sc)
        sb_sc[...] = jnp.full_like(sb_sc, NEG_INF)

    def step(s_new_ref, s_old_ref):
        m_prev = m_sc[...]
        m_new = jnp.maximum(m_prev, jnp.max(s_old_ref[...], axis=-1, keepdims=True))
        alpha = jnp.exp2(m_prev - m_new)
        p = jnp.exp2(s_old_ref[...] - jnp.tile(m_new, (1, tk // LANES))).astype(jnp.bfloat16)
        acc_sc[...] = jnp.tile(alpha, (1, 2)) * acc_sc[...] + jnp.dot(
            p, v_ref[0, 0], preferred_element_type=jnp.float32)
        m_sc[...] = m_new
        s_new_ref[...] = lax.dot_general(q2_sc[...], k_ref[0, 0], (((1,), (1,)), ((), ())),
                                         preferred_element_type=jnp.float32)

    @pl.when(j % 2 == 0)
    def _():
        step(sa_sc, sb_sc)

    @pl.when(j % 2 == 1)
    def _():
        step(sb_sc, sa_sc)

    @pl.when(j == nk)
    def _():
        acc = acc_sc[...]
        o = acc[:, :B_VD] / acc[:, B_VD:]
        o_ref[0] = (o[:tq, :] - lam_ref[0] * o[tq:, :]).astype(o_ref.dtype)


def _diff_attention(lam, q, keys, vals1, *, tq, tk, k_block_offset, nk):
    b, t_q, _ = q.shape
    return pl.pallas_call(
        functools.partial(_diff_flash_kernel, tq=tq, tk=tk, nk=nk),
        grid_spec=pltpu.PrefetchScalarGridSpec(
            num_scalar_prefetch=1, grid=(b, B_H, t_q // tq, nk + 1),
            in_specs=[pl.BlockSpec((1, tq, 2 * B_HD), lambda bi, h, i, j, *_: (bi, i, h)),
                      pl.BlockSpec((1, 1, tk, 2 * B_HD),
                                   lambda bi, h, i, j, *_: (bi, h, jnp.minimum(j, nk - 1) + k_block_offset, 0)),
                      pl.BlockSpec((1, 1, tk, 2 * B_VD),
                                   lambda bi, h, i, j, *_: (bi, h, jnp.maximum(j - 1, 0) + k_block_offset, 0))],
            out_specs=pl.BlockSpec((1, tq, B_VD), lambda bi, h, i, j, *_: (bi, i, h)),
            scratch_shapes=[pltpu.VMEM((2 * tq, 2 * B_HD), jnp.bfloat16),
                            pltpu.VMEM((2 * tq, tk), jnp.float32),
                            pltpu.VMEM((2 * tq, tk), jnp.float32),
                            pltpu.VMEM((2 * tq, LANES), jnp.float32),
                            pltpu.VMEM((2 * tq, 2 * B_VD), jnp.float32)]),
        out_shape=jax.ShapeDtypeStruct((b, t_q, B_V_W), jnp.float32),
        compiler_params=_cparams(("arbitrary", "arbitrary", "arbitrary", "arbitrary")),
        name="diff_flash",
    )(lam, q, keys, vals1)


def _hgrn_sub_chunk(q_ref, v_ref, lf_ref, o_ref, st_ref, r0, tri, rev):
    half = HGRN_SUB // 2
    row8 = lax.broadcasted_iota(jnp.int32, (half, C_DK), 0)
    lf = lf_ref[0, 0, pl.ds(r0, HGRN_SUB), :] * LOG2_E
    lf_a = lf.astype(jnp.bfloat16)
    rem = lf - lf_a.astype(jnp.float32)
    lf_b = rem.astype(jnp.bfloat16)
    lf_c = (rem - lf_b.astype(jnp.float32)).astype(jnp.bfloat16)
    bl = (jnp.dot(tri, lf_a, preferred_element_type=jnp.float32)
          + jnp.dot(tri, lf_b, preferred_element_type=jnp.float32)
          + jnp.dot(tri, lf_c, preferred_element_type=jnp.float32))
    q = q_ref[0, pl.ds(r0, HGRN_SUB), :]
    v = v_ref[0, pl.ds(r0, HGRN_SUB), :]
    last = 0 if rev else HGRN_SUB - 1
    for hh in range(C_H):
        sl = slice(hh * C_DK, (hh + 1) * C_DK)
        lfh, blh, qh, vh = lf[:, sl], bl[:, sl], q[:, sl], v[:, sl]
        tot = blh[last:last + 1, :]
        kh = 1.0 - jnp.exp2(lfh)
        qt = qh * jnp.exp2(blh)
        kt = kh * jnp.exp2(tot - blh)
        st = st_ref[hh]
        o_sub = lax.dot_general(qt.astype(jnp.bfloat16), st.astype(jnp.bfloat16),
                                (((1,), (1,)), ((), ())), preferred_element_type=jnp.float32)
        parts = [o_sub[:half, :], o_sub[half:, :]]
        for ss in range(HGRN_SUB):
            ks = kh[ss:ss + 1, :]
            bs = blh[ss:ss + 1, :]
            vs = vh[ss:ss + 1, :]
            for p in range(2):
                lo = p * half
                if rev:
                    reached = lo <= ss
                    full = lo + half - 1 <= ss
                    ok = row8 + lo <= ss
                else:
                    reached = lo + half - 1 >= ss
                    full = lo >= ss
                    ok = row8 + lo >= ss
                if not reached:
                    continue
                diff = blh[lo:lo + half, :] - bs
                if not full:
                    diff = jnp.where(ok, diff, NEG_INF)
                col = jnp.sum((qh[lo:lo + half, :] * ks) * jnp.exp2(diff), axis=1, keepdims=True)
                parts[p] = parts[p] + col * vs
        o_ref[0, 0, pl.ds(r0, half), sl] = parts[0]
        o_ref[0, 0, pl.ds(r0 + half, half), sl] = parts[1]
        upd = lax.dot_general(vh.astype(jnp.bfloat16), kt.astype(jnp.bfloat16),
                              (((0,), (0,)), ((), ())), preferred_element_type=jnp.float32)
        st_ref[hh] = jnp.exp2(tot) * st + upd


def _hgrn_kernel(q_ref, v_ref, lf_ref, o_ref, st_ref, *, tb):
    d = pl.program_id(1)
    t = pl.program_id(2)
    nsub = tb // HGRN_SUB

    @pl.when(t == 0)
    def _():
        st_ref[...] = jnp.zeros_like(st_ref)

    rr = lax.broadcasted_iota(jnp.int32, (HGRN_SUB, HGRN_SUB), 0)
    cc = lax.broadcasted_iota(jnp.int32, (HGRN_SUB, HGRN_SUB), 1)

    @pl.when(d == 0)
    def _():
        tri = (cc <= rr).astype(jnp.bfloat16)

        def body(i, carry):
            r0 = pl.multiple_of(i * HGRN_SUB, HGRN_SUB)
            _hgrn_sub_chunk(q_ref, v_ref, lf_ref, o_ref, st_ref, r0, tri, False)
            return carry

        lax.fori_loop(0, nsub, body, 0, unroll=HGRN_UNROLL)

    @pl.when(d == 1)
    def _():
        tri = (cc >= rr).astype(jnp.bfloat16)

        def body(i, carry):
            r0 = pl.multiple_of((nsub - 1 - i) * HGRN_SUB, HGRN_SUB)
            _hgrn_sub_chunk(q_ref, v_ref, lf_ref, o_ref, st_ref, r0, tri, True)
            return carry

        lax.fori_loop(0, nsub, body, 0, unroll=HGRN_UNROLL)


def _hgrn_scan(q, v, lf, *, n_ctx_blocks):
    b, t, w = q.shape
    tb = HGRN_TB
    nblk = t // tb
    nlat = nblk - n_ctx_blocks

    def row_block(di, ti):
        ctx_blk = jnp.where(di == 0, nlat + ti, nblk - 1 - ti)
        lat_blk = jnp.where(di == 0, ti - n_ctx_blocks, nblk - 1 - ti)
        return jnp.where(ti < n_ctx_blocks, ctx_blk, lat_blk)

    return pl.pallas_call(
        functools.partial(_hgrn_kernel, tb=tb),
        grid=(b, 2, nblk),
        in_specs=[pl.BlockSpec((1, tb, w), lambda bi, di, ti: (bi, row_block(di, ti), 0)),
                  pl.BlockSpec((1, tb, w), lambda bi, di, ti: (bi, row_block(di, ti), 0)),
                  pl.BlockSpec((1, 1, tb, w), lambda bi, di, ti: (di, bi, row_block(di, ti), 0))],
        out_specs=pl.BlockSpec((1, 1, tb, w), lambda bi, di, ti: (di, bi, row_block(di, ti), 0)),
        out_shape=jax.ShapeDtypeStruct((2, b, t, w), jnp.float32),
        scratch_shapes=[pltpu.VMEM((C_H, C_DV, C_DK), jnp.float32)],
        compiler_params=_cparams(("arbitrary", "arbitrary", "arbitrary")),
        name="hgrn2_scan",
    )(q, v, lf)


def _merge_kernel(oa_ref, ob_ref, oc_ref, g_ref, wbr_ref, wo_ref, y_ref):
    acc = None
    for i, o_ref in enumerate((oa_ref, ob_ref, oc_ref)):
        z = jnp.dot(o_ref[...], wbr_ref[i], preferred_element_type=jnp.float32)
        g = jax.nn.sigmoid(g_ref[:, i * D_MODEL:(i + 1) * D_MODEL].astype(jnp.float32))
        acc = g * z if acc is None else acc + g * z
    y_ref[...] = jnp.dot(acc.astype(jnp.bfloat16), wo_ref[...], preferred_element_type=jnp.float32)


def _merge(oa, ob, oc, gates, w_br, w_o):
    m = oa.shape[0]
    tm = MM_TM
    row = lambda i: (i, 0)
    return pl.pallas_call(
        _merge_kernel,
        grid=(m // tm,),
        in_specs=[pl.BlockSpec((tm, BRANCH_W), row), pl.BlockSpec((tm, BRANCH_W), row),
                  pl.BlockSpec((tm, BRANCH_W), row), pl.BlockSpec((tm, GATE_W), row),
                  pl.BlockSpec((N_BRANCH, BRANCH_W, D_MODEL), lambda i: (0, 0, 0)),
                  pl.BlockSpec((D_MODEL, D_MODEL), lambda i: (0, 0))],
        out_specs=pl.BlockSpec((tm, D_MODEL), row),
        out_shape=jax.ShapeDtypeStruct((m, D_MODEL), jnp.float32),
        compiler_params=_cparams(("arbitrary",)),
        name="branch_merge",
    )(oa, ob, oc, gates, w_br, w_o)


def _moe_row_copy(x_hbm, xbuf, sem, src_row, n_rows, slot, dst_row):
    return pltpu.make_async_copy(x_hbm.at[pl.ds(src_row, n_rows), :],
                                 xbuf.at[slot, pl.ds(dst_row, n_rows), :], sem.at[slot])


def _moe_kernel(blk_e_ref, n_used_ref, tok_ref, tok_next_ref, x_hbm, wgu_ref, bgu_ref, wdn_ref, bdn_ref,
                y_ref, xbuf, sem, wgu_sc, wdn_sc):
    i = pl.program_id(0)
    n_used = n_used_ref[0]
    slot = i % 2
    e = blk_e_ref[i]
    e_prev = blk_e_ref[jnp.maximum(i - 1, 0)]

    def start_gather(ids_ref, dst_slot):
        for r in range(MOE_TM):
            _moe_row_copy(x_hbm, xbuf, sem, ids_ref[0, 0, r], 1, dst_slot, r).start()

    @pl.when(i == 0)
    def _():
        start_gather(tok_ref, 0)

    @pl.when(i + 1 < n_used)
    def _():
        start_gather(tok_next_ref, 1 - slot)

    @pl.when((i == 0) | (e != e_prev))
    def _():
        wgu_sc[...] = wgu_ref[0].astype(jnp.bfloat16)
        wdn_sc[...] = wdn_ref[0].astype(jnp.bfloat16)

    @pl.when(i < n_used)
    def _():
        _moe_row_copy(x_hbm, xbuf, sem, 0, MOE_TM, slot, 0).wait()
        x = xbuf[slot].astype(jnp.bfloat16)
        gu = jnp.dot(x, wgu_sc[...], preferred_element_type=jnp.float32) + bgu_ref[0]
        g = jnp.minimum(gu[:, :EXPERT_FF], SWIGLU_LIMIT)
        u = jnp.clip(gu[:, EXPERT_FF:], -SWIGLU_LIMIT, SWIGLU_LIMIT)
        act = (u + 1.0) * (g * jax.nn.sigmoid(SWIGLU_ALPHA * g))
        y = jnp.dot(act.astype(jnp.bfloat16), wdn_sc[...], preferred_element_type=jnp.float32) + bdn_ref[0]
        y_ref[...] = y.astype(y_ref.dtype)

    @pl.when(i >= n_used)
    def _():
        y_ref[...] = jnp.zeros_like(y_ref)


def _moe_experts(blk_e, n_used, row_tok, x, w_gu, b_gu, w_dn, b_dn):
    dm = x.shape[1]
    tm = MOE_TM
    n_blk = blk_e.shape[0]
    r = n_blk * tm
    ids = row_tok.reshape(n_blk, 1, tm)
    return pl.pallas_call(
        _moe_kernel,
        grid_spec=pltpu.PrefetchScalarGridSpec(
            num_scalar_prefetch=2, grid=(n_blk,),
            in_specs=[pl.BlockSpec((1, 1, tm), lambda i, be, nu: (i, 0, 0), memory_space=pltpu.SMEM),
                      pl.BlockSpec((1, 1, tm), lambda i, be, nu: (jnp.minimum(i + 1, n_blk - 1), 0, 0),
                                   memory_space=pltpu.SMEM),
                      pl.BlockSpec(memory_space=pl.ANY),
                      pl.BlockSpec((1, dm, 2 * EXPERT_FF), lambda i, be, nu: (be[i], 0, 0)),
                      pl.BlockSpec((1, 1, 2 * EXPERT_FF), lambda i, be, nu: (be[i], 0, 0)),
                      pl.BlockSpec((1, EXPERT_FF, dm), lambda i, be, nu: (be[i], 0, 0)),
                      pl.BlockSpec((1, 1, dm), lambda i, be, nu: (be[i], 0, 0))],
            out_specs=pl.BlockSpec((tm, dm), lambda i, be, nu: (i, 0)),
            scratch_shapes=[pltpu.VMEM((2, tm, dm), jnp.float32),
                            pltpu.SemaphoreType.DMA((2,)),
                            pltpu.VMEM((dm, 2 * EXPERT_FF), jnp.bfloat16),
                            pltpu.VMEM((EXPERT_FF, dm), jnp.bfloat16)]),
        out_shape=jax.ShapeDtypeStruct((r, dm), jnp.bfloat16),
        compiler_params=_cparams(("arbitrary",)),
        name="moe_experts",
    )(blk_e, n_used, ids, ids, x, w_gu, b_gu.reshape(N_EXPERTS, 1, -1), w_dn, b_dn.reshape(N_EXPERTS, 1, -1))


def _moe_ffn(h, w_router, b_router, w_gu, b_gu, w_dn, b_dn):
    n, dm = h.shape
    nk = n * TOP_K
    w_r = jnp.zeros((dm, LANES), jnp.float32).at[:, :N_EXPERTS].set(w_router)
    logits = _matmul(h, w_r, tm=MM_TM, tn=LANES, out_dtype=jnp.float32,
                     precision=lax.Precision.HIGHEST, name="router")[:, :N_EXPERTS] + b_router
    top_v, top_e = lax.top_k(logits, TOP_K)
    gates = jax.nn.softmax(top_v, axis=-1)
    flat_e = top_e.reshape(-1).astype(jnp.int32)
    onehot = (flat_e[:, None] == jnp.arange(N_EXPERTS, dtype=jnp.int32)[None, :]).astype(jnp.int32)
    incl = jnp.cumsum(onehot, axis=0)
    rank = jnp.sum(incl * onehot, axis=1) - 1
    counts = incl[-1]
    padded = (counts + MOE_TM - 1) // MOE_TM * MOE_TM
    pad_end = jnp.cumsum(padded)
    pad_start = pad_end - padded
    grp_start = jnp.cumsum(counts) - counts
    dest = jnp.sum(onehot * pad_start[None, :], axis=1) + rank
    n_blk = (nk + MOE_TM - 1) // MOE_TM + N_EXPERTS
    rows = n_blk * MOE_TM
    blk_e = jnp.clip(jnp.searchsorted(pad_end, jnp.arange(n_blk, dtype=jnp.int32) * MOE_TM, side='right'),
                     0, N_EXPERTS - 1).astype(jnp.int32)
    n_used = (pad_end[-1] // MOE_TM).astype(jnp.int32).reshape(1)
    order = jnp.argsort(flat_e)
    row_e = jnp.repeat(blk_e, MOE_TM)
    sorted_pos = jnp.arange(rows, dtype=jnp.int32) - pad_start[row_e] + grp_start[row_e]
    row_tok = (order[jnp.clip(sorted_pos, 0, nk - 1)] // TOP_K).astype(jnp.int32)
    y = _moe_experts(blk_e, n_used, row_tok, h, w_gu, b_gu, w_dn, b_dn)
    yk = lax.optimization_barrier(y[dest]).reshape(n, TOP_K, dm).astype(jnp.float32)
    return jnp.sum(yk * gates[..., None], axis=1)


def _rms_norm(x, g):
    xf = x.astype(jnp.float32)
    y = xf * lax.rsqrt(jnp.mean(xf * xf, axis=-1, keepdims=True) + RMS_EPS)
    return y * g.astype(jnp.float32)


def _modulate(x, g, shift, scale):
    return _rms_norm(x, g) * (1 + scale[:, None, :]) + shift[:, None, :]


def _rope_tables(rows):
    row = jnp.repeat(jnp.arange(rows, dtype=jnp.float32), GRID_W)
    col = jnp.tile(jnp.arange(GRID_W, dtype=jnp.float32), rows)
    n_freq = HEAD_DIM // 4
    inv = ROPE_THETA ** (-jnp.arange(n_freq, dtype=jnp.float32) / n_freq)
    ang_r = row[:, None] * inv
    ang_c = col[:, None] * inv
    return jnp.cos(ang_r), jnp.sin(ang_r), jnp.cos(ang_c), jnp.sin(ang_c)


def _rope(x, tabs):
    cos_r, sin_r, cos_c, sin_c = tabs
    n = HEAD_DIM // 4
    half = HEAD_DIM // 2
    shape = (1, x.shape[1]) + (1,) * (x.ndim - 3) + (n,)

    def rot(xa, cos, sin):
        cos = cos.reshape(shape)
        sin = sin.reshape(shape)
        x1, x2 = xa[..., :n], xa[..., n:]
        return jnp.concatenate([x1 * cos - x2 * sin, x2 * cos + x1 * sin], axis=-1)

    return jnp.concatenate([rot(x[..., :half], cos_r, sin_r), rot(x[..., half:], cos_c, sin_c)], axis=-1)


def _log_forget(z, lb):
    lbf = lb.astype(jnp.float32)
    return jnp.logaddexp(jnp.log(lbf), jnp.log1p(-lbf) + jax.nn.log_sigmoid(z.astype(jnp.float32)))


def _hybrid_mixer(tok, seq, tabs, lb, w_in, sink, lam_par, lam_init, diff_w, hgrn_w, w_br, w_o):
    b, t, dm = tok.shape
    l = t - seq
    f32 = jnp.float32
    bf16 = jnp.bfloat16
    flat = tok.reshape(b * t, dm)
    w_in_bf = w_in.astype(bf16)
    proj = _matmul(flat, w_in_bf[:, :MIX_W], tm=MM_TM, tn=MIX_W // 2, out_dtype=bf16,
                   name="in_proj").reshape(b, t, MIX_W)
    gates = _matmul(flat, w_in_bf[:, MIX_W:], tm=MM_TM, tn=GATE_W, out_dtype=bf16, name="gate_proj")

    def cols(off, width, rows=slice(None)):
        return proj[:, rows, off:off + width].astype(f32)

    lat = slice(0, seq)
    cx = slice(seq, t)
    scale = HEAD_DIM ** -0.5

    aq = (_rope(cols(OFF_AQ, A_Q_W, lat).reshape(b, seq, A_HQ, HEAD_DIM), tabs) * scale)
    aq = aq.reshape(b, seq, A_Q_W).astype(bf16)
    ak = _rope(cols(OFF_AK, A_KV_W, lat).reshape(b, seq, A_HKV, HEAD_DIM), tabs).reshape(b, seq, A_KV_W).astype(bf16)
    av = proj[:, lat, OFF_AV:OFF_AV + A_KV_W]
    akx = proj[:, cx, OFF_AK:OFF_AK + A_KV_W]
    avx = proj[:, cx, OFF_AV:OFF_AV + A_KV_W]
    sink32 = sink.astype(f32)
    o_a = _window_attention(aq, ak, av, akx, avx, sink32)
    aqx = (cols(OFF_AQ, A_Q_W, cx) * scale).astype(bf16)
    o_ax = _context_attention(aqx, akx, avx, sink32)
    o_a = jnp.concatenate([o_a, o_ax], axis=1)

    lp = lam_par.astype(f32)
    lam = jnp.exp(jnp.sum(lp[0] * lp[1])) - jnp.exp(jnp.sum(lp[2] * lp[3])) + lam_init
    lam = lam.reshape(1).astype(f32)
    scale2 = scale * math.log2(math.e)
    bq = (_rope(cols(OFF_BQ, B_QK_W, lat).reshape(b, seq, B_H, 2, B_HD), tabs) * scale2)
    bq = bq.reshape(b, seq, B_QK_W).astype(bf16)
    bk = _rope(cols(OFF_BK, B_QK_W, lat).reshape(b, seq, B_H, 2, B_HD), tabs).reshape(b, seq, B_QK_W).astype(bf16)
    keys = jnp.concatenate([bk, proj[:, cx, OFF_BK:OFF_BK + B_QK_W]], axis=1)
    keys = keys.reshape(b, t, B_H, 2 * B_HD).transpose(0, 2, 1, 3)
    vals = proj[:, :, OFF_BV:OFF_BV + B_V_W].reshape(b, t, B_H, B_VD).transpose(0, 2, 1, 3)
    vals1 = jnp.concatenate([vals, jnp.ones_like(vals)], axis=-1)
    o_b = _diff_attention(lam, bq, keys, vals1, tq=FLASH_TQ, tk=FLASH_TK, k_block_offset=0, nk=t // FLASH_TK)
    bqx = (cols(OFF_BQ, B_QK_W, cx) * scale2).astype(bf16)
    o_bx = _diff_attention(lam, bqx, keys, vals1, tq=l, tk=l, k_block_offset=seq // l, nk=1)
    o_b = jnp.concatenate([o_b, o_bx], axis=1).reshape(b, t, B_H, B_VD)
    o_b = (_rms_norm(o_b, diff_w) * (1 - lam_init)).reshape(b, t, B_V_W).astype(bf16)

    qc = jax.nn.silu(cols(OFF_CQ, C_K_W))
    ic = cols(OFF_CI, C_V_W)
    lf = jnp.stack([_log_forget(cols(OFF_CFF, C_K_W), lb[0]), _log_forget(cols(OFF_CFB, C_K_W), lb[1])])
    o2 = _hgrn_scan(qc, ic, lf, n_ctx_blocks=l // HGRN_TB)
    o_c = (o2[0] + o2[1]).reshape(b, t, C_H, C_DV)
    o_c = _rms_norm(o_c, hgrn_w) * jax.nn.silu(cols(OFF_CG, C_V_W).reshape(b, t, C_H, C_DV))
    o_c = o_c.reshape(b, t, C_V_W).astype(bf16)

    y = _merge(o_a.reshape(b * t, A_Q_W), o_b.reshape(b * t, B_V_W), o_c.reshape(b * t, C_V_W), gates,
               w_br.astype(bf16), w_o.astype(bf16))
    return y.reshape(b, t, dm)


def kernel(x, c, ctx, c_ctx, w_mod, b_mod, norm_g, w_in, attn_sink, diff_lambda, diff_norm_w,
           hgrn_lb_logits, hgrn_norm_w, w_branch, w_out, w_router, b_router, w_gate_up, b_gate_up,
           w_down, b_down):
    b, s, dm = x.shape
    l = ctx.shape[1]
    tabs = _rope_tables(s // GRID_W)
    lb_cum = jnp.cumsum(jax.nn.softmax(hgrn_lb_logits.astype(jnp.float32), axis=0), axis=0)
    lower_bounds = lb_cum - lb_cum[0:1]
    cond_rows = 16
    cond = jnp.zeros((cond_rows, dm), jnp.float32).at[:b].set(jax.nn.silu(c)).at[b].set(jax.nn.silu(c_ctx))
    xc = ctx
    for layer in range(DEPTH):
        last = layer == DEPTH - 1
        lam_init = 0.8 - 0.6 * math.exp(-0.3 * layer)
        mod_all = _matmul(cond, w_mod[layer], tm=cond_rows, tn=3 * dm, out_dtype=jnp.float32,
                          precision=lax.Precision.HIGHEST, name="adaln") + b_mod[layer]
        mod = mod_all[:b].reshape(b, 6, dm)
        mod_c = mod_all[b:b + 1].reshape(1, 6, dm)
        g = norm_g[layer]
        h = _modulate(x, g[0], mod[:, 0], mod[:, 1])
        hc = _modulate(xc, g[0], mod_c[:, 0], mod_c[:, 1])
        tok = jnp.concatenate([h, hc], axis=1).astype(jnp.bfloat16)
        y = _hybrid_mixer(tok, s, tabs, lower_bounds[layer], w_in[layer], attn_sink[layer], diff_lambda[layer],
                          lam_init, diff_norm_w[layer], hgrn_norm_w[layer], w_branch[layer], w_out[layer])
        x = x + mod[:, 2][:, None, :] * _rms_norm(y[:, :s], g[1])
        h = _modulate(x, g[2], mod[:, 3], mod[:, 4])
        moe_w = (w_router[layer], b_router[layer], w_gate_up[layer], b_gate_up[layer], w_down[layer], b_down[layer])
        if last:
            f = _moe_ffn(h.reshape(b * s, dm), *moe_w).reshape(b, s, dm)
        else:
            xc = xc + mod_c[:, 2][:, None, :] * _rms_norm(y[:, s:], g[1])
            hc = _modulate(xc, g[2], mod_c[:, 3], mod_c[:, 4])
            rows = jnp.concatenate([h.reshape(b * s, dm), hc.reshape(b * l, dm)], axis=0)
            ft = _moe_ffn(rows, *moe_w)
            f = ft[:b * s].reshape(b, s, dm)
            fc = ft[b * s:].reshape(b, l, dm)
            xc = xc + mod_c[:, 5][:, None, :] * _rms_norm(fc, g[3])
        x = x + mod[:, 5][:, None, :] * _rms_norm(f, g[3])
    return x
```

```python
import functools
import math

import jax
import jax.numpy as jnp
from jax import lax
from jax.experimental import pallas as pl
from jax.experimental.pallas import tpu as pltpu

D_MODEL = 1024
DEPTH = 2
GRID_W = 64
HEAD_DIM = 64
ROPE_THETA = 10000.0
RMS_EPS = 1e-6
NEG_INF = -1e30
BLOCK = 128
A_HQ = 8
A_HKV = 2
A_GROUP = A_HQ // A_HKV
WINDOW = 128
B_H = 4
B_HD = HEAD_DIM
B_VD = 2 * HEAD_DIM
C_H = 4
C_DK = 128
C_DV = 128
N_BRANCH = 3
BRANCH_W = 512
A_Q_W = A_HQ * HEAD_DIM
A_KV_W = A_HKV * HEAD_DIM
B_QK_W = B_H * 2 * B_HD
B_V_W = B_H * B_VD
C_K_W = C_H * C_DK
C_V_W = C_H * C_DV
GATE_W = N_BRANCH * D_MODEL
N_EXPERTS = 32
TOP_K = 4
EXPERT_FF = D_MODEL
SWIGLU_LIMIT = 7.0
SWIGLU_ALPHA = 1.702

A_W = A_Q_W + 2 * A_KV_W
A_ROPE_W = A_Q_W + A_KV_W
B_W = 2 * B_QK_W + B_V_W
B_ROPE_W = 2 * B_QK_W
C_W = 3 * C_K_W + 2 * C_V_W
OFF_A = 0
OFF_B = OFF_A + A_W
OFF_C = OFF_B + B_W
OFF_GATE = OFF_C + C_W

V7X_VMEM_LIMIT_BYTES = 56 * 1024 * 1024
LANES = 128
LOG2_E = math.log2(math.e)

MM_TM = 512
HALF_TM = MM_TM // 2
FLASH_TQ = 512
FLASH_TK = 768
HGRN_TB = 256
HGRN_SUB = 16
HGRN_UNROLL = 4
MOE_TM = 512


def _cparams(sem):
    return pltpu.CompilerParams(dimension_semantics=sem, vmem_limit_bytes=V7X_VMEM_LIMIT_BYTES)


def _rms_rows(x, g):
    return (x * lax.rsqrt(jnp.mean(x * x, axis=-1, keepdims=True) + RMS_EPS)) * g


def _modulated_tile(x_ref, g, mod_a, mod_b, shift_slot, scale_slot):
    out = []
    for half, mod in enumerate((mod_a, mod_b)):
        x = x_ref[half * HALF_TM:(half + 1) * HALF_TM, :]
        out.append(_rms_rows(x, g) * (1.0 + mod[scale_slot]) + mod[shift_slot])
    return jnp.concatenate(out, axis=0)


def _residual_tile(x_ref, y, g, mod_a, mod_b, gate_slot):
    out = []
    for half, mod in enumerate((mod_a, mod_b)):
        rows = slice(half * HALF_TM, (half + 1) * HALF_TM)
        out.append(x_ref[rows, :] + mod[gate_slot] * _rms_rows(y[rows, :], g))
    return out


def _mod_specs(t_rows, seq):
    halves_per_batch = t_rows // HALF_TM
    lat_halves = seq // HALF_TM

    def row(u):
        return (u // halves_per_batch) * 2 + jnp.where(u % halves_per_batch >= lat_halves, 1, 0)

    blk = (1, 6, 1, D_MODEL)
    return [pl.BlockSpec(blk, lambda i: (row(2 * i), 0, 0, 0)),
            pl.BlockSpec(blk, lambda i: (row(2 * i + 1), 0, 0, 0))]


def _proj_rope_kernel(x_ref, g_ref, ma_ref, mb_ref, w_ref, wsw_ref, cos_ref, sin_ref, o_ref, *, rope_w):
    h = _modulated_tile(x_ref, g_ref[0], ma_ref[0], mb_ref[0], 0, 1).astype(jnp.bfloat16)
    z = jnp.dot(h, w_ref[...], preferred_element_type=jnp.float32)
    zs = jnp.dot(h, wsw_ref[...], preferred_element_type=jnp.float32)
    reps = rope_w // LANES
    rot = z[:, :rope_w] * jnp.tile(cos_ref[...], (1, reps)) + zs * jnp.tile(sin_ref[...], (1, reps))
    o_ref[:, :rope_w] = rot.astype(o_ref.dtype)
    o_ref[:, rope_w:] = z[:, rope_w:].astype(o_ref.dtype)


def _proj_plain_kernel(x_ref, g_ref, ma_ref, mb_ref, w_ref, o_ref):
    h = _modulated_tile(x_ref, g_ref[0], ma_ref[0], mb_ref[0], 0, 1).astype(jnp.bfloat16)
    o_ref[...] = jnp.dot(h, w_ref[...], preferred_element_type=jnp.float32).astype(o_ref.dtype)


def _log_forget(z, log_lb, log_1m_lb):
    log_sig = jnp.minimum(z, 0.0) - jnp.log(1.0 + jnp.exp(-jnp.abs(z)))
    c = log_1m_lb + log_sig
    return jnp.maximum(log_lb, c) + jnp.log(1.0 + jnp.exp(-jnp.abs(log_lb - c)))


def _proj_hgrn_kernel(x_ref, g_ref, ma_ref, mb_ref, w_ref, lb_ref, q_ref, v_ref, lf_ref, cg_ref):
    h = _modulated_tile(x_ref, g_ref[0], ma_ref[0], mb_ref[0], 0, 1).astype(jnp.bfloat16)
    z = jnp.dot(h, w_ref[...], preferred_element_type=jnp.float32)
    zq = z[:, :C_K_W]
    q_ref[...] = zq * jax.nn.sigmoid(zq)
    v_ref[...] = z[:, C_K_W:C_K_W + C_V_W]
    off = C_K_W + C_V_W
    lf_ref[0] = _log_forget(z[:, off:off + C_K_W], lb_ref[0], lb_ref[1])
    lf_ref[1] = _log_forget(z[:, off + C_K_W:off + 2 * C_K_W], lb_ref[2], lb_ref[3])
    cg_ref[...] = z[:, off + 2 * C_K_W:].astype(cg_ref.dtype)


def _proj_common_specs(t_rows, seq):
    row = lambda i: (i, 0)
    return [pl.BlockSpec((MM_TM, D_MODEL), row),
            pl.BlockSpec((1, 1, D_MODEL), lambda i: (0, 0, 0))] + _mod_specs(t_rows, seq)


def _proj_rope(x, g, mods, w, wsw, cos, sin, *, t_rows, seq, rope_w, name):
    m = x.shape[0]
    n = w.shape[1]
    row = lambda i: (i, 0)
    whole = lambda i: (0, 0)
    return pl.pallas_call(
        functools.partial(_proj_rope_kernel, rope_w=rope_w),
        grid=(m // MM_TM,),
        in_specs=_proj_common_specs(t_rows, seq) + [
            pl.BlockSpec((D_MODEL, n), whole), pl.BlockSpec((D_MODEL, rope_w), whole),
            pl.BlockSpec((MM_TM, LANES), row), pl.BlockSpec((MM_TM, LANES), row)],
        out_specs=pl.BlockSpec((MM_TM, n), row),
        out_shape=jax.ShapeDtypeStruct((m, n), jnp.bfloat16),
        compiler_params=_cparams(("arbitrary",)),
        name=name,
    )(x, g, mods, mods, w, wsw, cos, sin)


def _proj_plain(x, g, mods, w, *, t_rows, seq, name):
    m = x.shape[0]
    n = w.shape[1]
    row = lambda i: (i, 0)
    return pl.pallas_call(
        _proj_plain_kernel,
        grid=(m // MM_TM,),
        in_specs=_proj_common_specs(t_rows, seq) + [pl.BlockSpec((D_MODEL, n), lambda i: (0, 0))],
        out_specs=pl.BlockSpec((MM_TM, n), row),
        out_shape=jax.ShapeDtypeStruct((m, n), jnp.bfloat16),
        compiler_params=_cparams(("arbitrary",)),
        name=name,
    )(x, g, mods, mods, w)


def _proj_hgrn(x, g, mods, w, lb_par, *, t_rows, seq):
    m = x.shape[0]
    row = lambda i: (i, 0)
    f32 = jnp.float32
    return pl.pallas_call(
        _proj_hgrn_kernel,
        grid=(m // MM_TM,),
        in_specs=_proj_common_specs(t_rows, seq) + [
            pl.BlockSpec((D_MODEL, C_W), lambda i: (0, 0)),
            pl.BlockSpec((4, 1, C_K_W), lambda i: (0, 0, 0))],
        out_specs=[pl.BlockSpec((MM_TM, C_K_W), row), pl.BlockSpec((MM_TM, C_V_W), row),
                   pl.BlockSpec((2, MM_TM, C_K_W), lambda i: (0, i, 0)), pl.BlockSpec((MM_TM, C_V_W), row)],
        out_shape=[jax.ShapeDtypeStruct((m, C_K_W), f32), jax.ShapeDtypeStruct((m, C_V_W), f32),
                   jax.ShapeDtypeStruct((2, m, C_K_W), f32), jax.ShapeDtypeStruct((m, C_V_W), jnp.bfloat16)],
        compiler_params=_cparams(("arbitrary",)),
        name="hgrn_proj",
    )(x, g, mods, mods, w, lb_par)


def _mm_kernel(x_ref, w_ref, o_ref, *, precision):
    o_ref[...] = jnp.dot(x_ref[...], w_ref[...], preferred_element_type=jnp.float32,
                         precision=precision).astype(o_ref.dtype)


def _matmul(x, w, *, tm, tn, out_dtype, precision=None, name):
    m, k = x.shape
    n = w.shape[1]
    assert m % tm == 0 and n % tn == 0, (m, tm, n, tn)
    return pl.pallas_call(
        functools.partial(_mm_kernel, precision=precision),
        grid=(n // tn, m // tm),
        in_specs=[pl.BlockSpec((tm, k), lambda j, i: (i, 0)),
                  pl.BlockSpec((k, tn), lambda j, i: (0, j))],
        out_specs=pl.BlockSpec((tm, tn), lambda j, i: (i, j)),
        out_shape=jax.ShapeDtypeStruct((m, n), out_dtype),
        compiler_params=_cparams(("arbitrary", "arbitrary")),
        name=name,
    )(x, w)


A_K_BLK = A_Q_W // A_KV_W
A_V_BLK = A_K_BLK + 1


def _sink_attend(qh, kh, vh, mask, sk):
    s = lax.dot_general(qh, kh, (((1,), (1,)), ((), ())), preferred_element_type=jnp.float32)
    if mask is not None:
        s = jnp.where(mask, s, NEG_INF)
    m = jnp.maximum(jnp.max(s, axis=-1, keepdims=True), sk)
    p = jnp.exp(s - m)
    l = jnp.sum(p, axis=-1, keepdims=True) + jnp.exp(sk - m)
    o = jnp.dot(p.astype(vh.dtype), vh, preferred_element_type=jnp.float32)
    return o / l


def _attend_heads(sink_ref, q_ref, k_all, v_all, mask, o_ref):
    for hq in range(A_HQ):
        hk = hq // A_GROUP
        o = _sink_attend(q_ref[0, :, hq * HEAD_DIM:(hq + 1) * HEAD_DIM],
                         k_all[:, hk * HEAD_DIM:(hk + 1) * HEAD_DIM],
                         v_all[:, hk * HEAD_DIM:(hk + 1) * HEAD_DIM], mask, sink_ref[hq])
        o_ref[0, :, hq * HEAD_DIM:(hq + 1) * HEAD_DIM] = o.astype(o_ref.dtype)


def _attn_a_kernel(sink_ref, q_ref, kp_ref, kc_ref, kn_ref, vp_ref, vc_ref, vn_ref, kx_ref, vx_ref,
                   o_ref, *, seq):
    n = pl.program_id(1)
    k_all = jnp.concatenate([kp_ref[0], kc_ref[0], kn_ref[0], kx_ref[0]], axis=0)
    v_all = jnp.concatenate([vp_ref[0], vc_ref[0], vn_ref[0], vx_ref[0]], axis=0)
    n_keys = k_all.shape[0]
    qi = lax.broadcasted_iota(jnp.int32, (BLOCK, n_keys), 0)
    kj = lax.broadcasted_iota(jnp.int32, (BLOCK, n_keys), 1)
    kpos = n * BLOCK + kj - BLOCK
    local_ok = (jnp.abs(kj - BLOCK - qi) <= WINDOW) & (kpos >= 0) & (kpos < seq)
    mask = local_ok | (kj >= 3 * BLOCK)
    _attend_heads(sink_ref, q_ref, k_all, v_all, mask, o_ref)


def _attn_a_ctx_kernel(sink_ref, q_ref, kx_ref, vx_ref, o_in_ref, o_ref):
    del o_in_ref
    _attend_heads(sink_ref, q_ref, kx_ref[0], vx_ref[0], None, o_ref)


def _window_attention(proj_a, sink, *, seq):
    b, t, _ = proj_a.shape
    l = t - seq
    nb = seq // BLOCK
    ctx_blk = seq // l
    q_map = lambda bi, n, *_: (bi, n, 0)
    kv = lambda col, fn: pl.BlockSpec((1, BLOCK, A_KV_W), lambda bi, n, *_: (bi, fn(n), col))
    prev = lambda n: jnp.maximum(n - 1, 0)
    cur = lambda n: n
    nxt = lambda n: jnp.minimum(n + 1, nb - 1)
    ctx = lambda col: pl.BlockSpec((1, l, A_KV_W), lambda bi, n, *_: (bi, ctx_blk, col))
    return pl.pallas_call(
        functools.partial(_attn_a_kernel, seq=seq),
        grid_spec=pltpu.PrefetchScalarGridSpec(
            num_scalar_prefetch=1, grid=(b, nb),
            in_specs=[pl.BlockSpec((1, BLOCK, A_Q_W), q_map),
                      kv(A_K_BLK, prev), kv(A_K_BLK, cur), kv(A_K_BLK, nxt),
                      kv(A_V_BLK, prev), kv(A_V_BLK, cur), kv(A_V_BLK, nxt),
                      ctx(A_K_BLK), ctx(A_V_BLK)],
            out_specs=pl.BlockSpec((1, BLOCK, A_Q_W), q_map)),
        out_shape=jax.ShapeDtypeStruct((b, t, A_Q_W), jnp.bfloat16),
        compiler_params=_cparams(("arbitrary", "arbitrary")),
        name="window_gqa",
    )(sink, *([proj_a] * 9))


def _context_attention(proj_a, sink, o_a, *, seq):
    b, t, _ = proj_a.shape
    l = t - seq
    ctx_blk = seq // l
    ctx = lambda width, col: pl.BlockSpec((1, l, width), lambda bi, *_: (bi, ctx_blk, col))
    return pl.pallas_call(
        _attn_a_ctx_kernel,
        grid_spec=pltpu.PrefetchScalarGridSpec(
            num_scalar_prefetch=1, grid=(b,),
            in_specs=[ctx(A_Q_W, 0), ctx(A_KV_W, A_K_BLK), ctx(A_KV_W, A_V_BLK),
                      pl.BlockSpec(memory_space=pl.ANY)],
            out_specs=ctx(A_Q_W, 0)),
        out_shape=jax.ShapeDtypeStruct(o_a.shape, o_a.dtype),
        input_output_aliases={4: 0},
        compiler_params=_cparams(("arbitrary",)),
        name="context_gqa",
    )(sink, proj_a, proj_a, proj_a, o_a)


def _diff_flash_kernel(lam_ref, q_ref, k_ref, v_ref, *rest, tq, tk, nk, aliased):
    o_ref, q2_sc, v1_sc, sa_sc, sb_sc, m_sc, acc_sc = rest[1:] if aliased else rest
    j = pl.program_id(3)

    @pl.when(j == 0)
    def _():
        q = q_ref[0]
        lane = lax.broadcasted_iota(jnp.int32, q.shape, 1)
        q2_sc[:tq, :] = jnp.where(lane < B_HD, q, jnp.zeros_like(q))
        q2_sc[tq:, :] = jnp.where(lane >= B_HD, q, jnp.zeros_like(q))
        m_sc[...] = jnp.full_like(m_sc, NEG_INF)
        acc_sc[...] = jnp.zeros_like(acc_sc)
        v1_sc[:, B_VD:] = jnp.ones((tk, B_VD), v1_sc.dtype)
        sb_sc[...] = jnp.full_like(sb_sc, NEG_INF)

    def step(s_new_ref, s_old_ref):
        v1_sc[:, :B_VD] = v_ref[0]
        m_prev = m_sc[...]
        m_new = jnp.maximum(m_prev, jnp.max(s_old_ref[...], axis=-1, keepdims=True))
        alpha = jnp.exp2(m_prev - m_new)
        p = jnp.exp2(s_old_ref[...] - jnp.tile(m_new, (1, tk // LANES))).astype(jnp.bfloat16)
        acc_sc[...] = jnp.tile(alpha, (1, 2)) * acc_sc[...] + jnp.dot(
            p, v1_sc[...], preferred_element_type=jnp.float32)
        m_sc[...] = m_new
        s_new_ref[...] = lax.dot_general(q2_sc[...], k_ref[0], (((1,), (1,)), ((), ())),
                                         preferred_element_type=jnp.float32)

    @pl.when(j % 2 == 0)
    def _():
        step(sa_sc, sb_sc)

    @pl.when(j % 2 == 1)
    def _():
        step(sb_sc, sa_sc)

    @pl.when(j == nk)
    def _():
        acc = acc_sc[...]
        o = acc[:, :B_VD] / acc[:, B_VD:]
        o_ref[0] = (o[:tq, :] - lam_ref[0] * o[tq:, :]).astype(o_ref.dtype)


def _diff_attention(lam, proj_b, o_prev, *, tq, tk, q_block_offset, n_q_blocks, k_block_offset, nk):
    b, t, _ = proj_b.shape
    k_col = B_QK_W // LANES
    v_col = 2 * B_QK_W // LANES
    aliased = o_prev is not None
    q_map = lambda bi, h, i, j, *_: (bi, i + q_block_offset, h)
    in_specs = [pl.BlockSpec((1, tq, LANES), q_map),
                pl.BlockSpec((1, tk, LANES),
                             lambda bi, h, i, j, *_: (bi, jnp.minimum(j, nk - 1) + k_block_offset, k_col + h)),
                pl.BlockSpec((1, tk, LANES),
                             lambda bi, h, i, j, *_: (bi, jnp.maximum(j - 1, 0) + k_block_offset, v_col + h))]
    args = [lam, proj_b, proj_b, proj_b]
    if aliased:
        in_specs.append(pl.BlockSpec(memory_space=pl.ANY))
        args.append(o_prev)
    return pl.pallas_call(
        functools.partial(_diff_flash_kernel, tq=tq, tk=tk, nk=nk, aliased=aliased),
        grid_spec=pltpu.PrefetchScalarGridSpec(
            num_scalar_prefetch=1, grid=(b, B_H, n_q_blocks, nk + 1),
            in_specs=in_specs,
            out_specs=pl.BlockSpec((1, tq, B_VD), q_map),
            scratch_shapes=[pltpu.VMEM((2 * tq, 2 * B_HD), jnp.bfloat16),
                            pltpu.VMEM((tk, 2 * B_VD), jnp.bfloat16),
                            pltpu.VMEM((2 * tq, tk), jnp.float32),
                            pltpu.VMEM((2 * tq, tk), jnp.float32),
                            pltpu.VMEM((2 * tq, LANES), jnp.float32),
                            pltpu.VMEM((2 * tq, 2 * B_VD), jnp.float32)]),
        out_shape=jax.ShapeDtypeStruct((b, t, B_V_W), jnp.float32),
        input_output_aliases={4: 0} if aliased else {},
        compiler_params=_cparams(("arbitrary", "arbitrary", "arbitrary", "arbitrary")),
        name="diff_flash",
    )(*args)


def _hgrn_sub_chunk(q_ref, v_ref, lf_ref, o_ref, st_ref, r0, tri, rev):
    half = HGRN_SUB // 2
    row8 = lax.broadcasted_iota(jnp.int32, (half, C_DK), 0)
    lf = lf_ref[0, 0, pl.ds(r0, HGRN_SUB), :] * LOG2_E
    lf_a = lf.astype(jnp.bfloat16)
    rem = lf - lf_a.astype(jnp.float32)
    lf_b = rem.astype(jnp.bfloat16)
    lf_c = (rem - lf_b.astype(jnp.float32)).astype(jnp.bfloat16)
    bl = (jnp.dot(tri, lf_a, preferred_element_type=jnp.float32)
          + jnp.dot(tri, lf_b, preferred_element_type=jnp.float32)
          + jnp.dot(tri, lf_c, preferred_element_type=jnp.float32))
    q = q_ref[0, pl.ds(r0, HGRN_SUB), :]
    v = v_ref[0, pl.ds(r0, HGRN_SUB), :]
    last = 0 if rev else HGRN_SUB - 1
    for hh in range(C_H):
        sl = slice(hh * C_DK, (hh + 1) * C_DK)
        lfh, blh, qh, vh = lf[:, sl], bl[:, sl], q[:, sl], v[:, sl]
        tot = blh[last:last + 1, :]
        kh = 1.0 - jnp.exp2(lfh)
        qt = qh * jnp.exp2(blh)
        kt = kh * jnp.exp2(tot - blh)
        st = st_ref[hh]
        o_sub = lax.dot_general(qt.astype(jnp.bfloat16), st.astype(jnp.bfloat16),
                                (((1,), (1,)), ((), ())), preferred_element_type=jnp.float32)
        parts = [o_sub[:half, :], o_sub[half:, :]]
        for ss in range(HGRN_SUB):
            ks = kh[ss:ss + 1, :]
            bs = blh[ss:ss + 1, :]
            vs = vh[ss:ss + 1, :]
            for p in range(2):
                lo = p * half
                if rev:
                    reached = lo <= ss
                    full = lo + half - 1 <= ss
                    ok = row8 + lo <= ss
                else:
                    reached = lo + half - 1 >= ss
                    full = lo >= ss
                    ok = row8 + lo >= ss
                if not reached:
                    continue
                diff = blh[lo:lo + half, :] - bs
                if not full:
                    diff = jnp.where(ok, diff, NEG_INF)
                col = jnp.sum((qh[lo:lo + half, :] * ks) * jnp.exp2(diff), axis=1, keepdims=True)
                parts[p] = parts[p] + col * vs
        o_ref[0, 0, pl.ds(r0, half), sl] = parts[0]
        o_ref[0, 0, pl.ds(r0 + half, half), sl] = parts[1]
        upd = lax.dot_general(vh.astype(jnp.bfloat16), kt.astype(jnp.bfloat16),
                              (((0,), (0,)), ((), ())), preferred_element_type=jnp.float32)
        st_ref[hh] = jnp.exp2(tot) * st + upd


def _hgrn_kernel(q_ref, v_ref, lf_ref, o_ref, st_ref, *, tb):
    d = pl.program_id(1)
    t = pl.program_id(2)
    nsub = tb // HGRN_SUB

    @pl.when(t == 0)
    def _():
        st_ref[...] = jnp.zeros_like(st_ref)

    rr = lax.broadcasted_iota(jnp.int32, (HGRN_SUB, HGRN_SUB), 0)
    cc = lax.broadcasted_iota(jnp.int32, (HGRN_SUB, HGRN_SUB), 1)

    @pl.when(d == 0)
    def _():
        tri = (cc <= rr).astype(jnp.bfloat16)

        def body(i, carry):
            r0 = pl.multiple_of(i * HGRN_SUB, HGRN_SUB)
            _hgrn_sub_chunk(q_ref, v_ref, lf_ref, o_ref, st_ref, r0, tri, False)
            return carry

        lax.fori_loop(0, nsub, body, 0, unroll=HGRN_UNROLL)

    @pl.when(d == 1)
    def _():
        tri = (cc >= rr).astype(jnp.bfloat16)

        def body(i, carry):
            r0 = pl.multiple_of((nsub - 1 - i) * HGRN_SUB, HGRN_SUB)
            _hgrn_sub_chunk(q_ref, v_ref, lf_ref, o_ref, st_ref, r0, tri, True)
            return carry

        lax.fori_loop(0, nsub, body, 0, unroll=HGRN_UNROLL)


def _hgrn_scan(q, v, lf, *, n_ctx_blocks):
    b, t, w = q.shape
    tb = HGRN_TB
    nblk = t // tb
    nlat = nblk - n_ctx_blocks

    def row_block(di, ti):
        ctx_blk = jnp.where(di == 0, nlat + ti, nblk - 1 - ti)
        lat_blk = jnp.where(di == 0, ti - n_ctx_blocks, nblk - 1 - ti)
        return jnp.where(ti < n_ctx_blocks, ctx_blk, lat_blk)

    return pl.pallas_call(
        functools.partial(_hgrn_kernel, tb=tb),
        grid=(b, 2, nblk),
        in_specs=[pl.BlockSpec((1, tb, w), lambda bi, di, ti: (bi, row_block(di, ti), 0)),
                  pl.BlockSpec((1, tb, w), lambda bi, di, ti: (bi, row_block(di, ti), 0)),
                  pl.BlockSpec((1, 1, tb, w), lambda bi, di, ti: (di, bi, row_block(di, ti), 0))],
        out_specs=pl.BlockSpec((1, 1, tb, w), lambda bi, di, ti: (di, bi, row_block(di, ti), 0)),
        out_shape=jax.ShapeDtypeStruct((2, b, t, w), jnp.float32),
        scratch_shapes=[pltpu.VMEM((C_H, C_DV, C_DK), jnp.float32)],
        compiler_params=_cparams(("arbitrary", "arbitrary", "arbitrary")),
        name="hgrn2_scan",
    )(q, v, lf)


def _head_rms(x, w, n_heads, width):
    out = []
    for hh in range(n_heads):
        blk = x[:, hh * width:(hh + 1) * width]
        out.append(_rms_rows(blk, w))
    return jnp.concatenate(out, axis=1)


def _merge_kernel(oa_ref, ob_ref, o2_ref, cg_ref, gt_ref, x_ref, ng_ref, ma_ref, mb_ref, dw_ref, hw_ref,
                  wbr_ref, wo_ref, xo_ref, ho_ref, *, diff_out_scale):
    ob = (_head_rms(ob_ref[...], dw_ref[...], B_H, B_VD) * diff_out_scale).astype(jnp.bfloat16)
    cg = cg_ref[...].astype(jnp.float32)
    oc = (_head_rms(o2_ref[0] + o2_ref[1], hw_ref[...], C_H, C_DV) * (cg * jax.nn.sigmoid(cg)))
    acc = None
    for i, o in enumerate((oa_ref[...], ob, oc.astype(jnp.bfloat16))):
        z = jnp.dot(o, wbr_ref[i], preferred_element_type=jnp.float32)
        g = jax.nn.sigmoid(gt_ref[:, i * D_MODEL:(i + 1) * D_MODEL].astype(jnp.float32))
        acc = g * z if acc is None else acc + g * z
    y = jnp.dot(acc.astype(jnp.bfloat16), wo_ref[...], preferred_element_type=jnp.float32)
    ma, mb = ma_ref[0], mb_ref[0]
    for half, (xn, mod) in enumerate(zip(_residual_tile(x_ref, y, ng_ref[1], ma, mb, 2), (ma, mb))):
        rows = slice(half * HALF_TM, (half + 1) * HALF_TM)
        xo_ref[rows, :] = xn
        ho_ref[rows, :] = _rms_rows(xn, ng_ref[2]) * (1.0 + mod[4]) + mod[3]


def _merge(oa, ob, o2, cg, gates, x, norm_g, mods, diff_w, hgrn_w, w_br, w_o, *, t_rows, seq, diff_out_scale):
    m = x.shape[0]
    tm = MM_TM
    row = lambda i: (i, 0)
    f32 = jnp.float32
    return pl.pallas_call(
        functools.partial(_merge_kernel, diff_out_scale=diff_out_scale),
        grid=(m // tm,),
        in_specs=[pl.BlockSpec((tm, BRANCH_W), row), pl.BlockSpec((tm, BRANCH_W), row),
                  pl.BlockSpec((2, tm, BRANCH_W), lambda i: (0, i, 0)), pl.BlockSpec((tm, BRANCH_W), row),
                  pl.BlockSpec((tm, GATE_W), row), pl.BlockSpec((tm, D_MODEL), row),
                  pl.BlockSpec((4, 1, D_MODEL), lambda i: (0, 0, 0))] + _mod_specs(t_rows, seq) + [
                  pl.BlockSpec((1, B_VD), lambda i: (0, 0)), pl.BlockSpec((1, C_DV), lambda i: (0, 0)),
                  pl.BlockSpec((N_BRANCH, BRANCH_W, D_MODEL), lambda i: (0, 0, 0)),
                  pl.BlockSpec((D_MODEL, D_MODEL), lambda i: (0, 0))],
        out_specs=[pl.BlockSpec((tm, D_MODEL), row), pl.BlockSpec((tm, D_MODEL), row)],
        out_shape=[jax.ShapeDtypeStruct((m, D_MODEL), f32), jax.ShapeDtypeStruct((m, D_MODEL), f32)],
        compiler_params=_cparams(("arbitrary",)),
        name="branch_merge",
    )(oa, ob, o2, cg, gates, x, norm_g, mods, mods, diff_w, hgrn_w, w_br, w_o)


def _moe_row_copy(x_hbm, xbuf, sem, src_row, n_rows, slot, dst_row):
    return pltpu.make_async_copy(x_hbm.at[pl.ds(src_row, n_rows), :],
                                 xbuf.at[slot, pl.ds(dst_row, n_rows), :], sem.at[slot])


def _moe_kernel(blk_e_ref, n_used_ref, tok_ref, tok_next_ref, x_hbm, wgu_ref, bgu_ref, wdn_ref, bdn_ref,
                y_ref, xbuf, sem, wgu_sc, wdn_sc):
    i = pl.program_id(0)
    n_used = n_used_ref[0]
    slot = i % 2
    e = blk_e_ref[i]
    e_prev = blk_e_ref[jnp.maximum(i - 1, 0)]

    def start_gather(ids_ref, dst_slot):
        for r in range(MOE_TM):
            _moe_row_copy(x_hbm, xbuf, sem, ids_ref[0, 0, r], 1, dst_slot, r).start()

    @pl.when((i == 0) & (n_used > 0))
    def _():
        start_gather(tok_ref, 0)

    @pl.when(i + 1 < n_used)
    def _():
        start_gather(tok_next_ref, 1 - slot)

    @pl.when((i == 0) | (e != e_prev))
    def _():
        wgu_sc[...] = wgu_ref[0].astype(jnp.bfloat16)
        wdn_sc[...] = wdn_ref[0].astype(jnp.bfloat16)

    @pl.when(i < n_used)
    def _():
        _moe_row_copy(x_hbm, xbuf, sem, 0, MOE_TM, slot, 0).wait()
        x = xbuf[slot].astype(jnp.bfloat16)
        gu = jnp.dot(x, wgu_sc[...], preferred_element_type=jnp.float32) + bgu_ref[0]
        g = jnp.minimum(gu[:, :EXPERT_FF], SWIGLU_LIMIT)
        u = jnp.clip(gu[:, EXPERT_FF:], -SWIGLU_LIMIT, SWIGLU_LIMIT)
        act = (u + 1.0) * (g * jax.nn.sigmoid(SWIGLU_ALPHA * g))
        y = jnp.dot(act.astype(jnp.bfloat16), wdn_sc[...], preferred_element_type=jnp.float32) + bdn_ref[0]
        y_ref[...] = y.astype(y_ref.dtype)

    @pl.when(i >= n_used)
    def _():
        y_ref[...] = jnp.zeros_like(y_ref)


def _moe_experts(blk_e, n_used, row_tok, x, w_gu, b_gu, w_dn, b_dn):
    dm = x.shape[1]
    tm = MOE_TM
    n_blk = blk_e.shape[0]
    r = n_blk * tm
    ids = row_tok.reshape(n_blk, 1, tm)
    return pl.pallas_call(
        _moe_kernel,
        grid_spec=pltpu.PrefetchScalarGridSpec(
            num_scalar_prefetch=2, grid=(n_blk,),
            in_specs=[pl.BlockSpec((1, 1, tm), lambda i, be, nu: (i, 0, 0), memory_space=pltpu.SMEM),
                      pl.BlockSpec((1, 1, tm), lambda i, be, nu: (jnp.minimum(i + 1, n_blk - 1), 0, 0),
                                   memory_space=pltpu.SMEM),
                      pl.BlockSpec(memory_space=pl.ANY),
                      pl.BlockSpec((1, dm, 2 * EXPERT_FF), lambda i, be, nu: (be[i], 0, 0)),
                      pl.BlockSpec((1, 1, 2 * EXPERT_FF), lambda i, be, nu: (be[i], 0, 0)),
                      pl.BlockSpec((1, EXPERT_FF, dm), lambda i, be, nu: (be[i], 0, 0)),
                      pl.BlockSpec((1, 1, dm), lambda i, be, nu: (be[i], 0, 0))],
            out_specs=pl.BlockSpec((tm, dm), lambda i, be, nu: (i, 0)),
            scratch_shapes=[pltpu.VMEM((2, tm, dm), jnp.float32),
                            pltpu.SemaphoreType.DMA((2,)),
                            pltpu.VMEM((dm, 2 * EXPERT_FF), jnp.bfloat16),
                            pltpu.VMEM((EXPERT_FF, dm), jnp.bfloat16)]),
        out_shape=jax.ShapeDtypeStruct((r, dm), jnp.bfloat16),
        compiler_params=_cparams(("arbitrary",)),
        name="moe_experts",
    )(blk_e, n_used, ids, ids, x, w_gu, b_gu.reshape(N_EXPERTS, 1, -1), w_dn, b_dn.reshape(N_EXPERTS, 1, -1))


def _combine_kernel(yk_ref, gt_ref, x_ref, ng_ref, ma_ref, mb_ref, xo_ref):
    gt = gt_ref[...]
    f = None
    for k in range(TOP_K):
        term = gt[:, k:k + 1] * yk_ref[:, k * D_MODEL:(k + 1) * D_MODEL].astype(jnp.float32)
        f = term if f is None else f + term
    for half, xn in enumerate(_residual_tile(x_ref, f, ng_ref[3], ma_ref[0], mb_ref[0], 5)):
        xo_ref[half * HALF_TM:(half + 1) * HALF_TM, :] = xn


def _moe_combine(yk, gates, x, norm_g, mods, *, t_rows, seq):
    m = x.shape[0]
    tm = MM_TM
    row = lambda i: (i, 0)
    return pl.pallas_call(
        _combine_kernel,
        grid=(m // tm,),
        in_specs=[pl.BlockSpec((tm, TOP_K * D_MODEL), row), pl.BlockSpec((tm, TOP_K), row),
                  pl.BlockSpec((tm, D_MODEL), row),
                  pl.BlockSpec((4, 1, D_MODEL), lambda i: (0, 0, 0))] + _mod_specs(t_rows, seq),
        out_specs=pl.BlockSpec((tm, D_MODEL), row),
        out_shape=jax.ShapeDtypeStruct((m, D_MODEL), jnp.float32),
        compiler_params=_cparams(("arbitrary",)),
        name="moe_combine",
    )(yk, gates, x, norm_g, mods, mods)


def _moe_ffn(h, x, norm_g, mods, w_router, b_router, w_gu, b_gu, w_dn, b_dn, *, t_rows, seq):
    n, dm = h.shape
    nk = n * TOP_K
    w_r = jnp.zeros((dm, LANES), jnp.float32).at[:, :N_EXPERTS].set(w_router)
    logits = _matmul(h, w_r, tm=MM_TM, tn=LANES, out_dtype=jnp.float32,
                     precision=lax.Precision.HIGHEST, name="router")[:, :N_EXPERTS] + b_router
    top_v, top_e = lax.top_k(logits, TOP_K)
    gates = jax.nn.softmax(top_v, axis=-1)
    flat_e = top_e.reshape(-1).astype(jnp.int32)
    experts = jnp.arange(N_EXPERTS, dtype=jnp.int32)
    onehot = (flat_e[:, None] == experts[None, :]).astype(jnp.int32)
    incl = jnp.cumsum(onehot, axis=0)
    rank = jnp.sum(incl * onehot, axis=1) - 1
    counts = incl[-1]
    padded = (counts + MOE_TM - 1) // MOE_TM * MOE_TM
    pad_end = jnp.cumsum(padded)
    pad_start = pad_end - padded
    grp_start = jnp.cumsum(counts) - counts
    dest = jnp.sum(onehot * pad_start[None, :], axis=1) + rank
    n_blk = (nk + MOE_TM - 1) // MOE_TM + N_EXPERTS
    rows = n_blk * MOE_TM
    blk_start = jnp.arange(n_blk, dtype=jnp.int32) * MOE_TM
    blk_e = jnp.minimum(jnp.sum((pad_end[None, :] <= blk_start[:, None]).astype(jnp.int32), axis=1),
                        N_EXPERTS - 1)
    n_used = (pad_end[-1] // MOE_TM).astype(jnp.int32).reshape(1)
    order = jnp.argsort(flat_e)
    row_e = jnp.repeat(blk_e, MOE_TM)
    sorted_pos = jnp.arange(rows, dtype=jnp.int32) - pad_start[row_e] + grp_start[row_e]
    row_tok = (order[jnp.clip(sorted_pos, 0, nk - 1)] // TOP_K).astype(jnp.int32)
    y = _moe_experts(blk_e, n_used, row_tok, h, w_gu, b_gu, w_dn, b_dn)
    yk = lax.optimization_barrier(y[dest]).reshape(n, TOP_K * dm)
    return _moe_combine(yk, gates, x, norm_g, mods, t_rows=t_rows, seq=seq)


def _rope_tables(rows, n_ctx, batch):
    row = jnp.repeat(jnp.arange(rows, dtype=jnp.float32), GRID_W)
    col = jnp.tile(jnp.arange(GRID_W, dtype=jnp.float32), rows)
    n_freq = HEAD_DIM // 4
    inv = ROPE_THETA ** (-jnp.arange(n_freq, dtype=jnp.float32) / n_freq)
    ang_r = row[:, None] * inv
    ang_c = col[:, None] * inv
    cos_h = jnp.concatenate([jnp.cos(ang_r), jnp.cos(ang_r), jnp.cos(ang_c), jnp.cos(ang_c)], axis=1)
    sin_h = jnp.concatenate([-jnp.sin(ang_r), jnp.sin(ang_r), -jnp.sin(ang_c), jnp.sin(ang_c)], axis=1)
    cos_t = jnp.concatenate([cos_h, jnp.ones((n_ctx, HEAD_DIM), jnp.float32)], axis=0)
    sin_t = jnp.concatenate([sin_h, jnp.zeros((n_ctx, HEAD_DIM), jnp.float32)], axis=0)
    reps = (batch, LANES // HEAD_DIM)
    return jnp.tile(cos_t, reps), jnp.tile(sin_t, reps)


def _rope_partner(n_cols):
    idx = jnp.arange(n_cols)
    quarter = HEAD_DIM // 4
    return jnp.where((idx % (2 * quarter)) < quarter, idx + quarter, idx - quarter)


def kernel(x, c, ctx, c_ctx, w_mod, b_mod, norm_g, w_in, attn_sink, diff_lambda, diff_norm_w,
           hgrn_lb_logits, hgrn_norm_w, w_branch, w_out, w_router, b_router, w_gate_up, b_gate_up,
           w_down, b_down):
    b, s, dm = x.shape
    l = ctx.shape[1]
    t = s + l
    f32 = jnp.float32
    bf16 = jnp.bfloat16
    cos_t, sin_t = _rope_tables(s // GRID_W, l, b)
    lb_cum = jnp.cumsum(jax.nn.softmax(hgrn_lb_logits.astype(f32), axis=0), axis=0)
    lower_bounds = lb_cum - lb_cum[0:1]
    cond_rows = 16
    cond = jnp.zeros((cond_rows, dm), f32).at[:b].set(jax.nn.silu(c)).at[b].set(jax.nn.silu(c_ctx))
    scale = HEAD_DIM ** -0.5
    xa = jnp.concatenate([x, ctx], axis=1).reshape(b * t, dm)
    h_moe = None
    for layer in range(DEPTH):
        lam_init = 0.8 - 0.6 * math.exp(-0.3 * layer)
        mod_all = _matmul(cond, w_mod[layer], tm=cond_rows, tn=3 * dm, out_dtype=f32,
                          precision=lax.Precision.HIGHEST, name="adaln") + b_mod[layer]
        mod = mod_all[:b].reshape(b, 1, 6, 1, dm)
        mod_c = jnp.broadcast_to(mod_all[b:b + 1].reshape(1, 1, 6, 1, dm), (b, 1, 6, 1, dm))
        mods = jnp.concatenate([mod, mod_c], axis=1).reshape(2 * b, 6, 1, dm)
        ng = norm_g[layer].reshape(4, 1, dm)
        g0 = ng[0:1]
        geo = dict(t_rows=t, seq=s)

        w = w_in[layer]
        w_a = w[:, OFF_A:OFF_A + A_W].at[:, :A_Q_W].multiply(scale).astype(bf16)
        w_b = w[:, OFF_B:OFF_B + B_W].at[:, :B_QK_W].multiply(scale * LOG2_E).astype(bf16)
        proj_a = _proj_rope(xa, g0, mods, w_a, w_a[:, _rope_partner(A_ROPE_W)], cos_t, sin_t,
                            rope_w=A_ROPE_W, name="proj_a", **geo).reshape(b, t, A_W)
        proj_b = _proj_rope(xa, g0, mods, w_b, w_b[:, _rope_partner(B_ROPE_W)], cos_t, sin_t,
                            rope_w=B_ROPE_W, name="proj_b", **geo).reshape(b, t, B_W)
        lb = lower_bounds[layer]
        lb_par = jnp.stack([jnp.log(lb[0]), jnp.log1p(-lb[0]), jnp.log(lb[1]), jnp.log1p(-lb[1])])
        cq, ci, lf, cg = _proj_hgrn(xa, g0, mods, w[:, OFF_C:OFF_C + C_W].astype(bf16),
                                    lb_par.reshape(4, 1, C_K_W), **geo)
        gates = _proj_plain(xa, g0, mods, w[:, OFF_GATE:].astype(bf16), name="gate_proj", **geo)

        sink32 = attn_sink[layer].astype(f32)
        o_a = _window_attention(proj_a, sink32, seq=s)
        o_a = _context_attention(proj_a, sink32, o_a, seq=s)

        lp = diff_lambda[layer].astype(f32)
        lam = (jnp.exp(jnp.sum(lp[0] * lp[1])) - jnp.exp(jnp.sum(lp[2] * lp[3])) + lam_init).reshape(1).astype(f32)
        o_b = _diff_attention(lam, proj_b, None, tq=FLASH_TQ, tk=FLASH_TK, q_block_offset=0,
                              n_q_blocks=s // FLASH_TQ, k_block_offset=0, nk=t // FLASH_TK)
        o_b = _diff_attention(lam, proj_b, o_b, tq=l, tk=l, q_block_offset=s // l, n_q_blocks=1,
                              k_block_offset=s // l, nk=1)

        o2 = _hgrn_scan(cq.reshape(b, t, C_K_W), ci.reshape(b, t, C_V_W), lf.reshape(2, b, t, C_K_W),
                        n_ctx_blocks=l // HGRN_TB)

        xa, h_moe = _merge(o_a.reshape(b * t, A_Q_W), o_b.reshape(b * t, B_V_W), o2.reshape(2, b * t, C_V_W), cg,
                           gates, xa, ng, mods, diff_norm_w[layer].reshape(1, B_VD).astype(f32),
                           hgrn_norm_w[layer].reshape(1, C_DV).astype(f32),
                           w_branch[layer].astype(bf16), w_out[layer].astype(bf16),
                           diff_out_scale=1 - lam_init, **geo)

        xa = _moe_ffn(h_moe, xa, ng, mods, w_router[layer], b_router[layer], w_gate_up[layer], b_gate_up[layer],
                      w_down[layer], b_down[layer], **geo)
    return xa.reshape(b, t, dm)[:, :s]
```

```python
import functools
import math

import jax
import jax.numpy as jnp
from jax import lax
from jax.experimental import pallas as pl
from jax.experimental.pallas import tpu as pltpu

D_MODEL = 1024
DEPTH = 2
GRID_W = 64
HEAD_DIM = 64
ROPE_THETA = 10000.0
RMS_EPS = 1e-6
NEG_INF = -1e30
BLOCK = 128
A_HQ = 8
A_HKV = 2
A_GROUP = A_HQ // A_HKV
WINDOW = 128
B_H = 4
B_HD = HEAD_DIM
B_VD = 2 * HEAD_DIM
C_H = 4
C_DK = 128
C_DV = 128
N_BRANCH = 3
BRANCH_W = 512
A_Q_W = A_HQ * HEAD_DIM
A_KV_W = A_HKV * HEAD_DIM
B_QK_W = B_H * 2 * B_HD
B_V_W = B_H * B_VD
C_K_W = C_H * C_DK
C_V_W = C_H * C_DV
GATE_W = N_BRANCH * D_MODEL
N_EXPERTS = 32
TOP_K = 4
EXPERT_FF = D_MODEL
SWIGLU_LIMIT = 7.0
SWIGLU_ALPHA = 1.702

A_W = A_Q_W + 2 * A_KV_W
A_ROPE_W = A_Q_W + A_KV_W
B_W = 2 * B_QK_W + B_V_W
B_ROPE_W = 2 * B_QK_W
C_W = 3 * C_K_W + 2 * C_V_W
OFF_A = 0
OFF_B = OFF_A + A_W
OFF_C = OFF_B + B_W
OFF_GATE = OFF_C + C_W

V7X_VMEM_LIMIT_BYTES = 56 * 1024 * 1024
LANES = 128
LOG2_E = math.log2(math.e)

MM_TM = 512
HALF_TM = MM_TM // 2
FLASH_TQ = 512
FLASH_TK = 768
HGRN_TB = 256
HGRN_SUB = 16
HGRN_UNROLL = 4
MOE_TM = 512


def _cparams(sem):
    return pltpu.CompilerParams(dimension_semantics=sem, vmem_limit_bytes=V7X_VMEM_LIMIT_BYTES)


def _rms_rows(x, g):
    return (x * lax.rsqrt(jnp.mean(x * x, axis=-1, keepdims=True) + RMS_EPS)) * g


def _modulated_tile(x_ref, g, mod_a, mod_b, shift_slot, scale_slot):
    out = []
    for half, mod in enumerate((mod_a, mod_b)):
        x = x_ref[half * HALF_TM:(half + 1) * HALF_TM, :]
        out.append(_rms_rows(x, g) * (1.0 + mod[scale_slot]) + mod[shift_slot])
    return jnp.concatenate(out, axis=0)


def _residual_tile(x_ref, y, g, mod_a, mod_b, gate_slot):
    out = []
    for half, mod in enumerate((mod_a, mod_b)):
        rows = slice(half * HALF_TM, (half + 1) * HALF_TM)
        out.append(x_ref[rows, :] + mod[gate_slot] * _rms_rows(y[rows, :], g))
    return out


def _mod_specs(t_rows, seq):
    halves_per_batch = t_rows // HALF_TM
    lat_halves = seq // HALF_TM

    def row(u):
        return (u // halves_per_batch) * 2 + jnp.where(u % halves_per_batch >= lat_halves, 1, 0)

    blk = (1, 6, 1, D_MODEL)
    return [pl.BlockSpec(blk, lambda i: (row(2 * i), 0, 0, 0)),
            pl.BlockSpec(blk, lambda i: (row(2 * i + 1), 0, 0, 0))]


def _proj_rope_kernel(x_ref, g_ref, ma_ref, mb_ref, w_ref, wsw_ref, cos_ref, sin_ref, o_ref, *, rope_w):
    h = _modulated_tile(x_ref, g_ref[0], ma_ref[0], mb_ref[0], 0, 1).astype(jnp.bfloat16)
    z = jnp.dot(h, w_ref[...], preferred_element_type=jnp.float32)
    zs = jnp.dot(h, wsw_ref[...], preferred_element_type=jnp.float32)
    reps = rope_w // LANES
    rot = z[:, :rope_w] * jnp.tile(cos_ref[...], (1, reps)) + zs * jnp.tile(sin_ref[...], (1, reps))
    o_ref[:, :rope_w] = rot.astype(o_ref.dtype)
    o_ref[:, rope_w:] = z[:, rope_w:].astype(o_ref.dtype)


def _proj_plain_kernel(x_ref, g_ref, ma_ref, mb_ref, w_ref, o_ref):
    h = _modulated_tile(x_ref, g_ref[0], ma_ref[0], mb_ref[0], 0, 1).astype(jnp.bfloat16)
    o_ref[...] = jnp.dot(h, w_ref[...], preferred_element_type=jnp.float32).astype(o_ref.dtype)


def _log_forget(z, log_lb, log_1m_lb):
    log_sig = jnp.minimum(z, 0.0) - jnp.log(1.0 + jnp.exp(-jnp.abs(z)))
    c = log_1m_lb + log_sig
    return jnp.maximum(log_lb, c) + jnp.log(1.0 + jnp.exp(-jnp.abs(log_lb - c)))


def _proj_hgrn_kernel(x_ref, g_ref, ma_ref, mb_ref, w_ref, lb_ref, q_ref, v_ref, lf_ref, cg_ref):
    h = _modulated_tile(x_ref, g_ref[0], ma_ref[0], mb_ref[0], 0, 1).astype(jnp.bfloat16)
    z = jnp.dot(h, w_ref[...], preferred_element_type=jnp.float32)
    zq = z[:, :C_K_W]
    q_ref[...] = zq * jax.nn.sigmoid(zq)
    v_ref[...] = z[:, C_K_W:C_K_W + C_V_W]
    off = C_K_W + C_V_W
    lf_ref[0] = _log_forget(z[:, off:off + C_K_W], lb_ref[0], lb_ref[1])
    lf_ref[1] = _log_forget(z[:, off + C_K_W:off + 2 * C_K_W], lb_ref[2], lb_ref[3])
    cg_ref[...] = z[:, off + 2 * C_K_W:].astype(cg_ref.dtype)


def _proj_common_specs(t_rows, seq):
    row = lambda i: (i, 0)
    return [pl.BlockSpec((MM_TM, D_MODEL), row),
            pl.BlockSpec((1, 1, D_MODEL), lambda i: (0, 0, 0))] + _mod_specs(t_rows, seq)


def _proj_rope(x, g, mods, w, wsw, cos, sin, *, t_rows, seq, rope_w, name):
    m = x.shape[0]
    n = w.shape[1]
    row = lambda i: (i, 0)
    whole = lambda i: (0, 0)
    return pl.pallas_call(
        functools.partial(_proj_rope_kernel, rope_w=rope_w),
        grid=(m // MM_TM,),
        in_specs=_proj_common_specs(t_rows, seq) + [
            pl.BlockSpec((D_MODEL, n), whole), pl.BlockSpec((D_MODEL, rope_w), whole),
            pl.BlockSpec((MM_TM, LANES), row), pl.BlockSpec((MM_TM, LANES), row)],
        out_specs=pl.BlockSpec((MM_TM, n), row),
        out_shape=jax.ShapeDtypeStruct((m, n), jnp.bfloat16),
        compiler_params=_cparams(("arbitrary",)),
        name=name,
    )(x, g, mods, mods, w, wsw, cos, sin)


def _proj_plain(x, g, mods, w, *, t_rows, seq, name):
    m = x.shape[0]
    n = w.shape[1]
    row = lambda i: (i, 0)
    return pl.pallas_call(
        _proj_plain_kernel,
        grid=(m // MM_TM,),
        in_specs=_proj_common_specs(t_rows, seq) + [pl.BlockSpec((D_MODEL, n), lambda i: (0, 0))],
        out_specs=pl.BlockSpec((MM_TM, n), row),
        out_shape=jax.ShapeDtypeStruct((m, n), jnp.bfloat16),
        compiler_params=_cparams(("arbitrary",)),
        name=name,
    )(x, g, mods, mods, w)


def _proj_hgrn(x, g, mods, w, lb_par, *, t_rows, seq):
    m = x.shape[0]
    row = lambda i: (i, 0)
    f32 = jnp.float32
    return pl.pallas_call(
        _proj_hgrn_kernel,
        grid=(m // MM_TM,),
        in_specs=_proj_common_specs(t_rows, seq) + [
            pl.BlockSpec((D_MODEL, C_W), lambda i: (0, 0)),
            pl.BlockSpec((4, 1, C_K_W), lambda i: (0, 0, 0))],
        out_specs=[pl.BlockSpec((MM_TM, C_K_W), row), pl.BlockSpec((MM_TM, C_V_W), row),
                   pl.BlockSpec((2, MM_TM, C_K_W), lambda i: (0, i, 0)), pl.BlockSpec((MM_TM, C_V_W), row)],
        out_shape=[jax.ShapeDtypeStruct((m, C_K_W), f32), jax.ShapeDtypeStruct((m, C_V_W), f32),
                   jax.ShapeDtypeStruct((2, m, C_K_W), f32), jax.ShapeDtypeStruct((m, C_V_W), jnp.bfloat16)],
        compiler_params=_cparams(("arbitrary",)),
        name="hgrn_proj",
    )(x, g, mods, mods, w, lb_par)


def _mm_kernel(x_ref, w_ref, o_ref, *, precision):
    o_ref[...] = jnp.dot(x_ref[...], w_ref[...], preferred_element_type=jnp.float32,
                         precision=precision).astype(o_ref.dtype)


def _matmul(x, w, *, tm, tn, out_dtype, precision=None, name):
    m, k = x.shape
    n = w.shape[1]
    assert m % tm == 0 and n % tn == 0, (m, tm, n, tn)
    return pl.pallas_call(
        functools.partial(_mm_kernel, precision=precision),
        grid=(n // tn, m // tm),
        in_specs=[pl.BlockSpec((tm, k), lambda j, i: (i, 0)),
                  pl.BlockSpec((k, tn), lambda j, i: (0, j))],
        out_specs=pl.BlockSpec((tm, tn), lambda j, i: (i, j)),
        out_shape=jax.ShapeDtypeStruct((m, n), out_dtype),
        compiler_params=_cparams(("arbitrary", "arbitrary")),
        name=name,
    )(x, w)


A_K_BLK = A_Q_W // A_KV_W
A_V_BLK = A_K_BLK + 1


def _sink_attend(qh, kh, vh, mask, sk):
    s = lax.dot_general(qh, kh, (((1,), (1,)), ((), ())), preferred_element_type=jnp.float32)
    if mask is not None:
        s = jnp.where(mask, s, NEG_INF)
    m = jnp.maximum(jnp.max(s, axis=-1, keepdims=True), sk)
    p = jnp.exp(s - m)
    l = jnp.sum(p, axis=-1, keepdims=True) + jnp.exp(sk - m)
    o = jnp.dot(p.astype(vh.dtype), vh, preferred_element_type=jnp.float32)
    return o / l


def _attend_heads(sink_ref, q_ref, k_all, v_all, mask, o_ref):
    for hq in range(A_HQ):
        hk = hq // A_GROUP
        o = _sink_attend(q_ref[0, :, hq * HEAD_DIM:(hq + 1) * HEAD_DIM],
                         k_all[:, hk * HEAD_DIM:(hk + 1) * HEAD_DIM],
                         v_all[:, hk * HEAD_DIM:(hk + 1) * HEAD_DIM], mask, sink_ref[hq])
        o_ref[0, :, hq * HEAD_DIM:(hq + 1) * HEAD_DIM] = o.astype(o_ref.dtype)


def _attn_a_kernel(sink_ref, q_ref, kp_ref, kc_ref, kn_ref, vp_ref, vc_ref, vn_ref, kx_ref, vx_ref,
                   o_ref, *, seq):
    n = pl.program_id(1)
    k_all = jnp.concatenate([kp_ref[0], kc_ref[0], kn_ref[0], kx_ref[0]], axis=0)
    v_all = jnp.concatenate([vp_ref[0], vc_ref[0], vn_ref[0], vx_ref[0]], axis=0)
    n_keys = k_all.shape[0]
    qi = lax.broadcasted_iota(jnp.int32, (BLOCK, n_keys), 0)
    kj = lax.broadcasted_iota(jnp.int32, (BLOCK, n_keys), 1)
    kpos = n * BLOCK + kj - BLOCK
    local_ok = (jnp.abs(kj - BLOCK - qi) <= WINDOW) & (kpos >= 0) & (kpos < seq) & (n * BLOCK < seq)
    mask = local_ok | (kj >= 3 * BLOCK)
    _attend_heads(sink_ref, q_ref, k_all, v_all, mask, o_ref)


def _window_attention(proj_a, sink, *, seq):
    b, t, _ = proj_a.shape
    l = t - seq
    nb = seq // BLOCK
    ctx_blk = seq // l
    q_map = lambda bi, n, *_: (bi, n, 0)
    kv = lambda col, fn: pl.BlockSpec((1, BLOCK, A_KV_W), lambda bi, n, *_: (bi, fn(n), col))
    prev = lambda n: jnp.clip(n - 1, 0, nb - 1)
    cur = lambda n: jnp.minimum(n, nb - 1)
    nxt = lambda n: jnp.minimum(n + 1, nb - 1)
    ctx = lambda col: pl.BlockSpec((1, l, A_KV_W), lambda bi, n, *_: (bi, ctx_blk, col))
    return pl.pallas_call(
        functools.partial(_attn_a_kernel, seq=seq),
        grid_spec=pltpu.PrefetchScalarGridSpec(
            num_scalar_prefetch=1, grid=(b, t // BLOCK),
            in_specs=[pl.BlockSpec((1, BLOCK, A_Q_W), q_map),
                      kv(A_K_BLK, prev), kv(A_K_BLK, cur), kv(A_K_BLK, nxt),
                      kv(A_V_BLK, prev), kv(A_V_BLK, cur), kv(A_V_BLK, nxt),
                      ctx(A_K_BLK), ctx(A_V_BLK)],
            out_specs=pl.BlockSpec((1, BLOCK, A_Q_W), q_map)),
        out_shape=jax.ShapeDtypeStruct((b, t, A_Q_W), jnp.bfloat16),
        compiler_params=_cparams(("arbitrary", "arbitrary")),
        name="window_gqa",
    )(sink, *([proj_a] * 9))


def _diff_flash_kernel(lam_ref, q_ref, k_ref, v_ref, o_ref, q2_sc, v1_sc, sa_sc, sb_sc, m_sc, acc_sc,
                       *, tq, tk, nk):
    j = pl.program_id(3)

    @pl.when(j == 0)
    def _():
        q = q_ref[0]
        lane = lax.broadcasted_iota(jnp.int32, q.shape, 1)
        q2_sc[:tq, :] = jnp.where(lane < B_HD, q, jnp.zeros_like(q))
        q2_sc[tq:, :] = jnp.where(lane >= B_HD, q, jnp.zeros_like(q))
        m_sc[...] = jnp.full_like(m_sc, NEG_INF)
        acc_sc[...] = jnp.zeros_like(acc_sc)
        v1_sc[:, B_VD:] = jnp.ones((tk, B_VD), v1_sc.dtype)
        sb_sc[...] = jnp.full_like(sb_sc, NEG_INF)

    def step(s_new_ref, s_old_ref):
        v1_sc[:, :B_VD] = v_ref[0]
        m_prev = m_sc[...]
        m_new = jnp.maximum(m_prev, jnp.max(s_old_ref[...], axis=-1, keepdims=True))
        alpha = jnp.exp2(m_prev - m_new)
        p = jnp.exp2(s_old_ref[...] - jnp.tile(m_new, (1, tk // LANES))).astype(jnp.bfloat16)
        acc_sc[...] = jnp.tile(alpha, (1, 2)) * acc_sc[...] + jnp.dot(
            p, v1_sc[...], preferred_element_type=jnp.float32)
        m_sc[...] = m_new
        s_new_ref[...] = lax.dot_general(q2_sc[...], k_ref[0], (((1,), (1,)), ((), ())),
                                         preferred_element_type=jnp.float32)

    @pl.when(j % 2 == 0)
    def _():
        step(sa_sc, sb_sc)

    @pl.when(j % 2 == 1)
    def _():
        step(sb_sc, sa_sc)

    @pl.when(j == nk)
    def _():
        acc = acc_sc[...]
        o = acc[:, :B_VD] / acc[:, B_VD:]
        o_ref[0] = (o[:tq, :] - lam_ref[0] * o[tq:, :]).astype(o_ref.dtype)


def _diff_attention(lam, proj_b, *, tq, tk, q_block_offset, n_q_blocks, k_block_offset, nk):
    b = proj_b.shape[0]
    k_col = B_QK_W // LANES
    v_col = 2 * B_QK_W // LANES
    in_specs = [pl.BlockSpec((1, tq, LANES), lambda bi, h, i, j, *_: (bi, i + q_block_offset, h)),
                pl.BlockSpec((1, tk, LANES),
                             lambda bi, h, i, j, *_: (bi, jnp.minimum(j, nk - 1) + k_block_offset, k_col + h)),
                pl.BlockSpec((1, tk, LANES),
                             lambda bi, h, i, j, *_: (bi, jnp.maximum(j - 1, 0) + k_block_offset, v_col + h))]
    return pl.pallas_call(
        functools.partial(_diff_flash_kernel, tq=tq, tk=tk, nk=nk),
        grid_spec=pltpu.PrefetchScalarGridSpec(
            num_scalar_prefetch=1, grid=(b, B_H, n_q_blocks, nk + 1),
            in_specs=in_specs,
            out_specs=pl.BlockSpec((1, tq, B_VD), lambda bi, h, i, j, *_: (bi, i, h)),
            scratch_shapes=[pltpu.VMEM((2 * tq, 2 * B_HD), jnp.bfloat16),
                            pltpu.VMEM((tk, 2 * B_VD), jnp.bfloat16),
                            pltpu.VMEM((2 * tq, tk), jnp.float32),
                            pltpu.VMEM((2 * tq, tk), jnp.float32),
                            pltpu.VMEM((2 * tq, LANES), jnp.float32),
                            pltpu.VMEM((2 * tq, 2 * B_VD), jnp.float32)]),
        out_shape=jax.ShapeDtypeStruct((b, n_q_blocks * tq, B_V_W), jnp.float32),
        compiler_params=_cparams(("arbitrary", "arbitrary", "arbitrary", "arbitrary")),
        name="diff_flash",
    )(lam, proj_b, proj_b, proj_b)


def _hgrn_sub_chunk(q_ref, v_ref, lf_ref, o_ref, st_ref, r0, tri, rev):
    half = HGRN_SUB // 2
    row8 = lax.broadcasted_iota(jnp.int32, (half, C_DK), 0)
    lf = lf_ref[0, 0, pl.ds(r0, HGRN_SUB), :] * LOG2_E
    lf_a = lf.astype(jnp.bfloat16)
    rem = lf - lf_a.astype(jnp.float32)
    lf_b = rem.astype(jnp.bfloat16)
    lf_c = (rem - lf_b.astype(jnp.float32)).astype(jnp.bfloat16)
    bl = (jnp.dot(tri, lf_a, preferred_element_type=jnp.float32)
          + jnp.dot(tri, lf_b, preferred_element_type=jnp.float32)
          + jnp.dot(tri, lf_c, preferred_element_type=jnp.float32))
    q = q_ref[0, pl.ds(r0, HGRN_SUB), :]
    v = v_ref[0, pl.ds(r0, HGRN_SUB), :]
    last = 0 if rev else HGRN_SUB - 1
    for hh in range(C_H):
        sl = slice(hh * C_DK, (hh + 1) * C_DK)
        lfh, blh, qh, vh = lf[:, sl], bl[:, sl], q[:, sl], v[:, sl]
        tot = blh[last:last + 1, :]
        kh = 1.0 - jnp.exp2(lfh)
        qt = qh * jnp.exp2(blh)
        kt = kh * jnp.exp2(tot - blh)
        st = st_ref[hh]
        o_sub = lax.dot_general(qt.astype(jnp.bfloat16), st.astype(jnp.bfloat16),
                                (((1,), (1,)), ((), ())), preferred_element_type=jnp.float32)
        parts = [o_sub[:half, :], o_sub[half:, :]]
        for ss in range(HGRN_SUB):
            ks = kh[ss:ss + 1, :]
            bs = blh[ss:ss + 1, :]
            vs = vh[ss:ss + 1, :]
            for p in range(2):
                lo = p * half
                if rev:
                    reached = lo <= ss
                    full = lo + half - 1 <= ss
                    ok = row8 + lo <= ss
                else:
                    reached = lo + half - 1 >= ss
                    full = lo >= ss
                    ok = row8 + lo >= ss
                if not reached:
                    continue
                diff = blh[lo:lo + half, :] - bs
                if not full:
                    diff = jnp.where(ok, diff, NEG_INF)
                col = jnp.sum((qh[lo:lo + half, :] * ks) * jnp.exp2(diff), axis=1, keepdims=True)
                parts[p] = parts[p] + col * vs
        o_ref[0, 0, pl.ds(r0, half), sl] = parts[0]
        o_ref[0, 0, pl.ds(r0 + half, half), sl] = parts[1]
        upd = lax.dot_general(vh.astype(jnp.bfloat16), kt.astype(jnp.bfloat16),
                              (((0,), (0,)), ((), ())), preferred_element_type=jnp.float32)
        st_ref[hh] = jnp.exp2(tot) * st + upd


def _hgrn_kernel(q_ref, v_ref, lf_ref, o_ref, st_ref, *, tb):
    d = pl.program_id(1)
    t = pl.program_id(2)
    nsub = tb // HGRN_SUB

    @pl.when(t == 0)
    def _():
        st_ref[...] = jnp.zeros_like(st_ref)

    rr = lax.broadcasted_iota(jnp.int32, (HGRN_SUB, HGRN_SUB), 0)
    cc = lax.broadcasted_iota(jnp.int32, (HGRN_SUB, HGRN_SUB), 1)

    @pl.when(d == 0)
    def _():
        tri = (cc <= rr).astype(jnp.bfloat16)

        def body(i, carry):
            r0 = pl.multiple_of(i * HGRN_SUB, HGRN_SUB)
            _hgrn_sub_chunk(q_ref, v_ref, lf_ref, o_ref, st_ref, r0, tri, False)
            return carry

        lax.fori_loop(0, nsub, body, 0, unroll=HGRN_UNROLL)

    @pl.when(d == 1)
    def _():
        tri = (cc >= rr).astype(jnp.bfloat16)

        def body(i, carry):
            r0 = pl.multiple_of((nsub - 1 - i) * HGRN_SUB, HGRN_SUB)
            _hgrn_sub_chunk(q_ref, v_ref, lf_ref, o_ref, st_ref, r0, tri, True)
            return carry

        lax.fori_loop(0, nsub, body, 0, unroll=HGRN_UNROLL)


def _hgrn_scan(q, v, lf, *, n_ctx_blocks):
    b, t, w = q.shape
    tb = HGRN_TB
    nblk = t // tb
    nlat = nblk - n_ctx_blocks

    def row_block(di, ti):
        ctx_blk = jnp.where(di == 0, nlat + ti, nblk - 1 - ti)
        lat_blk = jnp.where(di == 0, ti - n_ctx_blocks, nblk - 1 - ti)
        return jnp.where(ti < n_ctx_blocks, ctx_blk, lat_blk)

    return pl.pallas_call(
        functools.partial(_hgrn_kernel, tb=tb),
        grid=(b, 2, nblk),
        in_specs=[pl.BlockSpec((1, tb, w), lambda bi, di, ti: (bi, row_block(di, ti), 0)),
                  pl.BlockSpec((1, tb, w), lambda bi, di, ti: (bi, row_block(di, ti), 0)),
                  pl.BlockSpec((1, 1, tb, w), lambda bi, di, ti: (di, bi, row_block(di, ti), 0))],
        out_specs=pl.BlockSpec((1, 1, tb, w), lambda bi, di, ti: (di, bi, row_block(di, ti), 0)),
        out_shape=jax.ShapeDtypeStruct((2, b, t, w), jnp.float32),
        scratch_shapes=[pltpu.VMEM((C_H, C_DV, C_DK), jnp.float32)],
        compiler_params=_cparams(("arbitrary", "arbitrary", "arbitrary")),
        name="hgrn2_scan",
    )(q, v, lf)


def _head_rms(x, w, n_heads, width):
    out = []
    for hh in range(n_heads):
        blk = x[:, hh * width:(hh + 1) * width]
        out.append(_rms_rows(blk, w))
    return jnp.concatenate(out, axis=1)


def _merge_kernel(oa_ref, ob_ref, o2_ref, cg_ref, gt_ref, x_ref, ng_ref, ma_ref, mb_ref, dw_ref, hw_ref,
                  wbr_ref, wo_ref, xo_ref, ho_ref, *, diff_out_scale):
    ob = (_head_rms(ob_ref[...], dw_ref[...], B_H, B_VD) * diff_out_scale).astype(jnp.bfloat16)
    cg = cg_ref[...].astype(jnp.float32)
    oc = (_head_rms(o2_ref[0] + o2_ref[1], hw_ref[...], C_H, C_DV) * (cg * jax.nn.sigmoid(cg)))
    acc = None
    for i, o in enumerate((oa_ref[...], ob, oc.astype(jnp.bfloat16))):
        z = jnp.dot(o, wbr_ref[i], preferred_element_type=jnp.float32)
        g = jax.nn.sigmoid(gt_ref[:, i * D_MODEL:(i + 1) * D_MODEL].astype(jnp.float32))
        acc = g * z if acc is None else acc + g * z
    y = jnp.dot(acc.astype(jnp.bfloat16), wo_ref[...], preferred_element_type=jnp.float32)
    ma, mb = ma_ref[0], mb_ref[0]
    for half, (xn, mod) in enumerate(zip(_residual_tile(x_ref, y, ng_ref[1], ma, mb, 2), (ma, mb))):
        rows = slice(half * HALF_TM, (half + 1) * HALF_TM)
        xo_ref[rows, :] = xn
        ho_ref[rows, :] = _rms_rows(xn, ng_ref[2]) * (1.0 + mod[4]) + mod[3]


def _merge(oa, ob, o2, cg, gates, x, norm_g, mods, diff_w, hgrn_w, w_br, w_o, *, t_rows, seq, diff_out_scale):
    m = x.shape[0]
    tm = MM_TM
    row = lambda i: (i, 0)
    f32 = jnp.float32
    return pl.pallas_call(
        functools.partial(_merge_kernel, diff_out_scale=diff_out_scale),
        grid=(m // tm,),
        in_specs=[pl.BlockSpec((tm, BRANCH_W), row), pl.BlockSpec((tm, BRANCH_W), row),
                  pl.BlockSpec((2, tm, BRANCH_W), lambda i: (0, i, 0)), pl.BlockSpec((tm, BRANCH_W), row),
                  pl.BlockSpec((tm, GATE_W), row), pl.BlockSpec((tm, D_MODEL), row),
                  pl.BlockSpec((4, 1, D_MODEL), lambda i: (0, 0, 0))] + _mod_specs(t_rows, seq) + [
                  pl.BlockSpec((1, B_VD), lambda i: (0, 0)), pl.BlockSpec((1, C_DV), lambda i: (0, 0)),
                  pl.BlockSpec((N_BRANCH, BRANCH_W, D_MODEL), lambda i: (0, 0, 0)),
                  pl.BlockSpec((D_MODEL, D_MODEL), lambda i: (0, 0))],
        out_specs=[pl.BlockSpec((tm, D_MODEL), row), pl.BlockSpec((tm, D_MODEL), row)],
        out_shape=[jax.ShapeDtypeStruct((m, D_MODEL), f32), jax.ShapeDtypeStruct((m, D_MODEL), f32)],
        compiler_params=_cparams(("arbitrary",)),
        name="branch_merge",
    )(oa, ob, o2, cg, gates, x, norm_g, mods, mods, diff_w, hgrn_w, w_br, w_o)


def _moe_row_copy(x_hbm, xbuf, sem, src_row, n_rows, slot, dst_row):
    return pltpu.make_async_copy(x_hbm.at[pl.ds(src_row, n_rows), :],
                                 xbuf.at[slot, pl.ds(dst_row, n_rows), :], sem.at[slot])


def _moe_kernel(blk_e_ref, n_used_ref, tok_ref, tok_next_ref, x_hbm, wgu_ref, bgu_ref, wdn_ref, bdn_ref,
                y_ref, xbuf, sem, wgu_sc, wdn_sc):
    i = pl.program_id(0)
    n_used = n_used_ref[0]
    slot = i % 2
    e = blk_e_ref[i]
    e_prev = blk_e_ref[jnp.maximum(i - 1, 0)]

    def start_gather(ids_ref, dst_slot):
        for r in range(MOE_TM):
            _moe_row_copy(x_hbm, xbuf, sem, ids_ref[0, 0, r], 1, dst_slot, r).start()

    @pl.when((i == 0) & (n_used > 0))
    def _():
        start_gather(tok_ref, 0)

    @pl.when(i + 1 < n_used)
    def _():
        start_gather(tok_next_ref, 1 - slot)

    @pl.when((i == 0) | (e != e_prev))
    def _():
        wgu_sc[...] = wgu_ref[0, 0].astype(jnp.bfloat16)
        wdn_sc[...] = wdn_ref[0, 0].astype(jnp.bfloat16)

    @pl.when(i < n_used)
    def _():
        _moe_row_copy(x_hbm, xbuf, sem, 0, MOE_TM, slot, 0).wait()
        x = xbuf[slot].astype(jnp.bfloat16)
        gu = jnp.dot(x, wgu_sc[...], preferred_element_type=jnp.float32) + bgu_ref[0, 0]
        g = jnp.minimum(gu[:, :EXPERT_FF], SWIGLU_LIMIT)
        u = jnp.clip(gu[:, EXPERT_FF:], -SWIGLU_LIMIT, SWIGLU_LIMIT)
        act = (u + 1.0) * (g * jax.nn.sigmoid(SWIGLU_ALPHA * g))
        y = jnp.dot(act.astype(jnp.bfloat16), wdn_sc[...], preferred_element_type=jnp.float32) + bdn_ref[0, 0]
        y_ref[...] = y.astype(y_ref.dtype)

    @pl.when(i >= n_used)
    def _():
        y_ref[...] = jnp.zeros_like(y_ref)


def _moe_experts(blk_e, n_used, row_tok, x, w_gu, b_gu, w_dn, b_dn, *, layer):
    dm = x.shape[1]
    tm = MOE_TM
    n_blk = blk_e.shape[0]
    r = n_blk * tm
    ids = row_tok.reshape(n_blk, 1, tm)
    return pl.pallas_call(
        _moe_kernel,
        grid_spec=pltpu.PrefetchScalarGridSpec(
            num_scalar_prefetch=2, grid=(n_blk,),
            in_specs=[pl.BlockSpec((1, 1, tm), lambda i, be, nu: (i, 0, 0), memory_space=pltpu.SMEM),
                      pl.BlockSpec((1, 1, tm), lambda i, be, nu: (jnp.minimum(i + 1, n_blk - 1), 0, 0),
                                   memory_space=pltpu.SMEM),
                      pl.BlockSpec(memory_space=pl.ANY),
                      pl.BlockSpec((1, 1, dm, 2 * EXPERT_FF), lambda i, be, nu: (layer, be[i], 0, 0)),
                      pl.BlockSpec((1, 1, 1, 2 * EXPERT_FF), lambda i, be, nu: (layer, be[i], 0, 0)),
                      pl.BlockSpec((1, 1, EXPERT_FF, dm), lambda i, be, nu: (layer, be[i], 0, 0)),
                      pl.BlockSpec((1, 1, 1, dm), lambda i, be, nu: (layer, be[i], 0, 0))],
            out_specs=pl.BlockSpec((tm, dm), lambda i, be, nu: (i, 0)),
            scratch_shapes=[pltpu.VMEM((2, tm, dm), jnp.float32),
                            pltpu.SemaphoreType.DMA((2,)),
                            pltpu.VMEM((dm, 2 * EXPERT_FF), jnp.bfloat16),
                            pltpu.VMEM((EXPERT_FF, dm), jnp.bfloat16)]),
        out_shape=jax.ShapeDtypeStruct((r, dm), jnp.bfloat16),
        compiler_params=_cparams(("arbitrary",)),
        name="moe_experts",
    )(blk_e, n_used, ids, ids, x, w_gu, b_gu.reshape(DEPTH, N_EXPERTS, 1, -1), w_dn,
      b_dn.reshape(DEPTH, N_EXPERTS, 1, -1))


def _combine_kernel(*refs):
    yk_refs = refs[:TOP_K]
    gt_ref, x_ref, ng_ref, ma_ref, mb_ref, xo_ref = refs[TOP_K:]
    gt = gt_ref[...]
    f = None
    for k in range(TOP_K):
        term = gt[:, k:k + 1] * yk_refs[k][...].astype(jnp.float32)
        f = term if f is None else f + term
    for half, xn in enumerate(_residual_tile(x_ref, f, ng_ref[3], ma_ref[0], mb_ref[0], 5)):
        xo_ref[half * HALF_TM:(half + 1) * HALF_TM, :] = xn


def _moe_combine(yk, gates, x, norm_g, mods, *, t_rows, seq):
    m = x.shape[0]
    tm = MM_TM
    n_tiles = m // tm
    row = lambda i: (i, 0)
    choice = lambda k: pl.BlockSpec((tm, D_MODEL), lambda i: (k * n_tiles + i, 0))
    return pl.pallas_call(
        _combine_kernel,
        grid=(n_tiles,),
        in_specs=[choice(k) for k in range(TOP_K)] + [
            pl.BlockSpec((tm, TOP_K), row), pl.BlockSpec((tm, D_MODEL), row),
            pl.BlockSpec((4, 1, D_MODEL), lambda i: (0, 0, 0))] + _mod_specs(t_rows, seq),
        out_specs=pl.BlockSpec((tm, D_MODEL), row),
        out_shape=jax.ShapeDtypeStruct((m, D_MODEL), jnp.float32),
        compiler_params=_cparams(("arbitrary",)),
        name="moe_combine",
    )(*([yk] * TOP_K), gates, x, norm_g, mods, mods)


def _moe_ffn(h, x, norm_g, mods, w_router, b_router, w_gu, b_gu, w_dn, b_dn, *, layer, t_rows, seq):
    n, dm = h.shape
    nk = n * TOP_K
    w_r = jnp.zeros((dm, LANES), jnp.float32).at[:, :N_EXPERTS].set(w_router)
    logits = _matmul(h, w_r, tm=MM_TM, tn=LANES, out_dtype=jnp.float32,
                     precision=lax.Precision.HIGHEST, name="router")[:, :N_EXPERTS] + b_router
    top_v, top_e = lax.top_k(logits, TOP_K)
    gates = jax.nn.softmax(top_v, axis=-1)
    flat_e = top_e.reshape(-1).astype(jnp.int32)
    experts = jnp.arange(N_EXPERTS, dtype=jnp.int32)
    onehot = (flat_e[:, None] == experts[None, :]).astype(jnp.int32)
    incl = jnp.cumsum(onehot, axis=0)
    rank = jnp.sum(incl * onehot, axis=1) - 1
    counts = incl[-1]
    padded = (counts + MOE_TM - 1) // MOE_TM * MOE_TM
    pad_end = jnp.cumsum(padded)
    pad_start = pad_end - padded
    grp_start = jnp.cumsum(counts) - counts
    dest = jnp.sum(onehot * pad_start[None, :], axis=1) + rank
    n_blk = (nk + MOE_TM - 1) // MOE_TM + N_EXPERTS
    rows = n_blk * MOE_TM
    blk_start = jnp.arange(n_blk, dtype=jnp.int32) * MOE_TM
    blk_e = jnp.minimum(jnp.sum((pad_end[None, :] <= blk_start[:, None]).astype(jnp.int32), axis=1),
                        N_EXPERTS - 1)
    n_used = (pad_end[-1] // MOE_TM).astype(jnp.int32).reshape(1)
    order = jnp.argsort(flat_e)
    row_e = jnp.repeat(blk_e, MOE_TM)
    sorted_pos = jnp.arange(rows, dtype=jnp.int32) - pad_start[row_e] + grp_start[row_e]
    row_tok = (order[jnp.clip(sorted_pos, 0, nk - 1)] // TOP_K).astype(jnp.int32)
    y = _moe_experts(blk_e, n_used, row_tok, h, w_gu, b_gu, w_dn, b_dn, layer=layer)
    yk = y[dest.reshape(n, TOP_K).T.reshape(-1)]
    return _moe_combine(yk, gates, x, norm_g, mods, t_rows=t_rows, seq=seq)


def _rope_tables(rows, n_ctx, batch):
    row = jnp.repeat(jnp.arange(rows, dtype=jnp.float32), GRID_W)
    col = jnp.tile(jnp.arange(GRID_W, dtype=jnp.float32), rows)
    n_freq = HEAD_DIM // 4
    inv = ROPE_THETA ** (-jnp.arange(n_freq, dtype=jnp.float32) / n_freq)
    ang_r = row[:, None] * inv
    ang_c = col[:, None] * inv
    cos_h = jnp.concatenate([jnp.cos(ang_r), jnp.cos(ang_r), jnp.cos(ang_c), jnp.cos(ang_c)], axis=1)
    sin_h = jnp.concatenate([-jnp.sin(ang_r), jnp.sin(ang_r), -jnp.sin(ang_c), jnp.sin(ang_c)], axis=1)
    cos_t = jnp.concatenate([cos_h, jnp.ones((n_ctx, HEAD_DIM), jnp.float32)], axis=0)
    sin_t = jnp.concatenate([sin_h, jnp.zeros((n_ctx, HEAD_DIM), jnp.float32)], axis=0)
    reps = (batch, LANES // HEAD_DIM)
    return jnp.tile(cos_t, reps), jnp.tile(sin_t, reps)


def _rope_partner(n_cols):
    idx = jnp.arange(n_cols)
    quarter = HEAD_DIM // 4
    return jnp.where((idx % (2 * quarter)) < quarter, idx + quarter, idx - quarter)


def kernel(x, c, ctx, c_ctx, w_mod, b_mod, norm_g, w_in, attn_sink, diff_lambda, diff_norm_w,
           hgrn_lb_logits, hgrn_norm_w, w_branch, w_out, w_router, b_router, w_gate_up, b_gate_up,
           w_down, b_down):
    b, s, dm = x.shape
    l = ctx.shape[1]
    t = s + l
    f32 = jnp.float32
    bf16 = jnp.bfloat16
    cos_t, sin_t = _rope_tables(s // GRID_W, l, b)
    lb_cum = jnp.cumsum(jax.nn.softmax(hgrn_lb_logits.astype(f32), axis=0), axis=0)
    lower_bounds = lb_cum - lb_cum[0:1]
    cond_rows = 16
    cond = jnp.zeros((cond_rows, dm), f32).at[:b].set(jax.nn.silu(c)).at[b].set(jax.nn.silu(c_ctx))
    scale = HEAD_DIM ** -0.5
    xa = jnp.concatenate([x, ctx], axis=1).reshape(b * t, dm)
    h_moe = None
    for layer in range(DEPTH):
        lam_init = 0.8 - 0.6 * math.exp(-0.3 * layer)
        mod_all = _matmul(cond, w_mod[layer], tm=cond_rows, tn=3 * dm, out_dtype=f32,
                          precision=lax.Precision.HIGHEST, name="adaln") + b_mod[layer]
        mod = mod_all[:b].reshape(b, 1, 6, 1, dm)
        mod_c = jnp.broadcast_to(mod_all[b:b + 1].reshape(1, 1, 6, 1, dm), (b, 1, 6, 1, dm))
        mods = jnp.concatenate([mod, mod_c], axis=1).reshape(2 * b, 6, 1, dm)
        ng = norm_g[layer].reshape(4, 1, dm)
        g0 = ng[0:1]
        geo = dict(t_rows=t, seq=s)

        w = w_in[layer]
        w_a = w[:, OFF_A:OFF_A + A_W].at[:, :A_Q_W].multiply(scale).astype(bf16)
        w_b = w[:, OFF_B:OFF_B + B_W].at[:, :B_QK_W].multiply(scale * LOG2_E).astype(bf16)
        proj_a = _proj_rope(xa, g0, mods, w_a, w_a[:, _rope_partner(A_ROPE_W)], cos_t, sin_t,
                            rope_w=A_ROPE_W, name="proj_a", **geo).reshape(b, t, A_W)
        proj_b = _proj_rope(xa, g0, mods, w_b, w_b[:, _rope_partner(B_ROPE_W)], cos_t, sin_t,
                            rope_w=B_ROPE_W, name="proj_b", **geo).reshape(b, t, B_W)
        lb = lower_bounds[layer]
        lb_par = jnp.stack([jnp.log(lb[0]), jnp.log1p(-lb[0]), jnp.log(lb[1]), jnp.log1p(-lb[1])])
        cq, ci, lf, cg = _proj_hgrn(xa, g0, mods, w[:, OFF_C:OFF_C + C_W].astype(bf16),
                                    lb_par.reshape(4, 1, C_K_W), **geo)
        gates = _proj_plain(xa, g0, mods, w[:, OFF_GATE:].astype(bf16), name="gate_proj", **geo)

        sink32 = attn_sink[layer].astype(f32)
        o_a = _window_attention(proj_a, sink32, seq=s)

        lp = diff_lambda[layer].astype(f32)
        lam = (jnp.exp(jnp.sum(lp[0] * lp[1])) - jnp.exp(jnp.sum(lp[2] * lp[3])) + lam_init).reshape(1).astype(f32)
        o_b = _diff_attention(lam, proj_b, tq=FLASH_TQ, tk=FLASH_TK, q_block_offset=0,
                              n_q_blocks=s // FLASH_TQ, k_block_offset=0, nk=t // FLASH_TK)
        o_bx = _diff_attention(lam, proj_b, tq=l, tk=l, q_block_offset=s // l, n_q_blocks=1,
                               k_block_offset=s // l, nk=1)
        o_b = jnp.concatenate([o_b, o_bx], axis=1)

        o2 = _hgrn_scan(cq.reshape(b, t, C_K_W), ci.reshape(b, t, C_V_W), lf.reshape(2, b, t, C_K_W),
                        n_ctx_blocks=l // HGRN_TB)

        xa, h_moe = _merge(o_a.reshape(b * t, A_Q_W), o_b.reshape(b * t, B_V_W), o2.reshape(2, b * t, C_V_W), cg,
                           gates, xa, ng, mods, diff_norm_w[layer].reshape(1, B_VD).astype(f32),
                           hgrn_norm_w[layer].reshape(1, C_DV).astype(f32),
                           w_branch[layer].astype(bf16), w_out[layer].astype(bf16),
                           diff_out_scale=1 - lam_init, **geo)

        xa = _moe_ffn(h_moe, xa, ng, mods, w_router[layer], b_router[layer], w_gate_up, b_gate_up,
                      w_down, b_down, layer=layer, **geo)
    return xa.reshape(b, t, dm)[:, :s]
```

```python
import functools
import math

import jax
import jax.numpy as jnp
from jax import lax
from jax.experimental import pallas as pl
from jax.experimental.pallas import tpu as pltpu

D_MODEL = 1024
DEPTH = 2
GRID_W = 64
HEAD_DIM = 64
ROPE_THETA = 10000.0
RMS_EPS = 1e-6
NEG_INF = -1e30
BLOCK = 128
A_HQ = 8
A_HKV = 2
A_GROUP = A_HQ // A_HKV
WINDOW = 128
B_H = 4
B_HD = HEAD_DIM
B_VD = 2 * HEAD_DIM
C_H = 4
C_DK = 128
C_DV = 128
N_BRANCH = 3
BRANCH_W = 512
A_Q_W = A_HQ * HEAD_DIM
A_KV_W = A_HKV * HEAD_DIM
B_QK_W = B_H * 2 * B_HD
B_V_W = B_H * B_VD
C_K_W = C_H * C_DK
C_V_W = C_H * C_DV
GATE_W = N_BRANCH * D_MODEL
N_EXPERTS = 32
TOP_K = 4
EXPERT_FF = D_MODEL
SWIGLU_LIMIT = 7.0
SWIGLU_ALPHA = 1.702

A_W = A_Q_W + 2 * A_KV_W
A_ROPE_W = A_Q_W + A_KV_W
B_W = 2 * B_QK_W + B_V_W
B_ROPE_W = 2 * B_QK_W
C_W = 3 * C_K_W + 2 * C_V_W
OFF_A = 0
OFF_B = OFF_A + A_W
OFF_C = OFF_B + B_W
OFF_GATE = OFF_C + C_W

V7X_VMEM_LIMIT_BYTES = 56 * 1024 * 1024
LANES = 128
LOG2_E = math.log2(math.e)

MM_TM = 512
HALF_TM = MM_TM // 2
FLASH_TQ = 512
FLASH_TK = 768
HGRN_TB = 256
HGRN_SUB = 16
HGRN_UNROLL = 4
MOE_TM = 512


def _cparams(sem):
    return pltpu.CompilerParams(dimension_semantics=sem, vmem_limit_bytes=V7X_VMEM_LIMIT_BYTES)


def _rms_rows(x, g):
    return (x * lax.rsqrt(jnp.mean(x * x, axis=-1, keepdims=True) + RMS_EPS)) * g


def _modulated_tile(x_ref, g, mod_a, mod_b, shift_slot, scale_slot):
    out = []
    for half, mod in enumerate((mod_a, mod_b)):
        x = x_ref[half * HALF_TM:(half + 1) * HALF_TM, :]
        out.append(_rms_rows(x, g) * (1.0 + mod[scale_slot]) + mod[shift_slot])
    return jnp.concatenate(out, axis=0)


def _residual_tile(x_ref, y, g, mod_a, mod_b, gate_slot):
    out = []
    for half, mod in enumerate((mod_a, mod_b)):
        rows = slice(half * HALF_TM, (half + 1) * HALF_TM)
        out.append(x_ref[rows, :] + mod[gate_slot] * _rms_rows(y[rows, :], g))
    return out


def _mod_specs(t_rows, seq):
    halves_per_batch = t_rows // HALF_TM
    lat_halves = seq // HALF_TM

    def row(u):
        return (u // halves_per_batch) * 2 + jnp.where(u % halves_per_batch >= lat_halves, 1, 0)

    blk = (1, 6, 1, D_MODEL)
    return [pl.BlockSpec(blk, lambda i: (row(2 * i), 0, 0, 0)),
            pl.BlockSpec(blk, lambda i: (row(2 * i + 1), 0, 0, 0))]


def _proj_rope_kernel(x_ref, g_ref, ma_ref, mb_ref, w_ref, wsw_ref, cos_ref, sin_ref, o_ref, *, rope_w):
    h = _modulated_tile(x_ref, g_ref[0], ma_ref[0], mb_ref[0], 0, 1).astype(jnp.bfloat16)
    z = jnp.dot(h, w_ref[...], preferred_element_type=jnp.float32)
    zs = jnp.dot(h, wsw_ref[...], preferred_element_type=jnp.float32)
    reps = rope_w // LANES
    rot = z[:, :rope_w] * jnp.tile(cos_ref[...], (1, reps)) + zs * jnp.tile(sin_ref[...], (1, reps))
    o_ref[:, :rope_w] = rot.astype(o_ref.dtype)
    o_ref[:, rope_w:] = z[:, rope_w:].astype(o_ref.dtype)


def _proj_b_kernel(x_ref, g_ref, ma_ref, mb_ref, w_ref, wsw_ref, cos_ref, sin_ref, q_ref, k_ref, v1_ref):
    h = _modulated_tile(x_ref, g_ref[0], ma_ref[0], mb_ref[0], 0, 1).astype(jnp.bfloat16)
    z = jnp.dot(h, w_ref[...], preferred_element_type=jnp.float32)
    zs = jnp.dot(h, wsw_ref[...], preferred_element_type=jnp.float32)
    reps = B_ROPE_W // LANES
    rot = z[:, :B_ROPE_W] * jnp.tile(cos_ref[...], (1, reps)) + zs * jnp.tile(sin_ref[...], (1, reps))
    q_ref[...] = rot[:, :B_QK_W].astype(q_ref.dtype)
    for hh in range(B_H):
        k_ref[hh] = rot[:, B_QK_W + hh * 2 * B_HD:B_QK_W + (hh + 1) * 2 * B_HD].astype(k_ref.dtype)
        v1_ref[hh, :, :B_VD] = z[:, B_ROPE_W + hh * B_VD:B_ROPE_W + (hh + 1) * B_VD].astype(v1_ref.dtype)
        v1_ref[hh, :, B_VD:] = jnp.ones((MM_TM, B_VD), v1_ref.dtype)


def _proj_plain_kernel(x_ref, g_ref, ma_ref, mb_ref, w_ref, o_ref):
    h = _modulated_tile(x_ref, g_ref[0], ma_ref[0], mb_ref[0], 0, 1).astype(jnp.bfloat16)
    o_ref[...] = jnp.dot(h, w_ref[...], preferred_element_type=jnp.float32).astype(o_ref.dtype)


def _log_forget(z, log_lb, log_1m_lb):
    log_sig = jnp.minimum(z, 0.0) - jnp.log(1.0 + jnp.exp(-jnp.abs(z)))
    c = log_1m_lb + log_sig
    return jnp.maximum(log_lb, c) + jnp.log(1.0 + jnp.exp(-jnp.abs(log_lb - c)))


def _proj_hgrn_kernel(x_ref, g_ref, ma_ref, mb_ref, w_ref, lb_ref, q_ref, v_ref, lf_ref, cg_ref):
    h = _modulated_tile(x_ref, g_ref[0], ma_ref[0], mb_ref[0], 0, 1).astype(jnp.bfloat16)
    z = jnp.dot(h, w_ref[...], preferred_element_type=jnp.float32)
    zq = z[:, :C_K_W]
    q_ref[...] = zq * jax.nn.sigmoid(zq)
    v_ref[...] = z[:, C_K_W:C_K_W + C_V_W]
    off = C_K_W + C_V_W
    lf_ref[0] = _log_forget(z[:, off:off + C_K_W], lb_ref[0], lb_ref[1])
    lf_ref[1] = _log_forget(z[:, off + C_K_W:off + 2 * C_K_W], lb_ref[2], lb_ref[3])
    cg_ref[...] = z[:, off + 2 * C_K_W:].astype(cg_ref.dtype)


def _proj_common_specs(t_rows, seq):
    row = lambda i: (i, 0)
    return [pl.BlockSpec((MM_TM, D_MODEL), row),
            pl.BlockSpec((1, 1, D_MODEL), lambda i: (0, 0, 0))] + _mod_specs(t_rows, seq)


def _proj_rope(x, g, mods, w, wsw, cos, sin, *, t_rows, seq, rope_w, name):
    m = x.shape[0]
    n = w.shape[1]
    row = lambda i: (i, 0)
    whole = lambda i: (0, 0)
    return pl.pallas_call(
        functools.partial(_proj_rope_kernel, rope_w=rope_w),
        grid=(m // MM_TM,),
        in_specs=_proj_common_specs(t_rows, seq) + [
            pl.BlockSpec((D_MODEL, n), whole), pl.BlockSpec((D_MODEL, rope_w), whole),
            pl.BlockSpec((MM_TM, LANES), row), pl.BlockSpec((MM_TM, LANES), row)],
        out_specs=pl.BlockSpec((MM_TM, n), row),
        out_shape=jax.ShapeDtypeStruct((m, n), jnp.bfloat16),
        compiler_params=_cparams(("arbitrary",)),
        name=name,
    )(x, g, mods, mods, w, wsw, cos, sin)


def _proj_b(x, g, mods, w, wsw, cos, sin, *, t_rows, seq):
    m = x.shape[0]
    row = lambda i: (i, 0)
    whole = lambda i: (0, 0)
    heads = lambda i: (0, i, 0)
    bf16 = jnp.bfloat16
    return pl.pallas_call(
        _proj_b_kernel,
        grid=(m // MM_TM,),
        in_specs=_proj_common_specs(t_rows, seq) + [
            pl.BlockSpec((D_MODEL, B_W), whole), pl.BlockSpec((D_MODEL, B_ROPE_W), whole),
            pl.BlockSpec((MM_TM, LANES), row), pl.BlockSpec((MM_TM, LANES), row)],
        out_specs=[pl.BlockSpec((MM_TM, B_QK_W), row), pl.BlockSpec((B_H, MM_TM, 2 * B_HD), heads),
                   pl.BlockSpec((B_H, MM_TM, 2 * B_VD), heads)],
        out_shape=[jax.ShapeDtypeStruct((m, B_QK_W), bf16), jax.ShapeDtypeStruct((B_H, m, 2 * B_HD), bf16),
                   jax.ShapeDtypeStruct((B_H, m, 2 * B_VD), bf16)],
        compiler_params=_cparams(("arbitrary",)),
        name="proj_b",
    )(x, g, mods, mods, w, wsw, cos, sin)


def _proj_plain(x, g, mods, w, *, t_rows, seq, name):
    m = x.shape[0]
    n = w.shape[1]
    row = lambda i: (i, 0)
    return pl.pallas_call(
        _proj_plain_kernel,
        grid=(m // MM_TM,),
        in_specs=_proj_common_specs(t_rows, seq) + [pl.BlockSpec((D_MODEL, n), lambda i: (0, 0))],
        out_specs=pl.BlockSpec((MM_TM, n), row),
        out_shape=jax.ShapeDtypeStruct((m, n), jnp.bfloat16),
        compiler_params=_cparams(("arbitrary",)),
        name=name,
    )(x, g, mods, mods, w)


def _proj_hgrn(x, g, mods, w, lb_par, *, t_rows, seq):
    m = x.shape[0]
    row = lambda i: (i, 0)
    f32 = jnp.float32
    return pl.pallas_call(
        _proj_hgrn_kernel,
        grid=(m // MM_TM,),
        in_specs=_proj_common_specs(t_rows, seq) + [
            pl.BlockSpec((D_MODEL, C_W), lambda i: (0, 0)),
            pl.BlockSpec((4, 1, C_K_W), lambda i: (0, 0, 0))],
        out_specs=[pl.BlockSpec((MM_TM, C_K_W), row), pl.BlockSpec((MM_TM, C_V_W), row),
                   pl.BlockSpec((2, MM_TM, C_K_W), lambda i: (0, i, 0)), pl.BlockSpec((MM_TM, C_V_W), row)],
        out_shape=[jax.ShapeDtypeStruct((m, C_K_W), f32), jax.ShapeDtypeStruct((m, C_V_W), f32),
                   jax.ShapeDtypeStruct((2, m, C_K_W), f32), jax.ShapeDtypeStruct((m, C_V_W), jnp.bfloat16)],
        compiler_params=_cparams(("arbitrary",)),
        name="hgrn_proj",
    )(x, g, mods, mods, w, lb_par)


def _mm_kernel(x_ref, w_ref, o_ref, *, precision):
    o_ref[...] = jnp.dot(x_ref[...], w_ref[...], preferred_element_type=jnp.float32,
                         precision=precision).astype(o_ref.dtype)


def _matmul(x, w, *, tm, tn, out_dtype, precision=None, name):
    m, k = x.shape
    n = w.shape[1]
    assert m % tm == 0 and n % tn == 0, (m, tm, n, tn)
    return pl.pallas_call(
        functools.partial(_mm_kernel, precision=precision),
        grid=(n // tn, m // tm),
        in_specs=[pl.BlockSpec((tm, k), lambda j, i: (i, 0)),
                  pl.BlockSpec((k, tn), lambda j, i: (0, j))],
        out_specs=pl.BlockSpec((tm, tn), lambda j, i: (i, j)),
        out_shape=jax.ShapeDtypeStruct((m, n), out_dtype),
        compiler_params=_cparams(("arbitrary", "arbitrary")),
        name=name,
    )(x, w)


A_Q8_W = A_HQ * LANES
A_W2 = A_Q8_W + 2 * A_KV_W
A_ROPE_W2 = A_Q8_W + A_KV_W
A_K_BLK = A_Q8_W // A_KV_W
A_V_BLK = A_K_BLK + 1


def _attn_a_kernel(sink_ref, q_ref, kp_ref, kc_ref, kn_ref, vp_ref, vc_ref, vn_ref, kx_ref, vx_ref,
                   o_ref, *, seq):
    n = pl.program_id(1)
    k_all = jnp.concatenate([kp_ref[0], kc_ref[0], kn_ref[0], kx_ref[0]], axis=0)
    v_all = jnp.concatenate([vp_ref[0], vc_ref[0], vn_ref[0], vx_ref[0]], axis=0)
    n_keys = k_all.shape[0]
    qi = lax.broadcasted_iota(jnp.int32, (BLOCK, n_keys), 0)
    kj = lax.broadcasted_iota(jnp.int32, (BLOCK, n_keys), 1)
    kpos = n * BLOCK + kj - BLOCK
    local_ok = (jnp.abs(kj - BLOCK - qi) <= WINDOW) & (kpos >= 0) & (kpos < seq) & (n * BLOCK < seq)
    mask = local_ok | (kj >= 3 * BLOCK)
    q8 = jnp.concatenate([q_ref[0, :, hq * LANES:(hq + 1) * LANES] for hq in range(A_HQ)], axis=0)
    s = lax.dot_general(q8, k_all, (((1,), (1,)), ((), ())), preferred_element_type=jnp.float32)
    p_blocks, inv_l = [], []
    for hq in range(A_HQ):
        sh = jnp.where(mask, s[hq * BLOCK:(hq + 1) * BLOCK, :], NEG_INF)
        sk = sink_ref[hq]
        m = jnp.maximum(jnp.max(sh, axis=-1, keepdims=True), sk)
        p = jnp.exp(sh - m)
        inv_l.append(1.0 / (jnp.sum(p, axis=-1, keepdims=True) + jnp.exp(sk - m)))
        p_blocks.append(p.astype(v_all.dtype))
    o = jnp.dot(jnp.concatenate(p_blocks, axis=0), v_all, preferred_element_type=jnp.float32)
    lane = lax.broadcasted_iota(jnp.int32, (BLOCK, LANES), 1)
    for g in range(A_GROUP):
        lo = o[g * BLOCK:(g + 1) * BLOCK, :] * inv_l[g]
        hi = o[(A_GROUP + g) * BLOCK:(A_GROUP + g + 1) * BLOCK, :] * inv_l[A_GROUP + g]
        o_ref[0, :, g * LANES:(g + 1) * LANES] = jnp.where(lane < HEAD_DIM, lo, hi).astype(o_ref.dtype)


def _window_attention(proj_a, sink, *, seq):
    b, t, _ = proj_a.shape
    l = t - seq
    nb = seq // BLOCK
    ctx_blk = seq // l
    q_map = lambda bi, n, *_: (bi, n, 0)
    kv = lambda col, fn: pl.BlockSpec((1, BLOCK, A_KV_W), lambda bi, n, *_: (bi, fn(n), col))
    prev = lambda n: jnp.clip(n - 1, 0, nb - 1)
    cur = lambda n: jnp.minimum(n, nb - 1)
    nxt = lambda n: jnp.minimum(n + 1, nb - 1)
    ctx = lambda col: pl.BlockSpec((1, l, A_KV_W), lambda bi, n, *_: (bi, ctx_blk, col))
    return pl.pallas_call(
        functools.partial(_attn_a_kernel, seq=seq),
        grid_spec=pltpu.PrefetchScalarGridSpec(
            num_scalar_prefetch=1, grid=(b, t // BLOCK),
            in_specs=[pl.BlockSpec((1, BLOCK, A_Q8_W), q_map),
                      kv(A_K_BLK, prev), kv(A_K_BLK, cur), kv(A_K_BLK, nxt),
                      kv(A_V_BLK, prev), kv(A_V_BLK, cur), kv(A_V_BLK, nxt),
                      ctx(A_K_BLK), ctx(A_V_BLK)],
            out_specs=pl.BlockSpec((1, BLOCK, A_Q_W), q_map)),
        out_shape=jax.ShapeDtypeStruct((b, t, A_Q_W), jnp.bfloat16),
        compiler_params=_cparams(("arbitrary", "arbitrary")),
        name="window_gqa",
    )(sink, *([proj_a] * 9))


def _diff_flash_kernel(lam_ref, q_ref, k_ref, v_ref, o_ref, q2_sc, sa_sc, sb_sc, m_sc, acc_sc, *, tq, tk, nk):
    u = pl.program_id(2)
    new_q = u % nk == 0

    @pl.when(u == 0)
    def _():
        m_sc[...] = jnp.full_like(m_sc, NEG_INF)
        acc_sc[...] = jnp.zeros_like(acc_sc)
        sb_sc[...] = jnp.full_like(sb_sc, NEG_INF)

    @pl.when(new_q)
    def _():
        q = q_ref[0]
        lane = lax.broadcasted_iota(jnp.int32, q.shape, 1)
        q2_sc[:tq, :] = jnp.where(lane < B_HD, q, jnp.zeros_like(q))
        q2_sc[tq:, :] = jnp.where(lane >= B_HD, q, jnp.zeros_like(q))

    def step(s_new_ref, s_old_ref):
        m_prev = m_sc[...]
        m_new = jnp.maximum(m_prev, jnp.max(s_old_ref[...], axis=-1, keepdims=True))
        alpha = jnp.exp2(m_prev - m_new)
        p = jnp.exp2(s_old_ref[...] - jnp.tile(m_new, (1, tk // LANES))).astype(jnp.bfloat16)
        acc_sc[...] = jnp.tile(alpha, (1, 2)) * acc_sc[...] + jnp.dot(
            p, v_ref[0], preferred_element_type=jnp.float32)
        m_sc[...] = m_new
        s_new_ref[...] = lax.dot_general(q2_sc[...], k_ref[0], (((1,), (1,)), ((), ())),
                                         preferred_element_type=jnp.float32)

    @pl.when(u % 2 == 0)
    def _():
        step(sa_sc, sb_sc)

    @pl.when(u % 2 == 1)
    def _():
        step(sb_sc, sa_sc)

    @pl.when(new_q & (u > 0))
    def _():
        acc = acc_sc[...]
        o = acc[:, :B_VD] / acc[:, B_VD:]
        o_ref[0] = (o[:tq, :] - lam_ref[0] * o[tq:, :]).astype(o_ref.dtype)
        m_sc[...] = jnp.full_like(m_sc, NEG_INF)
        acc_sc[...] = jnp.zeros_like(acc_sc)


def _diff_attention(lam, q, k_hm, v1_hm, *, tq, tk, q_block_offset, n_q_blocks, k_block_offset, nk):
    b, t, _ = q.shape
    kb = t // tk
    n_steps = n_q_blocks * nk + 1
    k_blk = lambda bi, u: bi * kb + k_block_offset + u % nk
    in_specs = [pl.BlockSpec((1, tq, 2 * B_HD),
                             lambda bi, h, u, *_: (bi, jnp.minimum(u // nk, n_q_blocks - 1) + q_block_offset, h)),
                pl.BlockSpec((1, tk, 2 * B_HD), lambda bi, h, u, *_: (h, k_blk(bi, u), 0)),
                pl.BlockSpec((1, tk, 2 * B_VD), lambda bi, h, u, *_: (h, k_blk(bi, jnp.maximum(u - 1, 0)), 0))]
    return pl.pallas_call(
        functools.partial(_diff_flash_kernel, tq=tq, tk=tk, nk=nk),
        grid_spec=pltpu.PrefetchScalarGridSpec(
            num_scalar_prefetch=1, grid=(b, B_H, n_steps),
            in_specs=in_specs,
            out_specs=pl.BlockSpec((1, tq, B_VD), lambda bi, h, u, *_: (bi, jnp.maximum(u - 1, 0) // nk, h)),
            scratch_shapes=[pltpu.VMEM((2 * tq, 2 * B_HD), jnp.bfloat16),
                            pltpu.VMEM((2 * tq, tk), jnp.float32),
                            pltpu.VMEM((2 * tq, tk), jnp.float32),
                            pltpu.VMEM((2 * tq, LANES), jnp.float32),
                            pltpu.VMEM((2 * tq, 2 * B_VD), jnp.float32)]),
        out_shape=jax.ShapeDtypeStruct((b, n_q_blocks * tq, B_V_W), jnp.float32),
        compiler_params=_cparams(("arbitrary", "arbitrary", "arbitrary")),
        name="diff_flash",
    )(lam, q, k_hm, v1_hm)


def _hgrn_sub_chunk(q_ref, v_ref, lf_ref, o_ref, st_ref, r0, tri, rev):
    half = HGRN_SUB // 2
    row8 = lax.broadcasted_iota(jnp.int32, (half, C_DK), 0)
    lf = lf_ref[0, 0, pl.ds(r0, HGRN_SUB), :] * LOG2_E
    lf_a = lf.astype(jnp.bfloat16)
    rem = lf - lf_a.astype(jnp.float32)
    lf_b = rem.astype(jnp.bfloat16)
    lf_c = (rem - lf_b.astype(jnp.float32)).astype(jnp.bfloat16)
    bl = (jnp.dot(tri, lf_a, preferred_element_type=jnp.float32)
          + jnp.dot(tri, lf_b, preferred_element_type=jnp.float32)
          + jnp.dot(tri, lf_c, preferred_element_type=jnp.float32))
    q = q_ref[0, pl.ds(r0, HGRN_SUB), :]
    v = v_ref[0, pl.ds(r0, HGRN_SUB), :]
    last = 0 if rev else HGRN_SUB - 1
    for hh in range(C_H):
        sl = slice(hh * C_DK, (hh + 1) * C_DK)
        lfh, blh, qh, vh = lf[:, sl], bl[:, sl], q[:, sl], v[:, sl]
        tot = blh[last:last + 1, :]
        kh = 1.0 - jnp.exp2(lfh)
        qt = qh * jnp.exp2(blh)
        kt = kh * jnp.exp2(tot - blh)
        st = st_ref[hh]
        o_sub = lax.dot_general(qt.astype(jnp.bfloat16), st.astype(jnp.bfloat16),
                                (((1,), (1,)), ((), ())), preferred_element_type=jnp.float32)
        parts = [o_sub[:half, :], o_sub[half:, :]]
        for ss in range(HGRN_SUB):
            ks = kh[ss:ss + 1, :]
            bs = blh[ss:ss + 1, :]
            vs = vh[ss:ss + 1, :]
            for p in range(2):
                lo = p * half
                if rev:
                    reached = lo <= ss
                    full = lo + half - 1 <= ss
                    ok = row8 + lo <= ss
                else:
                    reached = lo + half - 1 >= ss
                    full = lo >= ss
                    ok = row8 + lo >= ss
                if not reached:
                    continue
                diff = blh[lo:lo + half, :] - bs
                if not full:
                    diff = jnp.where(ok, diff, NEG_INF)
                col = jnp.sum((qh[lo:lo + half, :] * ks) * jnp.exp2(diff), axis=1, keepdims=True)
                parts[p] = parts[p] + col * vs
        o_ref[0, 0, pl.ds(r0, half), sl] = parts[0]
        o_ref[0, 0, pl.ds(r0 + half, half), sl] = parts[1]
        upd = lax.dot_general(vh.astype(jnp.bfloat16), kt.astype(jnp.bfloat16),
                              (((0,), (0,)), ((), ())), preferred_element_type=jnp.float32)
        st_ref[hh] = jnp.exp2(tot) * st + upd


def _hgrn_kernel(q_ref, v_ref, lf_ref, o_ref, st_ref, *, tb):
    d = pl.program_id(1)
    t = pl.program_id(2)
    nsub = tb // HGRN_SUB

    @pl.when(t == 0)
    def _():
        st_ref[...] = jnp.zeros_like(st_ref)

    rr = lax.broadcasted_iota(jnp.int32, (HGRN_SUB, HGRN_SUB), 0)
    cc = lax.broadcasted_iota(jnp.int32, (HGRN_SUB, HGRN_SUB), 1)

    @pl.when(d == 0)
    def _():
        tri = (cc <= rr).astype(jnp.bfloat16)

        def body(i, carry):
            r0 = pl.multiple_of(i * HGRN_SUB, HGRN_SUB)
            _hgrn_sub_chunk(q_ref, v_ref, lf_ref, o_ref, st_ref, r0, tri, False)
            return carry

        lax.fori_loop(0, nsub, body, 0, unroll=HGRN_UNROLL)

    @pl.when(d == 1)
    def _():
        tri = (cc >= rr).astype(jnp.bfloat16)

        def body(i, carry):
            r0 = pl.multiple_of((nsub - 1 - i) * HGRN_SUB, HGRN_SUB)
            _hgrn_sub_chunk(q_ref, v_ref, lf_ref, o_ref, st_ref, r0, tri, True)
            return carry

        lax.fori_loop(0, nsub, body, 0, unroll=HGRN_UNROLL)


def _hgrn_scan(q, v, lf, *, n_ctx_blocks):
    b, t, w = q.shape
    tb = HGRN_TB
    nblk = t // tb
    nlat = nblk - n_ctx_blocks

    def row_block(di, ti):
        ctx_blk = jnp.where(di == 0, nlat + ti, nblk - 1 - ti)
        lat_blk = jnp.where(di == 0, ti - n_ctx_blocks, nblk - 1 - ti)
        return jnp.where(ti < n_ctx_blocks, ctx_blk, lat_blk)

    return pl.pallas_call(
        functools.partial(_hgrn_kernel, tb=tb),
        grid=(b, 2, nblk),
        in_specs=[pl.BlockSpec((1, tb, w), lambda bi, di, ti: (bi, row_block(di, ti), 0)),
                  pl.BlockSpec((1, tb, w), lambda bi, di, ti: (bi, row_block(di, ti), 0)),
                  pl.BlockSpec((1, 1, tb, w), lambda bi, di, ti: (di, bi, row_block(di, ti), 0))],
        out_specs=pl.BlockSpec((1, 1, tb, w), lambda bi, di, ti: (di, bi, row_block(di, ti), 0)),
        out_shape=jax.ShapeDtypeStruct((2, b, t, w), jnp.float32),
        scratch_shapes=[pltpu.VMEM((C_H, C_DV, C_DK), jnp.float32)],
        compiler_params=_cparams(("arbitrary", "arbitrary", "arbitrary")),
        name="hgrn2_scan",
    )(q, v, lf)


def _head_rms(x, w, n_heads, width):
    out = []
    for hh in range(n_heads):
        blk = x[:, hh * width:(hh + 1) * width]
        out.append(_rms_rows(blk, w))
    return jnp.concatenate(out, axis=1)


def _merge_kernel(oa_ref, ob_ref, o2_ref, cg_ref, gt_ref, x_ref, ng_ref, ma_ref, mb_ref, dw_ref, hw_ref,
                  wbr_ref, wo_ref, xo_ref, ho_ref, *, diff_out_scale):
    ob = (_head_rms(ob_ref[...], dw_ref[...], B_H, B_VD) * diff_out_scale).astype(jnp.bfloat16)
    cg = cg_ref[...].astype(jnp.float32)
    oc = (_head_rms(o2_ref[0] + o2_ref[1], hw_ref[...], C_H, C_DV) * (cg * jax.nn.sigmoid(cg)))
    acc = None
    for i, o in enumerate((oa_ref[...], ob, oc.astype(jnp.bfloat16))):
        z = jnp.dot(o, wbr_ref[i], preferred_element_type=jnp.float32)
        g = jax.nn.sigmoid(gt_ref[:, i * D_MODEL:(i + 1) * D_MODEL].astype(jnp.float32))
        acc = g * z if acc is None else acc + g * z
    y = jnp.dot(acc.astype(jnp.bfloat16), wo_ref[...], preferred_element_type=jnp.float32)
    ma, mb = ma_ref[0], mb_ref[0]
    for half, (xn, mod) in enumerate(zip(_residual_tile(x_ref, y, ng_ref[1], ma, mb, 2), (ma, mb))):
        rows = slice(half * HALF_TM, (half + 1) * HALF_TM)
        xo_ref[rows, :] = xn
        ho_ref[rows, :] = _rms_rows(xn, ng_ref[2]) * (1.0 + mod[4]) + mod[3]


def _merge(oa, ob, o2, cg, gates, x, norm_g, mods, diff_w, hgrn_w, w_br, w_o, *, t_rows, seq, diff_out_scale):
    m = x.shape[0]
    tm = MM_TM
    row = lambda i: (i, 0)
    f32 = jnp.float32
    return pl.pallas_call(
        functools.partial(_merge_kernel, diff_out_scale=diff_out_scale),
        grid=(m // tm,),
        in_specs=[pl.BlockSpec((tm, BRANCH_W), row), pl.BlockSpec((tm, BRANCH_W), row),
                  pl.BlockSpec((2, tm, BRANCH_W), lambda i: (0, i, 0)), pl.BlockSpec((tm, BRANCH_W), row),
                  pl.BlockSpec((tm, GATE_W), row), pl.BlockSpec((tm, D_MODEL), row),
                  pl.BlockSpec((4, 1, D_MODEL), lambda i: (0, 0, 0))] + _mod_specs(t_rows, seq) + [
                  pl.BlockSpec((1, B_VD), lambda i: (0, 0)), pl.BlockSpec((1, C_DV), lambda i: (0, 0)),
                  pl.BlockSpec((N_BRANCH, BRANCH_W, D_MODEL), lambda i: (0, 0, 0)),
                  pl.BlockSpec((D_MODEL, D_MODEL), lambda i: (0, 0))],
        out_specs=[pl.BlockSpec((tm, D_MODEL), row), pl.BlockSpec((tm, D_MODEL), row)],
        out_shape=[jax.ShapeDtypeStruct((m, D_MODEL), f32), jax.ShapeDtypeStruct((m, D_MODEL), f32)],
        compiler_params=_cparams(("arbitrary",)),
        name="branch_merge",
    )(oa, ob, o2, cg, gates, x, norm_g, mods, mods, diff_w, hgrn_w, w_br, w_o)


def _moe_row_copy(x_hbm, xbuf, sem, src_row, n_rows, slot, dst_row):
    return pltpu.make_async_copy(x_hbm.at[pl.ds(src_row, n_rows), :],
                                 xbuf.at[slot, pl.ds(dst_row, n_rows), :], sem.at[slot])


def _moe_kernel(blk_e_ref, n_used_ref, tok_ref, tok_next_ref, x_hbm, wgu_ref, bgu_ref, wdn_ref, bdn_ref,
                y_ref, xbuf, sem, wgu_sc, wdn_sc):
    i = pl.program_id(0)
    n_used = n_used_ref[0]
    slot = i % 2
    e = blk_e_ref[i]
    e_prev = blk_e_ref[jnp.maximum(i - 1, 0)]

    def start_gather(ids_ref, dst_slot):
        for r in range(MOE_TM):
            _moe_row_copy(x_hbm, xbuf, sem, ids_ref[0, 0, r], 1, dst_slot, r).start()

    @pl.when((i == 0) & (n_used > 0))
    def _():
        start_gather(tok_ref, 0)

    @pl.when(i + 1 < n_used)
    def _():
        start_gather(tok_next_ref, 1 - slot)

    @pl.when((i == 0) | (e != e_prev))
    def _():
        wgu_sc[...] = wgu_ref[0, 0].astype(jnp.bfloat16)
        wdn_sc[...] = wdn_ref[0, 0].astype(jnp.bfloat16)

    @pl.when(i < n_used)
    def _():
        _moe_row_copy(x_hbm, xbuf, sem, 0, MOE_TM, slot, 0).wait()
        x = xbuf[slot].astype(jnp.bfloat16)
        gu = jnp.dot(x, wgu_sc[...], preferred_element_type=jnp.float32) + bgu_ref[0, 0]
        g = jnp.minimum(gu[:, :EXPERT_FF], SWIGLU_LIMIT)
        u = jnp.clip(gu[:, EXPERT_FF:], -SWIGLU_LIMIT, SWIGLU_LIMIT)
        act = (u + 1.0) * (g * jax.nn.sigmoid(SWIGLU_ALPHA * g))
        y = jnp.dot(act.astype(jnp.bfloat16), wdn_sc[...], preferred_element_type=jnp.float32) + bdn_ref[0, 0]
        y_ref[...] = y.astype(y_ref.dtype)

    @pl.when(i >= n_used)
    def _():
        y_ref[...] = jnp.zeros_like(y_ref)


def _moe_experts(blk_e, n_used, row_tok, x, w_gu, b_gu, w_dn, b_dn, *, layer):
    dm = x.shape[1]
    tm = MOE_TM
    n_blk = blk_e.shape[0]
    r = n_blk * tm
    ids = row_tok.reshape(n_blk, 1, tm)
    return pl.pallas_call(
        _moe_kernel,
        grid_spec=pltpu.PrefetchScalarGridSpec(
            num_scalar_prefetch=2, grid=(n_blk,),
            in_specs=[pl.BlockSpec((1, 1, tm), lambda i, be, nu: (i, 0, 0), memory_space=pltpu.SMEM),
                      pl.BlockSpec((1, 1, tm), lambda i, be, nu: (jnp.minimum(i + 1, n_blk - 1), 0, 0),
                                   memory_space=pltpu.SMEM),
                      pl.BlockSpec(memory_space=pl.ANY),
                      pl.BlockSpec((1, 1, dm, 2 * EXPERT_FF), lambda i, be, nu: (layer, be[i], 0, 0)),
                      pl.BlockSpec((1, 1, 1, 2 * EXPERT_FF), lambda i, be, nu: (layer, be[i], 0, 0)),
                      pl.BlockSpec((1, 1, EXPERT_FF, dm), lambda i, be, nu: (layer, be[i], 0, 0)),
                      pl.BlockSpec((1, 1, 1, dm), lambda i, be, nu: (layer, be[i], 0, 0))],
            out_specs=pl.BlockSpec((tm, dm), lambda i, be, nu: (i, 0)),
            scratch_shapes=[pltpu.VMEM((2, tm, dm), jnp.float32),
                            pltpu.SemaphoreType.DMA((2,)),
                            pltpu.VMEM((dm, 2 * EXPERT_FF), jnp.bfloat16),
                            pltpu.VMEM((EXPERT_FF, dm), jnp.bfloat16)]),
        out_shape=jax.ShapeDtypeStruct((r, dm), jnp.bfloat16),
        compiler_params=_cparams(("arbitrary",)),
        name="moe_experts",
    )(blk_e, n_used, ids, ids, x, w_gu, b_gu.reshape(DEPTH, N_EXPERTS, 1, -1), w_dn,
      b_dn.reshape(DEPTH, N_EXPERTS, 1, -1))


def _combine_kernel(*refs):
    yk_refs = refs[:TOP_K]
    gt_ref, x_ref, ng_ref, ma_ref, mb_ref, xo_ref = refs[TOP_K:]
    gt = gt_ref[...]
    f = None
    for k in range(TOP_K):
        term = gt[:, k:k + 1] * yk_refs[k][...].astype(jnp.float32)
        f = term if f is None else f + term
    for half, xn in enumerate(_residual_tile(x_ref, f, ng_ref[3], ma_ref[0], mb_ref[0], 5)):
        xo_ref[half * HALF_TM:(half + 1) * HALF_TM, :] = xn


def _moe_combine(yk, gates, x, norm_g, mods, *, t_rows, seq):
    m = x.shape[0]
    tm = MM_TM
    n_tiles = m // tm
    row = lambda i: (i, 0)
    choice = lambda k: pl.BlockSpec((tm, D_MODEL), lambda i: (k * n_tiles + i, 0))
    return pl.pallas_call(
        _combine_kernel,
        grid=(n_tiles,),
        in_specs=[choice(k) for k in range(TOP_K)] + [
            pl.BlockSpec((tm, TOP_K), row), pl.BlockSpec((tm, D_MODEL), row),
            pl.BlockSpec((4, 1, D_MODEL), lambda i: (0, 0, 0))] + _mod_specs(t_rows, seq),
        out_specs=pl.BlockSpec((tm, D_MODEL), row),
        out_shape=jax.ShapeDtypeStruct((m, D_MODEL), jnp.float32),
        compiler_params=_cparams(("arbitrary",)),
        name="moe_combine",
    )(*([yk] * TOP_K), gates, x, norm_g, mods, mods)


def _moe_ffn(h, x, norm_g, mods, w_router, b_router, w_gu, b_gu, w_dn, b_dn, *, layer, t_rows, seq):
    n, dm = h.shape
    nk = n * TOP_K
    w_r = jnp.zeros((dm, LANES), jnp.float32).at[:, :N_EXPERTS].set(w_router)
    logits = _matmul(h, w_r, tm=MM_TM, tn=LANES, out_dtype=jnp.float32,
                     precision=lax.Precision.HIGHEST, name="router")[:, :N_EXPERTS] + b_router
    top_v, top_e = lax.top_k(logits, TOP_K)
    gates = jax.nn.softmax(top_v, axis=-1)
    flat_e = top_e.reshape(-1).astype(jnp.int32)
    experts = jnp.arange(N_EXPERTS, dtype=jnp.int32)
    onehot = (flat_e[:, None] == experts[None, :]).astype(jnp.int32)
    incl = jnp.cumsum(onehot, axis=0)
    rank = jnp.sum(incl * onehot, axis=1) - 1
    counts = incl[-1]
    padded = (counts + MOE_TM - 1) // MOE_TM * MOE_TM
    pad_end = jnp.cumsum(padded)
    pad_start = pad_end - padded
    grp_start = jnp.cumsum(counts) - counts
    dest = jnp.sum(onehot * pad_start[None, :], axis=1) + rank
    n_blk = (nk + MOE_TM - 1) // MOE_TM + N_EXPERTS
    rows = n_blk * MOE_TM
    blk_start = jnp.arange(n_blk, dtype=jnp.int32) * MOE_TM
    blk_e = jnp.minimum(jnp.sum((pad_end[None, :] <= blk_start[:, None]).astype(jnp.int32), axis=1),
                        N_EXPERTS - 1)
    n_used = (pad_end[-1] // MOE_TM).astype(jnp.int32).reshape(1)
    order = jnp.argsort(flat_e)
    row_e = jnp.repeat(blk_e, MOE_TM)
    sorted_pos = jnp.arange(rows, dtype=jnp.int32) - pad_start[row_e] + grp_start[row_e]
    row_tok = (order[jnp.clip(sorted_pos, 0, nk - 1)] // TOP_K).astype(jnp.int32)
    y = _moe_experts(blk_e, n_used, row_tok, h, w_gu, b_gu, w_dn, b_dn, layer=layer)
    yk = y[dest.reshape(n, TOP_K).T.reshape(-1)]
    return _moe_combine(yk, gates, x, norm_g, mods, t_rows=t_rows, seq=seq)


def _rope_tables(rows, n_ctx, batch):
    row = jnp.repeat(jnp.arange(rows, dtype=jnp.float32), GRID_W)
    col = jnp.tile(jnp.arange(GRID_W, dtype=jnp.float32), rows)
    n_freq = HEAD_DIM // 4
    inv = ROPE_THETA ** (-jnp.arange(n_freq, dtype=jnp.float32) / n_freq)
    ang_r = row[:, None] * inv
    ang_c = col[:, None] * inv
    cos_h = jnp.concatenate([jnp.cos(ang_r), jnp.cos(ang_r), jnp.cos(ang_c), jnp.cos(ang_c)], axis=1)
    sin_h = jnp.concatenate([-jnp.sin(ang_r), jnp.sin(ang_r), -jnp.sin(ang_c), jnp.sin(ang_c)], axis=1)
    cos_t = jnp.concatenate([cos_h, jnp.ones((n_ctx, HEAD_DIM), jnp.float32)], axis=0)
    sin_t = jnp.concatenate([sin_h, jnp.zeros((n_ctx, HEAD_DIM), jnp.float32)], axis=0)
    reps = (batch, LANES // HEAD_DIM)
    return jnp.tile(cos_t, reps), jnp.tile(sin_t, reps)


def _rope_partner(n_cols):
    idx = jnp.arange(n_cols)
    quarter = HEAD_DIM // 4
    return jnp.where((idx % (2 * quarter)) < quarter, idx + quarter, idx - quarter)


def kernel(x, c, ctx, c_ctx, w_mod, b_mod, norm_g, w_in, attn_sink, diff_lambda, diff_norm_w,
           hgrn_lb_logits, hgrn_norm_w, w_branch, w_out, w_router, b_router, w_gate_up, b_gate_up,
           w_down, b_down):
    b, s, dm = x.shape
    l = ctx.shape[1]
    t = s + l
    f32 = jnp.float32
    bf16 = jnp.bfloat16
    cos_t, sin_t = _rope_tables(s // GRID_W, l, b)
    lb_cum = jnp.cumsum(jax.nn.softmax(hgrn_lb_logits.astype(f32), axis=0), axis=0)
    lower_bounds = lb_cum - lb_cum[0:1]
    cond_rows = 16
    cond = jnp.zeros((cond_rows, dm), f32).at[:b].set(jax.nn.silu(c)).at[b].set(jax.nn.silu(c_ctx))
    scale = HEAD_DIM ** -0.5
    xa = jnp.concatenate([x, ctx], axis=1).reshape(b * t, dm)
    h_moe = None
    for layer in range(DEPTH):
        lam_init = 0.8 - 0.6 * math.exp(-0.3 * layer)
        mod_all = _matmul(cond, w_mod[layer], tm=cond_rows, tn=3 * dm, out_dtype=f32,
                          precision=lax.Precision.HIGHEST, name="adaln") + b_mod[layer]
        mod = mod_all[:b].reshape(b, 1, 6, 1, dm)
        mod_c = jnp.broadcast_to(mod_all[b:b + 1].reshape(1, 1, 6, 1, dm), (b, 1, 6, 1, dm))
        mods = jnp.concatenate([mod, mod_c], axis=1).reshape(2 * b, 6, 1, dm)
        ng = norm_g[layer].reshape(4, 1, dm)
        g0 = ng[0:1]
        geo = dict(t_rows=t, seq=s)

        w = w_in[layer]
        w_aq = (w[:, OFF_A:OFF_A + A_Q_W] * scale).reshape(dm, A_HKV, A_GROUP, HEAD_DIM)
        pad = jnp.zeros((dm, A_GROUP, HEAD_DIM), f32)
        w_aq8 = jnp.concatenate([jnp.concatenate([w_aq[:, 0], pad], axis=-1),
                                 jnp.concatenate([pad, w_aq[:, 1]], axis=-1)], axis=1).reshape(dm, A_Q8_W)
        w_a = jnp.concatenate([w_aq8, w[:, OFF_A + A_Q_W:OFF_A + A_W]], axis=1).astype(bf16)
        w_b = w[:, OFF_B:OFF_B + B_W].at[:, :B_QK_W].multiply(scale * LOG2_E).astype(bf16)
        proj_a = _proj_rope(xa, g0, mods, w_a, w_a[:, _rope_partner(A_ROPE_W2)], cos_t, sin_t,
                            rope_w=A_ROPE_W2, name="proj_a", **geo).reshape(b, t, A_W2)
        bq, bk_hm, bv1_hm = _proj_b(xa, g0, mods, w_b, w_b[:, _rope_partner(B_ROPE_W)], cos_t, sin_t, **geo)
        bq = bq.reshape(b, t, B_QK_W)
        lb = lower_bounds[layer]
        lb_par = jnp.stack([jnp.log(lb[0]), jnp.log1p(-lb[0]), jnp.log(lb[1]), jnp.log1p(-lb[1])])
        cq, ci, lf, cg = _proj_hgrn(xa, g0, mods, w[:, OFF_C:OFF_C + C_W].astype(bf16),
                                    lb_par.reshape(4, 1, C_K_W), **geo)
        gates = _proj_plain(xa, g0, mods, w[:, OFF_GATE:].astype(bf16), name="gate_proj", **geo)

        sink32 = attn_sink[layer].astype(f32)
        o_a = _window_attention(proj_a, sink32, seq=s)

        lp = diff_lambda[layer].astype(f32)
        lam = (jnp.exp(jnp.sum(lp[0] * lp[1])) - jnp.exp(jnp.sum(lp[2] * lp[3])) + lam_init).reshape(1).astype(f32)
        o_b = _diff_attention(lam, bq, bk_hm, bv1_hm, tq=FLASH_TQ, tk=FLASH_TK, q_block_offset=0,
                              n_q_blocks=s // FLASH_TQ, k_block_offset=0, nk=t // FLASH_TK)
        o_bx = _diff_attention(lam, bq, bk_hm, bv1_hm, tq=l, tk=l, q_block_offset=s // l, n_q_blocks=1,
                               k_block_offset=s // l, nk=1)
        o_b = jnp.concatenate([o_b, o_bx], axis=1)

        o2 = _hgrn_scan(cq.reshape(b, t, C_K_W), ci.reshape(b, t, C_V_W), lf.reshape(2, b, t, C_K_W),
                        n_ctx_blocks=l // HGRN_TB)

        col = jnp.arange(A_Q_W)
        head_of_col = ((col % LANES) // HEAD_DIM) * A_GROUP + col // LANES
        w_br = w_branch[layer].at[0].set(w_branch[layer][0][head_of_col * HEAD_DIM + col % HEAD_DIM])
        xa, h_moe = _merge(o_a.reshape(b * t, A_Q_W), o_b.reshape(b * t, B_V_W), o2.reshape(2, b * t, C_V_W), cg,
                           gates, xa, ng, mods, diff_norm_w[layer].reshape(1, B_VD).astype(f32),
                           hgrn_norm_w[layer].reshape(1, C_DV).astype(f32),
                           w_br.astype(bf16), w_out[layer].astype(bf16),
                           diff_out_scale=1 - lam_init, **geo)

        xa = _moe_ffn(h_moe, xa, ng, mods, w_router[layer], b_router[layer], w_gate_up, b_gate_up,
                      w_down, b_down, layer=layer, **geo)
    return xa.reshape(b, t, dm)[:, :s]
```

```python
import functools
import math

import jax
import jax.numpy as jnp
from jax import lax
from jax.experimental import pallas as pl
from jax.experimental.pallas import tpu as pltpu

D_MODEL = 1024
DEPTH = 2
GRID_W = 64
HEAD_DIM = 64
ROPE_THETA = 10000.0
RMS_EPS = 1e-6
NEG_INF = -1e30
BLOCK = 128
A_HQ = 8
A_HKV = 2
A_GROUP = A_HQ // A_HKV
WINDOW = 128
B_H = 4
B_HD = HEAD_DIM
B_VD = 2 * HEAD_DIM
C_H = 4
C_DK = 128
C_DV = 128
N_BRANCH = 3
BRANCH_W = 512
A_Q_W = A_HQ * HEAD_DIM
A_KV_W = A_HKV * HEAD_DIM
B_QK_W = B_H * 2 * B_HD
B_V_W = B_H * B_VD
C_K_W = C_H * C_DK
C_V_W = C_H * C_DV
GATE_W = N_BRANCH * D_MODEL
N_EXPERTS = 32
TOP_K = 4
EXPERT_FF = D_MODEL
SWIGLU_LIMIT = 7.0
SWIGLU_ALPHA = 1.702

A_W = A_Q_W + 2 * A_KV_W
A_ROPE_W = A_Q_W + A_KV_W
B_W = 2 * B_QK_W + B_V_W
B_ROPE_W = 2 * B_QK_W
C_W = 3 * C_K_W + 2 * C_V_W
OFF_A = 0
OFF_B = OFF_A + A_W
OFF_C = OFF_B + B_W
OFF_GATE = OFF_C + C_W

V7X_VMEM_LIMIT_BYTES = 56 * 1024 * 1024
LANES = 128
LOG2_E = math.log2(math.e)

MM_TM = 512
HALF_TM = MM_TM // 2
FLASH_TQ = 2048
FLASH_TK = 768
HGRN_TB = 256
HGRN_SUB = 16
HGRN_UNROLL = 4
MOE_TM = 512


def _cparams(sem):
    return pltpu.CompilerParams(dimension_semantics=sem, vmem_limit_bytes=V7X_VMEM_LIMIT_BYTES)


def _rms_rows(x, g):
    return (x * lax.rsqrt(jnp.mean(x * x, axis=-1, keepdims=True) + RMS_EPS)) * g


def _modulated_tile(x_ref, g, mod_a, mod_b, shift_slot, scale_slot):
    out = []
    for half, mod in enumerate((mod_a, mod_b)):
        x = x_ref[half * HALF_TM:(half + 1) * HALF_TM, :]
        out.append(_rms_rows(x, g) * (1.0 + mod[scale_slot]) + mod[shift_slot])
    return jnp.concatenate(out, axis=0)


def _residual_tile(x_ref, y, g, mod_a, mod_b, gate_slot):
    out = []
    for half, mod in enumerate((mod_a, mod_b)):
        rows = slice(half * HALF_TM, (half + 1) * HALF_TM)
        out.append(x_ref[rows, :] + mod[gate_slot] * _rms_rows(y[rows, :], g))
    return out


def _mod_specs(t_rows, seq):
    halves_per_batch = t_rows // HALF_TM
    lat_halves = seq // HALF_TM

    def row(u):
        return (u // halves_per_batch) * 2 + jnp.where(u % halves_per_batch >= lat_halves, 1, 0)

    blk = (1, 6, 1, D_MODEL)
    return [pl.BlockSpec(blk, lambda i: (row(2 * i), 0, 0, 0)),
            pl.BlockSpec(blk, lambda i: (row(2 * i + 1), 0, 0, 0))]


def _proj_rope_kernel(x_ref, g_ref, ma_ref, mb_ref, w_ref, wsw_ref, cos_ref, sin_ref, o_ref, *, rope_w):
    h = _modulated_tile(x_ref, g_ref[0], ma_ref[0], mb_ref[0], 0, 1).astype(jnp.bfloat16)
    z = jnp.dot(h, w_ref[...], preferred_element_type=jnp.float32)
    zs = jnp.dot(h, wsw_ref[...], preferred_element_type=jnp.float32)
    reps = rope_w // LANES
    rot = z[:, :rope_w] * jnp.tile(cos_ref[...], (1, reps)) + zs * jnp.tile(sin_ref[...], (1, reps))
    o_ref[:, :rope_w] = rot.astype(o_ref.dtype)
    o_ref[:, rope_w:] = z[:, rope_w:].astype(o_ref.dtype)


def _proj_b_kernel(x_ref, g_ref, ma_ref, mb_ref, w_ref, wsw_ref, cos_ref, sin_ref, q_ref, k_ref, v1_ref):
    h = _modulated_tile(x_ref, g_ref[0], ma_ref[0], mb_ref[0], 0, 1).astype(jnp.bfloat16)
    z = jnp.dot(h, w_ref[...], preferred_element_type=jnp.float32)
    zs = jnp.dot(h, wsw_ref[...], preferred_element_type=jnp.float32)
    reps = B_ROPE_W // LANES
    rot = z[:, :B_ROPE_W] * jnp.tile(cos_ref[...], (1, reps)) + zs * jnp.tile(sin_ref[...], (1, reps))
    q_ref[...] = rot[:, :B_QK_W].astype(q_ref.dtype)
    for hh in range(B_H):
        k_ref[hh] = rot[:, B_QK_W + hh * 2 * B_HD:B_QK_W + (hh + 1) * 2 * B_HD].astype(k_ref.dtype)
        v1_ref[hh, :, :B_VD] = z[:, B_ROPE_W + hh * B_VD:B_ROPE_W + (hh + 1) * B_VD].astype(v1_ref.dtype)
        v1_ref[hh, :, B_VD:] = jnp.ones((MM_TM, B_VD), v1_ref.dtype)


def _proj_plain_kernel(x_ref, g_ref, ma_ref, mb_ref, w_ref, o_ref):
    h = _modulated_tile(x_ref, g_ref[0], ma_ref[0], mb_ref[0], 0, 1).astype(jnp.bfloat16)
    o_ref[...] = jnp.dot(h, w_ref[...], preferred_element_type=jnp.float32).astype(o_ref.dtype)


def _log_forget(z, log_lb, log_1m_lb):
    log_sig = jnp.minimum(z, 0.0) - jnp.log(1.0 + jnp.exp(-jnp.abs(z)))
    c = log_1m_lb + log_sig
    return jnp.maximum(log_lb, c) + jnp.log(1.0 + jnp.exp(-jnp.abs(log_lb - c)))


def _proj_hgrn_kernel(x_ref, g_ref, ma_ref, mb_ref, w_ref, lb_ref, q_ref, v_ref, lf_ref, cg_ref):
    h = _modulated_tile(x_ref, g_ref[0], ma_ref[0], mb_ref[0], 0, 1).astype(jnp.bfloat16)
    z = jnp.dot(h, w_ref[...], preferred_element_type=jnp.float32)
    zq = z[:, :C_K_W]
    q_ref[...] = zq * jax.nn.sigmoid(zq)
    v_ref[...] = z[:, C_K_W:C_K_W + C_V_W]
    off = C_K_W + C_V_W
    lf_ref[0] = _log_forget(z[:, off:off + C_K_W], lb_ref[0], lb_ref[1])
    lf_ref[1] = _log_forget(z[:, off + C_K_W:off + 2 * C_K_W], lb_ref[2], lb_ref[3])
    cg_ref[...] = z[:, off + 2 * C_K_W:].astype(cg_ref.dtype)


def _proj_common_specs(t_rows, seq):
    row = lambda i: (i, 0)
    return [pl.BlockSpec((MM_TM, D_MODEL), row),
            pl.BlockSpec((1, 1, D_MODEL), lambda i: (0, 0, 0))] + _mod_specs(t_rows, seq)


def _proj_rope(x, g, mods, w, wsw, cos, sin, *, t_rows, seq, rope_w, name):
    m = x.shape[0]
    n = w.shape[1]
    row = lambda i: (i, 0)
    whole = lambda i: (0, 0)
    return pl.pallas_call(
        functools.partial(_proj_rope_kernel, rope_w=rope_w),
        grid=(m // MM_TM,),
        in_specs=_proj_common_specs(t_rows, seq) + [
            pl.BlockSpec((D_MODEL, n), whole), pl.BlockSpec((D_MODEL, rope_w), whole),
            pl.BlockSpec((MM_TM, LANES), row), pl.BlockSpec((MM_TM, LANES), row)],
        out_specs=pl.BlockSpec((MM_TM, n), row),
        out_shape=jax.ShapeDtypeStruct((m, n), jnp.bfloat16),
        compiler_params=_cparams(("arbitrary",)),
        name=name,
    )(x, g, mods, mods, w, wsw, cos, sin)


def _proj_b(x, g, mods, w, wsw, cos, sin, *, t_rows, seq):
    m = x.shape[0]
    row = lambda i: (i, 0)
    whole = lambda i: (0, 0)
    heads = lambda i: (0, i, 0)
    bf16 = jnp.bfloat16
    return pl.pallas_call(
        _proj_b_kernel,
        grid=(m // MM_TM,),
        in_specs=_proj_common_specs(t_rows, seq) + [
            pl.BlockSpec((D_MODEL, B_W), whole), pl.BlockSpec((D_MODEL, B_ROPE_W), whole),
            pl.BlockSpec((MM_TM, LANES), row), pl.BlockSpec((MM_TM, LANES), row)],
        out_specs=[pl.BlockSpec((MM_TM, B_QK_W), row), pl.BlockSpec((B_H, MM_TM, 2 * B_HD), heads),
                   pl.BlockSpec((B_H, MM_TM, 2 * B_VD), heads)],
        out_shape=[jax.ShapeDtypeStruct((m, B_QK_W), bf16), jax.ShapeDtypeStruct((B_H, m, 2 * B_HD), bf16),
                   jax.ShapeDtypeStruct((B_H, m, 2 * B_VD), bf16)],
        compiler_params=_cparams(("arbitrary",)),
        name="proj_b",
    )(x, g, mods, mods, w, wsw, cos, sin)


def _proj_plain(x, g, mods, w, *, t_rows, seq, name):
    m = x.shape[0]
    n = w.shape[1]
    row = lambda i: (i, 0)
    return pl.pallas_call(
        _proj_plain_kernel,
        grid=(m // MM_TM,),
        in_specs=_proj_common_specs(t_rows, seq) + [pl.BlockSpec((D_MODEL, n), lambda i: (0, 0))],
        out_specs=pl.BlockSpec((MM_TM, n), row),
        out_shape=jax.ShapeDtypeStruct((m, n), jnp.bfloat16),
        compiler_params=_cparams(("arbitrary",)),
        name=name,
    )(x, g, mods, mods, w)


def _proj_hgrn(x, g, mods, w, lb_par, *, t_rows, seq):
    m = x.shape[0]
    row = lambda i: (i, 0)
    f32 = jnp.float32
    return pl.pallas_call(
        _proj_hgrn_kernel,
        grid=(m // MM_TM,),
        in_specs=_proj_common_specs(t_rows, seq) + [
            pl.BlockSpec((D_MODEL, C_W), lambda i: (0, 0)),
            pl.BlockSpec((4, 1, C_K_W), lambda i: (0, 0, 0))],
        out_specs=[pl.BlockSpec((MM_TM, C_K_W), row), pl.BlockSpec((MM_TM, C_V_W), row),
                   pl.BlockSpec((2, MM_TM, C_K_W), lambda i: (0, i, 0)), pl.BlockSpec((MM_TM, C_V_W), row)],
        out_shape=[jax.ShapeDtypeStruct((m, C_K_W), f32), jax.ShapeDtypeStruct((m, C_V_W), f32),
                   jax.ShapeDtypeStruct((2, m, C_K_W), f32), jax.ShapeDtypeStruct((m, C_V_W), jnp.bfloat16)],
        compiler_params=_cparams(("arbitrary",)),
        name="hgrn_proj",
    )(x, g, mods, mods, w, lb_par)


def _mm_kernel(x_ref, w_ref, o_ref, *, precision):
    o_ref[...] = jnp.dot(x_ref[...], w_ref[...], preferred_element_type=jnp.float32,
                         precision=precision).astype(o_ref.dtype)


def _matmul(x, w, *, tm, tn, out_dtype, precision=None, name):
    m, k = x.shape
    n = w.shape[1]
    assert m % tm == 0 and n % tn == 0, (m, tm, n, tn)
    return pl.pallas_call(
        functools.partial(_mm_kernel, precision=precision),
        grid=(n // tn, m // tm),
        in_specs=[pl.BlockSpec((tm, k), lambda j, i: (i, 0)),
                  pl.BlockSpec((k, tn), lambda j, i: (0, j))],
        out_specs=pl.BlockSpec((tm, tn), lambda j, i: (i, j)),
        out_shape=jax.ShapeDtypeStruct((m, n), out_dtype),
        compiler_params=_cparams(("arbitrary", "arbitrary")),
        name=name,
    )(x, w)


A_Q8_W = A_HQ * LANES
A_W2 = A_Q8_W + 2 * A_KV_W
A_ROPE_W2 = A_Q8_W + A_KV_W
A_K_BLK = A_Q8_W // A_KV_W
A_V_BLK = A_K_BLK + 1


def _attn_a_kernel(sink_ref, q_ref, kp_ref, kc_ref, kn_ref, vp_ref, vc_ref, vn_ref, kx_ref, vx_ref,
                   o_ref, *, seq):
    n = pl.program_id(1)
    k_all = jnp.concatenate([kp_ref[0], kc_ref[0], kn_ref[0], kx_ref[0]], axis=0)
    v_all = jnp.concatenate([vp_ref[0], vc_ref[0], vn_ref[0], vx_ref[0]], axis=0)
    n_keys = k_all.shape[0]
    qi = lax.broadcasted_iota(jnp.int32, (BLOCK, n_keys), 0)
    kj = lax.broadcasted_iota(jnp.int32, (BLOCK, n_keys), 1)
    kpos = n * BLOCK + kj - BLOCK
    local_ok = (jnp.abs(kj - BLOCK - qi) <= WINDOW) & (kpos >= 0) & (kpos < seq) & (n * BLOCK < seq)
    mask = local_ok | (kj >= 3 * BLOCK)
    q8 = jnp.concatenate([q_ref[0, :, hq * LANES:(hq + 1) * LANES] for hq in range(A_HQ)], axis=0)
    s = lax.dot_general(q8, k_all, (((1,), (1,)), ((), ())), preferred_element_type=jnp.float32)
    p_blocks, inv_l = [], []
    for hq in range(A_HQ):
        sh = jnp.where(mask, s[hq * BLOCK:(hq + 1) * BLOCK, :], NEG_INF)
        sk = sink_ref[hq]
        m = jnp.maximum(jnp.max(sh, axis=-1, keepdims=True), sk)
        p = jnp.exp(sh - m)
        inv_l.append(1.0 / (jnp.sum(p, axis=-1, keepdims=True) + jnp.exp(sk - m)))
        p_blocks.append(p.astype(v_all.dtype))
    o = jnp.dot(jnp.concatenate(p_blocks, axis=0), v_all, preferred_element_type=jnp.float32)
    lane = lax.broadcasted_iota(jnp.int32, (BLOCK, LANES), 1)
    for g in range(A_GROUP):
        lo = o[g * BLOCK:(g + 1) * BLOCK, :] * inv_l[g]
        hi = o[(A_GROUP + g) * BLOCK:(A_GROUP + g + 1) * BLOCK, :] * inv_l[A_GROUP + g]
        o_ref[0, :, g * LANES:(g + 1) * LANES] = jnp.where(lane < HEAD_DIM, lo, hi).astype(o_ref.dtype)


def _window_attention(proj_a, sink, *, seq):
    b, t, _ = proj_a.shape
    l = t - seq
    nb = seq // BLOCK
    ctx_blk = seq // l
    q_map = lambda bi, n, *_: (bi, n, 0)
    kv = lambda col, fn: pl.BlockSpec((1, BLOCK, A_KV_W), lambda bi, n, *_: (bi, fn(n), col))
    prev = lambda n: jnp.clip(n - 1, 0, nb - 1)
    cur = lambda n: jnp.minimum(n, nb - 1)
    nxt = lambda n: jnp.minimum(n + 1, nb - 1)
    ctx = lambda col: pl.BlockSpec((1, l, A_KV_W), lambda bi, n, *_: (bi, ctx_blk, col))
    return pl.pallas_call(
        functools.partial(_attn_a_kernel, seq=seq),
        grid_spec=pltpu.PrefetchScalarGridSpec(
            num_scalar_prefetch=1, grid=(b, t // BLOCK),
            in_specs=[pl.BlockSpec((1, BLOCK, A_Q8_W), q_map),
                      kv(A_K_BLK, prev), kv(A_K_BLK, cur), kv(A_K_BLK, nxt),
                      kv(A_V_BLK, prev), kv(A_V_BLK, cur), kv(A_V_BLK, nxt),
                      ctx(A_K_BLK), ctx(A_V_BLK)],
            out_specs=pl.BlockSpec((1, BLOCK, A_Q_W), q_map)),
        out_shape=jax.ShapeDtypeStruct((b, t, A_Q_W), jnp.bfloat16),
        compiler_params=_cparams(("arbitrary", "arbitrary")),
        name="window_gqa",
    )(sink, *([proj_a] * 9))


def _diff_flash_kernel(lam_ref, q_ref, k_ref, v_ref, o_ref, q2_sc, sa_sc, sb_sc, m_sc, acc_sc, *, tq, tk, nk):
    u = pl.program_id(2)
    new_q = u % nk == 0

    @pl.when(u == 0)
    def _():
        m_sc[...] = jnp.full_like(m_sc, NEG_INF)
        acc_sc[...] = jnp.zeros_like(acc_sc)
        sb_sc[...] = jnp.full_like(sb_sc, NEG_INF)

    @pl.when(new_q)
    def _():
        q = q_ref[0]
        lane = lax.broadcasted_iota(jnp.int32, q.shape, 1)
        q2_sc[:tq, :] = jnp.where(lane < B_HD, q, jnp.zeros_like(q))
        q2_sc[tq:, :] = jnp.where(lane >= B_HD, q, jnp.zeros_like(q))

    def step(s_new_ref, s_old_ref):
        m_prev = m_sc[...]
        m_new = jnp.maximum(m_prev, jnp.max(s_old_ref[...], axis=-1, keepdims=True))
        alpha = jnp.exp2(m_prev - m_new)
        p = jnp.exp2(s_old_ref[...] - jnp.tile(m_new, (1, tk // LANES))).astype(jnp.bfloat16)
        acc_sc[...] = jnp.tile(alpha, (1, 2)) * acc_sc[...] + jnp.dot(
            p, v_ref[0], preferred_element_type=jnp.float32)
        m_sc[...] = m_new
        s_new_ref[...] = lax.dot_general(q2_sc[...], k_ref[0], (((1,), (1,)), ((), ())),
                                         preferred_element_type=jnp.float32)

    @pl.when(u % 2 == 0)
    def _():
        step(sa_sc, sb_sc)

    @pl.when(u % 2 == 1)
    def _():
        step(sb_sc, sa_sc)

    @pl.when(new_q & (u > 0))
    def _():
        acc = acc_sc[...]
        o = acc[:, :B_VD] / acc[:, B_VD:]
        o_ref[0] = (o[:tq, :] - lam_ref[0] * o[tq:, :]).astype(o_ref.dtype)
        m_sc[...] = jnp.full_like(m_sc, NEG_INF)
        acc_sc[...] = jnp.zeros_like(acc_sc)


def _diff_attention(lam, q, k_hm, v1_hm, *, tq, tk, q_block_offset, n_q_blocks, k_block_offset, nk):
    b, t, _ = q.shape
    kb = t // tk
    n_steps = n_q_blocks * nk + 1
    k_blk = lambda bi, u: bi * kb + k_block_offset + u % nk
    in_specs = [pl.BlockSpec((1, tq, 2 * B_HD),
                             lambda bi, h, u, *_: (bi, jnp.minimum(u // nk, n_q_blocks - 1) + q_block_offset, h)),
                pl.BlockSpec((1, tk, 2 * B_HD), lambda bi, h, u, *_: (h, k_blk(bi, u), 0)),
                pl.BlockSpec((1, tk, 2 * B_VD), lambda bi, h, u, *_: (h, k_blk(bi, jnp.maximum(u - 1, 0)), 0))]
    return pl.pallas_call(
        functools.partial(_diff_flash_kernel, tq=tq, tk=tk, nk=nk),
        grid_spec=pltpu.PrefetchScalarGridSpec(
            num_scalar_prefetch=1, grid=(b, B_H, n_steps),
            in_specs=in_specs,
            out_specs=pl.BlockSpec((1, tq, B_VD), lambda bi, h, u, *_: (bi, jnp.maximum(u - 1, 0) // nk, h)),
            scratch_shapes=[pltpu.VMEM((2 * tq, 2 * B_HD), jnp.bfloat16),
                            pltpu.VMEM((2 * tq, tk), jnp.float32),
                            pltpu.VMEM((2 * tq, tk), jnp.float32),
                            pltpu.VMEM((2 * tq, LANES), jnp.float32),
                            pltpu.VMEM((2 * tq, 2 * B_VD), jnp.float32)]),
        out_shape=jax.ShapeDtypeStruct((b, n_q_blocks * tq, B_V_W), jnp.float32),
        compiler_params=_cparams(("arbitrary", "arbitrary", "arbitrary")),
        name="diff_flash",
    )(lam, q, k_hm, v1_hm)


def _hgrn_sub_chunk(q_ref, v_ref, lf_ref, o_ref, st_ref, r0, tri, rev):
    half = HGRN_SUB // 2
    row8 = lax.broadcasted_iota(jnp.int32, (half, C_DK), 0)
    lf = lf_ref[0, 0, pl.ds(r0, HGRN_SUB), :] * LOG2_E
    lf_a = lf.astype(jnp.bfloat16)
    rem = lf - lf_a.astype(jnp.float32)
    lf_b = rem.astype(jnp.bfloat16)
    lf_c = (rem - lf_b.astype(jnp.float32)).astype(jnp.bfloat16)
    bl = (jnp.dot(tri, lf_a, preferred_element_type=jnp.float32)
          + jnp.dot(tri, lf_b, preferred_element_type=jnp.float32)
          + jnp.dot(tri, lf_c, preferred_element_type=jnp.float32))
    q = q_ref[0, pl.ds(r0, HGRN_SUB), :]
    v = v_ref[0, pl.ds(r0, HGRN_SUB), :]
    last = 0 if rev else HGRN_SUB - 1
    for hh in range(C_H):
        sl = slice(hh * C_DK, (hh + 1) * C_DK)
        lfh, blh, qh, vh = lf[:, sl], bl[:, sl], q[:, sl], v[:, sl]
        tot = blh[last:last + 1, :]
        kh = 1.0 - jnp.exp2(lfh)
        qt = qh * jnp.exp2(blh)
        kt = kh * jnp.exp2(tot - blh)
        st = st_ref[hh]
        o_sub = lax.dot_general(qt.astype(jnp.bfloat16), st.astype(jnp.bfloat16),
                                (((1,), (1,)), ((), ())), preferred_element_type=jnp.float32)
        parts = [o_sub[:half, :], o_sub[half:, :]]
        for ss in range(HGRN_SUB):
            ks = kh[ss:ss + 1, :]
            bs = blh[ss:ss + 1, :]
            vs = vh[ss:ss + 1, :]
            for p in range(2):
                lo = p * half
                if rev:
                    reached = lo <= ss
                    full = lo + half - 1 <= ss
                    ok = row8 + lo <= ss
                else:
                    reached = lo + half - 1 >= ss
                    full = lo >= ss
                    ok = row8 + lo >= ss
                if not reached:
                    continue
                diff = blh[lo:lo + half, :] - bs
                if not full:
                    diff = jnp.where(ok, diff, NEG_INF)
                col = jnp.sum((qh[lo:lo + half, :] * ks) * jnp.exp2(diff), axis=1, keepdims=True)
                parts[p] = parts[p] + col * vs
        o_ref[0, 0, pl.ds(r0, half), sl] = parts[0]
        o_ref[0, 0, pl.ds(r0 + half, half), sl] = parts[1]
        upd = lax.dot_general(vh.astype(jnp.bfloat16), kt.astype(jnp.bfloat16),
                              (((0,), (0,)), ((), ())), preferred_element_type=jnp.float32)
        st_ref[hh] = jnp.exp2(tot) * st + upd


def _hgrn_kernel(q_ref, v_ref, lf_ref, o_ref, st_ref, *, tb):
    d = pl.program_id(1)
    t = pl.program_id(2)
    nsub = tb // HGRN_SUB

    @pl.when(t == 0)
    def _():
        st_ref[...] = jnp.zeros_like(st_ref)

    rr = lax.broadcasted_iota(jnp.int32, (HGRN_SUB, HGRN_SUB), 0)
    cc = lax.broadcasted_iota(jnp.int32, (HGRN_SUB, HGRN_SUB), 1)

    @pl.when(d == 0)
    def _():
        tri = (cc <= rr).astype(jnp.bfloat16)

        def body(i, carry):
            r0 = pl.multiple_of(i * HGRN_SUB, HGRN_SUB)
            _hgrn_sub_chunk(q_ref, v_ref, lf_ref, o_ref, st_ref, r0, tri, False)
            return carry

        lax.fori_loop(0, nsub, body, 0, unroll=HGRN_UNROLL)

    @pl.when(d == 1)
    def _():
        tri = (cc >= rr).astype(jnp.bfloat16)

        def body(i, carry):
            r0 = pl.multiple_of((nsub - 1 - i) * HGRN_SUB, HGRN_SUB)
            _hgrn_sub_chunk(q_ref, v_ref, lf_ref, o_ref, st_ref, r0, tri, True)
            return carry

        lax.fori_loop(0, nsub, body, 0, unroll=HGRN_UNROLL)


def _hgrn_scan(q, v, lf, *, n_ctx_blocks):
    b, t, w = q.shape
    tb = HGRN_TB
    nblk = t // tb
    nlat = nblk - n_ctx_blocks

    def row_block(di, ti):
        ctx_blk = jnp.where(di == 0, nlat + ti, nblk - 1 - ti)
        lat_blk = jnp.where(di == 0, ti - n_ctx_blocks, nblk - 1 - ti)
        return jnp.where(ti < n_ctx_blocks, ctx_blk, lat_blk)

    return pl.pallas_call(
        functools.partial(_hgrn_kernel, tb=tb),
        grid=(b, 2, nblk),
        in_specs=[pl.BlockSpec((1, tb, w), lambda bi, di, ti: (bi, row_block(di, ti), 0)),
                  pl.BlockSpec((1, tb, w), lambda bi, di, ti: (bi, row_block(di, ti), 0)),
                  pl.BlockSpec((1, 1, tb, w), lambda bi, di, ti: (di, bi, row_block(di, ti), 0))],
        out_specs=pl.BlockSpec((1, 1, tb, w), lambda bi, di, ti: (di, bi, row_block(di, ti), 0)),
        out_shape=jax.ShapeDtypeStruct((2, b, t, w), jnp.float32),
        scratch_shapes=[pltpu.VMEM((C_H, C_DV, C_DK), jnp.float32)],
        compiler_params=_cparams(("arbitrary", "arbitrary", "arbitrary")),
        name="hgrn2_scan",
    )(q, v, lf)


def _head_rms(x, w, n_heads, width):
    out = []
    for hh in range(n_heads):
        blk = x[:, hh * width:(hh + 1) * width]
        out.append(_rms_rows(blk, w))
    return jnp.concatenate(out, axis=1)


def _merge_kernel(oa_ref, ob_ref, o2_ref, cg_ref, gt_ref, x_ref, ng_ref, ma_ref, mb_ref, dw_ref, hw_ref,
                  wbr_ref, wo_ref, xo_ref, ho_ref, *, diff_out_scale):
    ob = (_head_rms(ob_ref[...], dw_ref[...], B_H, B_VD) * diff_out_scale).astype(jnp.bfloat16)
    cg = cg_ref[...].astype(jnp.float32)
    oc = (_head_rms(o2_ref[0] + o2_ref[1], hw_ref[...], C_H, C_DV) * (cg * jax.nn.sigmoid(cg)))
    acc = None
    for i, o in enumerate((oa_ref[...], ob, oc.astype(jnp.bfloat16))):
        z = jnp.dot(o, wbr_ref[i], preferred_element_type=jnp.float32)
        g = jax.nn.sigmoid(gt_ref[:, i * D_MODEL:(i + 1) * D_MODEL].astype(jnp.float32))
        acc = g * z if acc is None else acc + g * z
    y = jnp.dot(acc.astype(jnp.bfloat16), wo_ref[...], preferred_element_type=jnp.float32)
    ma, mb = ma_ref[0], mb_ref[0]
    for half, (xn, mod) in enumerate(zip(_residual_tile(x_ref, y, ng_ref[1], ma, mb, 2), (ma, mb))):
        rows = slice(half * HALF_TM, (half + 1) * HALF_TM)
        xo_ref[rows, :] = xn
        ho_ref[rows, :] = _rms_rows(xn, ng_ref[2]) * (1.0 + mod[4]) + mod[3]


def _merge(oa, ob, o2, cg, gates, x, norm_g, mods, diff_w, hgrn_w, w_br, w_o, *, t_rows, seq, diff_out_scale):
    m = x.shape[0]
    tm = MM_TM
    row = lambda i: (i, 0)
    f32 = jnp.float32
    return pl.pallas_call(
        functools.partial(_merge_kernel, diff_out_scale=diff_out_scale),
        grid=(m // tm,),
        in_specs=[pl.BlockSpec((tm, BRANCH_W), row), pl.BlockSpec((tm, BRANCH_W), row),
                  pl.BlockSpec((2, tm, BRANCH_W), lambda i: (0, i, 0)), pl.BlockSpec((tm, BRANCH_W), row),
                  pl.BlockSpec((tm, GATE_W), row), pl.BlockSpec((tm, D_MODEL), row),
                  pl.BlockSpec((4, 1, D_MODEL), lambda i: (0, 0, 0))] + _mod_specs(t_rows, seq) + [
                  pl.BlockSpec((1, B_VD), lambda i: (0, 0)), pl.BlockSpec((1, C_DV), lambda i: (0, 0)),
                  pl.BlockSpec((N_BRANCH, BRANCH_W, D_MODEL), lambda i: (0, 0, 0)),
                  pl.BlockSpec((D_MODEL, D_MODEL), lambda i: (0, 0))],
        out_specs=[pl.BlockSpec((tm, D_MODEL), row), pl.BlockSpec((tm, D_MODEL), row)],
        out_shape=[jax.ShapeDtypeStruct((m, D_MODEL), f32), jax.ShapeDtypeStruct((m, D_MODEL), f32)],
        compiler_params=_cparams(("arbitrary",)),
        name="branch_merge",
    )(oa, ob, o2, cg, gates, x, norm_g, mods, mods, diff_w, hgrn_w, w_br, w_o)


def _moe_row_copy(x_hbm, xbuf, sem, src_row, n_rows, slot, dst_row):
    return pltpu.make_async_copy(x_hbm.at[pl.ds(src_row, n_rows), :],
                                 xbuf.at[slot, pl.ds(dst_row, n_rows), :], sem.at[slot])


def _moe_kernel(blk_e_ref, n_used_ref, tok_ref, tok_next_ref, x_hbm, wgu_ref, bgu_ref, wdn_ref, bdn_ref,
                y_ref, xbuf, sem, wgu_sc, wdn_sc):
    i = pl.program_id(0)
    n_used = n_used_ref[0]
    slot = i % 2
    e = blk_e_ref[i]
    e_prev = blk_e_ref[jnp.maximum(i - 1, 0)]

    def start_gather(ids_ref, dst_slot):
        for r in range(MOE_TM):
            _moe_row_copy(x_hbm, xbuf, sem, ids_ref[0, 0, r], 1, dst_slot, r).start()

    @pl.when((i == 0) & (n_used > 0))
    def _():
        start_gather(tok_ref, 0)

    @pl.when(i + 1 < n_used)
    def _():
        start_gather(tok_next_ref, 1 - slot)

    @pl.when((i == 0) | (e != e_prev))
    def _():
        wgu_sc[...] = wgu_ref[0, 0].astype(jnp.bfloat16)
        wdn_sc[...] = wdn_ref[0, 0].astype(jnp.bfloat16)

    @pl.when(i < n_used)
    def _():
        _moe_row_copy(x_hbm, xbuf, sem, 0, MOE_TM, slot, 0).wait()
        x = xbuf[slot].astype(jnp.bfloat16)
        gu = jnp.dot(x, wgu_sc[...], preferred_element_type=jnp.float32) + bgu_ref[0, 0]
        g = jnp.minimum(gu[:, :EXPERT_FF], SWIGLU_LIMIT)
        u = jnp.clip(gu[:, EXPERT_FF:], -SWIGLU_LIMIT, SWIGLU_LIMIT)
        act = (u + 1.0) * (g * jax.nn.sigmoid(SWIGLU_ALPHA * g))
        y = jnp.dot(act.astype(jnp.bfloat16), wdn_sc[...], preferred_element_type=jnp.float32) + bdn_ref[0, 0]
        y_ref[...] = y.astype(y_ref.dtype)

    @pl.when(i >= n_used)
    def _():
        y_ref[...] = jnp.zeros_like(y_ref)


def _moe_experts(blk_e, n_used, row_tok, x, w_gu, b_gu, w_dn, b_dn, *, layer):
    dm = x.shape[1]
    tm = MOE_TM
    n_blk = blk_e.shape[0]
    r = n_blk * tm
    ids = row_tok.reshape(n_blk, 1, tm)
    return pl.pallas_call(
        _moe_kernel,
        grid_spec=pltpu.PrefetchScalarGridSpec(
            num_scalar_prefetch=2, grid=(n_blk,),
            in_specs=[pl.BlockSpec((1, 1, tm), lambda i, be, nu: (i, 0, 0), memory_space=pltpu.SMEM),
                      pl.BlockSpec((1, 1, tm), lambda i, be, nu: (jnp.minimum(i + 1, n_blk - 1), 0, 0),
                                   memory_space=pltpu.SMEM),
                      pl.BlockSpec(memory_space=pl.ANY),
                      pl.BlockSpec((1, 1, dm, 2 * EXPERT_FF), lambda i, be, nu: (layer, be[i], 0, 0)),
                      pl.BlockSpec((1, 1, 1, 2 * EXPERT_FF), lambda i, be, nu: (layer, be[i], 0, 0)),
                      pl.BlockSpec((1, 1, EXPERT_FF, dm), lambda i, be, nu: (layer, be[i], 0, 0)),
                      pl.BlockSpec((1, 1, 1, dm), lambda i, be, nu: (layer, be[i], 0, 0))],
            out_specs=pl.BlockSpec((tm, dm), lambda i, be, nu: (i, 0)),
            scratch_shapes=[pltpu.VMEM((2, tm, dm), jnp.float32),
                            pltpu.SemaphoreType.DMA((2,)),
                            pltpu.VMEM((dm, 2 * EXPERT_FF), jnp.bfloat16),
                            pltpu.VMEM((EXPERT_FF, dm), jnp.bfloat16)]),
        out_shape=jax.ShapeDtypeStruct((r, dm), jnp.bfloat16),
        compiler_params=_cparams(("arbitrary",)),
        name="moe_experts",
    )(blk_e, n_used, ids, ids, x, w_gu, b_gu.reshape(DEPTH, N_EXPERTS, 1, -1), w_dn,
      b_dn.reshape(DEPTH, N_EXPERTS, 1, -1))


def _combine_kernel(*refs):
    yk_refs = refs[:TOP_K]
    gt_ref, x_ref, ng_ref, ma_ref, mb_ref, xo_ref = refs[TOP_K:]
    gt = gt_ref[...]
    f = None
    for k in range(TOP_K):
        term = gt[:, k:k + 1] * yk_refs[k][...].astype(jnp.float32)
        f = term if f is None else f + term
    for half, xn in enumerate(_residual_tile(x_ref, f, ng_ref[3], ma_ref[0], mb_ref[0], 5)):
        xo_ref[half * HALF_TM:(half + 1) * HALF_TM, :] = xn


def _moe_combine(yk, gates, x, norm_g, mods, *, t_rows, seq):
    m = x.shape[0]
    tm = MM_TM
    n_tiles = m // tm
    row = lambda i: (i, 0)
    choice = lambda k: pl.BlockSpec((tm, D_MODEL), lambda i: (k * n_tiles + i, 0))
    return pl.pallas_call(
        _combine_kernel,
        grid=(n_tiles,),
        in_specs=[choice(k) for k in range(TOP_K)] + [
            pl.BlockSpec((tm, TOP_K), row), pl.BlockSpec((tm, D_MODEL), row),
            pl.BlockSpec((4, 1, D_MODEL), lambda i: (0, 0, 0))] + _mod_specs(t_rows, seq),
        out_specs=pl.BlockSpec((tm, D_MODEL), row),
        out_shape=jax.ShapeDtypeStruct((m, D_MODEL), jnp.float32),
        compiler_params=_cparams(("arbitrary",)),
        name="moe_combine",
    )(*([yk] * TOP_K), gates, x, norm_g, mods, mods)


def _moe_ffn(h, x, norm_g, mods, w_router, b_router, w_gu, b_gu, w_dn, b_dn, *, layer, t_rows, seq):
    n, dm = h.shape
    nk = n * TOP_K
    w_r = jnp.zeros((dm, LANES), jnp.float32).at[:, :N_EXPERTS].set(w_router)
    logits = _matmul(h, w_r, tm=MM_TM, tn=LANES, out_dtype=jnp.float32,
                     precision=lax.Precision.HIGHEST, name="router")[:, :N_EXPERTS] + b_router
    top_v, top_e = lax.top_k(logits, TOP_K)
    gates = jax.nn.softmax(top_v, axis=-1)
    flat_e = top_e.reshape(-1).astype(jnp.int32)
    experts = jnp.arange(N_EXPERTS, dtype=jnp.int32)
    onehot = (flat_e[:, None] == experts[None, :]).astype(jnp.int32)
    incl = jnp.cumsum(onehot, axis=0)
    rank = jnp.sum(incl * onehot, axis=1) - 1
    counts = incl[-1]
    padded = (counts + MOE_TM - 1) // MOE_TM * MOE_TM
    pad_end = jnp.cumsum(padded)
    pad_start = pad_end - padded
    grp_start = jnp.cumsum(counts) - counts
    dest = jnp.sum(onehot * pad_start[None, :], axis=1) + rank
    n_blk = (nk + MOE_TM - 1) // MOE_TM + N_EXPERTS
    rows = n_blk * MOE_TM
    blk_start = jnp.arange(n_blk, dtype=jnp.int32) * MOE_TM
    blk_e = jnp.minimum(jnp.sum((pad_end[None, :] <= blk_start[:, None]).astype(jnp.int32), axis=1),
                        N_EXPERTS - 1)
    n_used = (pad_end[-1] // MOE_TM).astype(jnp.int32).reshape(1)
    order = jnp.argsort(flat_e)
    row_e = jnp.repeat(blk_e, MOE_TM)
    sorted_pos = jnp.arange(rows, dtype=jnp.int32) - pad_start[row_e] + grp_start[row_e]
    row_tok = (order[jnp.clip(sorted_pos, 0, nk - 1)] // TOP_K).astype(jnp.int32)
    y = _moe_experts(blk_e, n_used, row_tok, h, w_gu, b_gu, w_dn, b_dn, layer=layer)
    yk = y[dest.reshape(n, TOP_K).T.reshape(-1)]
    return _moe_combine(yk, gates, x, norm_g, mods, t_rows=t_rows, seq=seq)


def _rope_tables(rows, n_ctx, batch):
    row = jnp.repeat(jnp.arange(rows, dtype=jnp.float32), GRID_W)
    col = jnp.tile(jnp.arange(GRID_W, dtype=jnp.float32), rows)
    n_freq = HEAD_DIM // 4
    inv = ROPE_THETA ** (-jnp.arange(n_freq, dtype=jnp.float32) / n_freq)
    ang_r = row[:, None] * inv
    ang_c = col[:, None] * inv
    cos_h = jnp.concatenate([jnp.cos(ang_r), jnp.cos(ang_r), jnp.cos(ang_c), jnp.cos(ang_c)], axis=1)
    sin_h = jnp.concatenate([-jnp.sin(ang_r), jnp.sin(ang_r), -jnp.sin(ang_c), jnp.sin(ang_c)], axis=1)
    cos_t = jnp.concatenate([cos_h, jnp.ones((n_ctx, HEAD_DIM), jnp.float32)], axis=0)
    sin_t = jnp.concatenate([sin_h, jnp.zeros((n_ctx, HEAD_DIM), jnp.float32)], axis=0)
    reps = (batch, LANES // HEAD_DIM)
    return jnp.tile(cos_t, reps), jnp.tile(sin_t, reps)


def _rope_partner(n_cols):
    idx = jnp.arange(n_cols)
    quarter = HEAD_DIM // 4
    return jnp.where((idx % (2 * quarter)) < quarter, idx + quarter, idx - quarter)


def kernel(x, c, ctx, c_ctx, w_mod, b_mod, norm_g, w_in, attn_sink, diff_lambda, diff_norm_w,
           hgrn_lb_logits, hgrn_norm_w, w_branch, w_out, w_router, b_router, w_gate_up, b_gate_up,
           w_down, b_down):
    b, s, dm = x.shape
    l = ctx.shape[1]
    t = s + l
    f32 = jnp.float32
    bf16 = jnp.bfloat16
    cos_t, sin_t = _rope_tables(s // GRID_W, l, b)
    lb_cum = jnp.cumsum(jax.nn.softmax(hgrn_lb_logits.astype(f32), axis=0), axis=0)
    lower_bounds = lb_cum - lb_cum[0:1]
    cond_rows = 16
    cond = jnp.zeros((cond_rows, dm), f32).at[:b].set(jax.nn.silu(c)).at[b].set(jax.nn.silu(c_ctx))
    scale = HEAD_DIM ** -0.5
    xa = jnp.concatenate([x, ctx], axis=1).reshape(b * t, dm)
    h_moe = None
    for layer in range(DEPTH):
        lam_init = 0.8 - 0.6 * math.exp(-0.3 * layer)
        mod_all = _matmul(cond, w_mod[layer], tm=cond_rows, tn=3 * dm, out_dtype=f32,
                          precision=lax.Precision.HIGHEST, name="adaln") + b_mod[layer]
        mod = mod_all[:b].reshape(b, 1, 6, 1, dm)
        mod_c = jnp.broadcast_to(mod_all[b:b + 1].reshape(1, 1, 6, 1, dm), (b, 1, 6, 1, dm))
        mods = jnp.concatenate([mod, mod_c], axis=1).reshape(2 * b, 6, 1, dm)
        ng = norm_g[layer].reshape(4, 1, dm)
        g0 = ng[0:1]
        geo = dict(t_rows=t, seq=s)

        w = w_in[layer]
        w_aq = (w[:, OFF_A:OFF_A + A_Q_W] * scale).reshape(dm, A_HKV, A_GROUP, HEAD_DIM)
        pad = jnp.zeros((dm, A_GROUP, HEAD_DIM), f32)
        w_aq8 = jnp.concatenate([jnp.concatenate([w_aq[:, 0], pad], axis=-1),
                                 jnp.concatenate([pad, w_aq[:, 1]], axis=-1)], axis=1).reshape(dm, A_Q8_W)
        w_a = jnp.concatenate([w_aq8, w[:, OFF_A + A_Q_W:OFF_A + A_W]], axis=1).astype(bf16)
        w_b = w[:, OFF_B:OFF_B + B_W].at[:, :B_QK_W].multiply(scale * LOG2_E).astype(bf16)
        proj_a = _proj_rope(xa, g0, mods, w_a, w_a[:, _rope_partner(A_ROPE_W2)], cos_t, sin_t,
                            rope_w=A_ROPE_W2, name="proj_a", **geo).reshape(b, t, A_W2)
        bq, bk_hm, bv1_hm = _proj_b(xa, g0, mods, w_b, w_b[:, _rope_partner(B_ROPE_W)], cos_t, sin_t, **geo)
        bq = bq.reshape(b, t, B_QK_W)
        lb = lower_bounds[layer]
        lb_par = jnp.stack([jnp.log(lb[0]), jnp.log1p(-lb[0]), jnp.log(lb[1]), jnp.log1p(-lb[1])])
        cq, ci, lf, cg = _proj_hgrn(xa, g0, mods, w[:, OFF_C:OFF_C + C_W].astype(bf16),
                                    lb_par.reshape(4, 1, C_K_W), **geo)
        gates = _proj_plain(xa, g0, mods, w[:, OFF_GATE:].astype(bf16), name="gate_proj", **geo)

        sink32 = attn_sink[layer].astype(f32)
        o_a = _window_attention(proj_a, sink32, seq=s)

        lp = diff_lambda[layer].astype(f32)
        lam = (jnp.exp(jnp.sum(lp[0] * lp[1])) - jnp.exp(jnp.sum(lp[2] * lp[3])) + lam_init).reshape(1).astype(f32)
        o_b = _diff_attention(lam, bq, bk_hm, bv1_hm, tq=FLASH_TQ, tk=FLASH_TK, q_block_offset=0,
                              n_q_blocks=s // FLASH_TQ, k_block_offset=0, nk=t // FLASH_TK)
        o_bx = _diff_attention(lam, bq, bk_hm, bv1_hm, tq=l, tk=l, q_block_offset=s // l, n_q_blocks=1,
                               k_block_offset=s // l, nk=1)
        o_b = jnp.concatenate([o_b, o_bx], axis=1)

        o2 = _hgrn_scan(cq.reshape(b, t, C_K_W), ci.reshape(b, t, C_V_W), lf.reshape(2, b, t, C_K_W),
                        n_ctx_blocks=l // HGRN_TB)

        col = jnp.arange(A_Q_W)
        head_of_col = ((col % LANES) // HEAD_DIM) * A_GROUP + col // LANES
        w_br = w_branch[layer].at[0].set(w_branch[layer][0][head_of_col * HEAD_DIM + col % HEAD_DIM])
        xa, h_moe = _merge(o_a.reshape(b * t, A_Q_W), o_b.reshape(b * t, B_V_W), o2.reshape(2, b * t, C_V_W), cg,
                           gates, xa, ng, mods, diff_norm_w[layer].reshape(1, B_VD).astype(f32),
                           hgrn_norm_w[layer].reshape(1, C_DV).astype(f32),
                           w_br.astype(bf16), w_out[layer].astype(bf16),
                           diff_out_scale=1 - lam_init, **geo)

        xa = _moe_ffn(h_moe, xa, ng, mods, w_router[layer], b_router[layer], w_gate_up, b_gate_up,
                      w_down, b_down, layer=layer, **geo)
    return xa.reshape(b, t, dm)[:, :s]
```

```python
import functools
import math

import jax
import jax.numpy as jnp
from jax import lax
from jax.experimental import pallas as pl
from jax.experimental.pallas import tpu as pltpu

D_MODEL = 1024
DEPTH = 2
GRID_W = 64
HEAD_DIM = 64
ROPE_THETA = 10000.0
RMS_EPS = 1e-6
NEG_INF = -1e30
BLOCK = 128
A_HQ = 8
A_HKV = 2
A_GROUP = A_HQ // A_HKV
WINDOW = 128
B_H = 4
B_HD = HEAD_DIM
B_VD = 2 * HEAD_DIM
C_H = 4
C_DK = 128
C_DV = 128
N_BRANCH = 3
BRANCH_W = 512
A_Q_W = A_HQ * HEAD_DIM
A_KV_W = A_HKV * HEAD_DIM
B_QK_W = B_H * 2 * B_HD
B_V_W = B_H * B_VD
C_K_W = C_H * C_DK
C_V_W = C_H * C_DV
GATE_W = N_BRANCH * D_MODEL
N_EXPERTS = 32
TOP_K = 4
EXPERT_FF = D_MODEL
SWIGLU_LIMIT = 7.0
SWIGLU_ALPHA = 1.702

A_W = A_Q_W + 2 * A_KV_W
A_ROPE_W = A_Q_W + A_KV_W
B_W = 2 * B_QK_W + B_V_W
B_ROPE_W = 2 * B_QK_W
C_W = 3 * C_K_W + 2 * C_V_W
OFF_A = 0
OFF_B = OFF_A + A_W
OFF_C = OFF_B + B_W
OFF_GATE = OFF_C + C_W

V7X_VMEM_LIMIT_BYTES = 56 * 1024 * 1024
LANES = 128
LOG2_E = math.log2(math.e)

MM_TM = 512
HALF_TM = MM_TM // 2
FLASH_TQ = 2048
FLASH_TK = 768
HGRN_TB = 256
HGRN_SUB = 16
HGRN_UNROLL = 4
MOE_TM = 512
MOE_CHUNKS = 4


def _cparams(sem):
    return pltpu.CompilerParams(dimension_semantics=sem, vmem_limit_bytes=V7X_VMEM_LIMIT_BYTES)


def _rms_rows(x, g):
    return (x * lax.rsqrt(jnp.mean(x * x, axis=-1, keepdims=True) + RMS_EPS)) * g


def _modulated_tile(x_ref, g, mod_a, mod_b, shift_slot, scale_slot):
    out = []
    for half, mod in enumerate((mod_a, mod_b)):
        x = x_ref[half * HALF_TM:(half + 1) * HALF_TM, :]
        out.append(_rms_rows(x, g) * (1.0 + mod[scale_slot]) + mod[shift_slot])
    return jnp.concatenate(out, axis=0)


def _residual_tile(x_ref, y, g, mod_a, mod_b, gate_slot):
    out = []
    for half, mod in enumerate((mod_a, mod_b)):
        rows = slice(half * HALF_TM, (half + 1) * HALF_TM)
        out.append(x_ref[rows, :] + mod[gate_slot] * _rms_rows(y[rows, :], g))
    return out


def _mod_specs(t_rows, seq):
    halves_per_batch = t_rows // HALF_TM
    lat_halves = seq // HALF_TM

    def row(u):
        return (u // halves_per_batch) * 2 + jnp.where(u % halves_per_batch >= lat_halves, 1, 0)

    blk = (1, 6, 1, D_MODEL)
    return [pl.BlockSpec(blk, lambda i: (row(2 * i), 0, 0, 0)),
            pl.BlockSpec(blk, lambda i: (row(2 * i + 1), 0, 0, 0))]


def _proj_rope_kernel(x_ref, g_ref, ma_ref, mb_ref, w_ref, wsw_ref, cos_ref, sin_ref, o_ref, *, rope_w):
    h = _modulated_tile(x_ref, g_ref[0], ma_ref[0], mb_ref[0], 0, 1).astype(jnp.bfloat16)
    z = jnp.dot(h, w_ref[...], preferred_element_type=jnp.float32)
    zs = jnp.dot(h, wsw_ref[...], preferred_element_type=jnp.float32)
    reps = rope_w // LANES
    rot = z[:, :rope_w] * jnp.tile(cos_ref[...], (1, reps)) + zs * jnp.tile(sin_ref[...], (1, reps))
    o_ref[:, :rope_w] = rot.astype(o_ref.dtype)
    o_ref[:, rope_w:] = z[:, rope_w:].astype(o_ref.dtype)


def _proj_b_kernel(x_ref, g_ref, ma_ref, mb_ref, w_ref, wsw_ref, cos_ref, sin_ref, q_ref, k_ref, v1_ref):
    h = _modulated_tile(x_ref, g_ref[0], ma_ref[0], mb_ref[0], 0, 1).astype(jnp.bfloat16)
    z = jnp.dot(h, w_ref[...], preferred_element_type=jnp.float32)
    zs = jnp.dot(h, wsw_ref[...], preferred_element_type=jnp.float32)
    reps = B_ROPE_W // LANES
    rot = z[:, :B_ROPE_W] * jnp.tile(cos_ref[...], (1, reps)) + zs * jnp.tile(sin_ref[...], (1, reps))
    q_ref[...] = rot[:, :B_QK_W].astype(q_ref.dtype)
    for hh in range(B_H):
        k_ref[hh] = rot[:, B_QK_W + hh * 2 * B_HD:B_QK_W + (hh + 1) * 2 * B_HD].astype(k_ref.dtype)
        v1_ref[hh, :, :B_VD] = z[:, B_ROPE_W + hh * B_VD:B_ROPE_W + (hh + 1) * B_VD].astype(v1_ref.dtype)
        v1_ref[hh, :, B_VD:] = jnp.ones((MM_TM, B_VD), v1_ref.dtype)


def _proj_plain_kernel(x_ref, g_ref, ma_ref, mb_ref, w_ref, o_ref):
    h = _modulated_tile(x_ref, g_ref[0], ma_ref[0], mb_ref[0], 0, 1).astype(jnp.bfloat16)
    o_ref[...] = jnp.dot(h, w_ref[...], preferred_element_type=jnp.float32).astype(o_ref.dtype)


def _log_forget(z, log_lb, log_1m_lb):
    log_sig = jnp.minimum(z, 0.0) - jnp.log(1.0 + jnp.exp(-jnp.abs(z)))
    c = log_1m_lb + log_sig
    return jnp.maximum(log_lb, c) + jnp.log(1.0 + jnp.exp(-jnp.abs(log_lb - c)))


def _proj_hgrn_kernel(x_ref, g_ref, ma_ref, mb_ref, w_ref, lb_ref, q_ref, v_ref, lf_ref, cg_ref):
    h = _modulated_tile(x_ref, g_ref[0], ma_ref[0], mb_ref[0], 0, 1).astype(jnp.bfloat16)
    z = jnp.dot(h, w_ref[...], preferred_element_type=jnp.float32)
    zq = z[:, :C_K_W]
    q_ref[...] = zq * jax.nn.sigmoid(zq)
    v_ref[...] = z[:, C_K_W:C_K_W + C_V_W]
    off = C_K_W + C_V_W
    lf_ref[0] = _log_forget(z[:, off:off + C_K_W], lb_ref[0], lb_ref[1])
    lf_ref[1] = _log_forget(z[:, off + C_K_W:off + 2 * C_K_W], lb_ref[2], lb_ref[3])
    cg_ref[...] = z[:, off + 2 * C_K_W:].astype(cg_ref.dtype)


def _proj_common_specs(t_rows, seq):
    row = lambda i: (i, 0)
    return [pl.BlockSpec((MM_TM, D_MODEL), row),
            pl.BlockSpec((1, 1, D_MODEL), lambda i: (0, 0, 0))] + _mod_specs(t_rows, seq)


def _proj_rope(x, g, mods, w, wsw, cos, sin, *, t_rows, seq, rope_w, name):
    m = x.shape[0]
    n = w.shape[1]
    row = lambda i: (i, 0)
    whole = lambda i: (0, 0)
    return pl.pallas_call(
        functools.partial(_proj_rope_kernel, rope_w=rope_w),
        grid=(m // MM_TM,),
        in_specs=_proj_common_specs(t_rows, seq) + [
            pl.BlockSpec((D_MODEL, n), whole), pl.BlockSpec((D_MODEL, rope_w), whole),
            pl.BlockSpec((MM_TM, LANES), row), pl.BlockSpec((MM_TM, LANES), row)],
        out_specs=pl.BlockSpec((MM_TM, n), row),
        out_shape=jax.ShapeDtypeStruct((m, n), jnp.bfloat16),
        compiler_params=_cparams(("arbitrary",)),
        name=name,
    )(x, g, mods, mods, w, wsw, cos, sin)


def _proj_b(x, g, mods, w, wsw, cos, sin, *, t_rows, seq):
    m = x.shape[0]
    row = lambda i: (i, 0)
    whole = lambda i: (0, 0)
    heads = lambda i: (0, i, 0)
    bf16 = jnp.bfloat16
    return pl.pallas_call(
        _proj_b_kernel,
        grid=(m // MM_TM,),
        in_specs=_proj_common_specs(t_rows, seq) + [
            pl.BlockSpec((D_MODEL, B_W), whole), pl.BlockSpec((D_MODEL, B_ROPE_W), whole),
            pl.BlockSpec((MM_TM, LANES), row), pl.BlockSpec((MM_TM, LANES), row)],
        out_specs=[pl.BlockSpec((MM_TM, B_QK_W), row), pl.BlockSpec((B_H, MM_TM, 2 * B_HD), heads),
                   pl.BlockSpec((B_H, MM_TM, 2 * B_VD), heads)],
        out_shape=[jax.ShapeDtypeStruct((m, B_QK_W), bf16), jax.ShapeDtypeStruct((B_H, m, 2 * B_HD), bf16),
                   jax.ShapeDtypeStruct((B_H, m, 2 * B_VD), bf16)],
        compiler_params=_cparams(("arbitrary",)),
        name="proj_b",
    )(x, g, mods, mods, w, wsw, cos, sin)


def _proj_plain(x, g, mods, w, *, t_rows, seq, name):
    m = x.shape[0]
    n = w.shape[1]
    row = lambda i: (i, 0)
    return pl.pallas_call(
        _proj_plain_kernel,
        grid=(m // MM_TM,),
        in_specs=_proj_common_specs(t_rows, seq) + [pl.BlockSpec((D_MODEL, n), lambda i: (0, 0))],
        out_specs=pl.BlockSpec((MM_TM, n), row),
        out_shape=jax.ShapeDtypeStruct((m, n), jnp.bfloat16),
        compiler_params=_cparams(("arbitrary",)),
        name=name,
    )(x, g, mods, mods, w)


def _proj_hgrn(x, g, mods, w, lb_par, *, t_rows, seq):
    m = x.shape[0]
    row = lambda i: (i, 0)
    f32 = jnp.float32
    return pl.pallas_call(
        _proj_hgrn_kernel,
        grid=(m // MM_TM,),
        in_specs=_proj_common_specs(t_rows, seq) + [
            pl.BlockSpec((D_MODEL, C_W), lambda i: (0, 0)),
            pl.BlockSpec((4, 1, C_K_W), lambda i: (0, 0, 0))],
        out_specs=[pl.BlockSpec((MM_TM, C_K_W), row), pl.BlockSpec((MM_TM, C_V_W), row),
                   pl.BlockSpec((2, MM_TM, C_K_W), lambda i: (0, i, 0)), pl.BlockSpec((MM_TM, C_V_W), row)],
        out_shape=[jax.ShapeDtypeStruct((m, C_K_W), f32), jax.ShapeDtypeStruct((m, C_V_W), f32),
                   jax.ShapeDtypeStruct((2, m, C_K_W), f32), jax.ShapeDtypeStruct((m, C_V_W), jnp.bfloat16)],
        compiler_params=_cparams(("arbitrary",)),
        name="hgrn_proj",
    )(x, g, mods, mods, w, lb_par)


def _mm_kernel(x_ref, w_ref, o_ref, *, precision):
    o_ref[...] = jnp.dot(x_ref[...], w_ref[...], preferred_element_type=jnp.float32,
                         precision=precision).astype(o_ref.dtype)


def _matmul(x, w, *, tm, tn, out_dtype, precision=None, name):
    m, k = x.shape
    n = w.shape[1]
    assert m % tm == 0 and n % tn == 0, (m, tm, n, tn)
    return pl.pallas_call(
        functools.partial(_mm_kernel, precision=precision),
        grid=(n // tn, m // tm),
        in_specs=[pl.BlockSpec((tm, k), lambda j, i: (i, 0)),
                  pl.BlockSpec((k, tn), lambda j, i: (0, j))],
        out_specs=pl.BlockSpec((tm, tn), lambda j, i: (i, j)),
        out_shape=jax.ShapeDtypeStruct((m, n), out_dtype),
        compiler_params=_cparams(("arbitrary", "arbitrary")),
        name=name,
    )(x, w)


A_Q8_W = A_HQ * LANES
A_W2 = A_Q8_W + 2 * A_KV_W
A_ROPE_W2 = A_Q8_W + A_KV_W
A_K_BLK = A_Q8_W // A_KV_W
A_V_BLK = A_K_BLK + 1


def _attn_a_kernel(sink_ref, q_ref, kp_ref, kc_ref, kn_ref, vp_ref, vc_ref, vn_ref, kx_ref, vx_ref,
                   o_ref, *, seq):
    n = pl.program_id(1)
    k_all = jnp.concatenate([kp_ref[0], kc_ref[0], kn_ref[0], kx_ref[0]], axis=0)
    v_all = jnp.concatenate([vp_ref[0], vc_ref[0], vn_ref[0], vx_ref[0]], axis=0)
    n_keys = k_all.shape[0]
    qi = lax.broadcasted_iota(jnp.int32, (BLOCK, n_keys), 0)
    kj = lax.broadcasted_iota(jnp.int32, (BLOCK, n_keys), 1)
    kpos = n * BLOCK + kj - BLOCK
    local_ok = (jnp.abs(kj - BLOCK - qi) <= WINDOW) & (kpos >= 0) & (kpos < seq) & (n * BLOCK < seq)
    mask = local_ok | (kj >= 3 * BLOCK)
    q8 = jnp.concatenate([q_ref[0, :, hq * LANES:(hq + 1) * LANES] for hq in range(A_HQ)], axis=0)
    s = lax.dot_general(q8, k_all, (((1,), (1,)), ((), ())), preferred_element_type=jnp.float32)
    p_blocks, inv_l = [], []
    for hq in range(A_HQ):
        sh = jnp.where(mask, s[hq * BLOCK:(hq + 1) * BLOCK, :], NEG_INF)
        sk = sink_ref[hq]
        m = jnp.maximum(jnp.max(sh, axis=-1, keepdims=True), sk)
        p = jnp.exp(sh - m)
        inv_l.append(1.0 / (jnp.sum(p, axis=-1, keepdims=True) + jnp.exp(sk - m)))
        p_blocks.append(p.astype(v_all.dtype))
    o = jnp.dot(jnp.concatenate(p_blocks, axis=0), v_all, preferred_element_type=jnp.float32)
    lane = lax.broadcasted_iota(jnp.int32, (BLOCK, LANES), 1)
    for g in range(A_GROUP):
        lo = o[g * BLOCK:(g + 1) * BLOCK, :] * inv_l[g]
        hi = o[(A_GROUP + g) * BLOCK:(A_GROUP + g + 1) * BLOCK, :] * inv_l[A_GROUP + g]
        o_ref[0, :, g * LANES:(g + 1) * LANES] = jnp.where(lane < HEAD_DIM, lo, hi).astype(o_ref.dtype)


def _window_attention(proj_a, sink, *, seq):
    b, t, _ = proj_a.shape
    l = t - seq
    nb = seq // BLOCK
    ctx_blk = seq // l
    q_map = lambda bi, n, *_: (bi, n, 0)
    kv = lambda col, fn: pl.BlockSpec((1, BLOCK, A_KV_W), lambda bi, n, *_: (bi, fn(n), col))
    prev = lambda n: jnp.clip(n - 1, 0, nb - 1)
    cur = lambda n: jnp.minimum(n, nb - 1)
    nxt = lambda n: jnp.minimum(n + 1, nb - 1)
    ctx = lambda col: pl.BlockSpec((1, l, A_KV_W), lambda bi, n, *_: (bi, ctx_blk, col))
    return pl.pallas_call(
        functools.partial(_attn_a_kernel, seq=seq),
        grid_spec=pltpu.PrefetchScalarGridSpec(
            num_scalar_prefetch=1, grid=(b, t // BLOCK),
            in_specs=[pl.BlockSpec((1, BLOCK, A_Q8_W), q_map),
                      kv(A_K_BLK, prev), kv(A_K_BLK, cur), kv(A_K_BLK, nxt),
                      kv(A_V_BLK, prev), kv(A_V_BLK, cur), kv(A_V_BLK, nxt),
                      ctx(A_K_BLK), ctx(A_V_BLK)],
            out_specs=pl.BlockSpec((1, BLOCK, A_Q_W), q_map)),
        out_shape=jax.ShapeDtypeStruct((b, t, A_Q_W), jnp.bfloat16),
        compiler_params=_cparams(("arbitrary", "arbitrary")),
        name="window_gqa",
    )(sink, *([proj_a] * 9))


def _diff_flash_kernel(lam_ref, q_ref, k_ref, v_ref, o_ref, q2_sc, sa_sc, sb_sc, m_sc, acc_sc, *, tq, tk, nk):
    u = pl.program_id(2)
    new_q = u % nk == 0

    @pl.when(u == 0)
    def _():
        m_sc[...] = jnp.full_like(m_sc, NEG_INF)
        acc_sc[...] = jnp.zeros_like(acc_sc)
        sb_sc[...] = jnp.full_like(sb_sc, NEG_INF)

    @pl.when(new_q)
    def _():
        q = q_ref[0]
        lane = lax.broadcasted_iota(jnp.int32, q.shape, 1)
        q2_sc[:tq, :] = jnp.where(lane < B_HD, q, jnp.zeros_like(q))
        q2_sc[tq:, :] = jnp.where(lane >= B_HD, q, jnp.zeros_like(q))

    def step(s_new_ref, s_old_ref):
        m_prev = m_sc[...]
        m_new = jnp.maximum(m_prev, jnp.max(s_old_ref[...], axis=-1, keepdims=True))
        alpha = jnp.exp2(m_prev - m_new)
        p = jnp.exp2(s_old_ref[...] - jnp.tile(m_new, (1, tk // LANES))).astype(jnp.bfloat16)
        acc_sc[...] = jnp.tile(alpha, (1, 2)) * acc_sc[...] + jnp.dot(
            p, v_ref[0], preferred_element_type=jnp.float32)
        m_sc[...] = m_new
        s_new_ref[...] = lax.dot_general(q2_sc[...], k_ref[0], (((1,), (1,)), ((), ())),
                                         preferred_element_type=jnp.float32)

    @pl.when(u % 2 == 0)
    def _():
        step(sa_sc, sb_sc)

    @pl.when(u % 2 == 1)
    def _():
        step(sb_sc, sa_sc)

    @pl.when(new_q & (u > 0))
    def _():
        acc = acc_sc[...]
        o = acc[:, :B_VD] / acc[:, B_VD:]
        o_ref[0] = (o[:tq, :] - lam_ref[0] * o[tq:, :]).astype(o_ref.dtype)
        m_sc[...] = jnp.full_like(m_sc, NEG_INF)
        acc_sc[...] = jnp.zeros_like(acc_sc)


def _diff_attention(lam, q, k_hm, v1_hm, *, tq, tk, q_block_offset, n_q_blocks, k_block_offset, nk):
    b, t, _ = q.shape
    kb = t // tk
    n_steps = n_q_blocks * nk + 1
    k_blk = lambda bi, u: bi * kb + k_block_offset + u % nk
    in_specs = [pl.BlockSpec((1, tq, 2 * B_HD),
                             lambda bi, h, u, *_: (bi, jnp.minimum(u // nk, n_q_blocks - 1) + q_block_offset, h)),
                pl.BlockSpec((1, tk, 2 * B_HD), lambda bi, h, u, *_: (h, k_blk(bi, u), 0)),
                pl.BlockSpec((1, tk, 2 * B_VD), lambda bi, h, u, *_: (h, k_blk(bi, jnp.maximum(u - 1, 0)), 0))]
    return pl.pallas_call(
        functools.partial(_diff_flash_kernel, tq=tq, tk=tk, nk=nk),
        grid_spec=pltpu.PrefetchScalarGridSpec(
            num_scalar_prefetch=1, grid=(b, B_H, n_steps),
            in_specs=in_specs,
            out_specs=pl.BlockSpec((1, tq, B_VD), lambda bi, h, u, *_: (bi, jnp.maximum(u - 1, 0) // nk, h)),
            scratch_shapes=[pltpu.VMEM((2 * tq, 2 * B_HD), jnp.bfloat16),
                            pltpu.VMEM((2 * tq, tk), jnp.float32),
                            pltpu.VMEM((2 * tq, tk), jnp.float32),
                            pltpu.VMEM((2 * tq, LANES), jnp.float32),
                            pltpu.VMEM((2 * tq, 2 * B_VD), jnp.float32)]),
        out_shape=jax.ShapeDtypeStruct((b, n_q_blocks * tq, B_V_W), jnp.float32),
        compiler_params=_cparams(("arbitrary", "arbitrary", "arbitrary")),
        name="diff_flash",
    )(lam, q, k_hm, v1_hm)


def _hgrn_sub_chunk(q_ref, v_ref, lf_ref, o_ref, st_ref, r0, tri, rev):
    half = HGRN_SUB // 2
    row8 = lax.broadcasted_iota(jnp.int32, (half, C_DK), 0)
    lf = lf_ref[0, 0, pl.ds(r0, HGRN_SUB), :] * LOG2_E
    lf_a = lf.astype(jnp.bfloat16)
    rem = lf - lf_a.astype(jnp.float32)
    lf_b = rem.astype(jnp.bfloat16)
    lf_c = (rem - lf_b.astype(jnp.float32)).astype(jnp.bfloat16)
    bl = (jnp.dot(tri, lf_a, preferred_element_type=jnp.float32)
          + jnp.dot(tri, lf_b, preferred_element_type=jnp.float32)
          + jnp.dot(tri, lf_c, preferred_element_type=jnp.float32))
    q = q_ref[0, pl.ds(r0, HGRN_SUB), :]
    v = v_ref[0, pl.ds(r0, HGRN_SUB), :]
    last = 0 if rev else HGRN_SUB - 1
    for hh in range(C_H):
        sl = slice(hh * C_DK, (hh + 1) * C_DK)
        lfh, blh, qh, vh = lf[:, sl], bl[:, sl], q[:, sl], v[:, sl]
        tot = blh[last:last + 1, :]
        kh = 1.0 - jnp.exp2(lfh)
        qt = qh * jnp.exp2(blh)
        kt = kh * jnp.exp2(tot - blh)
        st = st_ref[hh]
        o_sub = lax.dot_general(qt.astype(jnp.bfloat16), st.astype(jnp.bfloat16),
                                (((1,), (1,)), ((), ())), preferred_element_type=jnp.float32)
        parts = [o_sub[:half, :], o_sub[half:, :]]
        for ss in range(HGRN_SUB):
            ks = kh[ss:ss + 1, :]
            bs = blh[ss:ss + 1, :]
            vs = vh[ss:ss + 1, :]
            for p in range(2):
                lo = p * half
                if rev:
                    reached = lo <= ss
                    full = lo + half - 1 <= ss
                    ok = row8 + lo <= ss
                else:
                    reached = lo + half - 1 >= ss
                    full = lo >= ss
                    ok = row8 + lo >= ss
                if not reached:
                    continue
                diff = blh[lo:lo + half, :] - bs
                if not full:
                    diff = jnp.where(ok, diff, NEG_INF)
                col = jnp.sum((qh[lo:lo + half, :] * ks) * jnp.exp2(diff), axis=1, keepdims=True)
                parts[p] = parts[p] + col * vs
        o_ref[0, 0, pl.ds(r0, half), sl] = parts[0]
        o_ref[0, 0, pl.ds(r0 + half, half), sl] = parts[1]
        upd = lax.dot_general(vh.astype(jnp.bfloat16), kt.astype(jnp.bfloat16),
                              (((0,), (0,)), ((), ())), preferred_element_type=jnp.float32)
        st_ref[hh] = jnp.exp2(tot) * st + upd


def _hgrn_kernel(q_ref, v_ref, lf_ref, o_ref, st_ref, *, tb):
    d = pl.program_id(1)
    t = pl.program_id(2)
    nsub = tb // HGRN_SUB

    @pl.when(t == 0)
    def _():
        st_ref[...] = jnp.zeros_like(st_ref)

    rr = lax.broadcasted_iota(jnp.int32, (HGRN_SUB, HGRN_SUB), 0)
    cc = lax.broadcasted_iota(jnp.int32, (HGRN_SUB, HGRN_SUB), 1)

    @pl.when(d == 0)
    def _():
        tri = (cc <= rr).astype(jnp.bfloat16)

        def body(i, carry):
            r0 = pl.multiple_of(i * HGRN_SUB, HGRN_SUB)
            _hgrn_sub_chunk(q_ref, v_ref, lf_ref, o_ref, st_ref, r0, tri, False)
            return carry

        lax.fori_loop(0, nsub, body, 0, unroll=HGRN_UNROLL)

    @pl.when(d == 1)
    def _():
        tri = (cc >= rr).astype(jnp.bfloat16)

        def body(i, carry):
            r0 = pl.multiple_of((nsub - 1 - i) * HGRN_SUB, HGRN_SUB)
            _hgrn_sub_chunk(q_ref, v_ref, lf_ref, o_ref, st_ref, r0, tri, True)
            return carry

        lax.fori_loop(0, nsub, body, 0, unroll=HGRN_UNROLL)


def _hgrn_scan(q, v, lf, *, n_ctx_blocks):
    b, t, w = q.shape
    tb = HGRN_TB
    nblk = t // tb
    nlat = nblk - n_ctx_blocks

    def row_block(di, ti):
        ctx_blk = jnp.where(di == 0, nlat + ti, nblk - 1 - ti)
        lat_blk = jnp.where(di == 0, ti - n_ctx_blocks, nblk - 1 - ti)
        return jnp.where(ti < n_ctx_blocks, ctx_blk, lat_blk)

    return pl.pallas_call(
        functools.partial(_hgrn_kernel, tb=tb),
        grid=(b, 2, nblk),
        in_specs=[pl.BlockSpec((1, tb, w), lambda bi, di, ti: (bi, row_block(di, ti), 0)),
                  pl.BlockSpec((1, tb, w), lambda bi, di, ti: (bi, row_block(di, ti), 0)),
                  pl.BlockSpec((1, 1, tb, w), lambda bi, di, ti: (di, bi, row_block(di, ti), 0))],
        out_specs=pl.BlockSpec((1, 1, tb, w), lambda bi, di, ti: (di, bi, row_block(di, ti), 0)),
        out_shape=jax.ShapeDtypeStruct((2, b, t, w), jnp.float32),
        scratch_shapes=[pltpu.VMEM((C_H, C_DV, C_DK), jnp.float32)],
        compiler_params=_cparams(("arbitrary", "arbitrary", "arbitrary")),
        name="hgrn2_scan",
    )(q, v, lf)


def _head_rms(x, w, n_heads, width):
    out = []
    for hh in range(n_heads):
        blk = x[:, hh * width:(hh + 1) * width]
        out.append(_rms_rows(blk, w))
    return jnp.concatenate(out, axis=1)


def _merge_kernel(oa_ref, ob_ref, o2_ref, cg_ref, gt_ref, x_ref, ng_ref, ma_ref, mb_ref, dw_ref, hw_ref,
                  wbr_ref, wo_ref, xo_ref, ho_ref, *, diff_out_scale):
    ob = (_head_rms(ob_ref[...], dw_ref[...], B_H, B_VD) * diff_out_scale).astype(jnp.bfloat16)
    cg = cg_ref[...].astype(jnp.float32)
    oc = (_head_rms(o2_ref[0] + o2_ref[1], hw_ref[...], C_H, C_DV) * (cg * jax.nn.sigmoid(cg)))
    acc = None
    for i, o in enumerate((oa_ref[...], ob, oc.astype(jnp.bfloat16))):
        z = jnp.dot(o, wbr_ref[i], preferred_element_type=jnp.float32)
        g = jax.nn.sigmoid(gt_ref[:, i * D_MODEL:(i + 1) * D_MODEL].astype(jnp.float32))
        acc = g * z if acc is None else acc + g * z
    y = jnp.dot(acc.astype(jnp.bfloat16), wo_ref[...], preferred_element_type=jnp.float32)
    ma, mb = ma_ref[0], mb_ref[0]
    for half, (xn, mod) in enumerate(zip(_residual_tile(x_ref, y, ng_ref[1], ma, mb, 2), (ma, mb))):
        rows = slice(half * HALF_TM, (half + 1) * HALF_TM)
        xo_ref[rows, :] = xn
        ho_ref[rows, :] = _rms_rows(xn, ng_ref[2]) * (1.0 + mod[4]) + mod[3]


def _merge(oa, ob, o2, cg, gates, x, norm_g, mods, diff_w, hgrn_w, w_br, w_o, *, t_rows, seq, diff_out_scale):
    m = x.shape[0]
    tm = MM_TM
    row = lambda i: (i, 0)
    f32 = jnp.float32
    return pl.pallas_call(
        functools.partial(_merge_kernel, diff_out_scale=diff_out_scale),
        grid=(m // tm,),
        in_specs=[pl.BlockSpec((tm, BRANCH_W), row), pl.BlockSpec((tm, BRANCH_W), row),
                  pl.BlockSpec((2, tm, BRANCH_W), lambda i: (0, i, 0)), pl.BlockSpec((tm, BRANCH_W), row),
                  pl.BlockSpec((tm, GATE_W), row), pl.BlockSpec((tm, D_MODEL), row),
                  pl.BlockSpec((4, 1, D_MODEL), lambda i: (0, 0, 0))] + _mod_specs(t_rows, seq) + [
                  pl.BlockSpec((1, B_VD), lambda i: (0, 0)), pl.BlockSpec((1, C_DV), lambda i: (0, 0)),
                  pl.BlockSpec((N_BRANCH, BRANCH_W, D_MODEL), lambda i: (0, 0, 0)),
                  pl.BlockSpec((D_MODEL, D_MODEL), lambda i: (0, 0))],
        out_specs=[pl.BlockSpec((tm, D_MODEL), row), pl.BlockSpec((tm, D_MODEL), row)],
        out_shape=[jax.ShapeDtypeStruct((m, D_MODEL), f32), jax.ShapeDtypeStruct((m, D_MODEL), f32)],
        compiler_params=_cparams(("arbitrary",)),
        name="branch_merge",
    )(oa, ob, o2, cg, gates, x, norm_g, mods, mods, diff_w, hgrn_w, w_br, w_o)


def _moe_kernel(blk_e_ref, n_used_ref, x_ref, wgu_ref, bgu_ref, wdn_ref, bdn_ref, *rest, first_blk, cb):
    y_ref, wgu_sc, wdn_sc = rest[-3:]
    i = pl.program_id(0)
    n_used = n_used_ref[0]
    gi = first_blk + jnp.minimum(i, cb - 1)
    e = blk_e_ref[gi]
    e_prev = blk_e_ref[jnp.maximum(gi - 1, 0)]

    @pl.when((i == 0) | (e != e_prev))
    def _():
        wgu_sc[...] = wgu_ref[0, 0].astype(jnp.bfloat16)
        wdn_sc[...] = wdn_ref[0, 0].astype(jnp.bfloat16)

    @pl.when(i < n_used)
    def _():
        x = x_ref[...].astype(jnp.bfloat16)
        gu = jnp.dot(x, wgu_sc[...], preferred_element_type=jnp.float32) + bgu_ref[0, 0]
        g = jnp.minimum(gu[:, :EXPERT_FF], SWIGLU_LIMIT)
        u = jnp.clip(gu[:, EXPERT_FF:], -SWIGLU_LIMIT, SWIGLU_LIMIT)
        act = (u + 1.0) * (g * jax.nn.sigmoid(SWIGLU_ALPHA * g))
        y = jnp.dot(act.astype(jnp.bfloat16), wdn_sc[...], preferred_element_type=jnp.float32) + bdn_ref[0, 0]
        y_ref[...] = y.astype(y_ref.dtype)

    @pl.when(i >= n_used)
    def _():
        y_ref[...] = jnp.zeros_like(y_ref)


def _moe_experts(blk_e, n_used, xs, y_prev, w_gu, b_gu, w_dn, b_dn, *, layer, chunk):
    dm = xs.shape[1]
    tm = MOE_TM
    n_blk = blk_e.shape[0]
    cb = xs.shape[0] // tm
    first = chunk * cb
    expert = lambda i, be: be[first + jnp.minimum(i, cb - 1)]
    in_specs = [pl.BlockSpec((tm, dm), lambda i, be, nu: (jnp.minimum(i, cb - 1), 0)),
                pl.BlockSpec((1, 1, dm, 2 * EXPERT_FF), lambda i, be, nu: (layer, expert(i, be), 0, 0)),
                pl.BlockSpec((1, 1, 1, 2 * EXPERT_FF), lambda i, be, nu: (layer, expert(i, be), 0, 0)),
                pl.BlockSpec((1, 1, EXPERT_FF, dm), lambda i, be, nu: (layer, expert(i, be), 0, 0)),
                pl.BlockSpec((1, 1, 1, dm), lambda i, be, nu: (layer, expert(i, be), 0, 0))]
    args = [blk_e, n_used, xs, w_gu, b_gu.reshape(DEPTH, N_EXPERTS, 1, -1), w_dn,
            b_dn.reshape(DEPTH, N_EXPERTS, 1, -1)]
    aliases = {}
    if y_prev is not None:
        in_specs.append(pl.BlockSpec(memory_space=pl.ANY))
        args.append(y_prev)
        aliases = {len(args) - 1: 0}
    return pl.pallas_call(
        functools.partial(_moe_kernel, first_blk=first, cb=cb),
        grid_spec=pltpu.PrefetchScalarGridSpec(
            num_scalar_prefetch=2, grid=(n_blk if y_prev is None else cb,),
            in_specs=in_specs,
            out_specs=pl.BlockSpec((tm, dm), lambda i, be, nu: (first + i, 0)),
            scratch_shapes=[pltpu.VMEM((dm, 2 * EXPERT_FF), jnp.bfloat16),
                            pltpu.VMEM((EXPERT_FF, dm), jnp.bfloat16)]),
        out_shape=jax.ShapeDtypeStruct((n_blk * tm, dm), jnp.bfloat16),
        input_output_aliases=aliases,
        compiler_params=_cparams(("arbitrary",)),
        name="moe_experts",
    )(*args)


def _combine_kernel(*refs):
    yk_refs = refs[:TOP_K]
    gt_ref, x_ref, ng_ref, ma_ref, mb_ref, xo_ref = refs[TOP_K:]
    gt = gt_ref[...]
    f = None
    for k in range(TOP_K):
        term = gt[:, k:k + 1] * yk_refs[k][...].astype(jnp.float32)
        f = term if f is None else f + term
    for half, xn in enumerate(_residual_tile(x_ref, f, ng_ref[3], ma_ref[0], mb_ref[0], 5)):
        xo_ref[half * HALF_TM:(half + 1) * HALF_TM, :] = xn


def _moe_combine(yk, gates, x, norm_g, mods, *, t_rows, seq):
    m = x.shape[0]
    tm = MM_TM
    n_tiles = m // tm
    row = lambda i: (i, 0)
    choice = lambda k: pl.BlockSpec((tm, D_MODEL), lambda i: (k * n_tiles + i, 0))
    return pl.pallas_call(
        _combine_kernel,
        grid=(n_tiles,),
        in_specs=[choice(k) for k in range(TOP_K)] + [
            pl.BlockSpec((tm, TOP_K), row), pl.BlockSpec((tm, D_MODEL), row),
            pl.BlockSpec((4, 1, D_MODEL), lambda i: (0, 0, 0))] + _mod_specs(t_rows, seq),
        out_specs=pl.BlockSpec((tm, D_MODEL), row),
        out_shape=jax.ShapeDtypeStruct((m, D_MODEL), jnp.float32),
        compiler_params=_cparams(("arbitrary",)),
        name="moe_combine",
    )(*([yk] * TOP_K), gates, x, norm_g, mods, mods)


def _moe_ffn(h, x, norm_g, mods, w_router, b_router, w_gu, b_gu, w_dn, b_dn, *, layer, t_rows, seq):
    n, dm = h.shape
    nk = n * TOP_K
    w_r = jnp.zeros((dm, LANES), jnp.float32).at[:, :N_EXPERTS].set(w_router)
    logits = _matmul(h, w_r, tm=MM_TM, tn=LANES, out_dtype=jnp.float32,
                     precision=lax.Precision.HIGHEST, name="router")[:, :N_EXPERTS] + b_router
    top_v, top_e = lax.top_k(logits, TOP_K)
    gates = jax.nn.softmax(top_v, axis=-1)
    flat_e = top_e.reshape(-1).astype(jnp.int32)
    experts = jnp.arange(N_EXPERTS, dtype=jnp.int32)
    onehot = (flat_e[:, None] == experts[None, :]).astype(jnp.int32)
    incl = jnp.cumsum(onehot, axis=0)
    rank = jnp.sum(incl * onehot, axis=1) - 1
    counts = incl[-1]
    padded = (counts + MOE_TM - 1) // MOE_TM * MOE_TM
    pad_end = jnp.cumsum(padded)
    pad_start = pad_end - padded
    grp_start = jnp.cumsum(counts) - counts
    dest = jnp.sum(onehot * pad_start[None, :], axis=1) + rank
    n_blk = (nk + MOE_TM - 1) // MOE_TM + N_EXPERTS
    rows = n_blk * MOE_TM
    blk_start = jnp.arange(n_blk, dtype=jnp.int32) * MOE_TM
    blk_e = jnp.minimum(jnp.sum((pad_end[None, :] <= blk_start[:, None]).astype(jnp.int32), axis=1),
                        N_EXPERTS - 1)
    n_used = (pad_end[-1] // MOE_TM).astype(jnp.int32).reshape(1)
    order = jnp.argsort(flat_e)
    row_e = jnp.repeat(blk_e, MOE_TM)
    sorted_pos = jnp.arange(rows, dtype=jnp.int32) - pad_start[row_e] + grp_start[row_e]
    row_tok = (order[jnp.clip(sorted_pos, 0, nk - 1)] // TOP_K).astype(jnp.int32)
    assert n_blk % MOE_CHUNKS == 0, (n_blk, MOE_CHUNKS)
    cb = n_blk // MOE_CHUNKS
    y = None
    for ci in range(MOE_CHUNKS):
        xs = h[row_tok[ci * cb * MOE_TM:(ci + 1) * cb * MOE_TM]]
        y = _moe_experts(blk_e, jnp.clip(n_used - ci * cb, 0, cb), xs, y, w_gu, b_gu, w_dn, b_dn,
                         layer=layer, chunk=ci)
    yk = y[dest.reshape(n, TOP_K).T.reshape(-1)]
    return _moe_combine(yk, gates, x, norm_g, mods, t_rows=t_rows, seq=seq)


def _rope_tables(rows, n_ctx, batch):
    row = jnp.repeat(jnp.arange(rows, dtype=jnp.float32), GRID_W)
    col = jnp.tile(jnp.arange(GRID_W, dtype=jnp.float32), rows)
    n_freq = HEAD_DIM // 4
    inv = ROPE_THETA ** (-jnp.arange(n_freq, dtype=jnp.float32) / n_freq)
    ang_r = row[:, None] * inv
    ang_c = col[:, None] * inv
    cos_h = jnp.concatenate([jnp.cos(ang_r), jnp.cos(ang_r), jnp.cos(ang_c), jnp.cos(ang_c)], axis=1)
    sin_h = jnp.concatenate([-jnp.sin(ang_r), jnp.sin(ang_r), -jnp.sin(ang_c), jnp.sin(ang_c)], axis=1)
    cos_t = jnp.concatenate([cos_h, jnp.ones((n_ctx, HEAD_DIM), jnp.float32)], axis=0)
    sin_t = jnp.concatenate([sin_h, jnp.zeros((n_ctx, HEAD_DIM), jnp.float32)], axis=0)
    reps = (batch, LANES // HEAD_DIM)
    return jnp.tile(cos_t, reps), jnp.tile(sin_t, reps)


def _rope_partner(n_cols):
    idx = jnp.arange(n_cols)
    quarter = HEAD_DIM // 4
    return jnp.where((idx % (2 * quarter)) < quarter, idx + quarter, idx - quarter)


def kernel(x, c, ctx, c_ctx, w_mod, b_mod, norm_g, w_in, attn_sink, diff_lambda, diff_norm_w,
           hgrn_lb_logits, hgrn_norm_w, w_branch, w_out, w_router, b_router, w_gate_up, b_gate_up,
           w_down, b_down):
    b, s, dm = x.shape
    l = ctx.shape[1]
    t = s + l
    f32 = jnp.float32
    bf16 = jnp.bfloat16
    cos_t, sin_t = _rope_tables(s // GRID_W, l, b)
    lb_cum = jnp.cumsum(jax.nn.softmax(hgrn_lb_logits.astype(f32), axis=0), axis=0)
    lower_bounds = lb_cum - lb_cum[0:1]
    cond_rows = 16
    cond = jnp.zeros((cond_rows, dm), f32).at[:b].set(jax.nn.silu(c)).at[b].set(jax.nn.silu(c_ctx))
    scale = HEAD_DIM ** -0.5
    xa = jnp.concatenate([x, ctx], axis=1).reshape(b * t, dm)
    h_moe = None
    for layer in range(DEPTH):
        lam_init = 0.8 - 0.6 * math.exp(-0.3 * layer)
        mod_all = _matmul(cond, w_mod[layer], tm=cond_rows, tn=3 * dm, out_dtype=f32,
                          precision=lax.Precision.HIGHEST, name="adaln") + b_mod[layer]
        mod = mod_all[:b].reshape(b, 1, 6, 1, dm)
        mod_c = jnp.broadcast_to(mod_all[b:b + 1].reshape(1, 1, 6, 1, dm), (b, 1, 6, 1, dm))
        mods = jnp.concatenate([mod, mod_c], axis=1).reshape(2 * b, 6, 1, dm)
        ng = norm_g[layer].reshape(4, 1, dm)
        g0 = ng[0:1]
        geo = dict(t_rows=t, seq=s)

        w = w_in[layer]
        w_aq = (w[:, OFF_A:OFF_A + A_Q_W] * scale).reshape(dm, A_HKV, A_GROUP, HEAD_DIM)
        pad = jnp.zeros((dm, A_GROUP, HEAD_DIM), f32)
        w_aq8 = jnp.concatenate([jnp.concatenate([w_aq[:, 0], pad], axis=-1),
                                 jnp.concatenate([pad, w_aq[:, 1]], axis=-1)], axis=1).reshape(dm, A_Q8_W)
        w_a = jnp.concatenate([w_aq8, w[:, OFF_A + A_Q_W:OFF_A + A_W]], axis=1).astype(bf16)
        w_b = w[:, OFF_B:OFF_B + B_W].at[:, :B_QK_W].multiply(scale * LOG2_E).astype(bf16)
        proj_a = _proj_rope(xa, g0, mods, w_a, w_a[:, _rope_partner(A_ROPE_W2)], cos_t, sin_t,
                            rope_w=A_ROPE_W2, name="proj_a", **geo).reshape(b, t, A_W2)
        bq, bk_hm, bv1_hm = _proj_b(xa, g0, mods, w_b, w_b[:, _rope_partner(B_ROPE_W)], cos_t, sin_t, **geo)
        bq = bq.reshape(b, t, B_QK_W)
        lb = lower_bounds[layer]
        lb_par = jnp.stack([jnp.log(lb[0]), jnp.log1p(-lb[0]), jnp.log(lb[1]), jnp.log1p(-lb[1])])
        cq, ci, lf, cg = _proj_hgrn(xa, g0, mods, w[:, OFF_C:OFF_C + C_W].astype(bf16),
                                    lb_par.reshape(4, 1, C_K_W), **geo)
        gates = _proj_plain(xa, g0, mods, w[:, OFF_GATE:].astype(bf16), name="gate_proj", **geo)

        sink32 = attn_sink[layer].astype(f32)
        o_a = _window_attention(proj_a, sink32, seq=s)

        lp = diff_lambda[layer].astype(f32)
        lam = (jnp.exp(jnp.sum(lp[0] * lp[1])) - jnp.exp(jnp.sum(lp[2] * lp[3])) + lam_init).reshape(1).astype(f32)
        o_b = _diff_attention(lam, bq, bk_hm, bv1_hm, tq=FLASH_TQ, tk=FLASH_TK, q_block_offset=0,
                              n_q_blocks=s // FLASH_TQ, k_block_offset=0, nk=t // FLASH_TK)
        o_bx = _diff_attention(lam, bq, bk_hm, bv1_hm, tq=l, tk=l, q_block_offset=s // l, n_q_blocks=1,
                               k_block_offset=s // l, nk=1)
        o_b = jnp.concatenate([o_b, o_bx], axis=1)

        o2 = _hgrn_scan(cq.reshape(b, t, C_K_W), ci.reshape(b, t, C_V_W), lf.reshape(2, b, t, C_K_W),
                        n_ctx_blocks=l // HGRN_TB)

        col = jnp.arange(A_Q_W)
        head_of_col = ((col % LANES) // HEAD_DIM) * A_GROUP + col // LANES
        w_br = w_branch[layer].at[0].set(w_branch[layer][0][head_of_col * HEAD_DIM + col % HEAD_DIM])
        xa, h_moe = _merge(o_a.reshape(b * t, A_Q_W), o_b.reshape(b * t, B_V_W), o2.reshape(2, b * t, C_V_W), cg,
                           gates, xa, ng, mods, diff_norm_w[layer].reshape(1, B_VD).astype(f32),
                           hgrn_norm_w[layer].reshape(1, C_DV).astype(f32),
                           w_br.astype(bf16), w_out[layer].astype(bf16),
                           diff_out_scale=1 - lam_init, **geo)

        xa = _moe_ffn(h_moe, xa, ng, mods, w_router[layer], b_router[layer], w_gate_up, b_gate_up,
                      w_down, b_down, layer=layer, **geo)
    return xa.reshape(b, t, dm)[:, :s]
```

```python
import functools
import math

import jax
import jax.numpy as jnp
from jax import lax
from jax.experimental import pallas as pl
from jax.experimental.pallas import tpu as pltpu

D_MODEL = 1024
DEPTH = 2
GRID_W = 64
HEAD_DIM = 64
ROPE_THETA = 10000.0
RMS_EPS = 1e-6
NEG_INF = -1e30
BLOCK = 128
A_HQ = 8
A_HKV = 2
A_GROUP = A_HQ // A_HKV
WINDOW = 128
B_H = 4
B_HD = HEAD_DIM
B_VD = 2 * HEAD_DIM
C_H = 4
C_DK = 128
C_DV = 128
N_BRANCH = 3
BRANCH_W = 512
A_Q_W = A_HQ * HEAD_DIM
A_KV_W = A_HKV * HEAD_DIM
B_QK_W = B_H * 2 * B_HD
B_V_W = B_H * B_VD
C_K_W = C_H * C_DK
C_V_W = C_H * C_DV
GATE_W = N_BRANCH * D_MODEL
N_EXPERTS = 32
TOP_K = 4
EXPERT_FF = D_MODEL
SWIGLU_LIMIT = 7.0
SWIGLU_ALPHA = 1.702

A_W = A_Q_W + 2 * A_KV_W
A_ROPE_W = A_Q_W + A_KV_W
B_W = 2 * B_QK_W + B_V_W
B_ROPE_W = 2 * B_QK_W
C_W = 3 * C_K_W + 2 * C_V_W
OFF_A = 0
OFF_B = OFF_A + A_W
OFF_C = OFF_B + B_W
OFF_GATE = OFF_C + C_W

V7X_VMEM_LIMIT_BYTES = 56 * 1024 * 1024
LANES = 128
LOG2_E = math.log2(math.e)

MM_TM = 512
HALF_TM = MM_TM // 2
FLASH_TQ = 2048
FLASH_TK = 768
HGRN_TB = 256
HGRN_SUB = 16
HGRN_UNROLL = 4
MOE_TM = 512


def _cparams(sem):
    return pltpu.CompilerParams(dimension_semantics=sem, vmem_limit_bytes=V7X_VMEM_LIMIT_BYTES)


def _rms_rows(x, g):
    return (x * lax.rsqrt(jnp.mean(x * x, axis=-1, keepdims=True) + RMS_EPS)) * g


def _modulated_tile(x_ref, g, mod_a, mod_b, shift_slot, scale_slot):
    out = []
    for half, mod in enumerate((mod_a, mod_b)):
        x = x_ref[half * HALF_TM:(half + 1) * HALF_TM, :]
        out.append(_rms_rows(x, g) * (1.0 + mod[scale_slot]) + mod[shift_slot])
    return jnp.concatenate(out, axis=0)


def _residual_tile(x_ref, y, g, mod_a, mod_b, gate_slot):
    out = []
    for half, mod in enumerate((mod_a, mod_b)):
        rows = slice(half * HALF_TM, (half + 1) * HALF_TM)
        out.append(x_ref[rows, :] + mod[gate_slot] * _rms_rows(y[rows, :], g))
    return out


def _mod_specs(t_rows, seq):
    halves_per_batch = t_rows // HALF_TM
    lat_halves = seq // HALF_TM

    def row(u):
        return (u // halves_per_batch) * 2 + jnp.where(u % halves_per_batch >= lat_halves, 1, 0)

    blk = (1, 6, 1, D_MODEL)
    return [pl.BlockSpec(blk, lambda i: (row(2 * i), 0, 0, 0)),
            pl.BlockSpec(blk, lambda i: (row(2 * i + 1), 0, 0, 0))]


def _proj_rope_kernel(x_ref, g_ref, ma_ref, mb_ref, w_ref, wsw_ref, cos_ref, sin_ref, o_ref, *, rope_w):
    h = _modulated_tile(x_ref, g_ref[0], ma_ref[0], mb_ref[0], 0, 1).astype(jnp.bfloat16)
    z = jnp.dot(h, w_ref[...], preferred_element_type=jnp.float32)
    zs = jnp.dot(h, wsw_ref[...], preferred_element_type=jnp.float32)
    reps = rope_w // LANES
    rot = z[:, :rope_w] * jnp.tile(cos_ref[...], (1, reps)) + zs * jnp.tile(sin_ref[...], (1, reps))
    o_ref[:, :rope_w] = rot.astype(o_ref.dtype)
    o_ref[:, rope_w:] = z[:, rope_w:].astype(o_ref.dtype)


def _proj_b_kernel(x_ref, g_ref, ma_ref, mb_ref, w_ref, wsw_ref, cos_ref, sin_ref, q_ref, k_ref, v1_ref):
    h = _modulated_tile(x_ref, g_ref[0], ma_ref[0], mb_ref[0], 0, 1).astype(jnp.bfloat16)
    z = jnp.dot(h, w_ref[...], preferred_element_type=jnp.float32)
    zs = jnp.dot(h, wsw_ref[...], preferred_element_type=jnp.float32)
    reps = B_ROPE_W // LANES
    rot = z[:, :B_ROPE_W] * jnp.tile(cos_ref[...], (1, reps)) + zs * jnp.tile(sin_ref[...], (1, reps))
    q_ref[...] = rot[:, :B_QK_W].astype(q_ref.dtype)
    for hh in range(B_H):
        k_ref[hh] = rot[:, B_QK_W + hh * 2 * B_HD:B_QK_W + (hh + 1) * 2 * B_HD].astype(k_ref.dtype)
        v1_ref[hh, :, :B_VD] = z[:, B_ROPE_W + hh * B_VD:B_ROPE_W + (hh + 1) * B_VD].astype(v1_ref.dtype)
        v1_ref[hh, :, B_VD:] = jnp.ones((MM_TM, B_VD), v1_ref.dtype)


def _proj_plain_kernel(x_ref, g_ref, ma_ref, mb_ref, w_ref, o_ref):
    h = _modulated_tile(x_ref, g_ref[0], ma_ref[0], mb_ref[0], 0, 1).astype(jnp.bfloat16)
    o_ref[...] = jnp.dot(h, w_ref[...], preferred_element_type=jnp.float32).astype(o_ref.dtype)


def _log_forget(z, log_lb, log_1m_lb):
    log_sig = jnp.minimum(z, 0.0) - jnp.log(1.0 + jnp.exp(-jnp.abs(z)))
    c = log_1m_lb + log_sig
    return jnp.maximum(log_lb, c) + jnp.log(1.0 + jnp.exp(-jnp.abs(log_lb - c)))


def _proj_hgrn_kernel(x_ref, g_ref, ma_ref, mb_ref, w_ref, lb_ref, q_ref, v_ref, lf_ref, cg_ref):
    h = _modulated_tile(x_ref, g_ref[0], ma_ref[0], mb_ref[0], 0, 1).astype(jnp.bfloat16)
    z = jnp.dot(h, w_ref[...], preferred_element_type=jnp.float32)
    zq = z[:, :C_K_W]
    q_ref[...] = zq * jax.nn.sigmoid(zq)
    v_ref[...] = z[:, C_K_W:C_K_W + C_V_W]
    off = C_K_W + C_V_W
    lf_ref[0] = _log_forget(z[:, off:off + C_K_W], lb_ref[0], lb_ref[1])
    lf_ref[1] = _log_forget(z[:, off + C_K_W:off + 2 * C_K_W], lb_ref[2], lb_ref[3])
    cg_ref[...] = z[:, off + 2 * C_K_W:].astype(cg_ref.dtype)


def _proj_common_specs(t_rows, seq):
    row = lambda i: (i, 0)
    return [pl.BlockSpec((MM_TM, D_MODEL), row),
            pl.BlockSpec((1, 1, D_MODEL), lambda i: (0, 0, 0))] + _mod_specs(t_rows, seq)


def _proj_rope(x, g, mods, w, wsw, cos, sin, *, t_rows, seq, rope_w, name):
    m = x.shape[0]
    n = w.shape[1]
    row = lambda i: (i, 0)
    whole = lambda i: (0, 0)
    return pl.pallas_call(
        functools.partial(_proj_rope_kernel, rope_w=rope_w),
        grid=(m // MM_TM,),
        in_specs=_proj_common_specs(t_rows, seq) + [
            pl.BlockSpec((D_MODEL, n), whole), pl.BlockSpec((D_MODEL, rope_w), whole),
            pl.BlockSpec((MM_TM, LANES), row), pl.BlockSpec((MM_TM, LANES), row)],
        out_specs=pl.BlockSpec((MM_TM, n), row),
        out_shape=jax.ShapeDtypeStruct((m, n), jnp.bfloat16),
        compiler_params=_cparams(("arbitrary",)),
        name=name,
    )(x, g, mods, mods, w, wsw, cos, sin)


def _proj_b(x, g, mods, w, wsw, cos, sin, *, t_rows, seq):
    m = x.shape[0]
    row = lambda i: (i, 0)
    whole = lambda i: (0, 0)
    heads = lambda i: (0, i, 0)
    bf16 = jnp.bfloat16
    return pl.pallas_call(
        _proj_b_kernel,
        grid=(m // MM_TM,),
        in_specs=_proj_common_specs(t_rows, seq) + [
            pl.BlockSpec((D_MODEL, B_W), whole), pl.BlockSpec((D_MODEL, B_ROPE_W), whole),
            pl.BlockSpec((MM_TM, LANES), row), pl.BlockSpec((MM_TM, LANES), row)],
        out_specs=[pl.BlockSpec((MM_TM, B_QK_W), row), pl.BlockSpec((B_H, MM_TM, 2 * B_HD), heads),
                   pl.BlockSpec((B_H, MM_TM, 2 * B_VD), heads)],
        out_shape=[jax.ShapeDtypeStruct((m, B_QK_W), bf16), jax.ShapeDtypeStruct((B_H, m, 2 * B_HD), bf16),
                   jax.ShapeDtypeStruct((B_H, m, 2 * B_VD), bf16)],
        compiler_params=_cparams(("arbitrary",)),
        name="proj_b",
    )(x, g, mods, mods, w, wsw, cos, sin)


def _proj_plain(x, g, mods, w, *, t_rows, seq, name):
    m = x.shape[0]
    n = w.shape[1]
    row = lambda i: (i, 0)
    return pl.pallas_call(
        _proj_plain_kernel,
        grid=(m // MM_TM,),
        in_specs=_proj_common_specs(t_rows, seq) + [pl.BlockSpec((D_MODEL, n), lambda i: (0, 0))],
        out_specs=pl.BlockSpec((MM_TM, n), row),
        out_shape=jax.ShapeDtypeStruct((m, n), jnp.bfloat16),
        compiler_params=_cparams(("arbitrary",)),
        name=name,
    )(x, g, mods, mods, w)


def _proj_hgrn(x, g, mods, w, lb_par, *, t_rows, seq):
    m = x.shape[0]
    row = lambda i: (i, 0)
    f32 = jnp.float32
    return pl.pallas_call(
        _proj_hgrn_kernel,
        grid=(m // MM_TM,),
        in_specs=_proj_common_specs(t_rows, seq) + [
            pl.BlockSpec((D_MODEL, C_W), lambda i: (0, 0)),
            pl.BlockSpec((4, 1, C_K_W), lambda i: (0, 0, 0))],
        out_specs=[pl.BlockSpec((MM_TM, C_K_W), row), pl.BlockSpec((MM_TM, C_V_W), row),
                   pl.BlockSpec((2, MM_TM, C_K_W), lambda i: (0, i, 0)), pl.BlockSpec((MM_TM, C_V_W), row)],
        out_shape=[jax.ShapeDtypeStruct((m, C_K_W), f32), jax.ShapeDtypeStruct((m, C_V_W), f32),
                   jax.ShapeDtypeStruct((2, m, C_K_W), f32), jax.ShapeDtypeStruct((m, C_V_W), jnp.bfloat16)],
        compiler_params=_cparams(("arbitrary",)),
        name="hgrn_proj",
    )(x, g, mods, mods, w, lb_par)


def _mm_kernel(x_ref, w_ref, o_ref, *, precision):
    o_ref[...] = jnp.dot(x_ref[...], w_ref[...], preferred_element_type=jnp.float32,
                         precision=precision).astype(o_ref.dtype)


def _matmul(x, w, *, tm, tn, out_dtype, precision=None, name):
    m, k = x.shape
    n = w.shape[1]
    assert m % tm == 0 and n % tn == 0, (m, tm, n, tn)
    return pl.pallas_call(
        functools.partial(_mm_kernel, precision=precision),
        grid=(n // tn, m // tm),
        in_specs=[pl.BlockSpec((tm, k), lambda j, i: (i, 0)),
                  pl.BlockSpec((k, tn), lambda j, i: (0, j))],
        out_specs=pl.BlockSpec((tm, tn), lambda j, i: (i, j)),
        out_shape=jax.ShapeDtypeStruct((m, n), out_dtype),
        compiler_params=_cparams(("arbitrary", "arbitrary")),
        name=name,
    )(x, w)


A_Q8_W = A_HQ * LANES
A_W2 = A_Q8_W + 2 * A_KV_W
A_ROPE_W2 = A_Q8_W + A_KV_W
A_K_BLK = A_Q8_W // A_KV_W
A_V_BLK = A_K_BLK + 1


def _attn_a_kernel(sink_ref, q_ref, kp_ref, kc_ref, kn_ref, vp_ref, vc_ref, vn_ref, kx_ref, vx_ref,
                   o_ref, *, seq):
    n = pl.program_id(1)
    k_all = jnp.concatenate([kp_ref[0], kc_ref[0], kn_ref[0], kx_ref[0]], axis=0)
    v_all = jnp.concatenate([vp_ref[0], vc_ref[0], vn_ref[0], vx_ref[0]], axis=0)
    n_keys = k_all.shape[0]
    qi = lax.broadcasted_iota(jnp.int32, (BLOCK, n_keys), 0)
    kj = lax.broadcasted_iota(jnp.int32, (BLOCK, n_keys), 1)
    kpos = n * BLOCK + kj - BLOCK
    local_ok = (jnp.abs(kj - BLOCK - qi) <= WINDOW) & (kpos >= 0) & (kpos < seq) & (n * BLOCK < seq)
    mask = local_ok | (kj >= 3 * BLOCK)
    q8 = jnp.concatenate([q_ref[0, :, hq * LANES:(hq + 1) * LANES] for hq in range(A_HQ)], axis=0)
    s = lax.dot_general(q8, k_all, (((1,), (1,)), ((), ())), preferred_element_type=jnp.float32)
    p_blocks, inv_l = [], []
    for hq in range(A_HQ):
        sh = jnp.where(mask, s[hq * BLOCK:(hq + 1) * BLOCK, :], NEG_INF)
        sk = sink_ref[hq]
        m = jnp.maximum(jnp.max(sh, axis=-1, keepdims=True), sk)
        p = jnp.exp(sh - m)
        inv_l.append(1.0 / (jnp.sum(p, axis=-1, keepdims=True) + jnp.exp(sk - m)))
        p_blocks.append(p.astype(v_all.dtype))
    o = jnp.dot(jnp.concatenate(p_blocks, axis=0), v_all, preferred_element_type=jnp.float32)
    lane = lax.broadcasted_iota(jnp.int32, (BLOCK, LANES), 1)
    for g in range(A_GROUP):
        lo = o[g * BLOCK:(g + 1) * BLOCK, :] * inv_l[g]
        hi = o[(A_GROUP + g) * BLOCK:(A_GROUP + g + 1) * BLOCK, :] * inv_l[A_GROUP + g]
        o_ref[0, :, g * LANES:(g + 1) * LANES] = jnp.where(lane < HEAD_DIM, lo, hi).astype(o_ref.dtype)


def _window_attention(proj_a, sink, *, seq):
    b, t, _ = proj_a.shape
    l = t - seq
    nb = seq // BLOCK
    ctx_blk = seq // l
    q_map = lambda bi, n, *_: (bi, n, 0)
    kv = lambda col, fn: pl.BlockSpec((1, BLOCK, A_KV_W), lambda bi, n, *_: (bi, fn(n), col))
    prev = lambda n: jnp.clip(n - 1, 0, nb - 1)
    cur = lambda n: jnp.minimum(n, nb - 1)
    nxt = lambda n: jnp.minimum(n + 1, nb - 1)
    ctx = lambda col: pl.BlockSpec((1, l, A_KV_W), lambda bi, n, *_: (bi, ctx_blk, col))
    return pl.pallas_call(
        functools.partial(_attn_a_kernel, seq=seq),
        grid_spec=pltpu.PrefetchScalarGridSpec(
            num_scalar_prefetch=1, grid=(b, t // BLOCK),
            in_specs=[pl.BlockSpec((1, BLOCK, A_Q8_W), q_map),
                      kv(A_K_BLK, prev), kv(A_K_BLK, cur), kv(A_K_BLK, nxt),
                      kv(A_V_BLK, prev), kv(A_V_BLK, cur), kv(A_V_BLK, nxt),
                      ctx(A_K_BLK), ctx(A_V_BLK)],
            out_specs=pl.BlockSpec((1, BLOCK, A_Q_W), q_map)),
        out_shape=jax.ShapeDtypeStruct((b, t, A_Q_W), jnp.bfloat16),
        compiler_params=_cparams(("arbitrary", "arbitrary")),
        name="window_gqa",
    )(sink, *([proj_a] * 9))


def _diff_flash_kernel(lam_ref, q_ref, k_ref, v_ref, o_ref, q2_sc, sa_sc, sb_sc, m_sc, acc_sc, *, tq, tk, nk):
    u = pl.program_id(2)
    new_q = u % nk == 0

    @pl.when(u == 0)
    def _():
        m_sc[...] = jnp.full_like(m_sc, NEG_INF)
        acc_sc[...] = jnp.zeros_like(acc_sc)
        sb_sc[...] = jnp.full_like(sb_sc, NEG_INF)

    @pl.when(new_q)
    def _():
        q = q_ref[0]
        lane = lax.broadcasted_iota(jnp.int32, q.shape, 1)
        q2_sc[:tq, :] = jnp.where(lane < B_HD, q, jnp.zeros_like(q))
        q2_sc[tq:, :] = jnp.where(lane >= B_HD, q, jnp.zeros_like(q))

    def step(s_new_ref, s_old_ref):
        m_prev = m_sc[...]
        m_new = jnp.maximum(m_prev, jnp.max(s_old_ref[...], axis=-1, keepdims=True))
        alpha = jnp.exp2(m_prev - m_new)
        p = jnp.exp2(s_old_ref[...] - jnp.tile(m_new, (1, tk // LANES))).astype(jnp.bfloat16)
        acc_sc[...] = jnp.tile(alpha, (1, 2)) * acc_sc[...] + jnp.dot(
            p, v_ref[0], preferred_element_type=jnp.float32)
        m_sc[...] = m_new
        s_new_ref[...] = lax.dot_general(q2_sc[...], k_ref[0], (((1,), (1,)), ((), ())),
                                         preferred_element_type=jnp.float32)

    @pl.when(u % 2 == 0)
    def _():
        step(sa_sc, sb_sc)

    @pl.when(u % 2 == 1)
    def _():
        step(sb_sc, sa_sc)

    @pl.when(new_q & (u > 0))
    def _():
        acc = acc_sc[...]
        o = acc[:, :B_VD] / acc[:, B_VD:]
        o_ref[0] = (o[:tq, :] - lam_ref[0] * o[tq:, :]).astype(o_ref.dtype)
        m_sc[...] = jnp.full_like(m_sc, NEG_INF)
        acc_sc[...] = jnp.zeros_like(acc_sc)


def _diff_attention(lam, q, k_hm, v1_hm, *, tq, tk, q_block_offset, n_q_blocks, k_block_offset, nk):
    b, t, _ = q.shape
    kb = t // tk
    n_steps = n_q_blocks * nk + 1
    k_blk = lambda bi, u: bi * kb + k_block_offset + u % nk
    in_specs = [pl.BlockSpec((1, tq, 2 * B_HD),
                             lambda bi, h, u, *_: (bi, jnp.minimum(u // nk, n_q_blocks - 1) + q_block_offset, h)),
                pl.BlockSpec((1, tk, 2 * B_HD), lambda bi, h, u, *_: (h, k_blk(bi, u), 0)),
                pl.BlockSpec((1, tk, 2 * B_VD), lambda bi, h, u, *_: (h, k_blk(bi, jnp.maximum(u - 1, 0)), 0))]
    return pl.pallas_call(
        functools.partial(_diff_flash_kernel, tq=tq, tk=tk, nk=nk),
        grid_spec=pltpu.PrefetchScalarGridSpec(
            num_scalar_prefetch=1, grid=(b, B_H, n_steps),
            in_specs=in_specs,
            out_specs=pl.BlockSpec((1, tq, B_VD), lambda bi, h, u, *_: (bi, jnp.maximum(u - 1, 0) // nk, h)),
            scratch_shapes=[pltpu.VMEM((2 * tq, 2 * B_HD), jnp.bfloat16),
                            pltpu.VMEM((2 * tq, tk), jnp.float32),
                            pltpu.VMEM((2 * tq, tk), jnp.float32),
                            pltpu.VMEM((2 * tq, LANES), jnp.float32),
                            pltpu.VMEM((2 * tq, 2 * B_VD), jnp.float32)]),
        out_shape=jax.ShapeDtypeStruct((b, n_q_blocks * tq, B_V_W), jnp.float32),
        compiler_params=_cparams(("arbitrary", "arbitrary", "arbitrary")),
        name="diff_flash",
    )(lam, q, k_hm, v1_hm)


def _hgrn_sub_chunk(q_ref, v_ref, lf_ref, o_ref, st_ref, r0, tri, rev):
    half = HGRN_SUB // 2
    row8 = lax.broadcasted_iota(jnp.int32, (half, C_DK), 0)
    lf = lf_ref[pl.ds(r0, HGRN_SUB), :] * LOG2_E
    lf_a = lf.astype(jnp.bfloat16)
    rem = lf - lf_a.astype(jnp.float32)
    lf_b = rem.astype(jnp.bfloat16)
    lf_c = (rem - lf_b.astype(jnp.float32)).astype(jnp.bfloat16)
    bl = (jnp.dot(tri, lf_a, preferred_element_type=jnp.float32)
          + jnp.dot(tri, lf_b, preferred_element_type=jnp.float32)
          + jnp.dot(tri, lf_c, preferred_element_type=jnp.float32))
    q = q_ref[pl.ds(r0, HGRN_SUB), :]
    v = v_ref[pl.ds(r0, HGRN_SUB), :]
    last = 0 if rev else HGRN_SUB - 1
    for hh in range(C_H):
        sl = slice(hh * C_DK, (hh + 1) * C_DK)
        lfh, blh, qh, vh = lf[:, sl], bl[:, sl], q[:, sl], v[:, sl]
        tot = blh[last:last + 1, :]
        kh = 1.0 - jnp.exp2(lfh)
        qt = qh * jnp.exp2(blh)
        kt = kh * jnp.exp2(tot - blh)
        st = st_ref[hh]
        o_sub = lax.dot_general(qt.astype(jnp.bfloat16), st.astype(jnp.bfloat16),
                                (((1,), (1,)), ((), ())), preferred_element_type=jnp.float32)
        parts = [o_sub[:half, :], o_sub[half:, :]]
        for ss in range(HGRN_SUB):
            ks = kh[ss:ss + 1, :]
            bs = blh[ss:ss + 1, :]
            vs = vh[ss:ss + 1, :]
            for p in range(2):
                lo = p * half
                if rev:
                    reached = lo <= ss
                    full = lo + half - 1 <= ss
                    ok = row8 + lo <= ss
                else:
                    reached = lo + half - 1 >= ss
                    full = lo >= ss
                    ok = row8 + lo >= ss
                if not reached:
                    continue
                diff = blh[lo:lo + half, :] - bs
                if not full:
                    diff = jnp.where(ok, diff, NEG_INF)
                col = jnp.sum((qh[lo:lo + half, :] * ks) * jnp.exp2(diff), axis=1, keepdims=True)
                parts[p] = parts[p] + col * vs
        o_ref[pl.ds(r0, half), sl] = parts[0]
        o_ref[pl.ds(r0 + half, half), sl] = parts[1]
        upd = lax.dot_general(vh.astype(jnp.bfloat16), kt.astype(jnp.bfloat16),
                              (((0,), (0,)), ((), ())), preferred_element_type=jnp.float32)
        st_ref[hh] = jnp.exp2(tot) * st + upd


def _hgrn_kernel(qf_ref, vf_ref, lff_ref, qb_ref, vb_ref, lfb_ref, of_ref, ob_ref, st_ref, *, tb):
    t = pl.program_id(1)
    nsub = tb // HGRN_SUB

    @pl.when(t == 0)
    def _():
        st_ref[...] = jnp.zeros_like(st_ref)

    rr = lax.broadcasted_iota(jnp.int32, (HGRN_SUB, HGRN_SUB), 0)
    cc = lax.broadcasted_iota(jnp.int32, (HGRN_SUB, HGRN_SUB), 1)
    tri_f = (cc <= rr).astype(jnp.bfloat16)
    tri_b = (cc >= rr).astype(jnp.bfloat16)

    def body(i, carry):
        rf = pl.multiple_of(i * HGRN_SUB, HGRN_SUB)
        rb = pl.multiple_of((nsub - 1 - i) * HGRN_SUB, HGRN_SUB)
        _hgrn_sub_chunk(qf_ref.at[0], vf_ref.at[0], lff_ref.at[0, 0], of_ref.at[0], st_ref.at[0], rf, tri_f, False)
        _hgrn_sub_chunk(qb_ref.at[0], vb_ref.at[0], lfb_ref.at[0, 0], ob_ref.at[0], st_ref.at[1], rb, tri_b, True)
        return carry

    lax.fori_loop(0, nsub, body, 0, unroll=HGRN_UNROLL)


def _hgrn_scan(q, v, lf, *, n_ctx_blocks):
    b, t, w = q.shape
    tb = HGRN_TB
    nblk = t // tb
    nlat = nblk - n_ctx_blocks

    def fwd_block(ti):
        return jnp.where(ti < n_ctx_blocks, nlat + ti, ti - n_ctx_blocks)

    def bwd_block(ti):
        return nblk - 1 - ti

    rows = lambda blk: pl.BlockSpec((1, tb, w), lambda bi, ti: (bi, blk(ti), 0))
    gate = lambda di, blk: pl.BlockSpec((1, 1, tb, w), lambda bi, ti: (di, bi, blk(ti), 0))
    out = jax.ShapeDtypeStruct((b, t, w), jnp.float32)
    return pl.pallas_call(
        functools.partial(_hgrn_kernel, tb=tb),
        grid=(b, nblk),
        in_specs=[rows(fwd_block), rows(fwd_block), gate(0, fwd_block),
                  rows(bwd_block), rows(bwd_block), gate(1, bwd_block)],
        out_specs=[rows(fwd_block), rows(bwd_block)],
        out_shape=[out, out],
        scratch_shapes=[pltpu.VMEM((2, C_H, C_DV, C_DK), jnp.float32)],
        compiler_params=_cparams(("arbitrary", "arbitrary")),
        name="hgrn2_scan",
    )(q, v, lf, q, v, lf)


def _head_rms(x, w, n_heads, width):
    out = []
    for hh in range(n_heads):
        blk = x[:, hh * width:(hh + 1) * width]
        out.append(_rms_rows(blk, w))
    return jnp.concatenate(out, axis=1)


def _merge_kernel(oa_ref, ob_ref, ocf_ref, ocb_ref, cg_ref, gt_ref, x_ref, ng_ref, ma_ref, mb_ref, dw_ref, hw_ref,
                  wbr_ref, wo_ref, xo_ref, ho_ref, *, diff_out_scale):
    ob = (_head_rms(ob_ref[...], dw_ref[...], B_H, B_VD) * diff_out_scale).astype(jnp.bfloat16)
    cg = cg_ref[...].astype(jnp.float32)
    oc = (_head_rms(ocf_ref[...] + ocb_ref[...], hw_ref[...], C_H, C_DV) * (cg * jax.nn.sigmoid(cg)))
    acc = None
    for i, o in enumerate((oa_ref[...], ob, oc.astype(jnp.bfloat16))):
        z = jnp.dot(o, wbr_ref[i], preferred_element_type=jnp.float32)
        g = jax.nn.sigmoid(gt_ref[:, i * D_MODEL:(i + 1) * D_MODEL].astype(jnp.float32))
        acc = g * z if acc is None else acc + g * z
    y = jnp.dot(acc.astype(jnp.bfloat16), wo_ref[...], preferred_element_type=jnp.float32)
    ma, mb = ma_ref[0], mb_ref[0]
    for half, (xn, mod) in enumerate(zip(_residual_tile(x_ref, y, ng_ref[1], ma, mb, 2), (ma, mb))):
        rows = slice(half * HALF_TM, (half + 1) * HALF_TM)
        xo_ref[rows, :] = xn
        ho_ref[rows, :] = _rms_rows(xn, ng_ref[2]) * (1.0 + mod[4]) + mod[3]


def _merge(oa, ob, ocf, ocb, cg, gates, x, norm_g, mods, diff_w, hgrn_w, w_br, w_o, *, t_rows, seq,
           diff_out_scale):
    m = x.shape[0]
    tm = MM_TM
    row = lambda i: (i, 0)
    f32 = jnp.float32
    return pl.pallas_call(
        functools.partial(_merge_kernel, diff_out_scale=diff_out_scale),
        grid=(m // tm,),
        in_specs=[pl.BlockSpec((tm, BRANCH_W), row), pl.BlockSpec((tm, BRANCH_W), row),
                  pl.BlockSpec((tm, BRANCH_W), row), pl.BlockSpec((tm, BRANCH_W), row),
                  pl.BlockSpec((tm, BRANCH_W), row),
                  pl.BlockSpec((tm, GATE_W), row), pl.BlockSpec((tm, D_MODEL), row),
                  pl.BlockSpec((4, 1, D_MODEL), lambda i: (0, 0, 0))] + _mod_specs(t_rows, seq) + [
                  pl.BlockSpec((1, B_VD), lambda i: (0, 0)), pl.BlockSpec((1, C_DV), lambda i: (0, 0)),
                  pl.BlockSpec((N_BRANCH, BRANCH_W, D_MODEL), lambda i: (0, 0, 0)),
                  pl.BlockSpec((D_MODEL, D_MODEL), lambda i: (0, 0))],
        out_specs=[pl.BlockSpec((tm, D_MODEL), row), pl.BlockSpec((tm, D_MODEL), row)],
        out_shape=[jax.ShapeDtypeStruct((m, D_MODEL), f32), jax.ShapeDtypeStruct((m, D_MODEL), f32)],
        compiler_params=_cparams(("arbitrary",)),
        name="branch_merge",
    )(oa, ob, ocf, ocb, cg, gates, x, norm_g, mods, mods, diff_w, hgrn_w, w_br, w_o)


def _moe_row_copy(x_hbm, xbuf, sem, src_row, n_rows, slot, dst_row):
    return pltpu.make_async_copy(x_hbm.at[pl.ds(src_row, n_rows), :],
                                 xbuf.at[slot, pl.ds(dst_row, n_rows), :], sem.at[slot])


def _moe_kernel(blk_e_ref, n_used_ref, tok_ref, tok_next_ref, x_hbm, wgu_ref, bgu_ref, wdn_ref, bdn_ref,
                y_ref, xbuf, sem, wgu_sc, wdn_sc):
    i = pl.program_id(0)
    n_used = n_used_ref[0]
    slot = i % 2
    e = blk_e_ref[i]
    e_prev = blk_e_ref[jnp.maximum(i - 1, 0)]

    def start_gather(ids_ref, dst_slot):
        for r in range(MOE_TM):
            _moe_row_copy(x_hbm, xbuf, sem, ids_ref[0, 0, r], 1, dst_slot, r).start()

    @pl.when((i == 0) & (n_used > 0))
    def _():
        start_gather(tok_ref, 0)

    for nxt in range(2):
        @pl.when((i + 1 < n_used) & (slot == 1 - nxt))
        def _():
            start_gather(tok_next_ref, nxt)

    @pl.when((i == 0) | (e != e_prev))
    def _():
        wgu_sc[...] = wgu_ref[0, 0].astype(jnp.bfloat16)
        wdn_sc[...] = wdn_ref[0, 0].astype(jnp.bfloat16)

    @pl.when(i < n_used)
    def _():
        _moe_row_copy(x_hbm, xbuf, sem, 0, MOE_TM, slot, 0).wait()
        x = xbuf[slot].astype(jnp.bfloat16)
        gu = jnp.dot(x, wgu_sc[...], preferred_element_type=jnp.float32) + bgu_ref[0, 0]
        g = jnp.minimum(gu[:, :EXPERT_FF], SWIGLU_LIMIT)
        u = jnp.clip(gu[:, EXPERT_FF:], -SWIGLU_LIMIT, SWIGLU_LIMIT)
        act = (u + 1.0) * (g * jax.nn.sigmoid(SWIGLU_ALPHA * g))
        y = jnp.dot(act.astype(jnp.bfloat16), wdn_sc[...], preferred_element_type=jnp.float32) + bdn_ref[0, 0]
        y_ref[...] = y.astype(y_ref.dtype)

    @pl.when(i >= n_used)
    def _():
        y_ref[...] = jnp.zeros_like(y_ref)


def _moe_experts(blk_e, n_used, row_tok, x, w_gu, b_gu, w_dn, b_dn, *, layer):
    dm = x.shape[1]
    tm = MOE_TM
    n_blk = blk_e.shape[0]
    r = n_blk * tm
    ids = row_tok.reshape(n_blk, 1, tm)
    return pl.pallas_call(
        _moe_kernel,
        grid_spec=pltpu.PrefetchScalarGridSpec(
            num_scalar_prefetch=2, grid=(n_blk,),
            in_specs=[pl.BlockSpec((1, 1, tm), lambda i, be, nu: (i, 0, 0), memory_space=pltpu.SMEM),
                      pl.BlockSpec((1, 1, tm), lambda i, be, nu: (jnp.minimum(i + 1, n_blk - 1), 0, 0),
                                   memory_space=pltpu.SMEM),
                      pl.BlockSpec(memory_space=pl.ANY),
                      pl.BlockSpec((1, 1, dm, 2 * EXPERT_FF), lambda i, be, nu: (layer, be[i], 0, 0)),
                      pl.BlockSpec((1, 1, 1, 2 * EXPERT_FF), lambda i, be, nu: (layer, be[i], 0, 0)),
                      pl.BlockSpec((1, 1, EXPERT_FF, dm), lambda i, be, nu: (layer, be[i], 0, 0)),
                      pl.BlockSpec((1, 1, 1, dm), lambda i, be, nu: (layer, be[i], 0, 0))],
            out_specs=pl.BlockSpec((tm, dm), lambda i, be, nu: (i, 0)),
            scratch_shapes=[pltpu.VMEM((2, tm, dm), jnp.float32),
                            pltpu.SemaphoreType.DMA((2,)),
                            pltpu.VMEM((dm, 2 * EXPERT_FF), jnp.bfloat16),
                            pltpu.VMEM((EXPERT_FF, dm), jnp.bfloat16)]),
        out_shape=jax.ShapeDtypeStruct((r, dm), jnp.bfloat16),
        compiler_params=_cparams(("arbitrary",)),
        name="moe_experts",
    )(blk_e, n_used, ids, ids, x, w_gu, b_gu.reshape(DEPTH, N_EXPERTS, 1, -1), w_dn,
      b_dn.reshape(DEPTH, N_EXPERTS, 1, -1))


def _combine_kernel(*refs):
    yk_refs = refs[:TOP_K]
    gt_ref, x_ref, ng_ref, ma_ref, mb_ref, xo_ref = refs[TOP_K:]
    gt = gt_ref[...]
    f = None
    for k in range(TOP_K):
        term = gt[:, k:k + 1] * yk_refs[k][...].astype(jnp.float32)
        f = term if f is None else f + term
    for half, xn in enumerate(_residual_tile(x_ref, f, ng_ref[3], ma_ref[0], mb_ref[0], 5)):
        xo_ref[half * HALF_TM:(half + 1) * HALF_TM, :] = xn


def _moe_combine(yk, gates, x, norm_g, mods, *, t_rows, seq):
    m = x.shape[0]
    tm = MM_TM
    n_tiles = m // tm
    row = lambda i: (i, 0)
    choice = lambda k: pl.BlockSpec((tm, D_MODEL), lambda i: (k * n_tiles + i, 0))
    return pl.pallas_call(
        _combine_kernel,
        grid=(n_tiles,),
        in_specs=[choice(k) for k in range(TOP_K)] + [
            pl.BlockSpec((tm, TOP_K), row), pl.BlockSpec((tm, D_MODEL), row),
            pl.BlockSpec((4, 1, D_MODEL), lambda i: (0, 0, 0))] + _mod_specs(t_rows, seq),
        out_specs=pl.BlockSpec((tm, D_MODEL), row),
        out_shape=jax.ShapeDtypeStruct((m, D_MODEL), jnp.float32),
        compiler_params=_cparams(("arbitrary",)),
        name="moe_combine",
    )(*([yk] * TOP_K), gates, x, norm_g, mods, mods)


def _moe_ffn(h, x, norm_g, mods, w_router, b_router, w_gu, b_gu, w_dn, b_dn, *, layer, t_rows, seq):
    n, dm = h.shape
    nk = n * TOP_K
    w_r = jnp.zeros((dm, LANES), jnp.float32).at[:, :N_EXPERTS].set(w_router)
    logits = _matmul(h, w_r, tm=MM_TM, tn=LANES, out_dtype=jnp.float32,
                     precision=lax.Precision.HIGHEST, name="router")[:, :N_EXPERTS] + b_router
    top_v, top_e = lax.top_k(logits, TOP_K)
    gates = jax.nn.softmax(top_v, axis=-1)
    flat_e = top_e.reshape(-1).astype(jnp.int32)
    experts = jnp.arange(N_EXPERTS, dtype=jnp.int32)
    onehot = (flat_e[:, None] == experts[None, :]).astype(jnp.int32)
    incl = jnp.cumsum(onehot, axis=0)
    rank = jnp.sum(incl * onehot, axis=1) - 1
    counts = incl[-1]
    padded = (counts + MOE_TM - 1) // MOE_TM * MOE_TM
    pad_end = jnp.cumsum(padded)
    pad_start = pad_end - padded
    grp_start = jnp.cumsum(counts) - counts
    dest = jnp.sum(onehot * pad_start[None, :], axis=1) + rank
    n_blk = (nk + MOE_TM - 1) // MOE_TM + N_EXPERTS
    rows = n_blk * MOE_TM
    blk_start = jnp.arange(n_blk, dtype=jnp.int32) * MOE_TM
    blk_e = jnp.minimum(jnp.sum((pad_end[None, :] <= blk_start[:, None]).astype(jnp.int32), axis=1),
                        N_EXPERTS - 1)
    n_used = (pad_end[-1] // MOE_TM).astype(jnp.int32).reshape(1)
    order = jnp.argsort(flat_e)
    row_e = jnp.repeat(blk_e, MOE_TM)
    sorted_pos = jnp.arange(rows, dtype=jnp.int32) - pad_start[row_e] + grp_start[row_e]
    row_tok = (order[jnp.clip(sorted_pos, 0, nk - 1)] // TOP_K).astype(jnp.int32)
    y = _moe_experts(blk_e, n_used, row_tok, h, w_gu, b_gu, w_dn, b_dn, layer=layer)
    yk = y[dest.reshape(n, TOP_K).T.reshape(-1)]
    return _moe_combine(yk, gates, x, norm_g, mods, t_rows=t_rows, seq=seq)


def _rope_tables(rows, n_ctx, batch):
    row = jnp.repeat(jnp.arange(rows, dtype=jnp.float32), GRID_W)
    col = jnp.tile(jnp.arange(GRID_W, dtype=jnp.float32), rows)
    n_freq = HEAD_DIM // 4
    inv = ROPE_THETA ** (-jnp.arange(n_freq, dtype=jnp.float32) / n_freq)
    ang_r = row[:, None] * inv
    ang_c = col[:, None] * inv
    cos_h = jnp.concatenate([jnp.cos(ang_r), jnp.cos(ang_r), jnp.cos(ang_c), jnp.cos(ang_c)], axis=1)
    sin_h = jnp.concatenate([-jnp.sin(ang_r), jnp.sin(ang_r), -jnp.sin(ang_c), jnp.sin(ang_c)], axis=1)
    cos_t = jnp.concatenate([cos_h, jnp.ones((n_ctx, HEAD_DIM), jnp.float32)], axis=0)
    sin_t = jnp.concatenate([sin_h, jnp.zeros((n_ctx, HEAD_DIM), jnp.float32)], axis=0)
    reps = (batch, LANES // HEAD_DIM)
    return jnp.tile(cos_t, reps), jnp.tile(sin_t, reps)


def _rope_partner(n_cols):
    idx = jnp.arange(n_cols)
    quarter = HEAD_DIM // 4
    return jnp.where((idx % (2 * quarter)) < quarter, idx + quarter, idx - quarter)


def kernel(x, c, ctx, c_ctx, w_mod, b_mod, norm_g, w_in, attn_sink, diff_lambda, diff_norm_w,
           hgrn_lb_logits, hgrn_norm_w, w_branch, w_out, w_router, b_router, w_gate_up, b_gate_up,
           w_down, b_down):
    b, s, dm = x.shape
    l = ctx.shape[1]
    t = s + l
    f32 = jnp.float32
    bf16 = jnp.bfloat16
    cos_t, sin_t = _rope_tables(s // GRID_W, l, b)
    lb_cum = jnp.cumsum(jax.nn.softmax(hgrn_lb_logits.astype(f32), axis=0), axis=0)
    lower_bounds = lb_cum - lb_cum[0:1]
    cond_rows = 16
    cond = jnp.zeros((cond_rows, dm), f32).at[:b].set(jax.nn.silu(c)).at[b].set(jax.nn.silu(c_ctx))
    scale = HEAD_DIM ** -0.5
    xa = jnp.concatenate([x, ctx], axis=1).reshape(b * t, dm)
    h_moe = None
    for layer in range(DEPTH):
        lam_init = 0.8 - 0.6 * math.exp(-0.3 * layer)
        mod_all = _matmul(cond, w_mod[layer], tm=cond_rows, tn=3 * dm, out_dtype=f32,
                          precision=lax.Precision.HIGHEST, name="adaln") + b_mod[layer]
        mod = mod_all[:b].reshape(b, 1, 6, 1, dm)
        mod_c = jnp.broadcast_to(mod_all[b:b + 1].reshape(1, 1, 6, 1, dm), (b, 1, 6, 1, dm))
        mods = jnp.concatenate([mod, mod_c], axis=1).reshape(2 * b, 6, 1, dm)
        ng = norm_g[layer].reshape(4, 1, dm)
        g0 = ng[0:1]
        geo = dict(t_rows=t, seq=s)

        w = w_in[layer]
        w_aq = (w[:, OFF_A:OFF_A + A_Q_W] * scale).reshape(dm, A_HKV, A_GROUP, HEAD_DIM)
        pad = jnp.zeros((dm, A_GROUP, HEAD_DIM), f32)
        w_aq8 = jnp.concatenate([jnp.concatenate([w_aq[:, 0], pad], axis=-1),
                                 jnp.concatenate([pad, w_aq[:, 1]], axis=-1)], axis=1).reshape(dm, A_Q8_W)
        w_a = jnp.concatenate([w_aq8, w[:, OFF_A + A_Q_W:OFF_A + A_W]], axis=1).astype(bf16)
        w_b = w[:, OFF_B:OFF_B + B_W].at[:, :B_QK_W].multiply(scale * LOG2_E).astype(bf16)
        proj_a = _proj_rope(xa, g0, mods, w_a, w_a[:, _rope_partner(A_ROPE_W2)], cos_t, sin_t,
                            rope_w=A_ROPE_W2, name="proj_a", **geo).reshape(b, t, A_W2)
        bq, bk_hm, bv1_hm = _proj_b(xa, g0, mods, w_b, w_b[:, _rope_partner(B_ROPE_W)], cos_t, sin_t, **geo)
        bq = bq.reshape(b, t, B_QK_W)
        lb = lower_bounds[layer]
        lb_par = jnp.stack([jnp.log(lb[0]), jnp.log1p(-lb[0]), jnp.log(lb[1]), jnp.log1p(-lb[1])])
        cq, ci, lf, cg = _proj_hgrn(xa, g0, mods, w[:, OFF_C:OFF_C + C_W].astype(bf16),
                                    lb_par.reshape(4, 1, C_K_W), **geo)
        gates = _proj_plain(xa, g0, mods, w[:, OFF_GATE:].astype(bf16), name="gate_proj", **geo)

        sink32 = attn_sink[layer].astype(f32)
        o_a = _window_attention(proj_a, sink32, seq=s)

        lp = diff_lambda[layer].astype(f32)
        lam = (jnp.exp(jnp.sum(lp[0] * lp[1])) - jnp.exp(jnp.sum(lp[2] * lp[3])) + lam_init).reshape(1).astype(f32)
        o_b = _diff_attention(lam, bq, bk_hm, bv1_hm, tq=FLASH_TQ, tk=FLASH_TK, q_block_offset=0,
                              n_q_blocks=s // FLASH_TQ, k_block_offset=0, nk=t // FLASH_TK)
        o_bx = _diff_attention(lam, bq, bk_hm, bv1_hm, tq=l, tk=l, q_block_offset=s // l, n_q_blocks=1,
                               k_block_offset=s // l, nk=1)
        o_b = jnp.concatenate([o_b, o_bx], axis=1)

        o_cf, o_cb = _hgrn_scan(cq.reshape(b, t, C_K_W), ci.reshape(b, t, C_V_W), lf.reshape(2, b, t, C_K_W),
                                n_ctx_blocks=l // HGRN_TB)

        col = jnp.arange(A_Q_W)
        head_of_col = ((col % LANES) // HEAD_DIM) * A_GROUP + col // LANES
        w_br = w_branch[layer].at[0].set(w_branch[layer][0][head_of_col * HEAD_DIM + col % HEAD_DIM])
        xa, h_moe = _merge(o_a.reshape(b * t, A_Q_W), o_b.reshape(b * t, B_V_W), o_cf.reshape(b * t, C_V_W),
                           o_cb.reshape(b * t, C_V_W), cg,
                           gates, xa, ng, mods, diff_norm_w[layer].reshape(1, B_VD).astype(f32),
                           hgrn_norm_w[layer].reshape(1, C_DV).astype(f32),
                           w_br.astype(bf16), w_out[layer].astype(bf16),
                           diff_out_scale=1 - lam_init, **geo)

        xa = _moe_ffn(h_moe, xa, ng, mods, w_router[layer], b_router[layer], w_gate_up, b_gate_up,
                      w_down, b_down, layer=layer, **geo)
    return xa.reshape(b, t, dm)[:, :s]
```

```python
import functools
import math

import jax
import jax.numpy as jnp
from jax import lax
from jax.experimental import pallas as pl
from jax.experimental.pallas import tpu as pltpu

D_MODEL = 1024
DEPTH = 2
GRID_W = 64
HEAD_DIM = 64
ROPE_THETA = 10000.0
RMS_EPS = 1e-6
NEG_INF = -1e30
BLOCK = 128
A_HQ = 8
A_HKV = 2
A_GROUP = A_HQ // A_HKV
WINDOW = 128
B_H = 4
B_HD = HEAD_DIM
B_VD = 2 * HEAD_DIM
C_H = 4
C_DK = 128
C_DV = 128
N_BRANCH = 3
BRANCH_W = 512
A_Q_W = A_HQ * HEAD_DIM
A_KV_W = A_HKV * HEAD_DIM
B_QK_W = B_H * 2 * B_HD
B_V_W = B_H * B_VD
C_K_W = C_H * C_DK
C_V_W = C_H * C_DV
GATE_W = N_BRANCH * D_MODEL
N_EXPERTS = 32
TOP_K = 4
EXPERT_FF = D_MODEL
SWIGLU_LIMIT = 7.0
SWIGLU_ALPHA = 1.702

A_W = A_Q_W + 2 * A_KV_W
A_ROPE_W = A_Q_W + A_KV_W
B_W = 2 * B_QK_W + B_V_W
B_ROPE_W = 2 * B_QK_W
C_W = 3 * C_K_W + 2 * C_V_W
OFF_A = 0
OFF_B = OFF_A + A_W
OFF_C = OFF_B + B_W
OFF_GATE = OFF_C + C_W

V7X_VMEM_LIMIT_BYTES = 56 * 1024 * 1024
LANES = 128
LOG2_E = math.log2(math.e)

MM_TM = 512
HALF_TM = MM_TM // 2
FLASH_TQ = 2048
FLASH_TK = 768
HGRN_TB = 256
HGRN_SUB = 16
HGRN_UNROLL = 4
MOE_TM = 512


def _cparams(sem):
    return pltpu.CompilerParams(dimension_semantics=sem, vmem_limit_bytes=V7X_VMEM_LIMIT_BYTES)


def _rms_rows(x, g):
    return (x * lax.rsqrt(jnp.mean(x * x, axis=-1, keepdims=True) + RMS_EPS)) * g


def _modulated_tile(x_ref, g, mod_a, mod_b, shift_slot, scale_slot):
    out = []
    for half, mod in enumerate((mod_a, mod_b)):
        x = x_ref[half * HALF_TM:(half + 1) * HALF_TM, :]
        out.append(_rms_rows(x, g) * (1.0 + mod[scale_slot]) + mod[shift_slot])
    return jnp.concatenate(out, axis=0)


def _residual_tile(x_ref, y, g, mod_a, mod_b, gate_slot):
    out = []
    for half, mod in enumerate((mod_a, mod_b)):
        rows = slice(half * HALF_TM, (half + 1) * HALF_TM)
        out.append(x_ref[rows, :] + mod[gate_slot] * _rms_rows(y[rows, :], g))
    return out


def _mod_specs(t_rows, seq):
    halves_per_batch = t_rows // HALF_TM
    lat_halves = seq // HALF_TM

    def row(u):
        return (u // halves_per_batch) * 2 + jnp.where(u % halves_per_batch >= lat_halves, 1, 0)

    blk = (1, 6, 1, D_MODEL)
    return [pl.BlockSpec(blk, lambda i: (row(2 * i), 0, 0, 0)),
            pl.BlockSpec(blk, lambda i: (row(2 * i + 1), 0, 0, 0))]


def _proj_rope_kernel(x_ref, g_ref, ma_ref, mb_ref, w_ref, wsw_ref, cos_ref, sin_ref, o_ref, *, rope_w):
    h = _modulated_tile(x_ref, g_ref[0], ma_ref[0], mb_ref[0], 0, 1).astype(jnp.bfloat16)
    z = jnp.dot(h, w_ref[...], preferred_element_type=jnp.float32)
    zs = jnp.dot(h, wsw_ref[...], preferred_element_type=jnp.float32)
    reps = rope_w // LANES
    rot = z[:, :rope_w] * jnp.tile(cos_ref[...], (1, reps)) + zs * jnp.tile(sin_ref[...], (1, reps))
    o_ref[:, :rope_w] = rot.astype(o_ref.dtype)
    o_ref[:, rope_w:] = z[:, rope_w:].astype(o_ref.dtype)


def _proj_b_kernel(x_ref, g_ref, ma_ref, mb_ref, w_ref, wsw_ref, cos_ref, sin_ref, q_ref, k_ref, v1_ref):
    h = _modulated_tile(x_ref, g_ref[0], ma_ref[0], mb_ref[0], 0, 1).astype(jnp.bfloat16)
    z = jnp.dot(h, w_ref[...], preferred_element_type=jnp.float32)
    zs = jnp.dot(h, wsw_ref[...], preferred_element_type=jnp.float32)
    reps = B_ROPE_W // LANES
    rot = z[:, :B_ROPE_W] * jnp.tile(cos_ref[...], (1, reps)) + zs * jnp.tile(sin_ref[...], (1, reps))
    q_ref[...] = rot[:, :B_QK_W].astype(q_ref.dtype)
    for hh in range(B_H):
        k_ref[hh] = rot[:, B_QK_W + hh * 2 * B_HD:B_QK_W + (hh + 1) * 2 * B_HD].astype(k_ref.dtype)
        v1_ref[hh, :, :B_VD] = z[:, B_ROPE_W + hh * B_VD:B_ROPE_W + (hh + 1) * B_VD].astype(v1_ref.dtype)
        v1_ref[hh, :, B_VD:] = jnp.ones((MM_TM, B_VD), v1_ref.dtype)


def _proj_plain_kernel(x_ref, g_ref, ma_ref, mb_ref, w_ref, o_ref):
    h = _modulated_tile(x_ref, g_ref[0], ma_ref[0], mb_ref[0], 0, 1).astype(jnp.bfloat16)
    o_ref[...] = jnp.dot(h, w_ref[...], preferred_element_type=jnp.float32).astype(o_ref.dtype)


def _log_forget(z, log_lb, log_1m_lb):
    log_sig = jnp.minimum(z, 0.0) - jnp.log(1.0 + jnp.exp(-jnp.abs(z)))
    c = log_1m_lb + log_sig
    return jnp.maximum(log_lb, c) + jnp.log(1.0 + jnp.exp(-jnp.abs(log_lb - c)))


def _proj_hgrn_kernel(x_ref, g_ref, ma_ref, mb_ref, w_ref, lb_ref, q_ref, v_ref, lf_ref, cg_ref):
    h = _modulated_tile(x_ref, g_ref[0], ma_ref[0], mb_ref[0], 0, 1).astype(jnp.bfloat16)
    z = jnp.dot(h, w_ref[...], preferred_element_type=jnp.float32)
    zq = z[:, :C_K_W]
    q_ref[...] = zq * jax.nn.sigmoid(zq)
    v_ref[...] = z[:, C_K_W:C_K_W + C_V_W]
    off = C_K_W + C_V_W
    lf_ref[0] = _log_forget(z[:, off:off + C_K_W], lb_ref[0], lb_ref[1])
    lf_ref[1] = _log_forget(z[:, off + C_K_W:off + 2 * C_K_W], lb_ref[2], lb_ref[3])
    cg_ref[...] = z[:, off + 2 * C_K_W:].astype(cg_ref.dtype)


def _proj_common_specs(t_rows, seq):
    row = lambda i: (i, 0)
    return [pl.BlockSpec((MM_TM, D_MODEL), row),
            pl.BlockSpec((1, 1, D_MODEL), lambda i: (0, 0, 0))] + _mod_specs(t_rows, seq)


def _proj_rope(x, g, mods, w, wsw, cos, sin, *, t_rows, seq, rope_w, name):
    m = x.shape[0]
    n = w.shape[1]
    row = lambda i: (i, 0)
    whole = lambda i: (0, 0)
    return pl.pallas_call(
        functools.partial(_proj_rope_kernel, rope_w=rope_w),
        grid=(m // MM_TM,),
        in_specs=_proj_common_specs(t_rows, seq) + [
            pl.BlockSpec((D_MODEL, n), whole), pl.BlockSpec((D_MODEL, rope_w), whole),
            pl.BlockSpec((MM_TM, LANES), row), pl.BlockSpec((MM_TM, LANES), row)],
        out_specs=pl.BlockSpec((MM_TM, n), row),
        out_shape=jax.ShapeDtypeStruct((m, n), jnp.bfloat16),
        compiler_params=_cparams(("arbitrary",)),
        name=name,
    )(x, g, mods, mods, w, wsw, cos, sin)


def _proj_b(x, g, mods, w, wsw, cos, sin, *, t_rows, seq):
    m = x.shape[0]
    row = lambda i: (i, 0)
    whole = lambda i: (0, 0)
    heads = lambda i: (0, i, 0)
    bf16 = jnp.bfloat16
    return pl.pallas_call(
        _proj_b_kernel,
        grid=(m // MM_TM,),
        in_specs=_proj_common_specs(t_rows, seq) + [
            pl.BlockSpec((D_MODEL, B_W), whole), pl.BlockSpec((D_MODEL, B_ROPE_W), whole),
            pl.BlockSpec((MM_TM, LANES), row), pl.BlockSpec((MM_TM, LANES), row)],
        out_specs=[pl.BlockSpec((MM_TM, B_QK_W), row), pl.BlockSpec((B_H, MM_TM, 2 * B_HD), heads),
                   pl.BlockSpec((B_H, MM_TM, 2 * B_VD), heads)],
        out_shape=[jax.ShapeDtypeStruct((m, B_QK_W), bf16), jax.ShapeDtypeStruct((B_H, m, 2 * B_HD), bf16),
                   jax.ShapeDtypeStruct((B_H, m, 2 * B_VD), bf16)],
        compiler_params=_cparams(("arbitrary",)),
        name="proj_b",
    )(x, g, mods, mods, w, wsw, cos, sin)


def _proj_plain(x, g, mods, w, *, t_rows, seq, name):
    m = x.shape[0]
    n = w.shape[1]
    row = lambda i: (i, 0)
    return pl.pallas_call(
        _proj_plain_kernel,
        grid=(m // MM_TM,),
        in_specs=_proj_common_specs(t_rows, seq) + [pl.BlockSpec((D_MODEL, n), lambda i: (0, 0))],
        out_specs=pl.BlockSpec((MM_TM, n), row),
        out_shape=jax.ShapeDtypeStruct((m, n), jnp.bfloat16),
        compiler_params=_cparams(("arbitrary",)),
        name=name,
    )(x, g, mods, mods, w)


def _proj_hgrn(x, g, mods, w, lb_par, *, t_rows, seq):
    m = x.shape[0]
    row = lambda i: (i, 0)
    f32 = jnp.float32
    return pl.pallas_call(
        _proj_hgrn_kernel,
        grid=(m // MM_TM,),
        in_specs=_proj_common_specs(t_rows, seq) + [
            pl.BlockSpec((D_MODEL, C_W), lambda i: (0, 0)),
            pl.BlockSpec((4, 1, C_K_W), lambda i: (0, 0, 0))],
        out_specs=[pl.BlockSpec((MM_TM, C_K_W), row), pl.BlockSpec((MM_TM, C_V_W), row),
                   pl.BlockSpec((2, MM_TM, C_K_W), lambda i: (0, i, 0)), pl.BlockSpec((MM_TM, C_V_W), row)],
        out_shape=[jax.ShapeDtypeStruct((m, C_K_W), f32), jax.ShapeDtypeStruct((m, C_V_W), f32),
                   jax.ShapeDtypeStruct((2, m, C_K_W), f32), jax.ShapeDtypeStruct((m, C_V_W), jnp.bfloat16)],
        compiler_params=_cparams(("arbitrary",)),
        name="hgrn_proj",
    )(x, g, mods, mods, w, lb_par)


def _mm_kernel(x_ref, w_ref, o_ref, *, precision):
    o_ref[...] = jnp.dot(x_ref[...], w_ref[...], preferred_element_type=jnp.float32,
                         precision=precision).astype(o_ref.dtype)


def _matmul(x, w, *, tm, tn, out_dtype, precision=None, name):
    m, k = x.shape
    n = w.shape[1]
    assert m % tm == 0 and n % tn == 0, (m, tm, n, tn)
    return pl.pallas_call(
        functools.partial(_mm_kernel, precision=precision),
        grid=(n // tn, m // tm),
        in_specs=[pl.BlockSpec((tm, k), lambda j, i: (i, 0)),
                  pl.BlockSpec((k, tn), lambda j, i: (0, j))],
        out_specs=pl.BlockSpec((tm, tn), lambda j, i: (i, j)),
        out_shape=jax.ShapeDtypeStruct((m, n), out_dtype),
        compiler_params=_cparams(("arbitrary", "arbitrary")),
        name=name,
    )(x, w)


A_Q8_W = A_HQ * LANES
A_W2 = A_Q8_W + 2 * A_KV_W
A_ROPE_W2 = A_Q8_W + A_KV_W
A_K_BLK = A_Q8_W // A_KV_W
A_V_BLK = A_K_BLK + 1


def _attn_a_kernel(sink_ref, q_ref, kp_ref, kc_ref, kn_ref, vp_ref, vc_ref, vn_ref, kx_ref, vx_ref,
                   o_ref, *, seq):
    n = pl.program_id(1)
    k_all = jnp.concatenate([kp_ref[0], kc_ref[0], kn_ref[0], kx_ref[0]], axis=0)
    v_all = jnp.concatenate([vp_ref[0], vc_ref[0], vn_ref[0], vx_ref[0]], axis=0)
    n_keys = k_all.shape[0]
    qi = lax.broadcasted_iota(jnp.int32, (BLOCK, n_keys), 0)
    kj = lax.broadcasted_iota(jnp.int32, (BLOCK, n_keys), 1)
    kpos = n * BLOCK + kj - BLOCK
    local_ok = (jnp.abs(kj - BLOCK - qi) <= WINDOW) & (kpos >= 0) & (kpos < seq) & (n * BLOCK < seq)
    mask = local_ok | (kj >= 3 * BLOCK)
    q8 = jnp.concatenate([q_ref[0, :, hq * LANES:(hq + 1) * LANES] for hq in range(A_HQ)], axis=0)
    s = lax.dot_general(q8, k_all, (((1,), (1,)), ((), ())), preferred_element_type=jnp.float32)
    p_blocks, inv_l = [], []
    for hq in range(A_HQ):
        sh = jnp.where(mask, s[hq * BLOCK:(hq + 1) * BLOCK, :], NEG_INF)
        sk = sink_ref[hq]
        m = jnp.maximum(jnp.max(sh, axis=-1, keepdims=True), sk)
        p = jnp.exp(sh - m)
        inv_l.append(1.0 / (jnp.sum(p, axis=-1, keepdims=True) + jnp.exp(sk - m)))
        p_blocks.append(p.astype(v_all.dtype))
    o = jnp.dot(jnp.concatenate(p_blocks, axis=0), v_all, preferred_element_type=jnp.float32)
    lane = lax.broadcasted_iota(jnp.int32, (BLOCK, LANES), 1)
    for g in range(A_GROUP):
        lo = o[g * BLOCK:(g + 1) * BLOCK, :] * inv_l[g]
        hi = o[(A_GROUP + g) * BLOCK:(A_GROUP + g + 1) * BLOCK, :] * inv_l[A_GROUP + g]
        o_ref[0, :, g * LANES:(g + 1) * LANES] = jnp.where(lane < HEAD_DIM, lo, hi).astype(o_ref.dtype)


def _window_attention(proj_a, sink, *, seq):
    b, t, _ = proj_a.shape
    l = t - seq
    nb = seq // BLOCK
    ctx_blk = seq // l
    q_map = lambda bi, n, *_: (bi, n, 0)
    kv = lambda col, fn: pl.BlockSpec((1, BLOCK, A_KV_W), lambda bi, n, *_: (bi, fn(n), col))
    prev = lambda n: jnp.clip(n - 1, 0, nb - 1)
    cur = lambda n: jnp.minimum(n, nb - 1)
    nxt = lambda n: jnp.minimum(n + 1, nb - 1)
    ctx = lambda col: pl.BlockSpec((1, l, A_KV_W), lambda bi, n, *_: (bi, ctx_blk, col))
    return pl.pallas_call(
        functools.partial(_attn_a_kernel, seq=seq),
        grid_spec=pltpu.PrefetchScalarGridSpec(
            num_scalar_prefetch=1, grid=(b, t // BLOCK),
            in_specs=[pl.BlockSpec((1, BLOCK, A_Q8_W), q_map),
                      kv(A_K_BLK, prev), kv(A_K_BLK, cur), kv(A_K_BLK, nxt),
                      kv(A_V_BLK, prev), kv(A_V_BLK, cur), kv(A_V_BLK, nxt),
                      ctx(A_K_BLK), ctx(A_V_BLK)],
            out_specs=pl.BlockSpec((1, BLOCK, A_Q_W), q_map)),
        out_shape=jax.ShapeDtypeStruct((b, t, A_Q_W), jnp.bfloat16),
        compiler_params=_cparams(("arbitrary", "arbitrary")),
        name="window_gqa",
    )(sink, *([proj_a] * 9))


def _diff_flash_kernel(lam_ref, q_ref, k_ref, v_ref, o_ref, q2_sc, sa_sc, sb_sc, m_sc, acc_sc, *, tq, tk, nk):
    u = pl.program_id(2)
    new_q = u % nk == 0

    @pl.when(u == 0)
    def _():
        m_sc[...] = jnp.full_like(m_sc, NEG_INF)
        acc_sc[...] = jnp.zeros_like(acc_sc)
        sb_sc[...] = jnp.full_like(sb_sc, NEG_INF)

    @pl.when(new_q)
    def _():
        q = q_ref[0]
        lane = lax.broadcasted_iota(jnp.int32, q.shape, 1)
        q2_sc[:tq, :] = jnp.where(lane < B_HD, q, jnp.zeros_like(q))
        q2_sc[tq:, :] = jnp.where(lane >= B_HD, q, jnp.zeros_like(q))

    def step(s_new_ref, s_old_ref):
        m_prev = m_sc[...]
        m_new = jnp.maximum(m_prev, jnp.max(s_old_ref[...], axis=-1, keepdims=True))
        alpha = jnp.exp2(m_prev - m_new)
        p = jnp.exp2(s_old_ref[...] - jnp.tile(m_new, (1, tk // LANES))).astype(jnp.bfloat16)
        acc_sc[...] = jnp.tile(alpha, (1, 2)) * acc_sc[...] + jnp.dot(
            p, v_ref[0], preferred_element_type=jnp.float32)
        m_sc[...] = m_new
        s_new_ref[...] = lax.dot_general(q2_sc[...], k_ref[0], (((1,), (1,)), ((), ())),
                                         preferred_element_type=jnp.float32)

    @pl.when(u % 2 == 0)
    def _():
        step(sa_sc, sb_sc)

    @pl.when(u % 2 == 1)
    def _():
        step(sb_sc, sa_sc)

    @pl.when(new_q & (u > 0))
    def _():
        acc = acc_sc[...]
        o = acc[:, :B_VD] / acc[:, B_VD:]
        o_ref[0] = (o[:tq, :] - lam_ref[0] * o[tq:, :]).astype(o_ref.dtype)
        m_sc[...] = jnp.full_like(m_sc, NEG_INF)
        acc_sc[...] = jnp.zeros_like(acc_sc)


def _diff_attention(lam, q, k_hm, v1_hm, *, tq, tk, q_block_offset, n_q_blocks, k_block_offset, nk):
    b, t, _ = q.shape
    kb = t // tk
    n_steps = n_q_blocks * nk + 1
    k_blk = lambda bi, u: bi * kb + k_block_offset + u % nk
    in_specs = [pl.BlockSpec((1, tq, 2 * B_HD),
                             lambda bi, h, u, *_: (bi, jnp.minimum(u // nk, n_q_blocks - 1) + q_block_offset, h)),
                pl.BlockSpec((1, tk, 2 * B_HD), lambda bi, h, u, *_: (h, k_blk(bi, u), 0)),
                pl.BlockSpec((1, tk, 2 * B_VD), lambda bi, h, u, *_: (h, k_blk(bi, jnp.maximum(u - 1, 0)), 0))]
    return pl.pallas_call(
        functools.partial(_diff_flash_kernel, tq=tq, tk=tk, nk=nk),
        grid_spec=pltpu.PrefetchScalarGridSpec(
            num_scalar_prefetch=1, grid=(b, B_H, n_steps),
            in_specs=in_specs,
            out_specs=pl.BlockSpec((1, tq, B_VD), lambda bi, h, u, *_: (bi, jnp.maximum(u - 1, 0) // nk, h)),
            scratch_shapes=[pltpu.VMEM((2 * tq, 2 * B_HD), jnp.bfloat16),
                            pltpu.VMEM((2 * tq, tk), jnp.float32),
                            pltpu.VMEM((2 * tq, tk), jnp.float32),
                            pltpu.VMEM((2 * tq, LANES), jnp.float32),
                            pltpu.VMEM((2 * tq, 2 * B_VD), jnp.float32)]),
        out_shape=jax.ShapeDtypeStruct((b, n_q_blocks * tq, B_V_W), jnp.float32),
        compiler_params=_cparams(("arbitrary", "arbitrary", "arbitrary")),
        name="diff_flash",
    )(lam, q, k_hm, v1_hm)


def _hgrn_sub_chunk(q_ref, v_ref, lf_ref, o_ref, st_ref, r0, tri, rev):
    half = HGRN_SUB // 2
    row8 = lax.broadcasted_iota(jnp.int32, (half, C_DK), 0)
    lf = lf_ref[pl.ds(r0, HGRN_SUB), :] * LOG2_E
    lf_a = lf.astype(jnp.bfloat16)
    rem = lf - lf_a.astype(jnp.float32)
    lf_b = rem.astype(jnp.bfloat16)
    lf_c = (rem - lf_b.astype(jnp.float32)).astype(jnp.bfloat16)
    bl = (jnp.dot(tri, lf_a, preferred_element_type=jnp.float32)
          + jnp.dot(tri, lf_b, preferred_element_type=jnp.float32)
          + jnp.dot(tri, lf_c, preferred_element_type=jnp.float32))
    q = q_ref[pl.ds(r0, HGRN_SUB), :]
    v = v_ref[pl.ds(r0, HGRN_SUB), :]
    last = 0 if rev else HGRN_SUB - 1
    for hh in range(C_H):
        sl = slice(hh * C_DK, (hh + 1) * C_DK)
        lfh, blh, qh, vh = lf[:, sl], bl[:, sl], q[:, sl], v[:, sl]
        tot = blh[last:last + 1, :]
        kh = 1.0 - jnp.exp2(lfh)
        qt = qh * jnp.exp2(blh)
        kt = kh * jnp.exp2(tot - blh)
        st = st_ref[hh]
        o_sub = lax.dot_general(qt.astype(jnp.bfloat16), st.astype(jnp.bfloat16),
                                (((1,), (1,)), ((), ())), preferred_element_type=jnp.float32)
        parts = [o_sub[:half, :], o_sub[half:, :]]
        for ss in range(HGRN_SUB):
            ks = kh[ss:ss + 1, :]
            bs = blh[ss:ss + 1, :]
            vs = vh[ss:ss + 1, :]
            for p in range(2):
                lo = p * half
                if rev:
                    reached = lo <= ss
                    full = lo + half - 1 <= ss
                    ok = row8 + lo <= ss
                else:
                    reached = lo + half - 1 >= ss
                    full = lo >= ss
                    ok = row8 + lo >= ss
                if not reached:
                    continue
                diff = blh[lo:lo + half, :] - bs
                if not full:
                    diff = jnp.where(ok, diff, NEG_INF)
                col = jnp.sum((qh[lo:lo + half, :] * ks) * jnp.exp2(diff), axis=1, keepdims=True)
                parts[p] = parts[p] + col * vs
        o_ref[pl.ds(r0, half), sl] = parts[0]
        o_ref[pl.ds(r0 + half, half), sl] = parts[1]
        upd = lax.dot_general(vh.astype(jnp.bfloat16), kt.astype(jnp.bfloat16),
                              (((0,), (0,)), ((), ())), preferred_element_type=jnp.float32)
        st_ref[hh] = jnp.exp2(tot) * st + upd


def _hgrn_kernel(qf_ref, vf_ref, lff_ref, qb_ref, vb_ref, lfb_ref, of_ref, ob_ref, st_ref, *, tb):
    t = pl.program_id(1)
    nsub = tb // HGRN_SUB

    @pl.when(t == 0)
    def _():
        st_ref[...] = jnp.zeros_like(st_ref)

    rr = lax.broadcasted_iota(jnp.int32, (HGRN_SUB, HGRN_SUB), 0)
    cc = lax.broadcasted_iota(jnp.int32, (HGRN_SUB, HGRN_SUB), 1)
    tri_f = (cc <= rr).astype(jnp.bfloat16)
    tri_b = (cc >= rr).astype(jnp.bfloat16)

    def body(i, carry):
        rf = pl.multiple_of(i * HGRN_SUB, HGRN_SUB)
        rb = pl.multiple_of((nsub - 1 - i) * HGRN_SUB, HGRN_SUB)
        _hgrn_sub_chunk(qf_ref.at[0], vf_ref.at[0], lff_ref.at[0, 0], of_ref.at[0], st_ref.at[0], rf, tri_f, False)
        _hgrn_sub_chunk(qb_ref.at[0], vb_ref.at[0], lfb_ref.at[0, 0], ob_ref.at[0], st_ref.at[1], rb, tri_b, True)
        return carry

    lax.fori_loop(0, nsub, body, 0, unroll=HGRN_UNROLL)


def _hgrn_scan(q, v, lf, *, n_ctx_blocks):
    b, t, w = q.shape
    tb = HGRN_TB
    nblk = t // tb
    nlat = nblk - n_ctx_blocks

    def fwd_block(ti):
        return jnp.where(ti < n_ctx_blocks, nlat + ti, ti - n_ctx_blocks)

    def bwd_block(ti):
        return nblk - 1 - ti

    rows = lambda blk: pl.BlockSpec((1, tb, w), lambda bi, ti: (bi, blk(ti), 0))
    gate = lambda di, blk: pl.BlockSpec((1, 1, tb, w), lambda bi, ti: (di, bi, blk(ti), 0))
    out = jax.ShapeDtypeStruct((b, t, w), jnp.float32)
    return pl.pallas_call(
        functools.partial(_hgrn_kernel, tb=tb),
        grid=(b, nblk),
        in_specs=[rows(fwd_block), rows(fwd_block), gate(0, fwd_block),
                  rows(bwd_block), rows(bwd_block), gate(1, bwd_block)],
        out_specs=[rows(fwd_block), rows(bwd_block)],
        out_shape=[out, out],
        scratch_shapes=[pltpu.VMEM((2, C_H, C_DV, C_DK), jnp.float32)],
        compiler_params=_cparams(("arbitrary", "arbitrary")),
        name="hgrn2_scan",
    )(q, v, lf, q, v, lf)


def _head_rms(x, w, n_heads, width):
    out = []
    for hh in range(n_heads):
        blk = x[:, hh * width:(hh + 1) * width]
        out.append(_rms_rows(blk, w))
    return jnp.concatenate(out, axis=1)


def _merge_kernel(oa_ref, ob_ref, ocf_ref, ocb_ref, cg_ref, gt_ref, x_ref, ng_ref, ma_ref, mb_ref, dw_ref, hw_ref,
                  wbr_ref, wo_ref, xo_ref, ho_ref, hb_ref, *, diff_out_scale):
    ob = (_head_rms(ob_ref[...], dw_ref[...], B_H, B_VD) * diff_out_scale).astype(jnp.bfloat16)
    cg = cg_ref[...].astype(jnp.float32)
    oc = (_head_rms(ocf_ref[...] + ocb_ref[...], hw_ref[...], C_H, C_DV) * (cg * jax.nn.sigmoid(cg)))
    acc = None
    for i, o in enumerate((oa_ref[...], ob, oc.astype(jnp.bfloat16))):
        z = jnp.dot(o, wbr_ref[i], preferred_element_type=jnp.float32)
        g = jax.nn.sigmoid(gt_ref[:, i * D_MODEL:(i + 1) * D_MODEL].astype(jnp.float32))
        acc = g * z if acc is None else acc + g * z
    y = jnp.dot(acc.astype(jnp.bfloat16), wo_ref[...], preferred_element_type=jnp.float32)
    ma, mb = ma_ref[0], mb_ref[0]
    for half, (xn, mod) in enumerate(zip(_residual_tile(x_ref, y, ng_ref[1], ma, mb, 2), (ma, mb))):
        rows = slice(half * HALF_TM, (half + 1) * HALF_TM)
        xo_ref[rows, :] = xn
        h_next = _rms_rows(xn, ng_ref[2]) * (1.0 + mod[4]) + mod[3]
        ho_ref[rows, :] = h_next
        hb_ref[0, rows, :] = h_next.astype(hb_ref.dtype)
    hb_ref[1] = jnp.zeros((MM_TM, D_MODEL), hb_ref.dtype)


def _merge(oa, ob, ocf, ocb, cg, gates, x, norm_g, mods, diff_w, hgrn_w, w_br, w_o, *, t_rows, seq,
           diff_out_scale):
    m = x.shape[0]
    tm = MM_TM
    row = lambda i: (i, 0)
    f32 = jnp.float32
    return pl.pallas_call(
        functools.partial(_merge_kernel, diff_out_scale=diff_out_scale),
        grid=(m // tm,),
        in_specs=[pl.BlockSpec((tm, BRANCH_W), row), pl.BlockSpec((tm, BRANCH_W), row),
                  pl.BlockSpec((tm, BRANCH_W), row), pl.BlockSpec((tm, BRANCH_W), row),
                  pl.BlockSpec((tm, BRANCH_W), row),
                  pl.BlockSpec((tm, GATE_W), row), pl.BlockSpec((tm, D_MODEL), row),
                  pl.BlockSpec((4, 1, D_MODEL), lambda i: (0, 0, 0))] + _mod_specs(t_rows, seq) + [
                  pl.BlockSpec((1, B_VD), lambda i: (0, 0)), pl.BlockSpec((1, C_DV), lambda i: (0, 0)),
                  pl.BlockSpec((N_BRANCH, BRANCH_W, D_MODEL), lambda i: (0, 0, 0)),
                  pl.BlockSpec((D_MODEL, D_MODEL), lambda i: (0, 0))],
        out_specs=[pl.BlockSpec((tm, D_MODEL), row), pl.BlockSpec((tm, D_MODEL), row),
                   pl.BlockSpec((2, tm, D_MODEL), lambda i: (0, i, 0))],
        out_shape=[jax.ShapeDtypeStruct((m, D_MODEL), f32), jax.ShapeDtypeStruct((m, D_MODEL), f32),
                   jax.ShapeDtypeStruct((2, m, D_MODEL), jnp.bfloat16)],
        compiler_params=_cparams(("arbitrary",)),
        name="branch_merge",
    )(oa, ob, ocf, ocb, cg, gates, x, norm_g, mods, mods, diff_w, hgrn_w, w_br, w_o)


def _moe_kernel(blk_e_ref, n_used_ref, x_ref, wgu_ref, bgu_ref, wdn_ref, bdn_ref, y_ref, wgu_sc, wdn_sc):
    i = pl.program_id(0)
    n_used = n_used_ref[0]
    e = blk_e_ref[i]
    e_prev = blk_e_ref[jnp.maximum(i - 1, 0)]

    @pl.when((i == 0) | (e != e_prev))
    def _():
        wgu_sc[...] = wgu_ref[0, 0].astype(jnp.bfloat16)
        wdn_sc[...] = wdn_ref[0, 0].astype(jnp.bfloat16)

    @pl.when(i < n_used)
    def _():
        gu = jnp.dot(x_ref[...], wgu_sc[...], preferred_element_type=jnp.float32) + bgu_ref[0, 0]
        g = jnp.minimum(gu[:, :EXPERT_FF], SWIGLU_LIMIT)
        u = jnp.clip(gu[:, EXPERT_FF:], -SWIGLU_LIMIT, SWIGLU_LIMIT)
        act = (u + 1.0) * (g * jax.nn.sigmoid(SWIGLU_ALPHA * g))
        y = jnp.dot(act.astype(jnp.bfloat16), wdn_sc[...], preferred_element_type=jnp.float32) + bdn_ref[0, 0]
        y_ref[...] = y.astype(y_ref.dtype)

    @pl.when(i >= n_used)
    def _():
        y_ref[...] = jnp.zeros_like(y_ref)


def _moe_experts(blk_e, n_used, xs, w_gu, b_gu, w_dn, b_dn, *, layer):
    r, dm = xs.shape
    tm = MOE_TM
    n_blk = blk_e.shape[0]
    return pl.pallas_call(
        _moe_kernel,
        grid_spec=pltpu.PrefetchScalarGridSpec(
            num_scalar_prefetch=2, grid=(n_blk,),
            in_specs=[pl.BlockSpec((tm, dm), lambda i, be, nu: (i, 0)),
                      pl.BlockSpec((1, 1, dm, 2 * EXPERT_FF), lambda i, be, nu: (layer, be[i], 0, 0)),
                      pl.BlockSpec((1, 1, 1, 2 * EXPERT_FF), lambda i, be, nu: (layer, be[i], 0, 0)),
                      pl.BlockSpec((1, 1, EXPERT_FF, dm), lambda i, be, nu: (layer, be[i], 0, 0)),
                      pl.BlockSpec((1, 1, 1, dm), lambda i, be, nu: (layer, be[i], 0, 0))],
            out_specs=pl.BlockSpec((tm, dm), lambda i, be, nu: (i, 0)),
            scratch_shapes=[pltpu.VMEM((dm, 2 * EXPERT_FF), jnp.bfloat16),
                            pltpu.VMEM((EXPERT_FF, dm), jnp.bfloat16)]),
        out_shape=jax.ShapeDtypeStruct((r, dm), jnp.bfloat16),
        compiler_params=_cparams(("arbitrary",)),
        name="moe_experts",
    )(blk_e, n_used, xs, w_gu, b_gu.reshape(DEPTH, N_EXPERTS, 1, -1), w_dn,
      b_dn.reshape(DEPTH, N_EXPERTS, 1, -1))


def _combine_kernel(*refs):
    yk_refs = refs[:TOP_K]
    gt_ref, x_ref, ng_ref, ma_ref, mb_ref, xo_ref = refs[TOP_K:]
    gt = gt_ref[...]
    f = None
    for k in range(TOP_K):
        term = gt[:, k:k + 1] * yk_refs[k][...].astype(jnp.float32)
        f = term if f is None else f + term
    for half, xn in enumerate(_residual_tile(x_ref, f, ng_ref[3], ma_ref[0], mb_ref[0], 5)):
        xo_ref[half * HALF_TM:(half + 1) * HALF_TM, :] = xn


def _moe_combine(yk, gates, x, norm_g, mods, *, t_rows, seq):
    m = x.shape[0]
    tm = MM_TM
    n_tiles = m // tm
    row = lambda i: (i, 0)
    choice = lambda k: pl.BlockSpec((tm, D_MODEL), lambda i: (k * n_tiles + i, 0))
    return pl.pallas_call(
        _combine_kernel,
        grid=(n_tiles,),
        in_specs=[choice(k) for k in range(TOP_K)] + [
            pl.BlockSpec((tm, TOP_K), row), pl.BlockSpec((tm, D_MODEL), row),
            pl.BlockSpec((4, 1, D_MODEL), lambda i: (0, 0, 0))] + _mod_specs(t_rows, seq),
        out_specs=pl.BlockSpec((tm, D_MODEL), row),
        out_shape=jax.ShapeDtypeStruct((m, D_MODEL), jnp.float32),
        compiler_params=_cparams(("arbitrary",)),
        name="moe_combine",
    )(*([yk] * TOP_K), gates, x, norm_g, mods, mods)


def _moe_ffn(h, h_rows, x, norm_g, mods, w_router, b_router, w_gu, b_gu, w_dn, b_dn, *, layer, t_rows, seq):
    n, dm = h.shape
    nk = n * TOP_K
    w_r = jnp.zeros((dm, LANES), jnp.float32).at[:, :N_EXPERTS].set(w_router)
    logits = _matmul(h, w_r, tm=MM_TM, tn=LANES, out_dtype=jnp.float32,
                     precision=lax.Precision.HIGHEST, name="router")[:, :N_EXPERTS] + b_router
    top_v, top_e = lax.top_k(logits, TOP_K)
    gates = jax.nn.softmax(top_v, axis=-1)
    flat_e = top_e.reshape(-1).astype(jnp.int32)
    experts = jnp.arange(N_EXPERTS, dtype=jnp.int32)
    onehot = (flat_e[:, None] == experts[None, :]).astype(jnp.int32)
    incl = jnp.cumsum(onehot, axis=0)
    rank = jnp.sum(incl * onehot, axis=1) - 1
    counts = incl[-1]
    padded = (counts + MOE_TM - 1) // MOE_TM * MOE_TM
    pad_end = jnp.cumsum(padded)
    pad_start = pad_end - padded
    grp_start = jnp.cumsum(counts) - counts
    dest = jnp.sum(onehot * pad_start[None, :], axis=1) + rank
    n_blk = (nk + MOE_TM - 1) // MOE_TM + N_EXPERTS
    rows = n_blk * MOE_TM
    blk_start = jnp.arange(n_blk, dtype=jnp.int32) * MOE_TM
    blk_e = jnp.minimum(jnp.sum((pad_end[None, :] <= blk_start[:, None]).astype(jnp.int32), axis=1),
                        N_EXPERTS - 1)
    n_used = (pad_end[-1] // MOE_TM).astype(jnp.int32).reshape(1)
    order = jnp.argsort(flat_e)
    row_e = jnp.repeat(blk_e, MOE_TM)
    sorted_pos = jnp.arange(rows, dtype=jnp.int32) - pad_start[row_e] + grp_start[row_e]
    row_tok = (order[jnp.clip(sorted_pos, 0, nk - 1)] // TOP_K).astype(jnp.int32)
    xs = h_rows[row_tok]
    y = _moe_experts(blk_e, n_used, xs, w_gu, b_gu, w_dn, b_dn, layer=layer)
    yk = y[dest.reshape(n, TOP_K).T.reshape(-1)]
    return _moe_combine(yk, gates, x, norm_g, mods, t_rows=t_rows, seq=seq)


def _rope_tables(rows, n_ctx, batch):
    row = jnp.repeat(jnp.arange(rows, dtype=jnp.float32), GRID_W)
    col = jnp.tile(jnp.arange(GRID_W, dtype=jnp.float32), rows)
    n_freq = HEAD_DIM // 4
    inv = ROPE_THETA ** (-jnp.arange(n_freq, dtype=jnp.float32) / n_freq)
    ang_r = row[:, None] * inv
    ang_c = col[:, None] * inv
    cos_h = jnp.concatenate([jnp.cos(ang_r), jnp.cos(ang_r), jnp.cos(ang_c), jnp.cos(ang_c)], axis=1)
    sin_h = jnp.concatenate([-jnp.sin(ang_r), jnp.sin(ang_r), -jnp.sin(ang_c), jnp.sin(ang_c)], axis=1)
    cos_t = jnp.concatenate([cos_h, jnp.ones((n_ctx, HEAD_DIM), jnp.float32)], axis=0)
    sin_t = jnp.concatenate([sin_h, jnp.zeros((n_ctx, HEAD_DIM), jnp.float32)], axis=0)
    reps = (batch, LANES // HEAD_DIM)
    return jnp.tile(cos_t, reps), jnp.tile(sin_t, reps)


def _rope_partner(n_cols):
    idx = jnp.arange(n_cols)
    quarter = HEAD_DIM // 4
    return jnp.where((idx % (2 * quarter)) < quarter, idx + quarter, idx - quarter)


def kernel(x, c, ctx, c_ctx, w_mod, b_mod, norm_g, w_in, attn_sink, diff_lambda, diff_norm_w,
           hgrn_lb_logits, hgrn_norm_w, w_branch, w_out, w_router, b_router, w_gate_up, b_gate_up,
           w_down, b_down):
    b, s, dm = x.shape
    l = ctx.shape[1]
    t = s + l
    f32 = jnp.float32
    bf16 = jnp.bfloat16
    cos_t, sin_t = _rope_tables(s // GRID_W, l, b)
    lb_cum = jnp.cumsum(jax.nn.softmax(hgrn_lb_logits.astype(f32), axis=0), axis=0)
    lower_bounds = lb_cum - lb_cum[0:1]
    cond_rows = 16
    cond = jnp.zeros((cond_rows, dm), f32).at[:b].set(jax.nn.silu(c)).at[b].set(jax.nn.silu(c_ctx))
    scale = HEAD_DIM ** -0.5
    xa = jnp.concatenate([x, ctx], axis=1).reshape(b * t, dm)
    h_moe = None
    for layer in range(DEPTH):
        lam_init = 0.8 - 0.6 * math.exp(-0.3 * layer)
        mod_all = _matmul(cond, w_mod[layer], tm=cond_rows, tn=3 * dm, out_dtype=f32,
                          precision=lax.Precision.HIGHEST, name="adaln") + b_mod[layer]
        mod = mod_all[:b].reshape(b, 1, 6, 1, dm)
        mod_c = jnp.broadcast_to(mod_all[b:b + 1].reshape(1, 1, 6, 1, dm), (b, 1, 6, 1, dm))
        mods = jnp.concatenate([mod, mod_c], axis=1).reshape(2 * b, 6, 1, dm)
        ng = norm_g[layer].reshape(4, 1, dm)
        g0 = ng[0:1]
        geo = dict(t_rows=t, seq=s)

        w = w_in[layer]
        w_aq = (w[:, OFF_A:OFF_A + A_Q_W] * scale).reshape(dm, A_HKV, A_GROUP, HEAD_DIM)
        pad = jnp.zeros((dm, A_GROUP, HEAD_DIM), f32)
        w_aq8 = jnp.concatenate([jnp.concatenate([w_aq[:, 0], pad], axis=-1),
                                 jnp.concatenate([pad, w_aq[:, 1]], axis=-1)], axis=1).reshape(dm, A_Q8_W)
        w_a = jnp.concatenate([w_aq8, w[:, OFF_A + A_Q_W:OFF_A + A_W]], axis=1).astype(bf16)
        w_b = w[:, OFF_B:OFF_B + B_W].at[:, :B_QK_W].multiply(scale * LOG2_E).astype(bf16)
        proj_a = _proj_rope(xa, g0, mods, w_a, w_a[:, _rope_partner(A_ROPE_W2)], cos_t, sin_t,
                            rope_w=A_ROPE_W2, name="proj_a", **geo).reshape(b, t, A_W2)
        bq, bk_hm, bv1_hm = _proj_b(xa, g0, mods, w_b, w_b[:, _rope_partner(B_ROPE_W)], cos_t, sin_t, **geo)
        bq = bq.reshape(b, t, B_QK_W)
        lb = lower_bounds[layer]
        lb_par = jnp.stack([jnp.log(lb[0]), jnp.log1p(-lb[0]), jnp.log(lb[1]), jnp.log1p(-lb[1])])
        cq, ci, lf, cg = _proj_hgrn(xa, g0, mods, w[:, OFF_C:OFF_C + C_W].astype(bf16),
                                    lb_par.reshape(4, 1, C_K_W), **geo)
        gates = _proj_plain(xa, g0, mods, w[:, OFF_GATE:].astype(bf16), name="gate_proj", **geo)

        sink32 = attn_sink[layer].astype(f32)
        o_a = _window_attention(proj_a, sink32, seq=s)

        lp = diff_lambda[layer].astype(f32)
        lam = (jnp.exp(jnp.sum(lp[0] * lp[1])) - jnp.exp(jnp.sum(lp[2] * lp[3])) + lam_init).reshape(1).astype(f32)
        o_b = _diff_attention(lam, bq, bk_hm, bv1_hm, tq=FLASH_TQ, tk=FLASH_TK, q_block_offset=0,
                              n_q_blocks=s // FLASH_TQ, k_block_offset=0, nk=t // FLASH_TK)
        o_bx = _diff_attention(lam, bq, bk_hm, bv1_hm, tq=l, tk=l, q_block_offset=s // l, n_q_blocks=1,
                               k_block_offset=s // l, nk=1)
        o_b = jnp.concatenate([o_b, o_bx], axis=1)

        o_cf, o_cb = _hgrn_scan(cq.reshape(b, t, C_K_W), ci.reshape(b, t, C_V_W), lf.reshape(2, b, t, C_K_W),
                                n_ctx_blocks=l // HGRN_TB)

        col = jnp.arange(A_Q_W)
        head_of_col = ((col % LANES) // HEAD_DIM) * A_GROUP + col // LANES
        w_br = w_branch[layer].at[0].set(w_branch[layer][0][head_of_col * HEAD_DIM + col % HEAD_DIM])
        xa, h_moe, h_rows = _merge(o_a.reshape(b * t, A_Q_W), o_b.reshape(b * t, B_V_W), o_cf.reshape(b * t, C_V_W),
                           o_cb.reshape(b * t, C_V_W), cg,
                           gates, xa, ng, mods, diff_norm_w[layer].reshape(1, B_VD).astype(f32),
                           hgrn_norm_w[layer].reshape(1, C_DV).astype(f32),
                           w_br.astype(bf16), w_out[layer].astype(bf16),
                           diff_out_scale=1 - lam_init, **geo)

        xa = _moe_ffn(h_moe, h_rows.reshape(2 * b * t, dm), xa, ng, mods, w_router[layer], b_router[layer],
                      w_gate_up, b_gate_up, w_down, b_down, layer=layer, **geo)
    return xa.reshape(b, t, dm)[:, :s]
```

```python
import functools
import math

import jax
import jax.numpy as jnp
from jax import lax
from jax.experimental import pallas as pl
from jax.experimental.pallas import tpu as pltpu

D_MODEL = 1024
DEPTH = 2
GRID_W = 64
HEAD_DIM = 64
ROPE_THETA = 10000.0
RMS_EPS = 1e-6
NEG_INF = -1e30
BLOCK = 128
A_HQ = 8
A_HKV = 2
A_GROUP = A_HQ // A_HKV
WINDOW = 128
B_H = 4
B_HD = HEAD_DIM
B_VD = 2 * HEAD_DIM
C_H = 4
C_DK = 128
C_DV = 128
N_BRANCH = 3
BRANCH_W = 512
A_Q_W = A_HQ * HEAD_DIM
A_KV_W = A_HKV * HEAD_DIM
B_QK_W = B_H * 2 * B_HD
B_V_W = B_H * B_VD
C_K_W = C_H * C_DK
C_V_W = C_H * C_DV
GATE_W = N_BRANCH * D_MODEL
N_EXPERTS = 32
TOP_K = 4
EXPERT_FF = D_MODEL
SWIGLU_LIMIT = 7.0
SWIGLU_ALPHA = 1.702

A_W = A_Q_W + 2 * A_KV_W
A_ROPE_W = A_Q_W + A_KV_W
B_W = 2 * B_QK_W + B_V_W
B_ROPE_W = 2 * B_QK_W
C_W = 3 * C_K_W + 2 * C_V_W
OFF_A = 0
OFF_B = OFF_A + A_W
OFF_C = OFF_B + B_W
OFF_GATE = OFF_C + C_W

V7X_VMEM_LIMIT_BYTES = 56 * 1024 * 1024
LANES = 128
LOG2_E = math.log2(math.e)

MM_TM = 512
HALF_TM = MM_TM // 2
FLASH_TQ = 2048
FLASH_TK = 768
HGRN_TB = 256
HGRN_SUB = 16
HGRN_UNROLL = 4
MOE_TM = 512


def _cparams(sem):
    return pltpu.CompilerParams(dimension_semantics=sem, vmem_limit_bytes=V7X_VMEM_LIMIT_BYTES)


def _rms_rows(x, g):
    return (x * lax.rsqrt(jnp.mean(x * x, axis=-1, keepdims=True) + RMS_EPS)) * g


def _modulated_tile(x_ref, g, mod_a, mod_b, shift_slot, scale_slot):
    out = []
    for half, mod in enumerate((mod_a, mod_b)):
        x = x_ref[half * HALF_TM:(half + 1) * HALF_TM, :]
        out.append(_rms_rows(x, g) * (1.0 + mod[scale_slot]) + mod[shift_slot])
    return jnp.concatenate(out, axis=0)


def _residual_tile(x_ref, y, g, mod_a, mod_b, gate_slot):
    out = []
    for half, mod in enumerate((mod_a, mod_b)):
        rows = slice(half * HALF_TM, (half + 1) * HALF_TM)
        out.append(x_ref[rows, :] + mod[gate_slot] * _rms_rows(y[rows, :], g))
    return out


def _mod_specs(t_rows, seq):
    halves_per_batch = t_rows // HALF_TM
    lat_halves = seq // HALF_TM

    def row(u):
        return (u // halves_per_batch) * 2 + jnp.where(u % halves_per_batch >= lat_halves, 1, 0)

    blk = (1, 6, 1, D_MODEL)
    return [pl.BlockSpec(blk, lambda i: (row(2 * i), 0, 0, 0)),
            pl.BlockSpec(blk, lambda i: (row(2 * i + 1), 0, 0, 0))]


def _proj_rope_kernel(x_ref, g_ref, ma_ref, mb_ref, w_ref, wsw_ref, cos_ref, sin_ref, o_ref, *, rope_w):
    h = _modulated_tile(x_ref, g_ref[0], ma_ref[0], mb_ref[0], 0, 1).astype(jnp.bfloat16)
    z = jnp.dot(h, w_ref[...], preferred_element_type=jnp.float32)
    zs = jnp.dot(h, wsw_ref[...], preferred_element_type=jnp.float32)
    reps = rope_w // LANES
    rot = z[:, :rope_w] * jnp.tile(cos_ref[...], (1, reps)) + zs * jnp.tile(sin_ref[...], (1, reps))
    o_ref[:, :rope_w] = rot.astype(o_ref.dtype)
    o_ref[:, rope_w:] = z[:, rope_w:].astype(o_ref.dtype)


def _proj_b_kernel(x_ref, g_ref, ma_ref, mb_ref, w_ref, wsw_ref, cos_ref, sin_ref, q_ref, k_ref, v1_ref):
    h = _modulated_tile(x_ref, g_ref[0], ma_ref[0], mb_ref[0], 0, 1).astype(jnp.bfloat16)
    z = jnp.dot(h, w_ref[...], preferred_element_type=jnp.float32)
    zs = jnp.dot(h, wsw_ref[...], preferred_element_type=jnp.float32)
    reps = B_ROPE_W // LANES
    rot = z[:, :B_ROPE_W] * jnp.tile(cos_ref[...], (1, reps)) + zs * jnp.tile(sin_ref[...], (1, reps))
    q_ref[...] = rot[:, :B_QK_W].astype(q_ref.dtype)
    for hh in range(B_H):
        k_ref[hh] = rot[:, B_QK_W + hh * 2 * B_HD:B_QK_W + (hh + 1) * 2 * B_HD].astype(k_ref.dtype)
        v1_ref[hh, :, :B_VD] = z[:, B_ROPE_W + hh * B_VD:B_ROPE_W + (hh + 1) * B_VD].astype(v1_ref.dtype)
        v1_ref[hh, :, B_VD:] = jnp.ones((MM_TM, B_VD), v1_ref.dtype)


def _proj_plain_kernel(x_ref, g_ref, ma_ref, mb_ref, w_ref, o_ref):
    h = _modulated_tile(x_ref, g_ref[0], ma_ref[0], mb_ref[0], 0, 1).astype(jnp.bfloat16)
    o_ref[...] = jnp.dot(h, w_ref[...], preferred_element_type=jnp.float32).astype(o_ref.dtype)


def _log_forget(z, log_lb, log_1m_lb):
    log_sig = jnp.minimum(z, 0.0) - jnp.log(1.0 + jnp.exp(-jnp.abs(z)))
    c = log_1m_lb + log_sig
    return jnp.maximum(log_lb, c) + jnp.log(1.0 + jnp.exp(-jnp.abs(log_lb - c)))


def _proj_hgrn_kernel(x_ref, g_ref, ma_ref, mb_ref, w_ref, lb_ref, q_ref, v_ref, lf_ref, cg_ref):
    h = _modulated_tile(x_ref, g_ref[0], ma_ref[0], mb_ref[0], 0, 1).astype(jnp.bfloat16)
    z = jnp.dot(h, w_ref[...], preferred_element_type=jnp.float32)
    zq = z[:, :C_K_W]
    q_ref[...] = zq * jax.nn.sigmoid(zq)
    v_ref[...] = z[:, C_K_W:C_K_W + C_V_W]
    off = C_K_W + C_V_W
    lf_ref[0] = _log_forget(z[:, off:off + C_K_W], lb_ref[0], lb_ref[1])
    lf_ref[1] = _log_forget(z[:, off + C_K_W:off + 2 * C_K_W], lb_ref[2], lb_ref[3])
    cg_ref[...] = z[:, off + 2 * C_K_W:].astype(cg_ref.dtype)


def _proj_common_specs(t_rows, seq):
    row = lambda i: (i, 0)
    return [pl.BlockSpec((MM_TM, D_MODEL), row),
            pl.BlockSpec((1, 1, D_MODEL), lambda i: (0, 0, 0))] + _mod_specs(t_rows, seq)


def _proj_rope(x, g, mods, w, wsw, cos, sin, *, t_rows, seq, rope_w, name):
    m = x.shape[0]
    n = w.shape[1]
    row = lambda i: (i, 0)
    whole = lambda i: (0, 0)
    return pl.pallas_call(
        functools.partial(_proj_rope_kernel, rope_w=rope_w),
        grid=(m // MM_TM,),
        in_specs=_proj_common_specs(t_rows, seq) + [
            pl.BlockSpec((D_MODEL, n), whole), pl.BlockSpec((D_MODEL, rope_w), whole),
            pl.BlockSpec((MM_TM, LANES), row), pl.BlockSpec((MM_TM, LANES), row)],
        out_specs=pl.BlockSpec((MM_TM, n), row),
        out_shape=jax.ShapeDtypeStruct((m, n), jnp.bfloat16),
        compiler_params=_cparams(("arbitrary",)),
        name=name,
    )(x, g, mods, mods, w, wsw, cos, sin)


def _proj_b(x, g, mods, w, wsw, cos, sin, *, t_rows, seq):
    m = x.shape[0]
    row = lambda i: (i, 0)
    whole = lambda i: (0, 0)
    heads = lambda i: (0, i, 0)
    bf16 = jnp.bfloat16
    return pl.pallas_call(
        _proj_b_kernel,
        grid=(m // MM_TM,),
        in_specs=_proj_common_specs(t_rows, seq) + [
            pl.BlockSpec((D_MODEL, B_W), whole), pl.BlockSpec((D_MODEL, B_ROPE_W), whole),
            pl.BlockSpec((MM_TM, LANES), row), pl.BlockSpec((MM_TM, LANES), row)],
        out_specs=[pl.BlockSpec((MM_TM, B_QK_W), row), pl.BlockSpec((B_H, MM_TM, 2 * B_HD), heads),
                   pl.BlockSpec((B_H, MM_TM, 2 * B_VD), heads)],
        out_shape=[jax.ShapeDtypeStruct((m, B_QK_W), bf16), jax.ShapeDtypeStruct((B_H, m, 2 * B_HD), bf16),
                   jax.ShapeDtypeStruct((B_H, m, 2 * B_VD), bf16)],
        compiler_params=_cparams(("arbitrary",)),
        name="proj_b",
    )(x, g, mods, mods, w, wsw, cos, sin)


def _proj_plain(x, g, mods, w, *, t_rows, seq, name):
    m = x.shape[0]
    n = w.shape[1]
    row = lambda i: (i, 0)
    return pl.pallas_call(
        _proj_plain_kernel,
        grid=(m // MM_TM,),
        in_specs=_proj_common_specs(t_rows, seq) + [pl.BlockSpec((D_MODEL, n), lambda i: (0, 0))],
        out_specs=pl.BlockSpec((MM_TM, n), row),
        out_shape=jax.ShapeDtypeStruct((m, n), jnp.bfloat16),
        compiler_params=_cparams(("arbitrary",)),
        name=name,
    )(x, g, mods, mods, w)


def _proj_hgrn(x, g, mods, w, lb_par, *, t_rows, seq):
    m = x.shape[0]
    row = lambda i: (i, 0)
    f32 = jnp.float32
    return pl.pallas_call(
        _proj_hgrn_kernel,
        grid=(m // MM_TM,),
        in_specs=_proj_common_specs(t_rows, seq) + [
            pl.BlockSpec((D_MODEL, C_W), lambda i: (0, 0)),
            pl.BlockSpec((4, 1, C_K_W), lambda i: (0, 0, 0))],
        out_specs=[pl.BlockSpec((MM_TM, C_K_W), row), pl.BlockSpec((MM_TM, C_V_W), row),
                   pl.BlockSpec((2, MM_TM, C_K_W), lambda i: (0, i, 0)), pl.BlockSpec((MM_TM, C_V_W), row)],
        out_shape=[jax.ShapeDtypeStruct((m, C_K_W), f32), jax.ShapeDtypeStruct((m, C_V_W), f32),
                   jax.ShapeDtypeStruct((2, m, C_K_W), f32), jax.ShapeDtypeStruct((m, C_V_W), jnp.bfloat16)],
        compiler_params=_cparams(("arbitrary",)),
        name="hgrn_proj",
    )(x, g, mods, mods, w, lb_par)


def _mm_kernel(x_ref, w_ref, o_ref, *, precision):
    o_ref[...] = jnp.dot(x_ref[...], w_ref[...], preferred_element_type=jnp.float32,
                         precision=precision).astype(o_ref.dtype)


def _matmul(x, w, *, tm, tn, out_dtype, precision=None, name):
    m, k = x.shape
    n = w.shape[1]
    assert m % tm == 0 and n % tn == 0, (m, tm, n, tn)
    return pl.pallas_call(
        functools.partial(_mm_kernel, precision=precision),
        grid=(n // tn, m // tm),
        in_specs=[pl.BlockSpec((tm, k), lambda j, i: (i, 0)),
                  pl.BlockSpec((k, tn), lambda j, i: (0, j))],
        out_specs=pl.BlockSpec((tm, tn), lambda j, i: (i, j)),
        out_shape=jax.ShapeDtypeStruct((m, n), out_dtype),
        compiler_params=_cparams(("arbitrary", "arbitrary")),
        name=name,
    )(x, w)


A_Q8_W = A_HQ * LANES
A_W2 = A_Q8_W + 2 * A_KV_W
A_ROPE_W2 = A_Q8_W + A_KV_W
A_K_BLK = A_Q8_W // A_KV_W
A_V_BLK = A_K_BLK + 1


def _attn_a_kernel(sink_ref, q_ref, kp_ref, kc_ref, kn_ref, vp_ref, vc_ref, vn_ref, kx_ref, vx_ref,
                   o_ref, *, seq):
    n = pl.program_id(1)
    k_all = jnp.concatenate([kp_ref[0], kc_ref[0], kn_ref[0], kx_ref[0]], axis=0)
    v_all = jnp.concatenate([vp_ref[0], vc_ref[0], vn_ref[0], vx_ref[0]], axis=0)
    n_keys = k_all.shape[0]
    qi = lax.broadcasted_iota(jnp.int32, (BLOCK, n_keys), 0)
    kj = lax.broadcasted_iota(jnp.int32, (BLOCK, n_keys), 1)
    kpos = n * BLOCK + kj - BLOCK
    local_ok = (jnp.abs(kj - BLOCK - qi) <= WINDOW) & (kpos >= 0) & (kpos < seq) & (n * BLOCK < seq)
    mask = local_ok | (kj >= 3 * BLOCK)
    q8 = jnp.concatenate([q_ref[0, :, hq * LANES:(hq + 1) * LANES] for hq in range(A_HQ)], axis=0)
    s = lax.dot_general(q8, k_all, (((1,), (1,)), ((), ())), preferred_element_type=jnp.float32)
    p_blocks, inv_l = [], []
    for hq in range(A_HQ):
        sh = jnp.where(mask, s[hq * BLOCK:(hq + 1) * BLOCK, :], NEG_INF)
        sk = sink_ref[hq]
        m = jnp.maximum(jnp.max(sh, axis=-1, keepdims=True), sk)
        p = jnp.exp(sh - m)
        inv_l.append(1.0 / (jnp.sum(p, axis=-1, keepdims=True) + jnp.exp(sk - m)))
        p_blocks.append(p.astype(v_all.dtype))
    o = jnp.dot(jnp.concatenate(p_blocks, axis=0), v_all, preferred_element_type=jnp.float32)
    lane = lax.broadcasted_iota(jnp.int32, (BLOCK, LANES), 1)
    for g in range(A_GROUP):
        lo = o[g * BLOCK:(g + 1) * BLOCK, :] * inv_l[g]
        hi = o[(A_GROUP + g) * BLOCK:(A_GROUP + g + 1) * BLOCK, :] * inv_l[A_GROUP + g]
        o_ref[0, :, g * LANES:(g + 1) * LANES] = jnp.where(lane < HEAD_DIM, lo, hi).astype(o_ref.dtype)


def _window_attention(proj_a, sink, *, seq):
    b, t, _ = proj_a.shape
    l = t - seq
    nb = seq // BLOCK
    ctx_blk = seq // l
    q_map = lambda bi, n, *_: (bi, n, 0)
    kv = lambda col, fn: pl.BlockSpec((1, BLOCK, A_KV_W), lambda bi, n, *_: (bi, fn(n), col))
    prev = lambda n: jnp.clip(n - 1, 0, nb - 1)
    cur = lambda n: jnp.minimum(n, nb - 1)
    nxt = lambda n: jnp.minimum(n + 1, nb - 1)
    ctx = lambda col: pl.BlockSpec((1, l, A_KV_W), lambda bi, n, *_: (bi, ctx_blk, col))
    return pl.pallas_call(
        functools.partial(_attn_a_kernel, seq=seq),
        grid_spec=pltpu.PrefetchScalarGridSpec(
            num_scalar_prefetch=1, grid=(b, t // BLOCK),
            in_specs=[pl.BlockSpec((1, BLOCK, A_Q8_W), q_map),
                      kv(A_K_BLK, prev), kv(A_K_BLK, cur), kv(A_K_BLK, nxt),
                      kv(A_V_BLK, prev), kv(A_V_BLK, cur), kv(A_V_BLK, nxt),
                      ctx(A_K_BLK), ctx(A_V_BLK)],
            out_specs=pl.BlockSpec((1, BLOCK, A_Q_W), q_map)),
        out_shape=jax.ShapeDtypeStruct((b, t, A_Q_W), jnp.bfloat16),
        compiler_params=_cparams(("arbitrary", "arbitrary")),
        name="window_gqa",
    )(sink, *([proj_a] * 9))


def _diff_flash_kernel(lam_ref, q_ref, k_ref, v_ref, o_ref, q2_sc, sa_sc, sb_sc, m_sc, acc_sc, *, tq, tk, nk):
    u = pl.program_id(2)
    new_q = u % nk == 0

    @pl.when(u == 0)
    def _():
        m_sc[...] = jnp.full_like(m_sc, NEG_INF)
        acc_sc[...] = jnp.zeros_like(acc_sc)
        sb_sc[...] = jnp.full_like(sb_sc, NEG_INF)

    @pl.when(new_q)
    def _():
        q = q_ref[0]
        lane = lax.broadcasted_iota(jnp.int32, q.shape, 1)
        q2_sc[:tq, :] = jnp.where(lane < B_HD, q, jnp.zeros_like(q))
        q2_sc[tq:, :] = jnp.where(lane >= B_HD, q, jnp.zeros_like(q))

    def step(s_new_ref, s_old_ref):
        m_prev = m_sc[...]
        m_new = jnp.maximum(m_prev, jnp.max(s_old_ref[...], axis=-1, keepdims=True))
        alpha = jnp.exp2(m_prev - m_new)
        p = jnp.exp2(s_old_ref[...] - jnp.tile(m_new, (1, tk // LANES))).astype(jnp.bfloat16)
        acc_sc[...] = jnp.tile(alpha, (1, 2)) * acc_sc[...] + jnp.dot(
            p, v_ref[0], preferred_element_type=jnp.float32)
        m_sc[...] = m_new
        s_new_ref[...] = lax.dot_general(q2_sc[...], k_ref[0], (((1,), (1,)), ((), ())),
                                         preferred_element_type=jnp.float32)

    @pl.when(u % 2 == 0)
    def _():
        step(sa_sc, sb_sc)

    @pl.when(u % 2 == 1)
    def _():
        step(sb_sc, sa_sc)

    @pl.when(new_q & (u > 0))
    def _():
        acc = acc_sc[...]
        o = acc[:, :B_VD] / acc[:, B_VD:]
        o_ref[0] = (o[:tq, :] - lam_ref[0] * o[tq:, :]).astype(o_ref.dtype)
        m_sc[...] = jnp.full_like(m_sc, NEG_INF)
        acc_sc[...] = jnp.zeros_like(acc_sc)


def _diff_attention(lam, q, k_hm, v1_hm, *, tq, tk, q_block_offset, n_q_blocks, k_block_offset, nk):
    b, t, _ = q.shape
    kb = t // tk
    n_steps = n_q_blocks * nk + 1
    k_blk = lambda bi, u: bi * kb + k_block_offset + u % nk
    in_specs = [pl.BlockSpec((1, tq, 2 * B_HD),
                             lambda bi, h, u, *_: (bi, jnp.minimum(u // nk, n_q_blocks - 1) + q_block_offset, h)),
                pl.BlockSpec((1, tk, 2 * B_HD), lambda bi, h, u, *_: (h, k_blk(bi, u), 0)),
                pl.BlockSpec((1, tk, 2 * B_VD), lambda bi, h, u, *_: (h, k_blk(bi, jnp.maximum(u - 1, 0)), 0))]
    return pl.pallas_call(
        functools.partial(_diff_flash_kernel, tq=tq, tk=tk, nk=nk),
        grid_spec=pltpu.PrefetchScalarGridSpec(
            num_scalar_prefetch=1, grid=(b, B_H, n_steps),
            in_specs=in_specs,
            out_specs=pl.BlockSpec((1, tq, B_VD), lambda bi, h, u, *_: (bi, jnp.maximum(u - 1, 0) // nk, h)),
            scratch_shapes=[pltpu.VMEM((2 * tq, 2 * B_HD), jnp.bfloat16),
                            pltpu.VMEM((2 * tq, tk), jnp.float32),
                            pltpu.VMEM((2 * tq, tk), jnp.float32),
                            pltpu.VMEM((2 * tq, LANES), jnp.float32),
                            pltpu.VMEM((2 * tq, 2 * B_VD), jnp.float32)]),
        out_shape=jax.ShapeDtypeStruct((b, n_q_blocks * tq, B_V_W), jnp.float32),
        compiler_params=_cparams(("arbitrary", "arbitrary", "arbitrary")),
        name="diff_flash",
    )(lam, q, k_hm, v1_hm)


def _hgrn_sub_chunk(q_ref, v_ref, lf_ref, o_ref, st_ref, r0, tri, rev):
    half = HGRN_SUB // 2
    row8 = lax.broadcasted_iota(jnp.int32, (half, C_DK), 0)
    lf = lf_ref[pl.ds(r0, HGRN_SUB), :] * LOG2_E
    lf_a = lf.astype(jnp.bfloat16)
    rem = lf - lf_a.astype(jnp.float32)
    lf_b = rem.astype(jnp.bfloat16)
    lf_c = (rem - lf_b.astype(jnp.float32)).astype(jnp.bfloat16)
    bl = (jnp.dot(tri, lf_a, preferred_element_type=jnp.float32)
          + jnp.dot(tri, lf_b, preferred_element_type=jnp.float32)
          + jnp.dot(tri, lf_c, preferred_element_type=jnp.float32))
    q = q_ref[pl.ds(r0, HGRN_SUB), :]
    v = v_ref[pl.ds(r0, HGRN_SUB), :]
    last = 0 if rev else HGRN_SUB - 1
    for hh in range(C_H):
        sl = slice(hh * C_DK, (hh + 1) * C_DK)
        lfh, blh, qh, vh = lf[:, sl], bl[:, sl], q[:, sl], v[:, sl]
        tot = blh[last:last + 1, :]
        kh = 1.0 - jnp.exp2(lfh)
        qt = qh * jnp.exp2(blh)
        kt = kh * jnp.exp2(tot - blh)
        st = st_ref[hh]
        o_sub = lax.dot_general(qt.astype(jnp.bfloat16), st.astype(jnp.bfloat16),
                                (((1,), (1,)), ((), ())), preferred_element_type=jnp.float32)
        parts = [o_sub[:half, :], o_sub[half:, :]]
        for ss in range(HGRN_SUB):
            ks = kh[ss:ss + 1, :]
            bs = blh[ss:ss + 1, :]
            vs = vh[ss:ss + 1, :]
            for p in range(2):
                lo = p * half
                if rev:
                    reached = lo <= ss
                    full = lo + half - 1 <= ss
                    ok = row8 + lo <= ss
                else:
                    reached = lo + half - 1 >= ss
                    full = lo >= ss
                    ok = row8 + lo >= ss
                if not reached:
                    continue
                diff = blh[lo:lo + half, :] - bs
                if not full:
                    diff = jnp.where(ok, diff, NEG_INF)
                col = jnp.sum((qh[lo:lo + half, :] * ks) * jnp.exp2(diff), axis=1, keepdims=True)
                parts[p] = parts[p] + col * vs
        o_ref[pl.ds(r0, half), sl] = parts[0]
        o_ref[pl.ds(r0 + half, half), sl] = parts[1]
        upd = lax.dot_general(vh.astype(jnp.bfloat16), kt.astype(jnp.bfloat16),
                              (((0,), (0,)), ((), ())), preferred_element_type=jnp.float32)
        st_ref[hh] = jnp.exp2(tot) * st + upd


def _hgrn_kernel(qf_ref, vf_ref, lff_ref, qb_ref, vb_ref, lfb_ref, of_ref, ob_ref, st_ref, *, tb):
    t = pl.program_id(1)
    nsub = tb // HGRN_SUB

    @pl.when(t == 0)
    def _():
        st_ref[...] = jnp.zeros_like(st_ref)

    rr = lax.broadcasted_iota(jnp.int32, (HGRN_SUB, HGRN_SUB), 0)
    cc = lax.broadcasted_iota(jnp.int32, (HGRN_SUB, HGRN_SUB), 1)
    tri_f = (cc <= rr).astype(jnp.bfloat16)
    tri_b = (cc >= rr).astype(jnp.bfloat16)

    def body(i, carry):
        rf = pl.multiple_of(i * HGRN_SUB, HGRN_SUB)
        rb = pl.multiple_of((nsub - 1 - i) * HGRN_SUB, HGRN_SUB)
        _hgrn_sub_chunk(qf_ref.at[0], vf_ref.at[0], lff_ref.at[0, 0], of_ref.at[0], st_ref.at[0], rf, tri_f, False)
        _hgrn_sub_chunk(qb_ref.at[0], vb_ref.at[0], lfb_ref.at[0, 0], ob_ref.at[0], st_ref.at[1], rb, tri_b, True)
        return carry

    lax.fori_loop(0, nsub, body, 0, unroll=HGRN_UNROLL)


def _hgrn_scan(q, v, lf, *, n_ctx_blocks):
    b, t, w = q.shape
    tb = HGRN_TB
    nblk = t // tb
    nlat = nblk - n_ctx_blocks

    def fwd_block(ti):
        return jnp.where(ti < n_ctx_blocks, nlat + ti, ti - n_ctx_blocks)

    def bwd_block(ti):
        return nblk - 1 - ti

    rows = lambda blk: pl.BlockSpec((1, tb, w), lambda bi, ti: (bi, blk(ti), 0))
    gate = lambda di, blk: pl.BlockSpec((1, 1, tb, w), lambda bi, ti: (di, bi, blk(ti), 0))
    out = jax.ShapeDtypeStruct((b, t, w), jnp.float32)
    return pl.pallas_call(
        functools.partial(_hgrn_kernel, tb=tb),
        grid=(b, nblk),
        in_specs=[rows(fwd_block), rows(fwd_block), gate(0, fwd_block),
                  rows(bwd_block), rows(bwd_block), gate(1, bwd_block)],
        out_specs=[rows(fwd_block), rows(bwd_block)],
        out_shape=[out, out],
        scratch_shapes=[pltpu.VMEM((2, C_H, C_DV, C_DK), jnp.float32)],
        compiler_params=_cparams(("arbitrary", "arbitrary")),
        name="hgrn2_scan",
    )(q, v, lf, q, v, lf)


def _head_rms(x, w, n_heads, width):
    out = []
    for hh in range(n_heads):
        blk = x[:, hh * width:(hh + 1) * width]
        out.append(_rms_rows(blk, w))
    return jnp.concatenate(out, axis=1)


def _merge_kernel(oa_ref, ob_ref, ocf_ref, ocb_ref, cg_ref, gt_ref, x_ref, ng_ref, ma_ref, mb_ref, dw_ref, hw_ref,
                  wbr_ref, wo_ref, xo_ref, ho_ref, hb_ref, *, diff_out_scale):
    ob = (_head_rms(ob_ref[...], dw_ref[...], B_H, B_VD) * diff_out_scale).astype(jnp.bfloat16)
    cg = cg_ref[...].astype(jnp.float32)
    oc = (_head_rms(ocf_ref[...] + ocb_ref[...], hw_ref[...], C_H, C_DV) * (cg * jax.nn.sigmoid(cg)))
    acc = None
    for i, o in enumerate((oa_ref[...], ob, oc.astype(jnp.bfloat16))):
        z = jnp.dot(o, wbr_ref[i], preferred_element_type=jnp.float32)
        g = jax.nn.sigmoid(gt_ref[:, i * D_MODEL:(i + 1) * D_MODEL].astype(jnp.float32))
        acc = g * z if acc is None else acc + g * z
    y = jnp.dot(acc.astype(jnp.bfloat16), wo_ref[...], preferred_element_type=jnp.float32)
    ma, mb = ma_ref[0], mb_ref[0]
    for half, (xn, mod) in enumerate(zip(_residual_tile(x_ref, y, ng_ref[1], ma, mb, 2), (ma, mb))):
        rows = slice(half * HALF_TM, (half + 1) * HALF_TM)
        xo_ref[rows, :] = xn
        h_next = _rms_rows(xn, ng_ref[2]) * (1.0 + mod[4]) + mod[3]
        ho_ref[rows, :] = h_next
        hb_ref[0, rows, :] = h_next.astype(hb_ref.dtype)
    hb_ref[1] = jnp.zeros((MM_TM, D_MODEL), hb_ref.dtype)


def _merge(oa, ob, ocf, ocb, cg, gates, x, norm_g, mods, diff_w, hgrn_w, w_br, w_o, *, t_rows, seq,
           diff_out_scale):
    m = x.shape[0]
    tm = MM_TM
    row = lambda i: (i, 0)
    f32 = jnp.float32
    return pl.pallas_call(
        functools.partial(_merge_kernel, diff_out_scale=diff_out_scale),
        grid=(m // tm,),
        in_specs=[pl.BlockSpec((tm, BRANCH_W), row), pl.BlockSpec((tm, BRANCH_W), row),
                  pl.BlockSpec((tm, BRANCH_W), row), pl.BlockSpec((tm, BRANCH_W), row),
                  pl.BlockSpec((tm, BRANCH_W), row),
                  pl.BlockSpec((tm, GATE_W), row), pl.BlockSpec((tm, D_MODEL), row),
                  pl.BlockSpec((4, 1, D_MODEL), lambda i: (0, 0, 0))] + _mod_specs(t_rows, seq) + [
                  pl.BlockSpec((1, B_VD), lambda i: (0, 0)), pl.BlockSpec((1, C_DV), lambda i: (0, 0)),
                  pl.BlockSpec((N_BRANCH, BRANCH_W, D_MODEL), lambda i: (0, 0, 0)),
                  pl.BlockSpec((D_MODEL, D_MODEL), lambda i: (0, 0))],
        out_specs=[pl.BlockSpec((tm, D_MODEL), row), pl.BlockSpec((tm, D_MODEL), row),
                   pl.BlockSpec((2, tm, D_MODEL), lambda i: (0, i, 0))],
        out_shape=[jax.ShapeDtypeStruct((m, D_MODEL), f32), jax.ShapeDtypeStruct((m, D_MODEL), f32),
                   jax.ShapeDtypeStruct((2, m, D_MODEL), jnp.bfloat16)],
        compiler_params=_cparams(("arbitrary",)),
        name="branch_merge",
    )(oa, ob, ocf, ocb, cg, gates, x, norm_g, mods, mods, diff_w, hgrn_w, w_br, w_o)


def _moe_kernel(blk_e_ref, n_used_ref, x_ref, wgu_ref, bgu_ref, wdn_ref, bdn_ref, y_ref, wgu_sc, wdn_sc):
    i = pl.program_id(0)
    n_used = n_used_ref[0]
    e = blk_e_ref[i]
    e_prev = blk_e_ref[jnp.maximum(i - 1, 0)]

    @pl.when((i == 0) | (e != e_prev))
    def _():
        wgu_sc[...] = wgu_ref[0, 0].astype(jnp.bfloat16)
        wdn_sc[...] = wdn_ref[0, 0].astype(jnp.bfloat16)

    @pl.when(i < n_used)
    def _():
        gu = jnp.dot(x_ref[...], wgu_sc[...], preferred_element_type=jnp.float32) + bgu_ref[0, 0]
        g = jnp.minimum(gu[:, :EXPERT_FF], SWIGLU_LIMIT)
        u = jnp.clip(gu[:, EXPERT_FF:], -SWIGLU_LIMIT, SWIGLU_LIMIT)
        act = (u + 1.0) * (g * jax.nn.sigmoid(SWIGLU_ALPHA * g))
        y = jnp.dot(act.astype(jnp.bfloat16), wdn_sc[...], preferred_element_type=jnp.float32) + bdn_ref[0, 0]
        y_ref[...] = y.astype(y_ref.dtype)

    @pl.when(i >= n_used)
    def _():
        y_ref[...] = jnp.zeros_like(y_ref)


def _moe_experts(blk_e, n_used, xs, w_gu, b_gu, w_dn, b_dn, *, layer):
    r, dm = xs.shape
    tm = MOE_TM
    n_blk = blk_e.shape[0]
    return pl.pallas_call(
        _moe_kernel,
        grid_spec=pltpu.PrefetchScalarGridSpec(
            num_scalar_prefetch=2, grid=(n_blk,),
            in_specs=[pl.BlockSpec((tm, dm), lambda i, be, nu: (i, 0)),
                      pl.BlockSpec((1, 1, dm, 2 * EXPERT_FF), lambda i, be, nu: (layer, be[i], 0, 0)),
                      pl.BlockSpec((1, 1, 1, 2 * EXPERT_FF), lambda i, be, nu: (layer, be[i], 0, 0)),
                      pl.BlockSpec((1, 1, EXPERT_FF, dm), lambda i, be, nu: (layer, be[i], 0, 0)),
                      pl.BlockSpec((1, 1, 1, dm), lambda i, be, nu: (layer, be[i], 0, 0))],
            out_specs=pl.BlockSpec((tm, dm), lambda i, be, nu: (i, 0)),
            scratch_shapes=[pltpu.VMEM((dm, 2 * EXPERT_FF), jnp.bfloat16),
                            pltpu.VMEM((EXPERT_FF, dm), jnp.bfloat16)]),
        out_shape=jax.ShapeDtypeStruct((r, dm), jnp.bfloat16),
        compiler_params=_cparams(("arbitrary",)),
        name="moe_experts",
    )(blk_e, n_used, xs, w_gu, b_gu.reshape(DEPTH, N_EXPERTS, 1, -1), w_dn,
      b_dn.reshape(DEPTH, N_EXPERTS, 1, -1))


def _combine_kernel(*refs):
    yk_refs = refs[:TOP_K]
    gt_ref, x_ref, ng_ref, ma_ref, mb_ref, xo_ref = refs[TOP_K:]
    gt = gt_ref[...]
    f = None
    for k in range(TOP_K):
        term = gt[:, k:k + 1] * yk_refs[k][...].astype(jnp.float32)
        f = term if f is None else f + term
    for half, xn in enumerate(_residual_tile(x_ref, f, ng_ref[3], ma_ref[0], mb_ref[0], 5)):
        xo_ref[half * HALF_TM:(half + 1) * HALF_TM, :] = xn


def _moe_combine(yk, gates, x, norm_g, mods, *, t_rows, seq):
    m = x.shape[0]
    tm = MM_TM
    n_tiles = m // tm
    row = lambda i: (i, 0)
    choice = lambda k: pl.BlockSpec((tm, D_MODEL), lambda i: (k * n_tiles + i, 0))
    return pl.pallas_call(
        _combine_kernel,
        grid=(n_tiles,),
        in_specs=[choice(k) for k in range(TOP_K)] + [
            pl.BlockSpec((tm, TOP_K), row), pl.BlockSpec((tm, D_MODEL), row),
            pl.BlockSpec((4, 1, D_MODEL), lambda i: (0, 0, 0))] + _mod_specs(t_rows, seq),
        out_specs=pl.BlockSpec((tm, D_MODEL), row),
        out_shape=jax.ShapeDtypeStruct((m, D_MODEL), jnp.float32),
        compiler_params=_cparams(("arbitrary",)),
        name="moe_combine",
    )(*([yk] * TOP_K), gates, x, norm_g, mods, mods)


def _moe_ffn(h, h_rows, x, norm_g, mods, w_router, b_router, w_gu, b_gu, w_dn, b_dn, *, layer, t_rows, seq):
    n, dm = h.shape
    nk = n * TOP_K
    w_r = jnp.zeros((dm, LANES), jnp.float32).at[:, :N_EXPERTS].set(w_router)
    logits = _matmul(h, w_r, tm=MM_TM, tn=LANES, out_dtype=jnp.float32,
                     precision=lax.Precision.HIGHEST, name="router")[:, :N_EXPERTS] + b_router
    top_v, top_e = lax.top_k(logits, TOP_K)
    gates = jax.nn.softmax(top_v, axis=-1)
    flat_e = top_e.reshape(-1).astype(jnp.int32)
    experts = jnp.arange(N_EXPERTS, dtype=jnp.int32)
    onehot = (flat_e[:, None] == experts[None, :]).astype(jnp.int32)
    incl = jnp.cumsum(onehot, axis=0)
    rank = jnp.sum(incl * onehot, axis=1) - 1
    counts = incl[-1]
    padded = (counts + MOE_TM - 1) // MOE_TM * MOE_TM
    pad_end = jnp.cumsum(padded)
    pad_start = pad_end - padded
    grp_start = jnp.cumsum(counts) - counts
    dest = jnp.sum(onehot * pad_start[None, :], axis=1) + rank
    n_blk = (nk + MOE_TM - 1) // MOE_TM + N_EXPERTS
    rows = n_blk * MOE_TM
    blk_start = jnp.arange(n_blk, dtype=jnp.int32) * MOE_TM
    blk_e = jnp.minimum(jnp.sum((pad_end[None, :] <= blk_start[:, None]).astype(jnp.int32), axis=1),
                        N_EXPERTS - 1)
    n_used = (pad_end[-1] // MOE_TM).astype(jnp.int32).reshape(1)
    order = jnp.argsort(flat_e)
    row_e = jnp.repeat(blk_e, MOE_TM)
    sorted_pos = jnp.arange(rows, dtype=jnp.int32) - pad_start[row_e] + grp_start[row_e]
    row_tok = (order[jnp.clip(sorted_pos, 0, nk - 1)] // TOP_K).astype(jnp.int32)
    row_tok = jnp.where(sorted_pos < nk, row_tok, jnp.arange(rows, dtype=jnp.int32) % n)
    xs = h_rows[row_tok]
    y = _moe_experts(blk_e, n_used, xs, w_gu, b_gu, w_dn, b_dn, layer=layer)
    yk = y[dest.reshape(n, TOP_K).T.reshape(-1)]
    return _moe_combine(yk, gates, x, norm_g, mods, t_rows=t_rows, seq=seq)


def _rope_tables(rows, n_ctx, batch):
    row = jnp.repeat(jnp.arange(rows, dtype=jnp.float32), GRID_W)
    col = jnp.tile(jnp.arange(GRID_W, dtype=jnp.float32), rows)
    n_freq = HEAD_DIM // 4
    inv = ROPE_THETA ** (-jnp.arange(n_freq, dtype=jnp.float32) / n_freq)
    ang_r = row[:, None] * inv
    ang_c = col[:, None] * inv
    cos_h = jnp.concatenate([jnp.cos(ang_r), jnp.cos(ang_r), jnp.cos(ang_c), jnp.cos(ang_c)], axis=1)
    sin_h = jnp.concatenate([-jnp.sin(ang_r), jnp.sin(ang_r), -jnp.sin(ang_c), jnp.sin(ang_c)], axis=1)
    cos_t = jnp.concatenate([cos_h, jnp.ones((n_ctx, HEAD_DIM), jnp.float32)], axis=0)
    sin_t = jnp.concatenate([sin_h, jnp.zeros((n_ctx, HEAD_DIM), jnp.float32)], axis=0)
    reps = (batch, LANES // HEAD_DIM)
    return jnp.tile(cos_t, reps), jnp.tile(sin_t, reps)


def _rope_partner(n_cols):
    idx = jnp.arange(n_cols)
    quarter = HEAD_DIM // 4
    return jnp.where((idx % (2 * quarter)) < quarter, idx + quarter, idx - quarter)


def kernel(x, c, ctx, c_ctx, w_mod, b_mod, norm_g, w_in, attn_sink, diff_lambda, diff_norm_w,
           hgrn_lb_logits, hgrn_norm_w, w_branch, w_out, w_router, b_router, w_gate_up, b_gate_up,
           w_down, b_down):
    b, s, dm = x.shape
    l = ctx.shape[1]
    t = s + l
    f32 = jnp.float32
    bf16 = jnp.bfloat16
    cos_t, sin_t = _rope_tables(s // GRID_W, l, b)
    lb_cum = jnp.cumsum(jax.nn.softmax(hgrn_lb_logits.astype(f32), axis=0), axis=0)
    lower_bounds = lb_cum - lb_cum[0:1]
    cond_rows = 16
    cond = jnp.zeros((cond_rows, dm), f32).at[:b].set(jax.nn.silu(c)).at[b].set(jax.nn.silu(c_ctx))
    scale = HEAD_DIM ** -0.5
    xa = jnp.concatenate([x, ctx], axis=1).reshape(b * t, dm)
    h_moe = None
    for layer in range(DEPTH):
        lam_init = 0.8 - 0.6 * math.exp(-0.3 * layer)
        mod_all = _matmul(cond, w_mod[layer], tm=cond_rows, tn=3 * dm, out_dtype=f32,
                          precision=lax.Precision.HIGHEST, name="adaln") + b_mod[layer]
        mod = mod_all[:b].reshape(b, 1, 6, 1, dm)
        mod_c = jnp.broadcast_to(mod_all[b:b + 1].reshape(1, 1, 6, 1, dm), (b, 1, 6, 1, dm))
        mods = jnp.concatenate([mod, mod_c], axis=1).reshape(2 * b, 6, 1, dm)
        ng = norm_g[layer].reshape(4, 1, dm)
        g0 = ng[0:1]
        geo = dict(t_rows=t, seq=s)

        w = w_in[layer]
        w_aq = (w[:, OFF_A:OFF_A + A_Q_W] * scale).reshape(dm, A_HKV, A_GROUP, HEAD_DIM)
        pad = jnp.zeros((dm, A_GROUP, HEAD_DIM), f32)
        w_aq8 = jnp.concatenate([jnp.concatenate([w_aq[:, 0], pad], axis=-1),
                                 jnp.concatenate([pad, w_aq[:, 1]], axis=-1)], axis=1).reshape(dm, A_Q8_W)
        w_a = jnp.concatenate([w_aq8, w[:, OFF_A + A_Q_W:OFF_A + A_W]], axis=1).astype(bf16)
        w_b = w[:, OFF_B:OFF_B + B_W].at[:, :B_QK_W].multiply(scale * LOG2_E).astype(bf16)
        proj_a = _proj_rope(xa, g0, mods, w_a, w_a[:, _rope_partner(A_ROPE_W2)], cos_t, sin_t,
                            rope_w=A_ROPE_W2, name="proj_a", **geo).reshape(b, t, A_W2)
        bq, bk_hm, bv1_hm = _proj_b(xa, g0, mods, w_b, w_b[:, _rope_partner(B_ROPE_W)], cos_t, sin_t, **geo)
        bq = bq.reshape(b, t, B_QK_W)
        lb = lower_bounds[layer]
        lb_par = jnp.stack([jnp.log(lb[0]), jnp.log1p(-lb[0]), jnp.log(lb[1]), jnp.log1p(-lb[1])])
        cq, ci, lf, cg = _proj_hgrn(xa, g0, mods, w[:, OFF_C:OFF_C + C_W].astype(bf16),
                                    lb_par.reshape(4, 1, C_K_W), **geo)
        gates = _proj_plain(xa, g0, mods, w[:, OFF_GATE:].astype(bf16), name="gate_proj", **geo)

        sink32 = attn_sink[layer].astype(f32)
        o_a = _window_attention(proj_a, sink32, seq=s)

        lp = diff_lambda[layer].astype(f32)
        lam = (jnp.exp(jnp.sum(lp[0] * lp[1])) - jnp.exp(jnp.sum(lp[2] * lp[3])) + lam_init).reshape(1).astype(f32)
        o_b = _diff_attention(lam, bq, bk_hm, bv1_hm, tq=FLASH_TQ, tk=FLASH_TK, q_block_offset=0,
                              n_q_blocks=s // FLASH_TQ, k_block_offset=0, nk=t // FLASH_TK)
        o_bx = _diff_attention(lam, bq, bk_hm, bv1_hm, tq=l, tk=l, q_block_offset=s // l, n_q_blocks=1,
                               k_block_offset=s // l, nk=1)
        o_b = jnp.concatenate([o_b, o_bx], axis=1)

        o_cf, o_cb = _hgrn_scan(cq.reshape(b, t, C_K_W), ci.reshape(b, t, C_V_W), lf.reshape(2, b, t, C_K_W),
                                n_ctx_blocks=l // HGRN_TB)

        col = jnp.arange(A_Q_W)
        head_of_col = ((col % LANES) // HEAD_DIM) * A_GROUP + col // LANES
        w_br = w_branch[layer].at[0].set(w_branch[layer][0][head_of_col * HEAD_DIM + col % HEAD_DIM])
        xa, h_moe, h_rows = _merge(o_a.reshape(b * t, A_Q_W), o_b.reshape(b * t, B_V_W), o_cf.reshape(b * t, C_V_W),
                           o_cb.reshape(b * t, C_V_W), cg,
                           gates, xa, ng, mods, diff_norm_w[layer].reshape(1, B_VD).astype(f32),
                           hgrn_norm_w[layer].reshape(1, C_DV).astype(f32),
                           w_br.astype(bf16), w_out[layer].astype(bf16),
                           diff_out_scale=1 - lam_init, **geo)

        xa = _moe_ffn(h_moe, h_rows.reshape(2 * b * t, dm), xa, ng, mods, w_router[layer], b_router[layer],
                      w_gate_up, b_gate_up, w_down, b_down, layer=layer, **geo)
    return xa.reshape(b, t, dm)[:, :s]
```

```python
import functools
import math

import jax
import jax.numpy as jnp
from jax import lax
from jax.experimental import pallas as pl
from jax.experimental.pallas import tpu as pltpu

D_MODEL = 1024
DEPTH = 2
GRID_W = 64
HEAD_DIM = 64
ROPE_THETA = 10000.0
RMS_EPS = 1e-6
NEG_INF = -1e30
BLOCK = 128
A_HQ = 8
A_HKV = 2
A_GROUP = A_HQ // A_HKV
WINDOW = 128
B_H = 4
B_HD = HEAD_DIM
B_VD = 2 * HEAD_DIM
C_H = 4
C_DK = 128
C_DV = 128
N_BRANCH = 3
BRANCH_W = 512
A_Q_W = A_HQ * HEAD_DIM
A_KV_W = A_HKV * HEAD_DIM
B_QK_W = B_H * 2 * B_HD
B_V_W = B_H * B_VD
C_K_W = C_H * C_DK
C_V_W = C_H * C_DV
GATE_W = N_BRANCH * D_MODEL
N_EXPERTS = 32
TOP_K = 4
EXPERT_FF = D_MODEL
SWIGLU_LIMIT = 7.0
SWIGLU_ALPHA = 1.702

A_W = A_Q_W + 2 * A_KV_W
A_ROPE_W = A_Q_W + A_KV_W
B_W = 2 * B_QK_W + B_V_W
B_ROPE_W = 2 * B_QK_W
C_W = 3 * C_K_W + 2 * C_V_W
OFF_A = 0
OFF_B = OFF_A + A_W
OFF_C = OFF_B + B_W
OFF_GATE = OFF_C + C_W

V7X_VMEM_LIMIT_BYTES = 56 * 1024 * 1024
LANES = 128
LOG2_E = math.log2(math.e)

MM_TM = 512
HALF_TM = MM_TM // 2
FLASH_TQ = 2048
FLASH_TK = 768
HGRN_TB = 256
HGRN_SUB = 16
HGRN_UNROLL = 4
MOE_TM = 512


def _cparams(sem):
    return pltpu.CompilerParams(dimension_semantics=sem, vmem_limit_bytes=V7X_VMEM_LIMIT_BYTES)


def _rms_rows(x, g):
    return (x * lax.rsqrt(jnp.mean(x * x, axis=-1, keepdims=True) + RMS_EPS)) * g


def _modulated_tile(x_ref, g, mod_a, mod_b, shift_slot, scale_slot):
    out = []
    for half, mod in enumerate((mod_a, mod_b)):
        x = x_ref[half * HALF_TM:(half + 1) * HALF_TM, :]
        out.append(_rms_rows(x, g) * (1.0 + mod[scale_slot]) + mod[shift_slot])
    return jnp.concatenate(out, axis=0)


def _residual_tile(x_ref, y, g, mod_a, mod_b, gate_slot):
    out = []
    for half, mod in enumerate((mod_a, mod_b)):
        rows = slice(half * HALF_TM, (half + 1) * HALF_TM)
        out.append(x_ref[rows, :] + mod[gate_slot] * _rms_rows(y[rows, :], g))
    return out


def _mod_specs(t_rows, seq):
    halves_per_batch = t_rows // HALF_TM
    lat_halves = seq // HALF_TM

    def row(u):
        return (u // halves_per_batch) * 2 + jnp.where(u % halves_per_batch >= lat_halves, 1, 0)

    blk = (1, 6, 1, D_MODEL)
    return [pl.BlockSpec(blk, lambda i: (row(2 * i), 0, 0, 0)),
            pl.BlockSpec(blk, lambda i: (row(2 * i + 1), 0, 0, 0))]


def _proj_rope_kernel(x_ref, g_ref, ma_ref, mb_ref, w_ref, wsw_ref, cos_ref, sin_ref, o_ref, *, rope_w):
    h = _modulated_tile(x_ref, g_ref[0], ma_ref[0], mb_ref[0], 0, 1).astype(jnp.bfloat16)
    z = jnp.dot(h, w_ref[...], preferred_element_type=jnp.float32)
    zs = jnp.dot(h, wsw_ref[...], preferred_element_type=jnp.float32)
    reps = rope_w // LANES
    rot = z[:, :rope_w] * jnp.tile(cos_ref[...], (1, reps)) + zs * jnp.tile(sin_ref[...], (1, reps))
    o_ref[:, :rope_w] = rot.astype(o_ref.dtype)
    o_ref[:, rope_w:] = z[:, rope_w:].astype(o_ref.dtype)


def _proj_b_kernel(x_ref, g_ref, ma_ref, mb_ref, w_ref, wsw_ref, cos_ref, sin_ref, q_ref, k_ref, v1_ref):
    h = _modulated_tile(x_ref, g_ref[0], ma_ref[0], mb_ref[0], 0, 1).astype(jnp.bfloat16)
    z = jnp.dot(h, w_ref[...], preferred_element_type=jnp.float32)
    zs = jnp.dot(h, wsw_ref[...], preferred_element_type=jnp.float32)
    reps = B_ROPE_W // LANES
    rot = z[:, :B_ROPE_W] * jnp.tile(cos_ref[...], (1, reps)) + zs * jnp.tile(sin_ref[...], (1, reps))
    q_ref[...] = rot[:, :B_QK_W].astype(q_ref.dtype)
    for hh in range(B_H):
        k_ref[hh] = rot[:, B_QK_W + hh * 2 * B_HD:B_QK_W + (hh + 1) * 2 * B_HD].astype(k_ref.dtype)
        v1_ref[hh, :, :B_VD] = z[:, B_ROPE_W + hh * B_VD:B_ROPE_W + (hh + 1) * B_VD].astype(v1_ref.dtype)
        v1_ref[hh, :, B_VD:] = jnp.ones((MM_TM, B_VD), v1_ref.dtype)


def _proj_plain_kernel(x_ref, g_ref, ma_ref, mb_ref, w_ref, o_ref):
    h = _modulated_tile(x_ref, g_ref[0], ma_ref[0], mb_ref[0], 0, 1).astype(jnp.bfloat16)
    o_ref[...] = jnp.dot(h, w_ref[...], preferred_element_type=jnp.float32).astype(o_ref.dtype)


def _log_forget(z, log_lb, log_1m_lb):
    log_sig = jnp.minimum(z, 0.0) - jnp.log(1.0 + jnp.exp(-jnp.abs(z)))
    c = log_1m_lb + log_sig
    return jnp.maximum(log_lb, c) + jnp.log(1.0 + jnp.exp(-jnp.abs(log_lb - c)))


def _proj_hgrn_kernel(x_ref, g_ref, ma_ref, mb_ref, w_ref, lb_ref, q_ref, v_ref, lf_ref, cg_ref):
    h = _modulated_tile(x_ref, g_ref[0], ma_ref[0], mb_ref[0], 0, 1).astype(jnp.bfloat16)
    z = jnp.dot(h, w_ref[...], preferred_element_type=jnp.float32)
    zq = z[:, :C_K_W]
    q_ref[...] = zq * jax.nn.sigmoid(zq)
    v_ref[...] = z[:, C_K_W:C_K_W + C_V_W]
    off = C_K_W + C_V_W
    lf_ref[0] = _log_forget(z[:, off:off + C_K_W], lb_ref[0], lb_ref[1])
    lf_ref[1] = _log_forget(z[:, off + C_K_W:off + 2 * C_K_W], lb_ref[2], lb_ref[3])
    cg_ref[...] = z[:, off + 2 * C_K_W:].astype(cg_ref.dtype)


def _proj_common_specs(t_rows, seq):
    row = lambda i: (i, 0)
    return [pl.BlockSpec((MM_TM, D_MODEL), row),
            pl.BlockSpec((1, 1, D_MODEL), lambda i: (0, 0, 0))] + _mod_specs(t_rows, seq)


def _proj_rope(x, g, mods, w, wsw, cos, sin, *, t_rows, seq, rope_w, name):
    m = x.shape[0]
    n = w.shape[1]
    row = lambda i: (i, 0)
    whole = lambda i: (0, 0)
    return pl.pallas_call(
        functools.partial(_proj_rope_kernel, rope_w=rope_w),
        grid=(m // MM_TM,),
        in_specs=_proj_common_specs(t_rows, seq) + [
            pl.BlockSpec((D_MODEL, n), whole), pl.BlockSpec((D_MODEL, rope_w), whole),
            pl.BlockSpec((MM_TM, LANES), row), pl.BlockSpec((MM_TM, LANES), row)],
        out_specs=pl.BlockSpec((MM_TM, n), row),
        out_shape=jax.ShapeDtypeStruct((m, n), jnp.bfloat16),
        compiler_params=_cparams(("arbitrary",)),
        name=name,
    )(x, g, mods, mods, w, wsw, cos, sin)


def _proj_b(x, g, mods, w, wsw, cos, sin, *, t_rows, seq):
    m = x.shape[0]
    row = lambda i: (i, 0)
    whole = lambda i: (0, 0)
    heads = lambda i: (0, i, 0)
    bf16 = jnp.bfloat16
    return pl.pallas_call(
        _proj_b_kernel,
        grid=(m // MM_TM,),
        in_specs=_proj_common_specs(t_rows, seq) + [
            pl.BlockSpec((D_MODEL, B_W), whole), pl.BlockSpec((D_MODEL, B_ROPE_W), whole),
            pl.BlockSpec((MM_TM, LANES), row), pl.BlockSpec((MM_TM, LANES), row)],
        out_specs=[pl.BlockSpec((MM_TM, B_QK_W), row), pl.BlockSpec((B_H, MM_TM, 2 * B_HD), heads),
                   pl.BlockSpec((B_H, MM_TM, 2 * B_VD), heads)],
        out_shape=[jax.ShapeDtypeStruct((m, B_QK_W), bf16), jax.ShapeDtypeStruct((B_H, m, 2 * B_HD), bf16),
                   jax.ShapeDtypeStruct((B_H, m, 2 * B_VD), bf16)],
        compiler_params=_cparams(("arbitrary",)),
        name="proj_b",
    )(x, g, mods, mods, w, wsw, cos, sin)


def _proj_plain(x, g, mods, w, *, t_rows, seq, name):
    m = x.shape[0]
    n = w.shape[1]
    row = lambda i: (i, 0)
    return pl.pallas_call(
        _proj_plain_kernel,
        grid=(m // MM_TM,),
        in_specs=_proj_common_specs(t_rows, seq) + [pl.BlockSpec((D_MODEL, n), lambda i: (0, 0))],
        out_specs=pl.BlockSpec((MM_TM, n), row),
        out_shape=jax.ShapeDtypeStruct((m, n), jnp.bfloat16),
        compiler_params=_cparams(("arbitrary",)),
        name=name,
    )(x, g, mods, mods, w)


def _proj_hgrn(x, g, mods, w, lb_par, *, t_rows, seq):
    m = x.shape[0]
    row = lambda i: (i, 0)
    f32 = jnp.float32
    return pl.pallas_call(
        _proj_hgrn_kernel,
        grid=(m // MM_TM,),
        in_specs=_proj_common_specs(t_rows, seq) + [
            pl.BlockSpec((D_MODEL, C_W), lambda i: (0, 0)),
            pl.BlockSpec((4, 1, C_K_W), lambda i: (0, 0, 0))],
        out_specs=[pl.BlockSpec((MM_TM, C_K_W), row), pl.BlockSpec((MM_TM, C_V_W), row),
                   pl.BlockSpec((2, MM_TM, C_K_W), lambda i: (0, i, 0)), pl.BlockSpec((MM_TM, C_V_W), row)],
        out_shape=[jax.ShapeDtypeStruct((m, C_K_W), f32), jax.ShapeDtypeStruct((m, C_V_W), f32),
                   jax.ShapeDtypeStruct((2, m, C_K_W), f32), jax.ShapeDtypeStruct((m, C_V_W), jnp.bfloat16)],
        compiler_params=_cparams(("arbitrary",)),
        name="hgrn_proj",
    )(x, g, mods, mods, w, lb_par)


def _mm_kernel(x_ref, w_ref, o_ref, *, precision):
    o_ref[...] = jnp.dot(x_ref[...], w_ref[...], preferred_element_type=jnp.float32,
                         precision=precision).astype(o_ref.dtype)


def _matmul(x, w, *, tm, tn, out_dtype, precision=None, name):
    m, k = x.shape
    n = w.shape[1]
    assert m % tm == 0 and n % tn == 0, (m, tm, n, tn)
    return pl.pallas_call(
        functools.partial(_mm_kernel, precision=precision),
        grid=(n // tn, m // tm),
        in_specs=[pl.BlockSpec((tm, k), lambda j, i: (i, 0)),
                  pl.BlockSpec((k, tn), lambda j, i: (0, j))],
        out_specs=pl.BlockSpec((tm, tn), lambda j, i: (i, j)),
        out_shape=jax.ShapeDtypeStruct((m, n), out_dtype),
        compiler_params=_cparams(("arbitrary", "arbitrary")),
        name=name,
    )(x, w)


A_Q8_W = A_HQ * LANES
A_W2 = A_Q8_W + 2 * A_KV_W
A_ROPE_W2 = A_Q8_W + A_KV_W
A_K_BLK = A_Q8_W // A_KV_W
A_V_BLK = A_K_BLK + 1


def _attn_a_kernel(sink_ref, q_ref, kp_ref, kc_ref, kn_ref, vp_ref, vc_ref, vn_ref, kx_ref, vx_ref,
                   o_ref, *, seq):
    n = pl.program_id(1)
    k_all = jnp.concatenate([kp_ref[0], kc_ref[0], kn_ref[0], kx_ref[0]], axis=0)
    v_all = jnp.concatenate([vp_ref[0], vc_ref[0], vn_ref[0], vx_ref[0]], axis=0)
    n_keys = k_all.shape[0]
    qi = lax.broadcasted_iota(jnp.int32, (BLOCK, n_keys), 0)
    kj = lax.broadcasted_iota(jnp.int32, (BLOCK, n_keys), 1)
    kpos = n * BLOCK + kj - BLOCK
    local_ok = (jnp.abs(kj - BLOCK - qi) <= WINDOW) & (kpos >= 0) & (kpos < seq) & (n * BLOCK < seq)
    mask = local_ok | (kj >= 3 * BLOCK)
    q8 = jnp.concatenate([q_ref[0, :, hq * LANES:(hq + 1) * LANES] for hq in range(A_HQ)], axis=0)
    s = lax.dot_general(q8, k_all, (((1,), (1,)), ((), ())), preferred_element_type=jnp.float32)
    p_blocks, inv_l = [], []
    for hq in range(A_HQ):
        sh = jnp.where(mask, s[hq * BLOCK:(hq + 1) * BLOCK, :], NEG_INF)
        sk = sink_ref[hq]
        m = jnp.maximum(jnp.max(sh, axis=-1, keepdims=True), sk)
        p = jnp.exp(sh - m)
        inv_l.append(1.0 / (jnp.sum(p, axis=-1, keepdims=True) + jnp.exp(sk - m)))
        p_blocks.append(p.astype(v_all.dtype))
    o = jnp.dot(jnp.concatenate(p_blocks, axis=0), v_all, preferred_element_type=jnp.float32)
    lane = lax.broadcasted_iota(jnp.int32, (BLOCK, LANES), 1)
    for g in range(A_GROUP):
        lo = o[g * BLOCK:(g + 1) * BLOCK, :] * inv_l[g]
        hi = o[(A_GROUP + g) * BLOCK:(A_GROUP + g + 1) * BLOCK, :] * inv_l[A_GROUP + g]
        o_ref[0, :, g * LANES:(g + 1) * LANES] = jnp.where(lane < HEAD_DIM, lo, hi).astype(o_ref.dtype)


def _window_attention(proj_a, sink, *, seq):
    b, t, _ = proj_a.shape
    l = t - seq
    nb = seq // BLOCK
    ctx_blk = seq // l
    q_map = lambda bi, n, *_: (bi, n, 0)
    kv = lambda col, fn: pl.BlockSpec((1, BLOCK, A_KV_W), lambda bi, n, *_: (bi, fn(n), col))
    prev = lambda n: jnp.clip(n - 1, 0, nb - 1)
    cur = lambda n: jnp.minimum(n, nb - 1)
    nxt = lambda n: jnp.minimum(n + 1, nb - 1)
    ctx = lambda col: pl.BlockSpec((1, l, A_KV_W), lambda bi, n, *_: (bi, ctx_blk, col))
    return pl.pallas_call(
        functools.partial(_attn_a_kernel, seq=seq),
        grid_spec=pltpu.PrefetchScalarGridSpec(
            num_scalar_prefetch=1, grid=(b, t // BLOCK),
            in_specs=[pl.BlockSpec((1, BLOCK, A_Q8_W), q_map),
                      kv(A_K_BLK, prev), kv(A_K_BLK, cur), kv(A_K_BLK, nxt),
                      kv(A_V_BLK, prev), kv(A_V_BLK, cur), kv(A_V_BLK, nxt),
                      ctx(A_K_BLK), ctx(A_V_BLK)],
            out_specs=pl.BlockSpec((1, BLOCK, A_Q_W), q_map)),
        out_shape=jax.ShapeDtypeStruct((b, t, A_Q_W), jnp.bfloat16),
        compiler_params=_cparams(("arbitrary", "arbitrary")),
        name="window_gqa",
    )(sink, *([proj_a] * 9))


def _diff_flash_kernel(lam_ref, q_ref, k_ref, v_ref, o_ref, q2_sc, sa_sc, sb_sc, m_sc, acc_sc, *, tq, tk, nk):
    u = pl.program_id(2)
    new_q = u % nk == 0

    @pl.when(u == 0)
    def _():
        m_sc[...] = jnp.full_like(m_sc, NEG_INF)
        acc_sc[...] = jnp.zeros_like(acc_sc)
        sb_sc[...] = jnp.full_like(sb_sc, NEG_INF)

    @pl.when(new_q)
    def _():
        q = q_ref[0]
        lane = lax.broadcasted_iota(jnp.int32, q.shape, 1)
        q2_sc[:tq, :] = jnp.where(lane < B_HD, q, jnp.zeros_like(q))
        q2_sc[tq:, :] = jnp.where(lane >= B_HD, q, jnp.zeros_like(q))

    def step(s_new_ref, s_old_ref):
        m_prev = m_sc[...]
        m_new = jnp.maximum(m_prev, jnp.max(s_old_ref[...], axis=-1, keepdims=True))
        alpha = jnp.exp2(m_prev - m_new)
        p = jnp.exp2(s_old_ref[...] - jnp.tile(m_new, (1, tk // LANES))).astype(jnp.bfloat16)
        acc_sc[...] = jnp.tile(alpha, (1, 2)) * acc_sc[...] + jnp.dot(
            p, v_ref[0], preferred_element_type=jnp.float32)
        m_sc[...] = m_new
        s_new_ref[...] = lax.dot_general(q2_sc[...], k_ref[0], (((1,), (1,)), ((), ())),
                                         preferred_element_type=jnp.float32)

    @pl.when(u % 2 == 0)
    def _():
        step(sa_sc, sb_sc)

    @pl.when(u % 2 == 1)
    def _():
        step(sb_sc, sa_sc)

    @pl.when(new_q & (u > 0))
    def _():
        acc = acc_sc[...]
        o = acc[:, :B_VD] / acc[:, B_VD:]
        o_ref[0] = (o[:tq, :] - lam_ref[0] * o[tq:, :]).astype(o_ref.dtype)
        m_sc[...] = jnp.full_like(m_sc, NEG_INF)
        acc_sc[...] = jnp.zeros_like(acc_sc)


def _diff_attention(lam, q, k_hm, v1_hm, *, tq, tk, q_block_offset, n_q_blocks, k_block_offset, nk):
    b, t, _ = q.shape
    kb = t // tk
    n_steps = n_q_blocks * nk + 1
    k_blk = lambda bi, u: bi * kb + k_block_offset + u % nk
    in_specs = [pl.BlockSpec((1, tq, 2 * B_HD),
                             lambda bi, h, u, *_: (bi, jnp.minimum(u // nk, n_q_blocks - 1) + q_block_offset, h)),
                pl.BlockSpec((1, tk, 2 * B_HD), lambda bi, h, u, *_: (h, k_blk(bi, u), 0)),
                pl.BlockSpec((1, tk, 2 * B_VD), lambda bi, h, u, *_: (h, k_blk(bi, jnp.maximum(u - 1, 0)), 0))]
    return pl.pallas_call(
        functools.partial(_diff_flash_kernel, tq=tq, tk=tk, nk=nk),
        grid_spec=pltpu.PrefetchScalarGridSpec(
            num_scalar_prefetch=1, grid=(b, B_H, n_steps),
            in_specs=in_specs,
            out_specs=pl.BlockSpec((1, tq, B_VD), lambda bi, h, u, *_: (bi, jnp.maximum(u - 1, 0) // nk, h)),
            scratch_shapes=[pltpu.VMEM((2 * tq, 2 * B_HD), jnp.bfloat16),
                            pltpu.VMEM((2 * tq, tk), jnp.float32),
                            pltpu.VMEM((2 * tq, tk), jnp.float32),
                            pltpu.VMEM((2 * tq, LANES), jnp.float32),
                            pltpu.VMEM((2 * tq, 2 * B_VD), jnp.float32)]),
        out_shape=jax.ShapeDtypeStruct((b, n_q_blocks * tq, B_V_W), jnp.float32),
        compiler_params=_cparams(("arbitrary", "arbitrary", "arbitrary")),
        name="diff_flash",
    )(lam, q, k_hm, v1_hm)


def _hgrn_sub_chunk(q_ref, v_ref, lf_ref, o_ref, st_ref, r0, tri, rev):
    half = HGRN_SUB // 2
    row8 = lax.broadcasted_iota(jnp.int32, (half, C_DK), 0)
    lf = lf_ref[pl.ds(r0, HGRN_SUB), :] * LOG2_E
    lf_a = lf.astype(jnp.bfloat16)
    rem = lf - lf_a.astype(jnp.float32)
    lf_b = rem.astype(jnp.bfloat16)
    lf_c = (rem - lf_b.astype(jnp.float32)).astype(jnp.bfloat16)
    bl = (jnp.dot(tri, lf_a, preferred_element_type=jnp.float32)
          + jnp.dot(tri, lf_b, preferred_element_type=jnp.float32)
          + jnp.dot(tri, lf_c, preferred_element_type=jnp.float32))
    q = q_ref[pl.ds(r0, HGRN_SUB), :]
    v = v_ref[pl.ds(r0, HGRN_SUB), :]
    last = 0 if rev else HGRN_SUB - 1
    for hh in range(C_H):
        sl = slice(hh * C_DK, (hh + 1) * C_DK)
        lfh, blh, qh, vh = lf[:, sl], bl[:, sl], q[:, sl], v[:, sl]
        tot = blh[last:last + 1, :]
        kh = 1.0 - jnp.exp2(lfh)
        qt = qh * jnp.exp2(blh)
        kt = kh * jnp.exp2(tot - blh)
        st = st_ref[hh]
        o_sub = lax.dot_general(qt.astype(jnp.bfloat16), st.astype(jnp.bfloat16),
                                (((1,), (1,)), ((), ())), preferred_element_type=jnp.float32)
        parts = [o_sub[:half, :], o_sub[half:, :]]
        for ss in range(HGRN_SUB):
            ks = kh[ss:ss + 1, :]
            bs = blh[ss:ss + 1, :]
            vs = vh[ss:ss + 1, :]
            for p in range(2):
                lo = p * half
                if rev:
                    reached = lo <= ss
                    full = lo + half - 1 <= ss
                    ok = row8 + lo <= ss
                else:
                    reached = lo + half - 1 >= ss
                    full = lo >= ss
                    ok = row8 + lo >= ss
                if not reached:
                    continue
                diff = blh[lo:lo + half, :] - bs
                if not full:
                    diff = jnp.where(ok, diff, NEG_INF)
                col = jnp.sum((qh[lo:lo + half, :] * ks) * jnp.exp2(diff), axis=1, keepdims=True)
                parts[p] = parts[p] + col * vs
        o_ref[pl.ds(r0, half), sl] = parts[0]
        o_ref[pl.ds(r0 + half, half), sl] = parts[1]
        upd = lax.dot_general(vh.astype(jnp.bfloat16), kt.astype(jnp.bfloat16),
                              (((0,), (0,)), ((), ())), preferred_element_type=jnp.float32)
        st_ref[hh] = jnp.exp2(tot) * st + upd


def _hgrn_kernel(qf_ref, vf_ref, lff_ref, qb_ref, vb_ref, lfb_ref, of_ref, ob_ref, st_ref, *, tb):
    t = pl.program_id(1)
    nsub = tb // HGRN_SUB

    @pl.when(t == 0)
    def _():
        st_ref[...] = jnp.zeros_like(st_ref)

    rr = lax.broadcasted_iota(jnp.int32, (HGRN_SUB, HGRN_SUB), 0)
    cc = lax.broadcasted_iota(jnp.int32, (HGRN_SUB, HGRN_SUB), 1)
    tri_f = (cc <= rr).astype(jnp.bfloat16)
    tri_b = (cc >= rr).astype(jnp.bfloat16)

    def body(i, carry):
        rf = pl.multiple_of(i * HGRN_SUB, HGRN_SUB)
        rb = pl.multiple_of((nsub - 1 - i) * HGRN_SUB, HGRN_SUB)
        _hgrn_sub_chunk(qf_ref.at[0], vf_ref.at[0], lff_ref.at[0, 0], of_ref.at[0], st_ref.at[0], rf, tri_f, False)
        _hgrn_sub_chunk(qb_ref.at[0], vb_ref.at[0], lfb_ref.at[0, 0], ob_ref.at[0], st_ref.at[1], rb, tri_b, True)
        return carry

    lax.fori_loop(0, nsub, body, 0, unroll=HGRN_UNROLL)


def _hgrn_scan(q, v, lf, *, n_ctx_blocks):
    b, t, w = q.shape
    tb = HGRN_TB
    nblk = t // tb
    nlat = nblk - n_ctx_blocks

    def fwd_block(ti):
        return jnp.where(ti < n_ctx_blocks, nlat + ti, ti - n_ctx_blocks)

    def bwd_block(ti):
        return nblk - 1 - ti

    rows = lambda blk: pl.BlockSpec((1, tb, w), lambda bi, ti: (bi, blk(ti), 0))
    gate = lambda di, blk: pl.BlockSpec((1, 1, tb, w), lambda bi, ti: (di, bi, blk(ti), 0))
    out = jax.ShapeDtypeStruct((b, t, w), jnp.float32)
    return pl.pallas_call(
        functools.partial(_hgrn_kernel, tb=tb),
        grid=(b, nblk),
        in_specs=[rows(fwd_block), rows(fwd_block), gate(0, fwd_block),
                  rows(bwd_block), rows(bwd_block), gate(1, bwd_block)],
        out_specs=[rows(fwd_block), rows(bwd_block)],
        out_shape=[out, out],
        scratch_shapes=[pltpu.VMEM((2, C_H, C_DV, C_DK), jnp.float32)],
        compiler_params=_cparams(("arbitrary", "arbitrary")),
        name="hgrn2_scan",
    )(q, v, lf, q, v, lf)


def _head_rms(x, w, n_heads, width):
    out = []
    for hh in range(n_heads):
        blk = x[:, hh * width:(hh + 1) * width]
        out.append(_rms_rows(blk, w))
    return jnp.concatenate(out, axis=1)


def _merge_kernel(oa_ref, ob_ref, ocf_ref, ocb_ref, cg_ref, gt_ref, x_ref, ng_ref, ma_ref, mb_ref, dw_ref, hw_ref,
                  wbr_ref, wo_ref, xo_ref, ho_ref, hb_ref, *, diff_out_scale):
    ob = (_head_rms(ob_ref[...], dw_ref[...], B_H, B_VD) * diff_out_scale).astype(jnp.bfloat16)
    cg = cg_ref[...].astype(jnp.float32)
    oc = (_head_rms(ocf_ref[...] + ocb_ref[...], hw_ref[...], C_H, C_DV) * (cg * jax.nn.sigmoid(cg)))
    acc = None
    for i, o in enumerate((oa_ref[...], ob, oc.astype(jnp.bfloat16))):
        z = jnp.dot(o, wbr_ref[i], preferred_element_type=jnp.float32)
        g = jax.nn.sigmoid(gt_ref[:, i * D_MODEL:(i + 1) * D_MODEL].astype(jnp.float32))
        acc = g * z if acc is None else acc + g * z
    y = jnp.dot(acc.astype(jnp.bfloat16), wo_ref[...], preferred_element_type=jnp.float32)
    ma, mb = ma_ref[0], mb_ref[0]
    for half, (xn, mod) in enumerate(zip(_residual_tile(x_ref, y, ng_ref[1], ma, mb, 2), (ma, mb))):
        rows = slice(half * HALF_TM, (half + 1) * HALF_TM)
        xo_ref[rows, :] = xn
        h_next = _rms_rows(xn, ng_ref[2]) * (1.0 + mod[4]) + mod[3]
        ho_ref[rows, :] = h_next
        hb_ref[0, rows, :] = h_next.astype(hb_ref.dtype)
    hb_ref[1] = jnp.zeros((MM_TM, D_MODEL), hb_ref.dtype)


def _merge(oa, ob, ocf, ocb, cg, gates, x, norm_g, mods, diff_w, hgrn_w, w_br, w_o, *, t_rows, seq,
           diff_out_scale):
    m = x.shape[0]
    tm = MM_TM
    row = lambda i: (i, 0)
    f32 = jnp.float32
    return pl.pallas_call(
        functools.partial(_merge_kernel, diff_out_scale=diff_out_scale),
        grid=(m // tm,),
        in_specs=[pl.BlockSpec((tm, BRANCH_W), row), pl.BlockSpec((tm, BRANCH_W), row),
                  pl.BlockSpec((tm, BRANCH_W), row), pl.BlockSpec((tm, BRANCH_W), row),
                  pl.BlockSpec((tm, BRANCH_W), row),
                  pl.BlockSpec((tm, GATE_W), row), pl.BlockSpec((tm, D_MODEL), row),
                  pl.BlockSpec((4, 1, D_MODEL), lambda i: (0, 0, 0))] + _mod_specs(t_rows, seq) + [
                  pl.BlockSpec((1, B_VD), lambda i: (0, 0)), pl.BlockSpec((1, C_DV), lambda i: (0, 0)),
                  pl.BlockSpec((N_BRANCH, BRANCH_W, D_MODEL), lambda i: (0, 0, 0)),
                  pl.BlockSpec((D_MODEL, D_MODEL), lambda i: (0, 0))],
        out_specs=[pl.BlockSpec((tm, D_MODEL), row), pl.BlockSpec((tm, D_MODEL), row),
                   pl.BlockSpec((2, tm, D_MODEL), lambda i: (0, i, 0))],
        out_shape=[jax.ShapeDtypeStruct((m, D_MODEL), f32), jax.ShapeDtypeStruct((m, D_MODEL), f32),
                   jax.ShapeDtypeStruct((2, m, D_MODEL), jnp.bfloat16)],
        compiler_params=_cparams(("arbitrary",)),
        name="branch_merge",
    )(oa, ob, ocf, ocb, cg, gates, x, norm_g, mods, mods, diff_w, hgrn_w, w_br, w_o)


def _moe_kernel(blk_e_ref, n_used_ref, x_ref, wgu_ref, bgu_ref, wdn_ref, bdn_ref, y_ref, wgu_sc, wdn_sc):
    i = pl.program_id(0)
    n_used = n_used_ref[0]
    e = blk_e_ref[i]
    e_prev = blk_e_ref[jnp.maximum(i - 1, 0)]

    @pl.when((i == 0) | (e != e_prev))
    def _():
        wgu_sc[...] = wgu_ref[0, 0].astype(jnp.bfloat16)
        wdn_sc[...] = wdn_ref[0, 0].astype(jnp.bfloat16)

    @pl.when(i < n_used)
    def _():
        gu = jnp.dot(x_ref[...], wgu_sc[...], preferred_element_type=jnp.float32) + bgu_ref[0, 0]
        g = jnp.minimum(gu[:, :EXPERT_FF], SWIGLU_LIMIT)
        u = jnp.clip(gu[:, EXPERT_FF:], -SWIGLU_LIMIT, SWIGLU_LIMIT)
        act = (u + 1.0) * (g * jax.nn.sigmoid(SWIGLU_ALPHA * g))
        y = jnp.dot(act.astype(jnp.bfloat16), wdn_sc[...], preferred_element_type=jnp.float32) + bdn_ref[0, 0]
        y_ref[...] = y.astype(y_ref.dtype)

    @pl.when(i >= n_used)
    def _():
        y_ref[...] = jnp.zeros_like(y_ref)


def _moe_experts(blk_e, n_used, xs, w_gu, b_gu, w_dn, b_dn, *, layer):
    r, dm = xs.shape
    tm = MOE_TM
    n_blk = blk_e.shape[0]
    return pl.pallas_call(
        _moe_kernel,
        grid_spec=pltpu.PrefetchScalarGridSpec(
            num_scalar_prefetch=2, grid=(n_blk,),
            in_specs=[pl.BlockSpec((tm, dm), lambda i, be, nu: (i, 0)),
                      pl.BlockSpec((1, 1, dm, 2 * EXPERT_FF), lambda i, be, nu: (layer, be[i], 0, 0)),
                      pl.BlockSpec((1, 1, 1, 2 * EXPERT_FF), lambda i, be, nu: (layer, be[i], 0, 0)),
                      pl.BlockSpec((1, 1, EXPERT_FF, dm), lambda i, be, nu: (layer, be[i], 0, 0)),
                      pl.BlockSpec((1, 1, 1, dm), lambda i, be, nu: (layer, be[i], 0, 0))],
            out_specs=pl.BlockSpec((tm, dm), lambda i, be, nu: (i, 0)),
            scratch_shapes=[pltpu.VMEM((dm, 2 * EXPERT_FF), jnp.bfloat16),
                            pltpu.VMEM((EXPERT_FF, dm), jnp.bfloat16)]),
        out_shape=jax.ShapeDtypeStruct((r, dm), jnp.bfloat16),
        compiler_params=_cparams(("arbitrary",)),
        name="moe_experts",
    )(blk_e, n_used, xs, w_gu, b_gu.reshape(DEPTH, N_EXPERTS, 1, -1), w_dn,
      b_dn.reshape(DEPTH, N_EXPERTS, 1, -1))


def _combine_kernel(*refs):
    yk_refs = refs[:TOP_K]
    gt_ref, x_ref, ng_ref, ma_ref, mb_ref, xo_ref = refs[TOP_K:]
    gt = gt_ref[...]
    f = None
    for k in range(TOP_K):
        term = gt[:, k:k + 1] * yk_refs[k][...].astype(jnp.float32)
        f = term if f is None else f + term
    for half, xn in enumerate(_residual_tile(x_ref, f, ng_ref[3], ma_ref[0], mb_ref[0], 5)):
        xo_ref[half * HALF_TM:(half + 1) * HALF_TM, :] = xn


def _moe_combine(yk, gates, x, norm_g, mods, *, t_rows, seq):
    m = x.shape[0]
    tm = MM_TM
    n_tiles = m // tm
    row = lambda i: (i, 0)
    choice = lambda k: pl.BlockSpec((tm, D_MODEL), lambda i: (k * n_tiles + i, 0))
    return pl.pallas_call(
        _combine_kernel,
        grid=(n_tiles,),
        in_specs=[choice(k) for k in range(TOP_K)] + [
            pl.BlockSpec((tm, TOP_K), row), pl.BlockSpec((tm, D_MODEL), row),
            pl.BlockSpec((4, 1, D_MODEL), lambda i: (0, 0, 0))] + _mod_specs(t_rows, seq),
        out_specs=pl.BlockSpec((tm, D_MODEL), row),
        out_shape=jax.ShapeDtypeStruct((m, D_MODEL), jnp.float32),
        compiler_params=_cparams(("arbitrary",)),
        name="moe_combine",
    )(*([yk] * TOP_K), gates, x, norm_g, mods, mods)


def _route_kernel(h_ref, wr_ref, br_ref, e_ref, g_ref, rk_ref, cnt_ref, run_sc):
    i = pl.program_id(0)
    tm = h_ref.shape[0]

    @pl.when(i == 0)
    def _():
        run_sc[...] = jnp.zeros_like(run_sc)

    vals = jnp.dot(h_ref[...], wr_ref[...], preferred_element_type=jnp.float32,
                   precision=lax.Precision.HIGHEST) + br_ref[...]
    lane = lax.broadcasted_iota(jnp.int32, (tm, LANES), 1)
    top_v, top_e, picked = [], [], []
    for _ in range(TOP_K):
        m = jnp.max(vals, axis=-1, keepdims=True)
        idx = jnp.min(jnp.where(vals == m, lane, LANES), axis=-1, keepdims=True)
        sel = lane == idx
        top_v.append(m)
        top_e.append(idx)
        picked.append(sel)
        vals = jnp.where(sel, -jnp.inf, vals)
    ex = [jnp.exp(v - top_v[0]) for v in top_v]
    inv_den = 1.0 / (ex[0] + ex[1] + ex[2] + ex[3])
    any_pick = picked[0] | picked[1] | picked[2] | picked[3]
    rr = lax.broadcasted_iota(jnp.int32, (tm, tm), 0)
    cc = lax.broadcasted_iota(jnp.int32, (tm, tm), 1)
    earlier = (cc < rr).astype(jnp.bfloat16)
    before = jnp.dot(earlier, any_pick.astype(jnp.bfloat16), preferred_element_type=jnp.float32) + run_sc[...]
    e_out = jnp.zeros((tm, LANES), jnp.int32)
    g_out = jnp.zeros((tm, LANES), jnp.float32)
    r_out = jnp.zeros((tm, LANES), jnp.int32)
    for k in range(TOP_K):
        rank = jnp.sum(jnp.where(picked[k], before, 0.0), axis=-1, keepdims=True).astype(jnp.int32)
        e_out = jnp.where(lane == k, top_e[k], e_out)
        g_out = jnp.where(lane == k, ex[k] * inv_den, g_out)
        r_out = jnp.where(lane == k, rank, r_out)
    e_ref[...] = e_out
    g_ref[...] = g_out
    rk_ref[...] = r_out
    run_sc[...] += jnp.sum(any_pick.astype(jnp.float32), axis=0, keepdims=True)
    cnt_ref[...] = run_sc[...]


def _route(h, w_router, b_router):
    n, dm = h.shape
    tm = MM_TM
    w_r = jnp.zeros((dm, LANES), jnp.float32).at[:, :N_EXPERTS].set(w_router)
    b_r = jnp.full((1, LANES), NEG_INF, jnp.float32).at[0, :N_EXPERTS].set(b_router)
    row = lambda i: (i, 0)
    whole = lambda i: (0, 0)
    e, g, rk, cnt = pl.pallas_call(
        _route_kernel,
        grid=(n // tm,),
        in_specs=[pl.BlockSpec((tm, dm), row), pl.BlockSpec((dm, LANES), whole), pl.BlockSpec((1, LANES), whole)],
        out_specs=[pl.BlockSpec((tm, LANES), row), pl.BlockSpec((tm, LANES), row), pl.BlockSpec((tm, LANES), row),
                   pl.BlockSpec((1, LANES), whole)],
        out_shape=[jax.ShapeDtypeStruct((n, LANES), jnp.int32), jax.ShapeDtypeStruct((n, LANES), jnp.float32),
                   jax.ShapeDtypeStruct((n, LANES), jnp.int32), jax.ShapeDtypeStruct((1, LANES), jnp.float32)],
        scratch_shapes=[pltpu.VMEM((1, LANES), jnp.float32)],
        compiler_params=_cparams(("arbitrary",)),
        name="router",
    )(h, w_r, b_r)
    return e[:, :TOP_K], g[:, :TOP_K], rk[:, :TOP_K], cnt[0, :N_EXPERTS].astype(jnp.int32)


def _moe_ffn(h, h_rows, x, norm_g, mods, w_router, b_router, w_gu, b_gu, w_dn, b_dn, *, layer, t_rows, seq):
    n, dm = h.shape
    nk = n * TOP_K
    top_e, gates, rank, counts = _route(h, w_router, b_router)
    flat_e = top_e.reshape(-1)
    padded = (counts + MOE_TM - 1) // MOE_TM * MOE_TM
    pad_end = jnp.cumsum(padded)
    pad_start = pad_end - padded
    grp_start = jnp.cumsum(counts) - counts
    dest = (pad_start[top_e] + rank).reshape(-1)
    n_blk = (nk + MOE_TM - 1) // MOE_TM + N_EXPERTS
    rows = n_blk * MOE_TM
    blk_start = jnp.arange(n_blk, dtype=jnp.int32) * MOE_TM
    blk_e = jnp.minimum(jnp.sum((pad_end[None, :] <= blk_start[:, None]).astype(jnp.int32), axis=1),
                        N_EXPERTS - 1)
    n_used = (pad_end[-1] // MOE_TM).astype(jnp.int32).reshape(1)
    order = jnp.argsort(flat_e)
    row_e = jnp.repeat(blk_e, MOE_TM)
    sorted_pos = jnp.arange(rows, dtype=jnp.int32) - pad_start[row_e] + grp_start[row_e]
    row_tok = (order[jnp.clip(sorted_pos, 0, nk - 1)] // TOP_K).astype(jnp.int32)
    row_tok = jnp.where(sorted_pos < nk, row_tok, jnp.arange(rows, dtype=jnp.int32) % n)
    xs = h_rows[row_tok]
    y = _moe_experts(blk_e, n_used, xs, w_gu, b_gu, w_dn, b_dn, layer=layer)
    yk = y[dest.reshape(n, TOP_K).T.reshape(-1)]
    return _moe_combine(yk, gates, x, norm_g, mods, t_rows=t_rows, seq=seq)


def _rope_tables(rows, n_ctx, batch):
    row = jnp.repeat(jnp.arange(rows, dtype=jnp.float32), GRID_W)
    col = jnp.tile(jnp.arange(GRID_W, dtype=jnp.float32), rows)
    n_freq = HEAD_DIM // 4
    inv = ROPE_THETA ** (-jnp.arange(n_freq, dtype=jnp.float32) / n_freq)
    ang_r = row[:, None] * inv
    ang_c = col[:, None] * inv
    cos_h = jnp.concatenate([jnp.cos(ang_r), jnp.cos(ang_r), jnp.cos(ang_c), jnp.cos(ang_c)], axis=1)
    sin_h = jnp.concatenate([-jnp.sin(ang_r), jnp.sin(ang_r), -jnp.sin(ang_c), jnp.sin(ang_c)], axis=1)
    cos_t = jnp.concatenate([cos_h, jnp.ones((n_ctx, HEAD_DIM), jnp.float32)], axis=0)
    sin_t = jnp.concatenate([sin_h, jnp.zeros((n_ctx, HEAD_DIM), jnp.float32)], axis=0)
    reps = (batch, LANES // HEAD_DIM)
    return jnp.tile(cos_t, reps), jnp.tile(sin_t, reps)


def _rope_partner(n_cols):
    idx = jnp.arange(n_cols)
    quarter = HEAD_DIM // 4
    return jnp.where((idx % (2 * quarter)) < quarter, idx + quarter, idx - quarter)


def kernel(x, c, ctx, c_ctx, w_mod, b_mod, norm_g, w_in, attn_sink, diff_lambda, diff_norm_w,
           hgrn_lb_logits, hgrn_norm_w, w_branch, w_out, w_router, b_router, w_gate_up, b_gate_up,
           w_down, b_down):
    b, s, dm = x.shape
    l = ctx.shape[1]
    t = s + l
    f32 = jnp.float32
    bf16 = jnp.bfloat16
    cos_t, sin_t = _rope_tables(s // GRID_W, l, b)
    lb_cum = jnp.cumsum(jax.nn.softmax(hgrn_lb_logits.astype(f32), axis=0), axis=0)
    lower_bounds = lb_cum - lb_cum[0:1]
    cond_rows = 16
    cond = jnp.zeros((cond_rows, dm), f32).at[:b].set(jax.nn.silu(c)).at[b].set(jax.nn.silu(c_ctx))
    scale = HEAD_DIM ** -0.5
    xa = jnp.concatenate([x, ctx], axis=1).reshape(b * t, dm)
    h_moe = None
    for layer in range(DEPTH):
        lam_init = 0.8 - 0.6 * math.exp(-0.3 * layer)
        mod_all = _matmul(cond, w_mod[layer], tm=cond_rows, tn=3 * dm, out_dtype=f32,
                          precision=lax.Precision.HIGHEST, name="adaln") + b_mod[layer]
        mod = mod_all[:b].reshape(b, 1, 6, 1, dm)
        mod_c = jnp.broadcast_to(mod_all[b:b + 1].reshape(1, 1, 6, 1, dm), (b, 1, 6, 1, dm))
        mods = jnp.concatenate([mod, mod_c], axis=1).reshape(2 * b, 6, 1, dm)
        ng = norm_g[layer].reshape(4, 1, dm)
        g0 = ng[0:1]
        geo = dict(t_rows=t, seq=s)

        w = w_in[layer]
        w_aq = (w[:, OFF_A:OFF_A + A_Q_W] * scale).reshape(dm, A_HKV, A_GROUP, HEAD_DIM)
        pad = jnp.zeros((dm, A_GROUP, HEAD_DIM), f32)
        w_aq8 = jnp.concatenate([jnp.concatenate([w_aq[:, 0], pad], axis=-1),
                                 jnp.concatenate([pad, w_aq[:, 1]], axis=-1)], axis=1).reshape(dm, A_Q8_W)
        w_a = jnp.concatenate([w_aq8, w[:, OFF_A + A_Q_W:OFF_A + A_W]], axis=1).astype(bf16)
        w_b = w[:, OFF_B:OFF_B + B_W].at[:, :B_QK_W].multiply(scale * LOG2_E).astype(bf16)
        proj_a = _proj_rope(xa, g0, mods, w_a, w_a[:, _rope_partner(A_ROPE_W2)], cos_t, sin_t,
                            rope_w=A_ROPE_W2, name="proj_a", **geo).reshape(b, t, A_W2)
        bq, bk_hm, bv1_hm = _proj_b(xa, g0, mods, w_b, w_b[:, _rope_partner(B_ROPE_W)], cos_t, sin_t, **geo)
        bq = bq.reshape(b, t, B_QK_W)
        lb = lower_bounds[layer]
        lb_par = jnp.stack([jnp.log(lb[0]), jnp.log1p(-lb[0]), jnp.log(lb[1]), jnp.log1p(-lb[1])])
        cq, ci, lf, cg = _proj_hgrn(xa, g0, mods, w[:, OFF_C:OFF_C + C_W].astype(bf16),
                                    lb_par.reshape(4, 1, C_K_W), **geo)
        gates = _proj_plain(xa, g0, mods, w[:, OFF_GATE:].astype(bf16), name="gate_proj", **geo)

        sink32 = attn_sink[layer].astype(f32)
        o_a = _window_attention(proj_a, sink32, seq=s)

        lp = diff_lambda[layer].astype(f32)
        lam = (jnp.exp(jnp.sum(lp[0] * lp[1])) - jnp.exp(jnp.sum(lp[2] * lp[3])) + lam_init).reshape(1).astype(f32)
        o_b = _diff_attention(lam, bq, bk_hm, bv1_hm, tq=FLASH_TQ, tk=FLASH_TK, q_block_offset=0,
                              n_q_blocks=s // FLASH_TQ, k_block_offset=0, nk=t // FLASH_TK)
        o_bx = _diff_attention(lam, bq, bk_hm, bv1_hm, tq=l, tk=l, q_block_offset=s // l, n_q_blocks=1,
                               k_block_offset=s // l, nk=1)
        o_b = jnp.concatenate([o_b, o_bx], axis=1)

        o_cf, o_cb = _hgrn_scan(cq.reshape(b, t, C_K_W), ci.reshape(b, t, C_V_W), lf.reshape(2, b, t, C_K_W),
                                n_ctx_blocks=l // HGRN_TB)

        col = jnp.arange(A_Q_W)
        head_of_col = ((col % LANES) // HEAD_DIM) * A_GROUP + col // LANES
        w_br = w_branch[layer].at[0].set(w_branch[layer][0][head_of_col * HEAD_DIM + col % HEAD_DIM])
        xa, h_moe, h_rows = _merge(o_a.reshape(b * t, A_Q_W), o_b.reshape(b * t, B_V_W), o_cf.reshape(b * t, C_V_W),
                           o_cb.reshape(b * t, C_V_W), cg,
                           gates, xa, ng, mods, diff_norm_w[layer].reshape(1, B_VD).astype(f32),
                           hgrn_norm_w[layer].reshape(1, C_DV).astype(f32),
                           w_br.astype(bf16), w_out[layer].astype(bf16),
                           diff_out_scale=1 - lam_init, **geo)

        xa = _moe_ffn(h_moe, h_rows.reshape(2 * b * t, dm), xa, ng, mods, w_router[layer], b_router[layer],
                      w_gate_up, b_gate_up, w_down, b_down, layer=layer, **geo)
    return xa.reshape(b, t, dm)[:, :s]
```

```python
import functools
import math

import jax
import jax.numpy as jnp
from jax import lax
from jax.experimental import pallas as pl
from jax.experimental.pallas import tpu as pltpu

D_MODEL = 1024
DEPTH = 2
GRID_W = 64
HEAD_DIM = 64
ROPE_THETA = 10000.0
RMS_EPS = 1e-6
NEG_INF = -1e30
BLOCK = 128
A_HQ = 8
A_HKV = 2
A_GROUP = A_HQ // A_HKV
WINDOW = 128
B_H = 4
B_HD = HEAD_DIM
B_VD = 2 * HEAD_DIM
C_H = 4
C_DK = 128
C_DV = 128
N_BRANCH = 3
BRANCH_W = 512
A_Q_W = A_HQ * HEAD_DIM
A_KV_W = A_HKV * HEAD_DIM
B_QK_W = B_H * 2 * B_HD
B_V_W = B_H * B_VD
C_K_W = C_H * C_DK
C_V_W = C_H * C_DV
GATE_W = N_BRANCH * D_MODEL
N_EXPERTS = 32
TOP_K = 4
EXPERT_FF = D_MODEL
SWIGLU_LIMIT = 7.0
SWIGLU_ALPHA = 1.702

A_W = A_Q_W + 2 * A_KV_W
A_ROPE_W = A_Q_W + A_KV_W
B_W = 2 * B_QK_W + B_V_W
B_ROPE_W = 2 * B_QK_W
C_W = 3 * C_K_W + 2 * C_V_W
OFF_A = 0
OFF_B = OFF_A + A_W
OFF_C = OFF_B + B_W
OFF_GATE = OFF_C + C_W

V7X_VMEM_LIMIT_BYTES = 56 * 1024 * 1024
LANES = 128
LOG2_E = math.log2(math.e)

MM_TM = 512
HALF_TM = MM_TM // 2
FLASH_TQ = 2048
FLASH_TK = 768
HGRN_TB = 256
HGRN_SUB = 16
HGRN_UNROLL = 4
MOE_TM = 512


def _cparams(sem):
    return pltpu.CompilerParams(dimension_semantics=sem, vmem_limit_bytes=V7X_VMEM_LIMIT_BYTES)


def _rms_rows(x, g):
    return (x * lax.rsqrt(jnp.mean(x * x, axis=-1, keepdims=True) + RMS_EPS)) * g


def _modulated_tile(x_ref, g, mod_a, mod_b, shift_slot, scale_slot):
    out = []
    for half, mod in enumerate((mod_a, mod_b)):
        x = x_ref[half * HALF_TM:(half + 1) * HALF_TM, :]
        out.append(_rms_rows(x, g) * (1.0 + mod[scale_slot]) + mod[shift_slot])
    return jnp.concatenate(out, axis=0)


def _residual_tile(x_ref, y, g, mod_a, mod_b, gate_slot):
    out = []
    for half, mod in enumerate((mod_a, mod_b)):
        rows = slice(half * HALF_TM, (half + 1) * HALF_TM)
        out.append(x_ref[rows, :] + mod[gate_slot] * _rms_rows(y[rows, :], g))
    return out


def _mod_specs(t_rows, seq):
    halves_per_batch = t_rows // HALF_TM
    lat_halves = seq // HALF_TM

    def row(u):
        return (u // halves_per_batch) * 2 + jnp.where(u % halves_per_batch >= lat_halves, 1, 0)

    blk = (1, 6, 1, D_MODEL)
    return [pl.BlockSpec(blk, lambda i: (row(2 * i), 0, 0, 0)),
            pl.BlockSpec(blk, lambda i: (row(2 * i + 1), 0, 0, 0))]


def _proj_rope_kernel(x_ref, g_ref, ma_ref, mb_ref, w_ref, wsw_ref, cos_ref, sin_ref, o_ref, *, rope_w):
    h = _modulated_tile(x_ref, g_ref[0], ma_ref[0], mb_ref[0], 0, 1).astype(jnp.bfloat16)
    z = jnp.dot(h, w_ref[...], preferred_element_type=jnp.float32)
    zs = jnp.dot(h, wsw_ref[...], preferred_element_type=jnp.float32)
    reps = rope_w // LANES
    rot = z[:, :rope_w] * jnp.tile(cos_ref[...], (1, reps)) + zs * jnp.tile(sin_ref[...], (1, reps))
    o_ref[:, :rope_w] = rot.astype(o_ref.dtype)
    o_ref[:, rope_w:] = z[:, rope_w:].astype(o_ref.dtype)


def _proj_b_kernel(x_ref, g_ref, ma_ref, mb_ref, w_ref, wsw_ref, cos_ref, sin_ref, q_ref, k_ref, v1_ref):
    h = _modulated_tile(x_ref, g_ref[0], ma_ref[0], mb_ref[0], 0, 1).astype(jnp.bfloat16)
    z = jnp.dot(h, w_ref[...], preferred_element_type=jnp.float32)
    zs = jnp.dot(h, wsw_ref[...], preferred_element_type=jnp.float32)
    reps = B_ROPE_W // LANES
    rot = z[:, :B_ROPE_W] * jnp.tile(cos_ref[...], (1, reps)) + zs * jnp.tile(sin_ref[...], (1, reps))
    q_ref[...] = rot[:, :B_QK_W].astype(q_ref.dtype)
    for hh in range(B_H):
        k_ref[hh] = rot[:, B_QK_W + hh * 2 * B_HD:B_QK_W + (hh + 1) * 2 * B_HD].astype(k_ref.dtype)
        v1_ref[hh, :, :B_VD] = z[:, B_ROPE_W + hh * B_VD:B_ROPE_W + (hh + 1) * B_VD].astype(v1_ref.dtype)
        v1_ref[hh, :, B_VD:] = jnp.ones((MM_TM, B_VD), v1_ref.dtype)


def _proj_plain_kernel(x_ref, g_ref, ma_ref, mb_ref, w_ref, o_ref):
    h = _modulated_tile(x_ref, g_ref[0], ma_ref[0], mb_ref[0], 0, 1).astype(jnp.bfloat16)
    o_ref[...] = jnp.dot(h, w_ref[...], preferred_element_type=jnp.float32).astype(o_ref.dtype)


def _log_forget(z, log_lb, log_1m_lb):
    log_sig = jnp.minimum(z, 0.0) - jnp.log(1.0 + jnp.exp(-jnp.abs(z)))
    c = log_1m_lb + log_sig
    return jnp.maximum(log_lb, c) + jnp.log(1.0 + jnp.exp(-jnp.abs(log_lb - c)))


def _proj_hgrn_kernel(x_ref, g_ref, ma_ref, mb_ref, w_ref, lb_ref, q_ref, v_ref, lf_ref, cg_ref):
    h = _modulated_tile(x_ref, g_ref[0], ma_ref[0], mb_ref[0], 0, 1).astype(jnp.bfloat16)
    z = jnp.dot(h, w_ref[...], preferred_element_type=jnp.float32)
    zq = z[:, :C_K_W]
    q_ref[...] = zq * jax.nn.sigmoid(zq)
    v_ref[...] = z[:, C_K_W:C_K_W + C_V_W]
    off = C_K_W + C_V_W
    lf_ref[0] = _log_forget(z[:, off:off + C_K_W], lb_ref[0], lb_ref[1])
    lf_ref[1] = _log_forget(z[:, off + C_K_W:off + 2 * C_K_W], lb_ref[2], lb_ref[3])
    cg_ref[...] = z[:, off + 2 * C_K_W:].astype(cg_ref.dtype)


def _proj_common_specs(t_rows, seq):
    row = lambda i: (i, 0)
    return [pl.BlockSpec((MM_TM, D_MODEL), row),
            pl.BlockSpec((1, 1, D_MODEL), lambda i: (0, 0, 0))] + _mod_specs(t_rows, seq)


def _proj_rope(x, g, mods, w, wsw, cos, sin, *, t_rows, seq, rope_w, name):
    m = x.shape[0]
    n = w.shape[1]
    row = lambda i: (i, 0)
    whole = lambda i: (0, 0)
    return pl.pallas_call(
        functools.partial(_proj_rope_kernel, rope_w=rope_w),
        grid=(m // MM_TM,),
        in_specs=_proj_common_specs(t_rows, seq) + [
            pl.BlockSpec((D_MODEL, n), whole), pl.BlockSpec((D_MODEL, rope_w), whole),
            pl.BlockSpec((MM_TM, LANES), row), pl.BlockSpec((MM_TM, LANES), row)],
        out_specs=pl.BlockSpec((MM_TM, n), row),
        out_shape=jax.ShapeDtypeStruct((m, n), jnp.bfloat16),
        compiler_params=_cparams(("arbitrary",)),
        name=name,
    )(x, g, mods, mods, w, wsw, cos, sin)


def _proj_b(x, g, mods, w, wsw, cos, sin, *, t_rows, seq):
    m = x.shape[0]
    row = lambda i: (i, 0)
    whole = lambda i: (0, 0)
    heads = lambda i: (0, i, 0)
    bf16 = jnp.bfloat16
    return pl.pallas_call(
        _proj_b_kernel,
        grid=(m // MM_TM,),
        in_specs=_proj_common_specs(t_rows, seq) + [
            pl.BlockSpec((D_MODEL, B_W), whole), pl.BlockSpec((D_MODEL, B_ROPE_W), whole),
            pl.BlockSpec((MM_TM, LANES), row), pl.BlockSpec((MM_TM, LANES), row)],
        out_specs=[pl.BlockSpec((MM_TM, B_QK_W), row), pl.BlockSpec((B_H, MM_TM, 2 * B_HD), heads),
                   pl.BlockSpec((B_H, MM_TM, 2 * B_VD), heads)],
        out_shape=[jax.ShapeDtypeStruct((m, B_QK_W), bf16), jax.ShapeDtypeStruct((B_H, m, 2 * B_HD), bf16),
                   jax.ShapeDtypeStruct((B_H, m, 2 * B_VD), bf16)],
        compiler_params=_cparams(("arbitrary",)),
        name="proj_b",
    )(x, g, mods, mods, w, wsw, cos, sin)


def _proj_plain(x, g, mods, w, *, t_rows, seq, name):
    m = x.shape[0]
    n = w.shape[1]
    row = lambda i: (i, 0)
    return pl.pallas_call(
        _proj_plain_kernel,
        grid=(m // MM_TM,),
        in_specs=_proj_common_specs(t_rows, seq) + [pl.BlockSpec((D_MODEL, n), lambda i: (0, 0))],
        out_specs=pl.BlockSpec((MM_TM, n), row),
        out_shape=jax.ShapeDtypeStruct((m, n), jnp.bfloat16),
        compiler_params=_cparams(("arbitrary",)),
        name=name,
    )(x, g, mods, mods, w)


def _proj_hgrn(x, g, mods, w, lb_par, *, t_rows, seq):
    m = x.shape[0]
    row = lambda i: (i, 0)
    f32 = jnp.float32
    return pl.pallas_call(
        _proj_hgrn_kernel,
        grid=(m // MM_TM,),
        in_specs=_proj_common_specs(t_rows, seq) + [
            pl.BlockSpec((D_MODEL, C_W), lambda i: (0, 0)),
            pl.BlockSpec((4, 1, C_K_W), lambda i: (0, 0, 0))],
        out_specs=[pl.BlockSpec((MM_TM, C_K_W), row), pl.BlockSpec((MM_TM, C_V_W), row),
                   pl.BlockSpec((2, MM_TM, C_K_W), lambda i: (0, i, 0)), pl.BlockSpec((MM_TM, C_V_W), row)],
        out_shape=[jax.ShapeDtypeStruct((m, C_K_W), f32), jax.ShapeDtypeStruct((m, C_V_W), f32),
                   jax.ShapeDtypeStruct((2, m, C_K_W), f32), jax.ShapeDtypeStruct((m, C_V_W), jnp.bfloat16)],
        compiler_params=_cparams(("arbitrary",)),
        name="hgrn_proj",
    )(x, g, mods, mods, w, lb_par)


def _mm_kernel(x_ref, w_ref, o_ref, *, precision):
    o_ref[...] = jnp.dot(x_ref[...], w_ref[...], preferred_element_type=jnp.float32,
                         precision=precision).astype(o_ref.dtype)


def _matmul(x, w, *, tm, tn, out_dtype, precision=None, name):
    m, k = x.shape
    n = w.shape[1]
    assert m % tm == 0 and n % tn == 0, (m, tm, n, tn)
    return pl.pallas_call(
        functools.partial(_mm_kernel, precision=precision),
        grid=(n // tn, m // tm),
        in_specs=[pl.BlockSpec((tm, k), lambda j, i: (i, 0)),
                  pl.BlockSpec((k, tn), lambda j, i: (0, j))],
        out_specs=pl.BlockSpec((tm, tn), lambda j, i: (i, j)),
        out_shape=jax.ShapeDtypeStruct((m, n), out_dtype),
        compiler_params=_cparams(("arbitrary", "arbitrary")),
        name=name,
    )(x, w)


A_Q8_W = A_HQ * LANES
A_W2 = A_Q8_W + 2 * A_KV_W
A_ROPE_W2 = A_Q8_W + A_KV_W
A_K_BLK = A_Q8_W // A_KV_W
A_V_BLK = A_K_BLK + 1


def _attn_a_kernel(sink_ref, q_ref, kp_ref, kc_ref, kn_ref, vp_ref, vc_ref, vn_ref, kx_ref, vx_ref,
                   o_ref, *, seq):
    n = pl.program_id(1)
    k_all = jnp.concatenate([kp_ref[0], kc_ref[0], kn_ref[0], kx_ref[0]], axis=0)
    v_all = jnp.concatenate([vp_ref[0], vc_ref[0], vn_ref[0], vx_ref[0]], axis=0)
    n_keys = k_all.shape[0]
    qi = lax.broadcasted_iota(jnp.int32, (BLOCK, n_keys), 0)
    kj = lax.broadcasted_iota(jnp.int32, (BLOCK, n_keys), 1)
    kpos = n * BLOCK + kj - BLOCK
    local_ok = (jnp.abs(kj - BLOCK - qi) <= WINDOW) & (kpos >= 0) & (kpos < seq) & (n * BLOCK < seq)
    mask = local_ok | (kj >= 3 * BLOCK)
    q8 = jnp.concatenate([q_ref[0, :, hq * LANES:(hq + 1) * LANES] for hq in range(A_HQ)], axis=0)
    s = lax.dot_general(q8, k_all, (((1,), (1,)), ((), ())), preferred_element_type=jnp.float32)
    p_blocks, inv_l = [], []
    for hq in range(A_HQ):
        sh = jnp.where(mask, s[hq * BLOCK:(hq + 1) * BLOCK, :], NEG_INF)
        sk = sink_ref[hq]
        m = jnp.maximum(jnp.max(sh, axis=-1, keepdims=True), sk)
        p = jnp.exp(sh - m)
        inv_l.append(1.0 / (jnp.sum(p, axis=-1, keepdims=True) + jnp.exp(sk - m)))
        p_blocks.append(p.astype(v_all.dtype))
    o = jnp.dot(jnp.concatenate(p_blocks, axis=0), v_all, preferred_element_type=jnp.float32)
    lane = lax.broadcasted_iota(jnp.int32, (BLOCK, LANES), 1)
    for g in range(A_GROUP):
        lo = o[g * BLOCK:(g + 1) * BLOCK, :] * inv_l[g]
        hi = o[(A_GROUP + g) * BLOCK:(A_GROUP + g + 1) * BLOCK, :] * inv_l[A_GROUP + g]
        o_ref[0, :, g * LANES:(g + 1) * LANES] = jnp.where(lane < HEAD_DIM, lo, hi).astype(o_ref.dtype)


def _window_attention(proj_a, sink, *, seq):
    b, t, _ = proj_a.shape
    l = t - seq
    nb = seq // BLOCK
    ctx_blk = seq // l
    q_map = lambda bi, n, *_: (bi, n, 0)
    kv = lambda col, fn: pl.BlockSpec((1, BLOCK, A_KV_W), lambda bi, n, *_: (bi, fn(n), col))
    prev = lambda n: jnp.clip(n - 1, 0, nb - 1)
    cur = lambda n: jnp.minimum(n, nb - 1)
    nxt = lambda n: jnp.minimum(n + 1, nb - 1)
    ctx = lambda col: pl.BlockSpec((1, l, A_KV_W), lambda bi, n, *_: (bi, ctx_blk, col))
    return pl.pallas_call(
        functools.partial(_attn_a_kernel, seq=seq),
        grid_spec=pltpu.PrefetchScalarGridSpec(
            num_scalar_prefetch=1, grid=(b, t // BLOCK),
            in_specs=[pl.BlockSpec((1, BLOCK, A_Q8_W), q_map),
                      kv(A_K_BLK, prev), kv(A_K_BLK, cur), kv(A_K_BLK, nxt),
                      kv(A_V_BLK, prev), kv(A_V_BLK, cur), kv(A_V_BLK, nxt),
                      ctx(A_K_BLK), ctx(A_V_BLK)],
            out_specs=pl.BlockSpec((1, BLOCK, A_Q_W), q_map)),
        out_shape=jax.ShapeDtypeStruct((b, t, A_Q_W), jnp.bfloat16),
        compiler_params=_cparams(("arbitrary", "arbitrary")),
        name="window_gqa",
    )(sink, *([proj_a] * 9))


def _diff_flash_kernel(lam_ref, q_ref, k_ref, v_ref, o_ref, q2_sc, sa_sc, sb_sc, m_sc, acc_sc, *, tq, tk, nk):
    u = pl.program_id(2)
    new_q = u % nk == 0

    @pl.when(u == 0)
    def _():
        m_sc[...] = jnp.full_like(m_sc, NEG_INF)
        acc_sc[...] = jnp.zeros_like(acc_sc)
        sb_sc[...] = jnp.full_like(sb_sc, NEG_INF)

    @pl.when(new_q)
    def _():
        q = q_ref[0]
        lane = lax.broadcasted_iota(jnp.int32, q.shape, 1)
        q2_sc[:tq, :] = jnp.where(lane < B_HD, q, jnp.zeros_like(q))
        q2_sc[tq:, :] = jnp.where(lane >= B_HD, q, jnp.zeros_like(q))

    def step(s_new_ref, s_old_ref):
        m_prev = m_sc[...]
        m_new = jnp.maximum(m_prev, jnp.max(s_old_ref[...], axis=-1, keepdims=True))
        alpha = jnp.exp2(m_prev - m_new)
        p = jnp.exp2(s_old_ref[...] - jnp.tile(m_new, (1, tk // LANES))).astype(jnp.bfloat16)
        acc_sc[...] = jnp.tile(alpha, (1, 2)) * acc_sc[...] + jnp.dot(
            p, v_ref[0], preferred_element_type=jnp.float32)
        m_sc[...] = m_new
        s_new_ref[...] = lax.dot_general(q2_sc[...], k_ref[0], (((1,), (1,)), ((), ())),
                                         preferred_element_type=jnp.float32)

    @pl.when(u % 2 == 0)
    def _():
        step(sa_sc, sb_sc)

    @pl.when(u % 2 == 1)
    def _():
        step(sb_sc, sa_sc)

    @pl.when(new_q & (u > 0))
    def _():
        acc = acc_sc[...]
        o = acc[:, :B_VD] / acc[:, B_VD:]
        o_ref[0] = (o[:tq, :] - lam_ref[0] * o[tq:, :]).astype(o_ref.dtype)
        m_sc[...] = jnp.full_like(m_sc, NEG_INF)
        acc_sc[...] = jnp.zeros_like(acc_sc)


def _diff_attention(lam, q, k_hm, v1_hm, *, tq, tk, q_block_offset, n_q_blocks, k_block_offset, nk):
    b, t, _ = q.shape
    kb = t // tk
    n_steps = n_q_blocks * nk + 1
    k_blk = lambda bi, u: bi * kb + k_block_offset + u % nk
    in_specs = [pl.BlockSpec((1, tq, 2 * B_HD),
                             lambda bi, h, u, *_: (bi, jnp.minimum(u // nk, n_q_blocks - 1) + q_block_offset, h)),
                pl.BlockSpec((1, tk, 2 * B_HD), lambda bi, h, u, *_: (h, k_blk(bi, u), 0)),
                pl.BlockSpec((1, tk, 2 * B_VD), lambda bi, h, u, *_: (h, k_blk(bi, jnp.maximum(u - 1, 0)), 0))]
    return pl.pallas_call(
        functools.partial(_diff_flash_kernel, tq=tq, tk=tk, nk=nk),
        grid_spec=pltpu.PrefetchScalarGridSpec(
            num_scalar_prefetch=1, grid=(b, B_H, n_steps),
            in_specs=in_specs,
            out_specs=pl.BlockSpec((1, tq, B_VD), lambda bi, h, u, *_: (bi, jnp.maximum(u - 1, 0) // nk, h)),
            scratch_shapes=[pltpu.VMEM((2 * tq, 2 * B_HD), jnp.bfloat16),
                            pltpu.VMEM((2 * tq, tk), jnp.float32),
                            pltpu.VMEM((2 * tq, tk), jnp.float32),
                            pltpu.VMEM((2 * tq, LANES), jnp.float32),
                            pltpu.VMEM((2 * tq, 2 * B_VD), jnp.float32)]),
        out_shape=jax.ShapeDtypeStruct((b, n_q_blocks * tq, B_V_W), jnp.float32),
        compiler_params=_cparams(("arbitrary", "arbitrary", "arbitrary")),
        name="diff_flash",
    )(lam, q, k_hm, v1_hm)


def _hgrn_sub_chunk(q_ref, v_ref, lf_ref, o_ref, st_ref, r0, tri, rev):
    half = HGRN_SUB // 2
    row8 = lax.broadcasted_iota(jnp.int32, (half, C_DK), 0)
    lf = lf_ref[pl.ds(r0, HGRN_SUB), :] * LOG2_E
    lf_a = lf.astype(jnp.bfloat16)
    rem = lf - lf_a.astype(jnp.float32)
    lf_b = rem.astype(jnp.bfloat16)
    lf_c = (rem - lf_b.astype(jnp.float32)).astype(jnp.bfloat16)
    bl = (jnp.dot(tri, lf_a, preferred_element_type=jnp.float32)
          + jnp.dot(tri, lf_b, preferred_element_type=jnp.float32)
          + jnp.dot(tri, lf_c, preferred_element_type=jnp.float32))
    q = q_ref[pl.ds(r0, HGRN_SUB), :]
    v = v_ref[pl.ds(r0, HGRN_SUB), :]
    last = 0 if rev else HGRN_SUB - 1
    for hh in range(C_H):
        sl = slice(hh * C_DK, (hh + 1) * C_DK)
        lfh, blh, qh, vh = lf[:, sl], bl[:, sl], q[:, sl], v[:, sl]
        tot = blh[last:last + 1, :]
        kh = 1.0 - jnp.exp2(lfh)
        qt = qh * jnp.exp2(blh)
        kt = kh * jnp.exp2(tot - blh)
        st = st_ref[hh]
        o_sub = lax.dot_general(qt.astype(jnp.bfloat16), st.astype(jnp.bfloat16),
                                (((1,), (1,)), ((), ())), preferred_element_type=jnp.float32)
        parts = [o_sub[:half, :], o_sub[half:, :]]
        for ss in range(HGRN_SUB):
            ks = kh[ss:ss + 1, :]
            bs = blh[ss:ss + 1, :]
            vs = vh[ss:ss + 1, :]
            for p in range(2):
                lo = p * half
                if rev:
                    reached = lo <= ss
                    full = lo + half - 1 <= ss
                    ok = row8 + lo <= ss
                else:
                    reached = lo + half - 1 >= ss
                    full = lo >= ss
                    ok = row8 + lo >= ss
                if not reached:
                    continue
                diff = blh[lo:lo + half, :] - bs
                if not full:
                    diff = jnp.where(ok, diff, NEG_INF)
                col = jnp.sum((qh[lo:lo + half, :] * ks) * jnp.exp2(diff), axis=1, keepdims=True)
                parts[p] = parts[p] + col * vs
        o_ref[pl.ds(r0, half), sl] = parts[0]
        o_ref[pl.ds(r0 + half, half), sl] = parts[1]
        upd = lax.dot_general(vh.astype(jnp.bfloat16), kt.astype(jnp.bfloat16),
                              (((0,), (0,)), ((), ())), preferred_element_type=jnp.float32)
        st_ref[hh] = jnp.exp2(tot) * st + upd


def _hgrn_kernel(qf_ref, vf_ref, lff_ref, qb_ref, vb_ref, lfb_ref, of_ref, ob_ref, st_ref, *, tb):
    t = pl.program_id(1)
    nsub = tb // HGRN_SUB

    @pl.when(t == 0)
    def _():
        st_ref[...] = jnp.zeros_like(st_ref)

    rr = lax.broadcasted_iota(jnp.int32, (HGRN_SUB, HGRN_SUB), 0)
    cc = lax.broadcasted_iota(jnp.int32, (HGRN_SUB, HGRN_SUB), 1)
    tri_f = (cc <= rr).astype(jnp.bfloat16)
    tri_b = (cc >= rr).astype(jnp.bfloat16)

    def body(i, carry):
        rf = pl.multiple_of(i * HGRN_SUB, HGRN_SUB)
        rb = pl.multiple_of((nsub - 1 - i) * HGRN_SUB, HGRN_SUB)
        _hgrn_sub_chunk(qf_ref.at[0], vf_ref.at[0], lff_ref.at[0, 0], of_ref.at[0], st_ref.at[0], rf, tri_f, False)
        _hgrn_sub_chunk(qb_ref.at[0], vb_ref.at[0], lfb_ref.at[0, 0], ob_ref.at[0], st_ref.at[1], rb, tri_b, True)
        return carry

    lax.fori_loop(0, nsub, body, 0, unroll=HGRN_UNROLL)


def _hgrn_scan(q, v, lf, *, n_ctx_blocks):
    b, t, w = q.shape
    tb = HGRN_TB
    nblk = t // tb
    nlat = nblk - n_ctx_blocks

    def fwd_block(ti):
        return jnp.where(ti < n_ctx_blocks, nlat + ti, ti - n_ctx_blocks)

    def bwd_block(ti):
        return nblk - 1 - ti

    rows = lambda blk: pl.BlockSpec((1, tb, w), lambda bi, ti: (bi, blk(ti), 0))
    gate = lambda di, blk: pl.BlockSpec((1, 1, tb, w), lambda bi, ti: (di, bi, blk(ti), 0))
    out = jax.ShapeDtypeStruct((b, t, w), jnp.float32)
    return pl.pallas_call(
        functools.partial(_hgrn_kernel, tb=tb),
        grid=(b, nblk),
        in_specs=[rows(fwd_block), rows(fwd_block), gate(0, fwd_block),
                  rows(bwd_block), rows(bwd_block), gate(1, bwd_block)],
        out_specs=[rows(fwd_block), rows(bwd_block)],
        out_shape=[out, out],
        scratch_shapes=[pltpu.VMEM((2, C_H, C_DV, C_DK), jnp.float32)],
        compiler_params=_cparams(("arbitrary", "arbitrary")),
        name="hgrn2_scan",
    )(q, v, lf, q, v, lf)


def _head_rms(x, w, n_heads, width):
    out = []
    for hh in range(n_heads):
        blk = x[:, hh * width:(hh + 1) * width]
        out.append(_rms_rows(blk, w))
    return jnp.concatenate(out, axis=1)


def _merge_kernel(oa_ref, ob_ref, ocf_ref, ocb_ref, cg_ref, gt_ref, x_ref, ng_ref, ma_ref, mb_ref, dw_ref, hw_ref,
                  wbr_ref, wo_ref, xo_ref, ho_ref, hb_ref, *, diff_out_scale):
    ob = (_head_rms(ob_ref[...], dw_ref[...], B_H, B_VD) * diff_out_scale).astype(jnp.bfloat16)
    cg = cg_ref[...].astype(jnp.float32)
    oc = (_head_rms(ocf_ref[...] + ocb_ref[...], hw_ref[...], C_H, C_DV) * (cg * jax.nn.sigmoid(cg)))
    acc = None
    for i, o in enumerate((oa_ref[...], ob, oc.astype(jnp.bfloat16))):
        z = jnp.dot(o, wbr_ref[i], preferred_element_type=jnp.float32)
        g = jax.nn.sigmoid(gt_ref[:, i * D_MODEL:(i + 1) * D_MODEL].astype(jnp.float32))
        acc = g * z if acc is None else acc + g * z
    y = jnp.dot(acc.astype(jnp.bfloat16), wo_ref[...], preferred_element_type=jnp.float32)
    ma, mb = ma_ref[0], mb_ref[0]
    for half, (xn, mod) in enumerate(zip(_residual_tile(x_ref, y, ng_ref[1], ma, mb, 2), (ma, mb))):
        rows = slice(half * HALF_TM, (half + 1) * HALF_TM)
        xo_ref[rows, :] = xn
        h_next = _rms_rows(xn, ng_ref[2]) * (1.0 + mod[4]) + mod[3]
        ho_ref[rows, :] = h_next
        hb_ref[0, rows, :] = h_next.astype(hb_ref.dtype)
    hb_ref[1] = jnp.zeros((MM_TM, D_MODEL), hb_ref.dtype)


def _merge(oa, ob, ocf, ocb, cg, gates, x, norm_g, mods, diff_w, hgrn_w, w_br, w_o, *, t_rows, seq,
           diff_out_scale):
    m = x.shape[0]
    tm = MM_TM
    row = lambda i: (i, 0)
    f32 = jnp.float32
    return pl.pallas_call(
        functools.partial(_merge_kernel, diff_out_scale=diff_out_scale),
        grid=(m // tm,),
        in_specs=[pl.BlockSpec((tm, BRANCH_W), row), pl.BlockSpec((tm, BRANCH_W), row),
                  pl.BlockSpec((tm, BRANCH_W), row), pl.BlockSpec((tm, BRANCH_W), row),
                  pl.BlockSpec((tm, BRANCH_W), row),
                  pl.BlockSpec((tm, GATE_W), row), pl.BlockSpec((tm, D_MODEL), row),
                  pl.BlockSpec((4, 1, D_MODEL), lambda i: (0, 0, 0))] + _mod_specs(t_rows, seq) + [
                  pl.BlockSpec((1, B_VD), lambda i: (0, 0)), pl.BlockSpec((1, C_DV), lambda i: (0, 0)),
                  pl.BlockSpec((N_BRANCH, BRANCH_W, D_MODEL), lambda i: (0, 0, 0)),
                  pl.BlockSpec((D_MODEL, D_MODEL), lambda i: (0, 0))],
        out_specs=[pl.BlockSpec((tm, D_MODEL), row), pl.BlockSpec((tm, D_MODEL), row),
                   pl.BlockSpec((2, tm, D_MODEL), lambda i: (0, i, 0))],
        out_shape=[jax.ShapeDtypeStruct((m, D_MODEL), f32), jax.ShapeDtypeStruct((m, D_MODEL), f32),
                   jax.ShapeDtypeStruct((2, m, D_MODEL), jnp.bfloat16)],
        compiler_params=_cparams(("arbitrary",)),
        name="branch_merge",
    )(oa, ob, ocf, ocb, cg, gates, x, norm_g, mods, mods, diff_w, hgrn_w, w_br, w_o)


def _moe_kernel(blk_e_ref, n_used_ref, x_ref, wgu_ref, bgu_ref, wdn_ref, bdn_ref, y_ref, wgu_sc, wdn_sc):
    i = pl.program_id(0)
    n_used = n_used_ref[0]
    e = blk_e_ref[i]
    e_prev = blk_e_ref[jnp.maximum(i - 1, 0)]

    @pl.when((i == 0) | (e != e_prev))
    def _():
        wgu_sc[...] = wgu_ref[0, 0].astype(jnp.bfloat16)
        wdn_sc[...] = wdn_ref[0, 0].astype(jnp.bfloat16)

    @pl.when(i < n_used)
    def _():
        gu = jnp.dot(x_ref[...], wgu_sc[...], preferred_element_type=jnp.float32) + bgu_ref[0, 0]
        g = jnp.minimum(gu[:, :EXPERT_FF], SWIGLU_LIMIT)
        u = jnp.clip(gu[:, EXPERT_FF:], -SWIGLU_LIMIT, SWIGLU_LIMIT)
        act = (u + 1.0) * (g * jax.nn.sigmoid(SWIGLU_ALPHA * g))
        y = jnp.dot(act.astype(jnp.bfloat16), wdn_sc[...], preferred_element_type=jnp.float32) + bdn_ref[0, 0]
        y_ref[...] = y.astype(y_ref.dtype)

    @pl.when(i >= n_used)
    def _():
        y_ref[...] = jnp.zeros_like(y_ref)


def _moe_experts(blk_e, n_used, xs, w_gu, b_gu, w_dn, b_dn, *, layer):
    r, dm = xs.shape
    tm = MOE_TM
    n_blk = blk_e.shape[0]
    return pl.pallas_call(
        _moe_kernel,
        grid_spec=pltpu.PrefetchScalarGridSpec(
            num_scalar_prefetch=2, grid=(n_blk,),
            in_specs=[pl.BlockSpec((tm, dm), lambda i, be, nu: (i, 0)),
                      pl.BlockSpec((1, 1, dm, 2 * EXPERT_FF), lambda i, be, nu: (layer, be[i], 0, 0)),
                      pl.BlockSpec((1, 1, 1, 2 * EXPERT_FF), lambda i, be, nu: (layer, be[i], 0, 0)),
                      pl.BlockSpec((1, 1, EXPERT_FF, dm), lambda i, be, nu: (layer, be[i], 0, 0)),
                      pl.BlockSpec((1, 1, 1, dm), lambda i, be, nu: (layer, be[i], 0, 0))],
            out_specs=pl.BlockSpec((tm, dm), lambda i, be, nu: (i, 0)),
            scratch_shapes=[pltpu.VMEM((dm, 2 * EXPERT_FF), jnp.bfloat16),
                            pltpu.VMEM((EXPERT_FF, dm), jnp.bfloat16)]),
        out_shape=jax.ShapeDtypeStruct((r, dm), jnp.bfloat16),
        compiler_params=_cparams(("arbitrary",)),
        name="moe_experts",
    )(blk_e, n_used, xs, w_gu, b_gu.reshape(DEPTH, N_EXPERTS, 1, -1), w_dn,
      b_dn.reshape(DEPTH, N_EXPERTS, 1, -1))


def _combine_kernel(*refs):
    yk_refs = refs[:TOP_K]
    gt_ref, x_ref, ng_ref, ma_ref, mb_ref, xo_ref = refs[TOP_K:]
    gt = gt_ref[...]
    f = None
    for k in range(TOP_K):
        term = gt[:, k:k + 1] * yk_refs[k][...].astype(jnp.float32)
        f = term if f is None else f + term
    for half, xn in enumerate(_residual_tile(x_ref, f, ng_ref[3], ma_ref[0], mb_ref[0], 5)):
        xo_ref[half * HALF_TM:(half + 1) * HALF_TM, :] = xn


def _moe_combine(yk, gates, x, norm_g, mods, *, t_rows, seq):
    m = x.shape[0]
    tm = MM_TM
    n_tiles = m // tm
    row = lambda i: (i, 0)
    choice = lambda k: pl.BlockSpec((tm, D_MODEL), lambda i: (k * n_tiles + i, 0))
    return pl.pallas_call(
        _combine_kernel,
        grid=(n_tiles,),
        in_specs=[choice(k) for k in range(TOP_K)] + [
            pl.BlockSpec((tm, TOP_K), row), pl.BlockSpec((tm, D_MODEL), row),
            pl.BlockSpec((4, 1, D_MODEL), lambda i: (0, 0, 0))] + _mod_specs(t_rows, seq),
        out_specs=pl.BlockSpec((tm, D_MODEL), row),
        out_shape=jax.ShapeDtypeStruct((m, D_MODEL), jnp.float32),
        compiler_params=_cparams(("arbitrary",)),
        name="moe_combine",
    )(*([yk] * TOP_K), gates, x, norm_g, mods, mods)


def _route_kernel(h_ref, wr_ref, br_ref, e_ref, g_ref, rk_ref, cnt_ref, run_sc):
    i = pl.program_id(0)
    tm = h_ref.shape[0]

    @pl.when(i == 0)
    def _():
        run_sc[...] = jnp.zeros_like(run_sc)

    vals = jnp.dot(h_ref[...], wr_ref[...], preferred_element_type=jnp.float32,
                   precision=lax.Precision.HIGHEST) + br_ref[...]
    lane = lax.broadcasted_iota(jnp.int32, (tm, LANES), 1)
    top_v, top_e, picked = [], [], []
    for _ in range(TOP_K):
        m = jnp.max(vals, axis=-1, keepdims=True)
        idx = jnp.min(jnp.where(vals == m, lane, LANES), axis=-1, keepdims=True)
        sel = lane == idx
        top_v.append(m)
        top_e.append(idx)
        picked.append(sel)
        vals = jnp.where(sel, -jnp.inf, vals)
    ex = [jnp.exp(v - top_v[0]) for v in top_v]
    inv_den = 1.0 / (ex[0] + ex[1] + ex[2] + ex[3])
    any_pick = picked[0] | picked[1] | picked[2] | picked[3]
    rr = lax.broadcasted_iota(jnp.int32, (tm, tm), 0)
    cc = lax.broadcasted_iota(jnp.int32, (tm, tm), 1)
    earlier = (cc < rr).astype(jnp.bfloat16)
    before = jnp.dot(earlier, any_pick.astype(jnp.bfloat16), preferred_element_type=jnp.float32) + run_sc[...]
    e_out = jnp.zeros((tm, LANES), jnp.int32)
    g_out = jnp.zeros((tm, LANES), jnp.float32)
    r_out = jnp.zeros((tm, LANES), jnp.int32)
    for k in range(TOP_K):
        rank = jnp.sum(jnp.where(picked[k], before, 0.0), axis=-1, keepdims=True).astype(jnp.int32)
        e_out = jnp.where(lane == k, top_e[k], e_out)
        g_out = jnp.where(lane == k, ex[k] * inv_den, g_out)
        r_out = jnp.where(lane == k, rank, r_out)
    e_ref[...] = e_out
    g_ref[...] = g_out
    rk_ref[...] = r_out
    run_sc[...] += jnp.sum(any_pick.astype(jnp.float32), axis=0, keepdims=True)
    cnt_ref[...] = run_sc[...]


def _route(h, w_router, b_router):
    n, dm = h.shape
    tm = MM_TM
    w_r = jnp.zeros((dm, LANES), jnp.float32).at[:, :N_EXPERTS].set(w_router)
    b_r = jnp.full((1, LANES), NEG_INF, jnp.float32).at[0, :N_EXPERTS].set(b_router)
    row = lambda i: (i, 0)
    whole = lambda i: (0, 0)
    e, g, rk, cnt = pl.pallas_call(
        _route_kernel,
        grid=(n // tm,),
        in_specs=[pl.BlockSpec((tm, dm), row), pl.BlockSpec((dm, LANES), whole), pl.BlockSpec((1, LANES), whole)],
        out_specs=[pl.BlockSpec((tm, LANES), row), pl.BlockSpec((tm, LANES), row), pl.BlockSpec((tm, LANES), row),
                   pl.BlockSpec((1, LANES), whole)],
        out_shape=[jax.ShapeDtypeStruct((n, LANES), jnp.int32), jax.ShapeDtypeStruct((n, LANES), jnp.float32),
                   jax.ShapeDtypeStruct((n, LANES), jnp.int32), jax.ShapeDtypeStruct((1, LANES), jnp.float32)],
        scratch_shapes=[pltpu.VMEM((1, LANES), jnp.float32)],
        compiler_params=_cparams(("arbitrary",)),
        name="router",
    )(h, w_r, b_r)
    return e[:, :TOP_K], g[:, :TOP_K], rk[:, :TOP_K], cnt[0, :N_EXPERTS].astype(jnp.int32)


def _moe_ffn(h, h_rows, x, norm_g, mods, w_router, b_router, w_gu, b_gu, w_dn, b_dn, *, layer, t_rows, seq):
    n, dm = h.shape
    nk = n * TOP_K
    top_e, gates, rank, counts = _route(h, w_router, b_router)
    flat_e = top_e.reshape(-1)
    padded = (counts + MOE_TM - 1) // MOE_TM * MOE_TM
    pad_end = jnp.cumsum(padded)
    pad_start = pad_end - padded
    grp_start = jnp.cumsum(counts) - counts
    dest = (pad_start[top_e] + rank).reshape(-1)
    n_blk = (nk + MOE_TM - 1) // MOE_TM + N_EXPERTS
    rows = n_blk * MOE_TM
    blk_start = jnp.arange(n_blk, dtype=jnp.int32) * MOE_TM
    blk_e = jnp.minimum(jnp.sum((pad_end[None, :] <= blk_start[:, None]).astype(jnp.int32), axis=1),
                        N_EXPERTS - 1)
    n_used = (pad_end[-1] // MOE_TM).astype(jnp.int32).reshape(1)
    order = jnp.argsort(flat_e)
    blk_shift = (grp_start - pad_start)[blk_e]
    sorted_pos = (jnp.arange(rows, dtype=jnp.int32).reshape(n_blk, MOE_TM) + blk_shift[:, None]).reshape(-1)
    row_tok = (order[jnp.clip(sorted_pos, 0, nk - 1)] // TOP_K).astype(jnp.int32)
    row_tok = jnp.where(sorted_pos < nk, row_tok, jnp.arange(rows, dtype=jnp.int32) % n)
    xs = h_rows[row_tok]
    y = _moe_experts(blk_e, n_used, xs, w_gu, b_gu, w_dn, b_dn, layer=layer)
    yk = y[dest.reshape(n, TOP_K).T.reshape(-1)]
    return _moe_combine(yk, gates, x, norm_g, mods, t_rows=t_rows, seq=seq)


def _rope_tables(rows, n_ctx, batch):
    row = jnp.repeat(jnp.arange(rows, dtype=jnp.float32), GRID_W)
    col = jnp.tile(jnp.arange(GRID_W, dtype=jnp.float32), rows)
    n_freq = HEAD_DIM // 4
    inv = ROPE_THETA ** (-jnp.arange(n_freq, dtype=jnp.float32) / n_freq)
    ang_r = row[:, None] * inv
    ang_c = col[:, None] * inv
    cos_h = jnp.concatenate([jnp.cos(ang_r), jnp.cos(ang_r), jnp.cos(ang_c), jnp.cos(ang_c)], axis=1)
    sin_h = jnp.concatenate([-jnp.sin(ang_r), jnp.sin(ang_r), -jnp.sin(ang_c), jnp.sin(ang_c)], axis=1)
    cos_t = jnp.concatenate([cos_h, jnp.ones((n_ctx, HEAD_DIM), jnp.float32)], axis=0)
    sin_t = jnp.concatenate([sin_h, jnp.zeros((n_ctx, HEAD_DIM), jnp.float32)], axis=0)
    reps = (batch, LANES // HEAD_DIM)
    return jnp.tile(cos_t, reps), jnp.tile(sin_t, reps)


def _rope_partner(n_cols):
    idx = jnp.arange(n_cols)
    quarter = HEAD_DIM // 4
    return jnp.where((idx % (2 * quarter)) < quarter, idx + quarter, idx - quarter)


def kernel(x, c, ctx, c_ctx, w_mod, b_mod, norm_g, w_in, attn_sink, diff_lambda, diff_norm_w,
           hgrn_lb_logits, hgrn_norm_w, w_branch, w_out, w_router, b_router, w_gate_up, b_gate_up,
           w_down, b_down):
    b, s, dm = x.shape
    l = ctx.shape[1]
    t = s + l
    f32 = jnp.float32
    bf16 = jnp.bfloat16
    cos_t, sin_t = _rope_tables(s // GRID_W, l, b)
    lb_cum = jnp.cumsum(jax.nn.softmax(hgrn_lb_logits.astype(f32), axis=0), axis=0)
    lower_bounds = lb_cum - lb_cum[0:1]
    cond_rows = 16
    cond = jnp.zeros((cond_rows, dm), f32).at[:b].set(jax.nn.silu(c)).at[b].set(jax.nn.silu(c_ctx))
    scale = HEAD_DIM ** -0.5
    xa = jnp.concatenate([x, ctx], axis=1).reshape(b * t, dm)
    h_moe = None
    for layer in range(DEPTH):
        lam_init = 0.8 - 0.6 * math.exp(-0.3 * layer)
        mod_all = _matmul(cond, w_mod[layer], tm=cond_rows, tn=3 * dm, out_dtype=f32,
                          precision=lax.Precision.HIGHEST, name="adaln") + b_mod[layer]
        mod = mod_all[:b].reshape(b, 1, 6, 1, dm)
        mod_c = jnp.broadcast_to(mod_all[b:b + 1].reshape(1, 1, 6, 1, dm), (b, 1, 6, 1, dm))
        mods = jnp.concatenate([mod, mod_c], axis=1).reshape(2 * b, 6, 1, dm)
        ng = norm_g[layer].reshape(4, 1, dm)
        g0 = ng[0:1]
        geo = dict(t_rows=t, seq=s)

        w = w_in[layer]
        w_aq = (w[:, OFF_A:OFF_A + A_Q_W] * scale).reshape(dm, A_HKV, A_GROUP, HEAD_DIM)
        pad = jnp.zeros((dm, A_GROUP, HEAD_DIM), f32)
        w_aq8 = jnp.concatenate([jnp.concatenate([w_aq[:, 0], pad], axis=-1),
                                 jnp.concatenate([pad, w_aq[:, 1]], axis=-1)], axis=1).reshape(dm, A_Q8_W)
        w_a = jnp.concatenate([w_aq8, w[:, OFF_A + A_Q_W:OFF_A + A_W]], axis=1).astype(bf16)
        w_b = w[:, OFF_B:OFF_B + B_W].at[:, :B_QK_W].multiply(scale * LOG2_E).astype(bf16)
        proj_a = _proj_rope(xa, g0, mods, w_a, w_a[:, _rope_partner(A_ROPE_W2)], cos_t, sin_t,
                            rope_w=A_ROPE_W2, name="proj_a", **geo).reshape(b, t, A_W2)
        bq, bk_hm, bv1_hm = _proj_b(xa, g0, mods, w_b, w_b[:, _rope_partner(B_ROPE_W)], cos_t, sin_t, **geo)
        bq = bq.reshape(b, t, B_QK_W)
        lb = lower_bounds[layer]
        lb_par = jnp.stack([jnp.log(lb[0]), jnp.log1p(-lb[0]), jnp.log(lb[1]), jnp.log1p(-lb[1])])
        cq, ci, lf, cg = _proj_hgrn(xa, g0, mods, w[:, OFF_C:OFF_C + C_W].astype(bf16),
                                    lb_par.reshape(4, 1, C_K_W), **geo)
        gates = _proj_plain(xa, g0, mods, w[:, OFF_GATE:].astype(bf16), name="gate_proj", **geo)

        sink32 = attn_sink[layer].astype(f32)
        o_a = _window_attention(proj_a, sink32, seq=s)

        lp = diff_lambda[layer].astype(f32)
        lam = (jnp.exp(jnp.sum(lp[0] * lp[1])) - jnp.exp(jnp.sum(lp[2] * lp[3])) + lam_init).reshape(1).astype(f32)
        o_b = _diff_attention(lam, bq, bk_hm, bv1_hm, tq=FLASH_TQ, tk=FLASH_TK, q_block_offset=0,
                              n_q_blocks=s // FLASH_TQ, k_block_offset=0, nk=t // FLASH_TK)
        o_bx = _diff_attention(lam, bq, bk_hm, bv1_hm, tq=l, tk=l, q_block_offset=s // l, n_q_blocks=1,
                               k_block_offset=s // l, nk=1)
        o_b = jnp.concatenate([o_b, o_bx], axis=1)

        o_cf, o_cb = _hgrn_scan(cq.reshape(b, t, C_K_W), ci.reshape(b, t, C_V_W), lf.reshape(2, b, t, C_K_W),
                                n_ctx_blocks=l // HGRN_TB)

        col = jnp.arange(A_Q_W)
        head_of_col = ((col % LANES) // HEAD_DIM) * A_GROUP + col // LANES
        w_br = w_branch[layer].at[0].set(w_branch[layer][0][head_of_col * HEAD_DIM + col % HEAD_DIM])
        xa, h_moe, h_rows = _merge(o_a.reshape(b * t, A_Q_W), o_b.reshape(b * t, B_V_W), o_cf.reshape(b * t, C_V_W),
                           o_cb.reshape(b * t, C_V_W), cg,
                           gates, xa, ng, mods, diff_norm_w[layer].reshape(1, B_VD).astype(f32),
                           hgrn_norm_w[layer].reshape(1, C_DV).astype(f32),
                           w_br.astype(bf16), w_out[layer].astype(bf16),
                           diff_out_scale=1 - lam_init, **geo)

        xa = _moe_ffn(h_moe, h_rows.reshape(2 * b * t, dm), xa, ng, mods, w_router[layer], b_router[layer],
                      w_gate_up, b_gate_up, w_down, b_down, layer=layer, **geo)
    return xa.reshape(b, t, dm)[:, :s]
```

```python
import functools
import math

import jax
import jax.numpy as jnp
from jax import lax
from jax.experimental import pallas as pl
from jax.experimental.pallas import tpu as pltpu

D_MODEL = 1024
DEPTH = 2
GRID_W = 64
HEAD_DIM = 64
ROPE_THETA = 10000.0
RMS_EPS = 1e-6
NEG_INF = -1e30
BLOCK = 128
A_HQ = 8
A_HKV = 2
A_GROUP = A_HQ // A_HKV
WINDOW = 128
B_H = 4
B_HD = HEAD_DIM
B_VD = 2 * HEAD_DIM
C_H = 4
C_DK = 128
C_DV = 128
N_BRANCH = 3
BRANCH_W = 512
A_Q_W = A_HQ * HEAD_DIM
A_KV_W = A_HKV * HEAD_DIM
B_QK_W = B_H * 2 * B_HD
B_V_W = B_H * B_VD
C_K_W = C_H * C_DK
C_V_W = C_H * C_DV
GATE_W = N_BRANCH * D_MODEL
N_EXPERTS = 32
TOP_K = 4
EXPERT_FF = D_MODEL
SWIGLU_LIMIT = 7.0
SWIGLU_ALPHA = 1.702

A_W = A_Q_W + 2 * A_KV_W
A_ROPE_W = A_Q_W + A_KV_W
B_W = 2 * B_QK_W + B_V_W
B_ROPE_W = 2 * B_QK_W
C_W = 3 * C_K_W + 2 * C_V_W
OFF_A = 0
OFF_B = OFF_A + A_W
OFF_C = OFF_B + B_W
OFF_GATE = OFF_C + C_W

V7X_VMEM_LIMIT_BYTES = 56 * 1024 * 1024
LANES = 128
LOG2_E = math.log2(math.e)

MM_TM = 512
HALF_TM = MM_TM // 2
FLASH_TQ = 2048
FLASH_TK = 768
HGRN_TB = 256
HGRN_SUB = 16
HGRN_UNROLL = 4
MOE_TM = 512


def _cparams(sem):
    return pltpu.CompilerParams(dimension_semantics=sem, vmem_limit_bytes=V7X_VMEM_LIMIT_BYTES)


def _rms_rows(x, g):
    return (x * lax.rsqrt(jnp.mean(x * x, axis=-1, keepdims=True) + RMS_EPS)) * g


def _modulated_tile(x_ref, g, mod_a, mod_b, shift_slot, scale_slot):
    out = []
    for half, mod in enumerate((mod_a, mod_b)):
        x = x_ref[half * HALF_TM:(half + 1) * HALF_TM, :]
        out.append(_rms_rows(x, g) * (1.0 + mod[scale_slot]) + mod[shift_slot])
    return jnp.concatenate(out, axis=0)


def _residual_tile(x_ref, y, g, mod_a, mod_b, gate_slot):
    out = []
    for half, mod in enumerate((mod_a, mod_b)):
        rows = slice(half * HALF_TM, (half + 1) * HALF_TM)
        out.append(x_ref[rows, :] + mod[gate_slot] * _rms_rows(y[rows, :], g))
    return out


def _mod_specs(t_rows, seq):
    halves_per_batch = t_rows // HALF_TM
    lat_halves = seq // HALF_TM

    def row(u):
        return (u // halves_per_batch) * 2 + jnp.where(u % halves_per_batch >= lat_halves, 1, 0)

    blk = (1, 6, 1, D_MODEL)
    return [pl.BlockSpec(blk, lambda i: (row(2 * i), 0, 0, 0)),
            pl.BlockSpec(blk, lambda i: (row(2 * i + 1), 0, 0, 0))]


def _proj_rope_kernel(x_ref, g_ref, ma_ref, mb_ref, w_ref, wsw_ref, cos_ref, sin_ref, o_ref, *, rope_w):
    h = _modulated_tile(x_ref, g_ref[0], ma_ref[0], mb_ref[0], 0, 1).astype(jnp.bfloat16)
    z = jnp.dot(h, w_ref[...], preferred_element_type=jnp.float32)
    zs = jnp.dot(h, wsw_ref[...], preferred_element_type=jnp.float32)
    reps = rope_w // LANES
    rot = z[:, :rope_w] * jnp.tile(cos_ref[...], (1, reps)) + zs * jnp.tile(sin_ref[...], (1, reps))
    o_ref[:, :rope_w] = rot.astype(o_ref.dtype)
    o_ref[:, rope_w:] = z[:, rope_w:].astype(o_ref.dtype)


def _proj_b_kernel(x_ref, g_ref, ma_ref, mb_ref, w_ref, wsw_ref, cos_ref, sin_ref, q_ref, k_ref, v1_ref):
    h = _modulated_tile(x_ref, g_ref[0], ma_ref[0], mb_ref[0], 0, 1).astype(jnp.bfloat16)
    z = jnp.dot(h, w_ref[...], preferred_element_type=jnp.float32)
    zs = jnp.dot(h, wsw_ref[...], preferred_element_type=jnp.float32)
    reps = B_ROPE_W // LANES
    rot = z[:, :B_ROPE_W] * jnp.tile(cos_ref[...], (1, reps)) + zs * jnp.tile(sin_ref[...], (1, reps))
    q_ref[...] = rot[:, :B_QK_W].astype(q_ref.dtype)
    for hh in range(B_H):
        k_ref[hh] = rot[:, B_QK_W + hh * 2 * B_HD:B_QK_W + (hh + 1) * 2 * B_HD].astype(k_ref.dtype)
        v1_ref[hh, :, :B_VD] = z[:, B_ROPE_W + hh * B_VD:B_ROPE_W + (hh + 1) * B_VD].astype(v1_ref.dtype)
        v1_ref[hh, :, B_VD:] = jnp.ones((MM_TM, B_VD), v1_ref.dtype)


def _proj_plain_kernel(x_ref, g_ref, ma_ref, mb_ref, w_ref, o_ref):
    h = _modulated_tile(x_ref, g_ref[0], ma_ref[0], mb_ref[0], 0, 1).astype(jnp.bfloat16)
    o_ref[...] = jnp.dot(h, w_ref[...], preferred_element_type=jnp.float32).astype(o_ref.dtype)


def _log_forget(z, log_lb, log_1m_lb):
    log_sig = jnp.minimum(z, 0.0) - jnp.log(1.0 + jnp.exp(-jnp.abs(z)))
    c = log_1m_lb + log_sig
    return jnp.maximum(log_lb, c) + jnp.log(1.0 + jnp.exp(-jnp.abs(log_lb - c)))


def _proj_hgrn_kernel(x_ref, g_ref, ma_ref, mb_ref, w_ref, lb_ref, q_ref, v_ref, lf_ref, cg_ref):
    h = _modulated_tile(x_ref, g_ref[0], ma_ref[0], mb_ref[0], 0, 1).astype(jnp.bfloat16)
    z = jnp.dot(h, w_ref[...], preferred_element_type=jnp.float32)
    zq = z[:, :C_K_W]
    q_ref[...] = zq * jax.nn.sigmoid(zq)
    v_ref[...] = z[:, C_K_W:C_K_W + C_V_W]
    off = C_K_W + C_V_W
    lf_ref[0] = _log_forget(z[:, off:off + C_K_W], lb_ref[0], lb_ref[1])
    lf_ref[1] = _log_forget(z[:, off + C_K_W:off + 2 * C_K_W], lb_ref[2], lb_ref[3])
    cg_ref[...] = z[:, off + 2 * C_K_W:].astype(cg_ref.dtype)


def _proj_common_specs(t_rows, seq):
    row = lambda i: (i, 0)
    return [pl.BlockSpec((MM_TM, D_MODEL), row),
            pl.BlockSpec((1, 1, D_MODEL), lambda i: (0, 0, 0))] + _mod_specs(t_rows, seq)


def _proj_rope(x, g, mods, w, wsw, cos, sin, *, t_rows, seq, rope_w, name):
    m = x.shape[0]
    n = w.shape[1]
    row = lambda i: (i, 0)
    whole = lambda i: (0, 0)
    return pl.pallas_call(
        functools.partial(_proj_rope_kernel, rope_w=rope_w),
        grid=(m // MM_TM,),
        in_specs=_proj_common_specs(t_rows, seq) + [
            pl.BlockSpec((D_MODEL, n), whole), pl.BlockSpec((D_MODEL, rope_w), whole),
            pl.BlockSpec((MM_TM, LANES), row), pl.BlockSpec((MM_TM, LANES), row)],
        out_specs=pl.BlockSpec((MM_TM, n), row),
        out_shape=jax.ShapeDtypeStruct((m, n), jnp.bfloat16),
        compiler_params=_cparams(("arbitrary",)),
        name=name,
    )(x, g, mods, mods, w, wsw, cos, sin)


def _proj_b(x, g, mods, w, wsw, cos, sin, *, t_rows, seq):
    m = x.shape[0]
    row = lambda i: (i, 0)
    whole = lambda i: (0, 0)
    heads = lambda i: (0, i, 0)
    bf16 = jnp.bfloat16
    return pl.pallas_call(
        _proj_b_kernel,
        grid=(m // MM_TM,),
        in_specs=_proj_common_specs(t_rows, seq) + [
            pl.BlockSpec((D_MODEL, B_W), whole), pl.BlockSpec((D_MODEL, B_ROPE_W), whole),
            pl.BlockSpec((MM_TM, LANES), row), pl.BlockSpec((MM_TM, LANES), row)],
        out_specs=[pl.BlockSpec((MM_TM, B_QK_W), row), pl.BlockSpec((B_H, MM_TM, 2 * B_HD), heads),
                   pl.BlockSpec((B_H, MM_TM, 2 * B_VD), heads)],
        out_shape=[jax.ShapeDtypeStruct((m, B_QK_W), bf16), jax.ShapeDtypeStruct((B_H, m, 2 * B_HD), bf16),
                   jax.ShapeDtypeStruct((B_H, m, 2 * B_VD), bf16)],
        compiler_params=_cparams(("arbitrary",)),
        name="proj_b",
    )(x, g, mods, mods, w, wsw, cos, sin)


def _proj_plain(x, g, mods, w, *, t_rows, seq, name):
    m = x.shape[0]
    n = w.shape[1]
    row = lambda i: (i, 0)
    return pl.pallas_call(
        _proj_plain_kernel,
        grid=(m // MM_TM,),
        in_specs=_proj_common_specs(t_rows, seq) + [pl.BlockSpec((D_MODEL, n), lambda i: (0, 0))],
        out_specs=pl.BlockSpec((MM_TM, n), row),
        out_shape=jax.ShapeDtypeStruct((m, n), jnp.bfloat16),
        compiler_params=_cparams(("arbitrary",)),
        name=name,
    )(x, g, mods, mods, w)


def _proj_hgrn(x, g, mods, w, lb_par, *, t_rows, seq):
    m = x.shape[0]
    row = lambda i: (i, 0)
    f32 = jnp.float32
    return pl.pallas_call(
        _proj_hgrn_kernel,
        grid=(m // MM_TM,),
        in_specs=_proj_common_specs(t_rows, seq) + [
            pl.BlockSpec((D_MODEL, C_W), lambda i: (0, 0)),
            pl.BlockSpec((4, 1, C_K_W), lambda i: (0, 0, 0))],
        out_specs=[pl.BlockSpec((MM_TM, C_K_W), row), pl.BlockSpec((MM_TM, C_V_W), row),
                   pl.BlockSpec((2, MM_TM, C_K_W), lambda i: (0, i, 0)), pl.BlockSpec((MM_TM, C_V_W), row)],
        out_shape=[jax.ShapeDtypeStruct((m, C_K_W), f32), jax.ShapeDtypeStruct((m, C_V_W), f32),
                   jax.ShapeDtypeStruct((2, m, C_K_W), f32), jax.ShapeDtypeStruct((m, C_V_W), jnp.bfloat16)],
        compiler_params=_cparams(("arbitrary",)),
        name="hgrn_proj",
    )(x, g, mods, mods, w, lb_par)


def _mm_kernel(x_ref, w_ref, o_ref, *, precision):
    o_ref[...] = jnp.dot(x_ref[...], w_ref[...], preferred_element_type=jnp.float32,
                         precision=precision).astype(o_ref.dtype)


def _matmul(x, w, *, tm, tn, out_dtype, precision=None, name):
    m, k = x.shape
    n = w.shape[1]
    assert m % tm == 0 and n % tn == 0, (m, tm, n, tn)
    return pl.pallas_call(
        functools.partial(_mm_kernel, precision=precision),
        grid=(n // tn, m // tm),
        in_specs=[pl.BlockSpec((tm, k), lambda j, i: (i, 0)),
                  pl.BlockSpec((k, tn), lambda j, i: (0, j))],
        out_specs=pl.BlockSpec((tm, tn), lambda j, i: (i, j)),
        out_shape=jax.ShapeDtypeStruct((m, n), out_dtype),
        compiler_params=_cparams(("arbitrary", "arbitrary")),
        name=name,
    )(x, w)


A_Q8_W = A_HQ * LANES
A_W2 = A_Q8_W + 2 * A_KV_W
A_ROPE_W2 = A_Q8_W + A_KV_W
A_K_BLK = A_Q8_W // A_KV_W
A_V_BLK = A_K_BLK + 1
A_STEP_BLOCKS = 2


def _attn_a_kernel(sink_ref, q_ref, kp_ref, kc_ref, kn_ref, vp_ref, vc_ref, vn_ref, kx_ref, vx_ref,
                   o_ref, *, seq):
    n = pl.program_id(1)
    k_loc = jnp.concatenate([kp_ref[0], kc_ref[0], kn_ref[0]], axis=0)
    v_loc = jnp.concatenate([vp_ref[0], vc_ref[0], vn_ref[0]], axis=0)
    n_keys = 3 * BLOCK + kx_ref.shape[1]
    qi = lax.broadcasted_iota(jnp.int32, (BLOCK, n_keys), 0)
    kj = lax.broadcasted_iota(jnp.int32, (BLOCK, n_keys), 1)
    lane = lax.broadcasted_iota(jnp.int32, (BLOCK, LANES), 1)
    for j in range(A_STEP_BLOCKS):
        blk = n * A_STEP_BLOCKS + j
        rows = slice(j * BLOCK, (j + 1) * BLOCK)
        k_all = jnp.concatenate([k_loc[j * BLOCK:(j + 3) * BLOCK, :], kx_ref[0]], axis=0)
        v_all = jnp.concatenate([v_loc[j * BLOCK:(j + 3) * BLOCK, :], vx_ref[0]], axis=0)
        kpos = blk * BLOCK + kj - BLOCK
        local_ok = (jnp.abs(kj - BLOCK - qi) <= WINDOW) & (kpos >= 0) & (kpos < seq) & (blk * BLOCK < seq)
        mask = local_ok | (kj >= 3 * BLOCK)
        q8 = jnp.concatenate([q_ref[0, rows, hq * LANES:(hq + 1) * LANES] for hq in range(A_HQ)], axis=0)
        s = lax.dot_general(q8, k_all, (((1,), (1,)), ((), ())), preferred_element_type=jnp.float32)
        p_blocks, inv_l = [], []
        for hq in range(A_HQ):
            sh = jnp.where(mask, s[hq * BLOCK:(hq + 1) * BLOCK, :], NEG_INF)
            sk = sink_ref[hq]
            m = jnp.maximum(jnp.max(sh, axis=-1, keepdims=True), sk)
            p = jnp.exp(sh - m)
            inv_l.append(1.0 / (jnp.sum(p, axis=-1, keepdims=True) + jnp.exp(sk - m)))
            p_blocks.append(p.astype(v_all.dtype))
        o = jnp.dot(jnp.concatenate(p_blocks, axis=0), v_all, preferred_element_type=jnp.float32)
        for g in range(A_GROUP):
            lo = o[g * BLOCK:(g + 1) * BLOCK, :] * inv_l[g]
            hi = o[(A_GROUP + g) * BLOCK:(A_GROUP + g + 1) * BLOCK, :] * inv_l[A_GROUP + g]
            o_ref[0, rows, g * LANES:(g + 1) * LANES] = jnp.where(lane < HEAD_DIM, lo, hi).astype(o_ref.dtype)


def _window_attention(proj_a, sink, *, seq):
    b, t, _ = proj_a.shape
    l = t - seq
    step = A_STEP_BLOCKS * BLOCK
    nb = seq // BLOCK
    ns = seq // step
    ctx_blk = seq // l
    q_map = lambda bi, n, *_: (bi, n, 0)
    edge = lambda col, fn: pl.BlockSpec((1, BLOCK, A_KV_W), lambda bi, n, *_: (bi, fn(n), col))
    mid = lambda col: pl.BlockSpec((1, step, A_KV_W), lambda bi, n, *_: (bi, jnp.minimum(n, ns - 1), col))
    prev = lambda n: jnp.clip(n * A_STEP_BLOCKS - 1, 0, nb - 1)
    nxt = lambda n: jnp.minimum((n + 1) * A_STEP_BLOCKS, nb - 1)
    ctx = lambda col: pl.BlockSpec((1, l, A_KV_W), lambda bi, n, *_: (bi, ctx_blk, col))
    return pl.pallas_call(
        functools.partial(_attn_a_kernel, seq=seq),
        grid_spec=pltpu.PrefetchScalarGridSpec(
            num_scalar_prefetch=1, grid=(b, t // step),
            in_specs=[pl.BlockSpec((1, step, A_Q8_W), q_map),
                      edge(A_K_BLK, prev), mid(A_K_BLK), edge(A_K_BLK, nxt),
                      edge(A_V_BLK, prev), mid(A_V_BLK), edge(A_V_BLK, nxt),
                      ctx(A_K_BLK), ctx(A_V_BLK)],
            out_specs=pl.BlockSpec((1, step, A_Q_W), q_map)),
        out_shape=jax.ShapeDtypeStruct((b, t, A_Q_W), jnp.bfloat16),
        compiler_params=_cparams(("arbitrary", "arbitrary")),
        name="window_gqa",
    )(sink, *([proj_a] * 9))


def _diff_flash_kernel(lam_ref, q_ref, k_ref, v_ref, o_ref, q2_sc, sa_sc, sb_sc, m_sc, acc_sc, *, tq, tk, nk):
    u = pl.program_id(2)
    new_q = u % nk == 0

    @pl.when(u == 0)
    def _():
        m_sc[...] = jnp.full_like(m_sc, NEG_INF)
        acc_sc[...] = jnp.zeros_like(acc_sc)
        sb_sc[...] = jnp.full_like(sb_sc, NEG_INF)

    @pl.when(new_q)
    def _():
        q = q_ref[0]
        lane = lax.broadcasted_iota(jnp.int32, q.shape, 1)
        q2_sc[:tq, :] = jnp.where(lane < B_HD, q, jnp.zeros_like(q))
        q2_sc[tq:, :] = jnp.where(lane >= B_HD, q, jnp.zeros_like(q))

    def step(s_new_ref, s_old_ref):
        m_prev = m_sc[...]
        m_new = jnp.maximum(m_prev, jnp.max(s_old_ref[...], axis=-1, keepdims=True))
        alpha = jnp.exp2(m_prev - m_new)
        p = jnp.exp2(s_old_ref[...] - jnp.tile(m_new, (1, tk // LANES))).astype(jnp.bfloat16)
        acc_sc[...] = jnp.tile(alpha, (1, 2)) * acc_sc[...] + jnp.dot(
            p, v_ref[0], preferred_element_type=jnp.float32)
        m_sc[...] = m_new
        s_new_ref[...] = lax.dot_general(q2_sc[...], k_ref[0], (((1,), (1,)), ((), ())),
                                         preferred_element_type=jnp.float32)

    @pl.when(u % 2 == 0)
    def _():
        step(sa_sc, sb_sc)

    @pl.when(u % 2 == 1)
    def _():
        step(sb_sc, sa_sc)

    @pl.when(new_q & (u > 0))
    def _():
        acc = acc_sc[...]
        o = acc[:, :B_VD] / acc[:, B_VD:]
        o_ref[0] = (o[:tq, :] - lam_ref[0] * o[tq:, :]).astype(o_ref.dtype)
        m_sc[...] = jnp.full_like(m_sc, NEG_INF)
        acc_sc[...] = jnp.zeros_like(acc_sc)


def _diff_attention(lam, q, k_hm, v1_hm, *, tq, tk, q_block_offset, n_q_blocks, k_block_offset, nk):
    b, t, _ = q.shape
    kb = t // tk
    n_steps = n_q_blocks * nk + 1
    k_blk = lambda bi, u: bi * kb + k_block_offset + u % nk
    in_specs = [pl.BlockSpec((1, tq, 2 * B_HD),
                             lambda bi, h, u, *_: (bi, jnp.minimum(u // nk, n_q_blocks - 1) + q_block_offset, h)),
                pl.BlockSpec((1, tk, 2 * B_HD), lambda bi, h, u, *_: (h, k_blk(bi, u), 0)),
                pl.BlockSpec((1, tk, 2 * B_VD), lambda bi, h, u, *_: (h, k_blk(bi, jnp.maximum(u - 1, 0)), 0))]
    return pl.pallas_call(
        functools.partial(_diff_flash_kernel, tq=tq, tk=tk, nk=nk),
        grid_spec=pltpu.PrefetchScalarGridSpec(
            num_scalar_prefetch=1, grid=(b, B_H, n_steps),
            in_specs=in_specs,
            out_specs=pl.BlockSpec((1, tq, B_VD), lambda bi, h, u, *_: (bi, jnp.maximum(u - 1, 0) // nk, h)),
            scratch_shapes=[pltpu.VMEM((2 * tq, 2 * B_HD), jnp.bfloat16),
                            pltpu.VMEM((2 * tq, tk), jnp.float32),
                            pltpu.VMEM((2 * tq, tk), jnp.float32),
                            pltpu.VMEM((2 * tq, LANES), jnp.float32),
                            pltpu.VMEM((2 * tq, 2 * B_VD), jnp.float32)]),
        out_shape=jax.ShapeDtypeStruct((b, n_q_blocks * tq, B_V_W), jnp.float32),
        compiler_params=_cparams(("arbitrary", "arbitrary", "arbitrary")),
        name="diff_flash",
    )(lam, q, k_hm, v1_hm)


def _hgrn_sub_chunk(q_ref, v_ref, lf_ref, o_ref, st_ref, r0, tri, rev):
    half = HGRN_SUB // 2
    row8 = lax.broadcasted_iota(jnp.int32, (half, C_DK), 0)
    lf = lf_ref[pl.ds(r0, HGRN_SUB), :] * LOG2_E
    lf_a = lf.astype(jnp.bfloat16)
    rem = lf - lf_a.astype(jnp.float32)
    lf_b = rem.astype(jnp.bfloat16)
    lf_c = (rem - lf_b.astype(jnp.float32)).astype(jnp.bfloat16)
    bl = (jnp.dot(tri, lf_a, preferred_element_type=jnp.float32)
          + jnp.dot(tri, lf_b, preferred_element_type=jnp.float32)
          + jnp.dot(tri, lf_c, preferred_element_type=jnp.float32))
    q = q_ref[pl.ds(r0, HGRN_SUB), :]
    v = v_ref[pl.ds(r0, HGRN_SUB), :]
    last = 0 if rev else HGRN_SUB - 1
    for hh in range(C_H):
        sl = slice(hh * C_DK, (hh + 1) * C_DK)
        lfh, blh, qh, vh = lf[:, sl], bl[:, sl], q[:, sl], v[:, sl]
        tot = blh[last:last + 1, :]
        kh = 1.0 - jnp.exp2(lfh)
        qt = qh * jnp.exp2(blh)
        kt = kh * jnp.exp2(tot - blh)
        st = st_ref[hh]
        o_sub = lax.dot_general(qt.astype(jnp.bfloat16), st.astype(jnp.bfloat16),
                                (((1,), (1,)), ((), ())), preferred_element_type=jnp.float32)
        parts = [o_sub[:half, :], o_sub[half:, :]]
        for ss in range(HGRN_SUB):
            ks = kh[ss:ss + 1, :]
            bs = blh[ss:ss + 1, :]
            vs = vh[ss:ss + 1, :]
            for p in range(2):
                lo = p * half
                if rev:
                    reached = lo <= ss
                    full = lo + half - 1 <= ss
                    ok = row8 + lo <= ss
                else:
                    reached = lo + half - 1 >= ss
                    full = lo >= ss
                    ok = row8 + lo >= ss
                if not reached:
                    continue
                diff = blh[lo:lo + half, :] - bs
                if not full:
                    diff = jnp.where(ok, diff, NEG_INF)
                col = jnp.sum((qh[lo:lo + half, :] * ks) * jnp.exp2(diff), axis=1, keepdims=True)
                parts[p] = parts[p] + col * vs
        o_ref[pl.ds(r0, half), sl] = parts[0]
        o_ref[pl.ds(r0 + half, half), sl] = parts[1]
        upd = lax.dot_general(vh.astype(jnp.bfloat16), kt.astype(jnp.bfloat16),
                              (((0,), (0,)), ((), ())), preferred_element_type=jnp.float32)
        st_ref[hh] = jnp.exp2(tot) * st + upd


def _hgrn_kernel(qf_ref, vf_ref, lff_ref, qb_ref, vb_ref, lfb_ref, of_ref, ob_ref, st_ref, *, tb):
    t = pl.program_id(1)
    nsub = tb // HGRN_SUB

    @pl.when(t == 0)
    def _():
        st_ref[...] = jnp.zeros_like(st_ref)

    rr = lax.broadcasted_iota(jnp.int32, (HGRN_SUB, HGRN_SUB), 0)
    cc = lax.broadcasted_iota(jnp.int32, (HGRN_SUB, HGRN_SUB), 1)
    tri_f = (cc <= rr).astype(jnp.bfloat16)
    tri_b = (cc >= rr).astype(jnp.bfloat16)

    def body(i, carry):
        rf = pl.multiple_of(i * HGRN_SUB, HGRN_SUB)
        rb = pl.multiple_of((nsub - 1 - i) * HGRN_SUB, HGRN_SUB)
        _hgrn_sub_chunk(qf_ref.at[0], vf_ref.at[0], lff_ref.at[0, 0], of_ref.at[0], st_ref.at[0], rf, tri_f, False)
        _hgrn_sub_chunk(qb_ref.at[0], vb_ref.at[0], lfb_ref.at[0, 0], ob_ref.at[0], st_ref.at[1], rb, tri_b, True)
        return carry

    lax.fori_loop(0, nsub, body, 0, unroll=HGRN_UNROLL)


def _hgrn_scan(q, v, lf, *, n_ctx_blocks):
    b, t, w = q.shape
    tb = HGRN_TB
    nblk = t // tb
    nlat = nblk - n_ctx_blocks

    def fwd_block(ti):
        return jnp.where(ti < n_ctx_blocks, nlat + ti, ti - n_ctx_blocks)

    def bwd_block(ti):
        return nblk - 1 - ti

    rows = lambda blk: pl.BlockSpec((1, tb, w), lambda bi, ti: (bi, blk(ti), 0))
    gate = lambda di, blk: pl.BlockSpec((1, 1, tb, w), lambda bi, ti: (di, bi, blk(ti), 0))
    out = jax.ShapeDtypeStruct((b, t, w), jnp.float32)
    return pl.pallas_call(
        functools.partial(_hgrn_kernel, tb=tb),
        grid=(b, nblk),
        in_specs=[rows(fwd_block), rows(fwd_block), gate(0, fwd_block),
                  rows(bwd_block), rows(bwd_block), gate(1, bwd_block)],
        out_specs=[rows(fwd_block), rows(bwd_block)],
        out_shape=[out, out],
        scratch_shapes=[pltpu.VMEM((2, C_H, C_DV, C_DK), jnp.float32)],
        compiler_params=_cparams(("arbitrary", "arbitrary")),
        name="hgrn2_scan",
    )(q, v, lf, q, v, lf)


def _head_rms(x, w, n_heads, width):
    out = []
    for hh in range(n_heads):
        blk = x[:, hh * width:(hh + 1) * width]
        out.append(_rms_rows(blk, w))
    return jnp.concatenate(out, axis=1)


def _merge_kernel(oa_ref, ob_ref, ocf_ref, ocb_ref, cg_ref, gt_ref, x_ref, ng_ref, ma_ref, mb_ref, dw_ref, hw_ref,
                  wbr_ref, wo_ref, xo_ref, ho_ref, hb_ref, *, diff_out_scale):
    ob = (_head_rms(ob_ref[...], dw_ref[...], B_H, B_VD) * diff_out_scale).astype(jnp.bfloat16)
    cg = cg_ref[...].astype(jnp.float32)
    oc = (_head_rms(ocf_ref[...] + ocb_ref[...], hw_ref[...], C_H, C_DV) * (cg * jax.nn.sigmoid(cg)))
    acc = None
    for i, o in enumerate((oa_ref[...], ob, oc.astype(jnp.bfloat16))):
        z = jnp.dot(o, wbr_ref[i], preferred_element_type=jnp.float32)
        g = jax.nn.sigmoid(gt_ref[:, i * D_MODEL:(i + 1) * D_MODEL].astype(jnp.float32))
        acc = g * z if acc is None else acc + g * z
    y = jnp.dot(acc.astype(jnp.bfloat16), wo_ref[...], preferred_element_type=jnp.float32)
    ma, mb = ma_ref[0], mb_ref[0]
    for half, (xn, mod) in enumerate(zip(_residual_tile(x_ref, y, ng_ref[1], ma, mb, 2), (ma, mb))):
        rows = slice(half * HALF_TM, (half + 1) * HALF_TM)
        xo_ref[rows, :] = xn
        h_next = _rms_rows(xn, ng_ref[2]) * (1.0 + mod[4]) + mod[3]
        ho_ref[rows, :] = h_next
        hb_ref[0, rows, :] = h_next.astype(hb_ref.dtype)
    hb_ref[1] = jnp.zeros((MM_TM, D_MODEL), hb_ref.dtype)


def _merge(oa, ob, ocf, ocb, cg, gates, x, norm_g, mods, diff_w, hgrn_w, w_br, w_o, *, t_rows, seq,
           diff_out_scale):
    m = x.shape[0]
    tm = MM_TM
    row = lambda i: (i, 0)
    f32 = jnp.float32
    return pl.pallas_call(
        functools.partial(_merge_kernel, diff_out_scale=diff_out_scale),
        grid=(m // tm,),
        in_specs=[pl.BlockSpec((tm, BRANCH_W), row), pl.BlockSpec((tm, BRANCH_W), row),
                  pl.BlockSpec((tm, BRANCH_W), row), pl.BlockSpec((tm, BRANCH_W), row),
                  pl.BlockSpec((tm, BRANCH_W), row),
                  pl.BlockSpec((tm, GATE_W), row), pl.BlockSpec((tm, D_MODEL), row),
                  pl.BlockSpec((4, 1, D_MODEL), lambda i: (0, 0, 0))] + _mod_specs(t_rows, seq) + [
                  pl.BlockSpec((1, B_VD), lambda i: (0, 0)), pl.BlockSpec((1, C_DV), lambda i: (0, 0)),
                  pl.BlockSpec((N_BRANCH, BRANCH_W, D_MODEL), lambda i: (0, 0, 0)),
                  pl.BlockSpec((D_MODEL, D_MODEL), lambda i: (0, 0))],
        out_specs=[pl.BlockSpec((tm, D_MODEL), row), pl.BlockSpec((tm, D_MODEL), row),
                   pl.BlockSpec((2, tm, D_MODEL), lambda i: (0, i, 0))],
        out_shape=[jax.ShapeDtypeStruct((m, D_MODEL), f32), jax.ShapeDtypeStruct((m, D_MODEL), f32),
                   jax.ShapeDtypeStruct((2, m, D_MODEL), jnp.bfloat16)],
        compiler_params=_cparams(("arbitrary",)),
        name="branch_merge",
    )(oa, ob, ocf, ocb, cg, gates, x, norm_g, mods, mods, diff_w, hgrn_w, w_br, w_o)


def _moe_kernel(blk_e_ref, n_used_ref, x_ref, wgu_ref, bgu_ref, wdn_ref, bdn_ref, y_ref, wgu_sc, wdn_sc):
    i = pl.program_id(0)
    n_used = n_used_ref[0]
    e = blk_e_ref[i]
    e_prev = blk_e_ref[jnp.maximum(i - 1, 0)]

    @pl.when((i == 0) | (e != e_prev))
    def _():
        wgu_sc[...] = wgu_ref[0, 0].astype(jnp.bfloat16)
        wdn_sc[...] = wdn_ref[0, 0].astype(jnp.bfloat16)

    @pl.when(i < n_used)
    def _():
        gu = jnp.dot(x_ref[...], wgu_sc[...], preferred_element_type=jnp.float32) + bgu_ref[0, 0]
        g = jnp.minimum(gu[:, :EXPERT_FF], SWIGLU_LIMIT)
        u = jnp.clip(gu[:, EXPERT_FF:], -SWIGLU_LIMIT, SWIGLU_LIMIT)
        act = (u + 1.0) * (g * jax.nn.sigmoid(SWIGLU_ALPHA * g))
        y = jnp.dot(act.astype(jnp.bfloat16), wdn_sc[...], preferred_element_type=jnp.float32) + bdn_ref[0, 0]
        y_ref[...] = y.astype(y_ref.dtype)

    @pl.when(i >= n_used)
    def _():
        y_ref[...] = jnp.zeros_like(y_ref)


def _moe_experts(blk_e, n_used, xs, w_gu, b_gu, w_dn, b_dn, *, layer):
    r, dm = xs.shape
    tm = MOE_TM
    n_blk = blk_e.shape[0]
    return pl.pallas_call(
        _moe_kernel,
        grid_spec=pltpu.PrefetchScalarGridSpec(
            num_scalar_prefetch=2, grid=(n_blk,),
            in_specs=[pl.BlockSpec((tm, dm), lambda i, be, nu: (i, 0)),
                      pl.BlockSpec((1, 1, dm, 2 * EXPERT_FF), lambda i, be, nu: (layer, be[i], 0, 0)),
                      pl.BlockSpec((1, 1, 1, 2 * EXPERT_FF), lambda i, be, nu: (layer, be[i], 0, 0)),
                      pl.BlockSpec((1, 1, EXPERT_FF, dm), lambda i, be, nu: (layer, be[i], 0, 0)),
                      pl.BlockSpec((1, 1, 1, dm), lambda i, be, nu: (layer, be[i], 0, 0))],
            out_specs=pl.BlockSpec((tm, dm), lambda i, be, nu: (i, 0)),
            scratch_shapes=[pltpu.VMEM((dm, 2 * EXPERT_FF), jnp.bfloat16),
                            pltpu.VMEM((EXPERT_FF, dm), jnp.bfloat16)]),
        out_shape=jax.ShapeDtypeStruct((r, dm), jnp.bfloat16),
        compiler_params=_cparams(("arbitrary",)),
        name="moe_experts",
    )(blk_e, n_used, xs, w_gu, b_gu.reshape(DEPTH, N_EXPERTS, 1, -1), w_dn,
      b_dn.reshape(DEPTH, N_EXPERTS, 1, -1))


def _combine_kernel(*refs):
    yk_refs = refs[:TOP_K]
    gt_ref, x_ref, ng_ref, ma_ref, mb_ref, xo_ref = refs[TOP_K:]
    gt = gt_ref[...]
    f = None
    for k in range(TOP_K):
        term = gt[:, k:k + 1] * yk_refs[k][...].astype(jnp.float32)
        f = term if f is None else f + term
    for half, xn in enumerate(_residual_tile(x_ref, f, ng_ref[3], ma_ref[0], mb_ref[0], 5)):
        xo_ref[half * HALF_TM:(half + 1) * HALF_TM, :] = xn


def _moe_combine(yk, gates, x, norm_g, mods, *, t_rows, seq):
    m = x.shape[0]
    tm = MM_TM
    n_tiles = m // tm
    row = lambda i: (i, 0)
    choice = lambda k: pl.BlockSpec((tm, D_MODEL), lambda i: (k * n_tiles + i, 0))
    return pl.pallas_call(
        _combine_kernel,
        grid=(n_tiles,),
        in_specs=[choice(k) for k in range(TOP_K)] + [
            pl.BlockSpec((tm, TOP_K), row), pl.BlockSpec((tm, D_MODEL), row),
            pl.BlockSpec((4, 1, D_MODEL), lambda i: (0, 0, 0))] + _mod_specs(t_rows, seq),
        out_specs=pl.BlockSpec((tm, D_MODEL), row),
        out_shape=jax.ShapeDtypeStruct((m, D_MODEL), jnp.float32),
        compiler_params=_cparams(("arbitrary",)),
        name="moe_combine",
    )(*([yk] * TOP_K), gates, x, norm_g, mods, mods)


def _route_kernel(h_ref, wr_ref, br_ref, e_ref, g_ref, rk_ref, cnt_ref, run_sc):
    i = pl.program_id(0)
    tm = h_ref.shape[0]

    @pl.when(i == 0)
    def _():
        run_sc[...] = jnp.zeros_like(run_sc)

    vals = jnp.dot(h_ref[...], wr_ref[...], preferred_element_type=jnp.float32,
                   precision=lax.Precision.HIGHEST) + br_ref[...]
    lane = lax.broadcasted_iota(jnp.int32, (tm, LANES), 1)
    top_v, top_e, picked = [], [], []
    for _ in range(TOP_K):
        m = jnp.max(vals, axis=-1, keepdims=True)
        idx = jnp.min(jnp.where(vals == m, lane, LANES), axis=-1, keepdims=True)
        sel = lane == idx
        top_v.append(m)
        top_e.append(idx)
        picked.append(sel)
        vals = jnp.where(sel, -jnp.inf, vals)
    ex = [jnp.exp(v - top_v[0]) for v in top_v]
    inv_den = 1.0 / (ex[0] + ex[1] + ex[2] + ex[3])
    any_pick = picked[0] | picked[1] | picked[2] | picked[3]
    rr = lax.broadcasted_iota(jnp.int32, (tm, tm), 0)
    cc = lax.broadcasted_iota(jnp.int32, (tm, tm), 1)
    earlier = (cc < rr).astype(jnp.bfloat16)
    before = jnp.dot(earlier, any_pick.astype(jnp.bfloat16), preferred_element_type=jnp.float32) + run_sc[...]
    e_out = jnp.zeros((tm, LANES), jnp.int32)
    g_out = jnp.zeros((tm, LANES), jnp.float32)
    r_out = jnp.zeros((tm, LANES), jnp.int32)
    for k in range(TOP_K):
        rank = jnp.sum(jnp.where(picked[k], before, 0.0), axis=-1, keepdims=True).astype(jnp.int32)
        e_out = jnp.where(lane == k, top_e[k], e_out)
        g_out = jnp.where(lane == k, ex[k] * inv_den, g_out)
        r_out = jnp.where(lane == k, rank, r_out)
    e_ref[...] = e_out
    g_ref[...] = g_out
    rk_ref[...] = r_out
    run_sc[...] += jnp.sum(any_pick.astype(jnp.float32), axis=0, keepdims=True)
    cnt_ref[...] = run_sc[...]


def _route(h, w_router, b_router):
    n, dm = h.shape
    tm = MM_TM
    w_r = jnp.zeros((dm, LANES), jnp.float32).at[:, :N_EXPERTS].set(w_router)
    b_r = jnp.full((1, LANES), NEG_INF, jnp.float32).at[0, :N_EXPERTS].set(b_router)
    row = lambda i: (i, 0)
    whole = lambda i: (0, 0)
    e, g, rk, cnt = pl.pallas_call(
        _route_kernel,
        grid=(n // tm,),
        in_specs=[pl.BlockSpec((tm, dm), row), pl.BlockSpec((dm, LANES), whole), pl.BlockSpec((1, LANES), whole)],
        out_specs=[pl.BlockSpec((tm, LANES), row), pl.BlockSpec((tm, LANES), row), pl.BlockSpec((tm, LANES), row),
                   pl.BlockSpec((1, LANES), whole)],
        out_shape=[jax.ShapeDtypeStruct((n, LANES), jnp.int32), jax.ShapeDtypeStruct((n, LANES), jnp.float32),
                   jax.ShapeDtypeStruct((n, LANES), jnp.int32), jax.ShapeDtypeStruct((1, LANES), jnp.float32)],
        scratch_shapes=[pltpu.VMEM((1, LANES), jnp.float32)],
        compiler_params=_cparams(("arbitrary",)),
        name="router",
    )(h, w_r, b_r)
    return e[:, :TOP_K], g[:, :TOP_K], rk[:, :TOP_K], cnt[0, :N_EXPERTS].astype(jnp.int32)


def _moe_ffn(h, h_rows, x, norm_g, mods, w_router, b_router, w_gu, b_gu, w_dn, b_dn, *, layer, t_rows, seq):
    n, dm = h.shape
    nk = n * TOP_K
    top_e, gates, rank, counts = _route(h, w_router, b_router)
    flat_e = top_e.reshape(-1)
    padded = (counts + MOE_TM - 1) // MOE_TM * MOE_TM
    pad_end = jnp.cumsum(padded)
    pad_start = pad_end - padded
    grp_start = jnp.cumsum(counts) - counts
    dest = (pad_start[top_e] + rank).reshape(-1)
    n_blk = (nk + MOE_TM - 1) // MOE_TM + N_EXPERTS
    rows = n_blk * MOE_TM
    blk_start = jnp.arange(n_blk, dtype=jnp.int32) * MOE_TM
    blk_e = jnp.minimum(jnp.sum((pad_end[None, :] <= blk_start[:, None]).astype(jnp.int32), axis=1),
                        N_EXPERTS - 1)
    n_used = (pad_end[-1] // MOE_TM).astype(jnp.int32).reshape(1)
    order = jnp.argsort(flat_e)
    blk_shift = (grp_start - pad_start)[blk_e]
    sorted_pos = (jnp.arange(rows, dtype=jnp.int32).reshape(n_blk, MOE_TM) + blk_shift[:, None]).reshape(-1)
    row_tok = (order[jnp.clip(sorted_pos, 0, nk - 1)] // TOP_K).astype(jnp.int32)
    row_tok = jnp.where(sorted_pos < nk, row_tok, jnp.arange(rows, dtype=jnp.int32) % n)
    xs = h_rows[row_tok]
    y = _moe_experts(blk_e, n_used, xs, w_gu, b_gu, w_dn, b_dn, layer=layer)
    yk = y[dest.reshape(n, TOP_K).T.reshape(-1)]
    return _moe_combine(yk, gates, x, norm_g, mods, t_rows=t_rows, seq=seq)


def _rope_tables(rows, n_ctx, batch):
    row = jnp.repeat(jnp.arange(rows, dtype=jnp.float32), GRID_W)
    col = jnp.tile(jnp.arange(GRID_W, dtype=jnp.float32), rows)
    n_freq = HEAD_DIM // 4
    inv = ROPE_THETA ** (-jnp.arange(n_freq, dtype=jnp.float32) / n_freq)
    ang_r = row[:, None] * inv
    ang_c = col[:, None] * inv
    cos_h = jnp.concatenate([jnp.cos(ang_r), jnp.cos(ang_r), jnp.cos(ang_c), jnp.cos(ang_c)], axis=1)
    sin_h = jnp.concatenate([-jnp.sin(ang_r), jnp.sin(ang_r), -jnp.sin(ang_c), jnp.sin(ang_c)], axis=1)
    cos_t = jnp.concatenate([cos_h, jnp.ones((n_ctx, HEAD_DIM), jnp.float32)], axis=0)
    sin_t = jnp.concatenate([sin_h, jnp.zeros((n_ctx, HEAD_DIM), jnp.float32)], axis=0)
    reps = (batch, LANES // HEAD_DIM)
    return jnp.tile(cos_t, reps), jnp.tile(sin_t, reps)


def _rope_partner(n_cols):
    idx = jnp.arange(n_cols)
    quarter = HEAD_DIM // 4
    return jnp.where((idx % (2 * quarter)) < quarter, idx + quarter, idx - quarter)


def kernel(x, c, ctx, c_ctx, w_mod, b_mod, norm_g, w_in, attn_sink, diff_lambda, diff_norm_w,
           hgrn_lb_logits, hgrn_norm_w, w_branch, w_out, w_router, b_router, w_gate_up, b_gate_up,
           w_down, b_down):
    b, s, dm = x.shape
    l = ctx.shape[1]
    t = s + l
    f32 = jnp.float32
    bf16 = jnp.bfloat16
    cos_t, sin_t = _rope_tables(s // GRID_W, l, b)
    lb_cum = jnp.cumsum(jax.nn.softmax(hgrn_lb_logits.astype(f32), axis=0), axis=0)
    lower_bounds = lb_cum - lb_cum[0:1]
    cond_rows = 16
    cond = jnp.zeros((cond_rows, dm), f32).at[:b].set(jax.nn.silu(c)).at[b].set(jax.nn.silu(c_ctx))
    scale = HEAD_DIM ** -0.5
    xa = jnp.concatenate([x, ctx], axis=1).reshape(b * t, dm)
    h_moe = None
    for layer in range(DEPTH):
        lam_init = 0.8 - 0.6 * math.exp(-0.3 * layer)
        mod_all = _matmul(cond, w_mod[layer], tm=cond_rows, tn=3 * dm, out_dtype=f32,
                          precision=lax.Precision.HIGHEST, name="adaln") + b_mod[layer]
        mod = mod_all[:b].reshape(b, 1, 6, 1, dm)
        mod_c = jnp.broadcast_to(mod_all[b:b + 1].reshape(1, 1, 6, 1, dm), (b, 1, 6, 1, dm))
        mods = jnp.concatenate([mod, mod_c], axis=1).reshape(2 * b, 6, 1, dm)
        ng = norm_g[layer].reshape(4, 1, dm)
        g0 = ng[0:1]
        geo = dict(t_rows=t, seq=s)

        w = w_in[layer]
        w_aq = (w[:, OFF_A:OFF_A + A_Q_W] * scale).reshape(dm, A_HKV, A_GROUP, HEAD_DIM)
        pad = jnp.zeros((dm, A_GROUP, HEAD_DIM), f32)
        w_aq8 = jnp.concatenate([jnp.concatenate([w_aq[:, 0], pad], axis=-1),
                                 jnp.concatenate([pad, w_aq[:, 1]], axis=-1)], axis=1).reshape(dm, A_Q8_W)
        w_a = jnp.concatenate([w_aq8, w[:, OFF_A + A_Q_W:OFF_A + A_W]], axis=1).astype(bf16)
        w_b = w[:, OFF_B:OFF_B + B_W].at[:, :B_QK_W].multiply(scale * LOG2_E).astype(bf16)
        proj_a = _proj_rope(xa, g0, mods, w_a, w_a[:, _rope_partner(A_ROPE_W2)], cos_t, sin_t,
                            rope_w=A_ROPE_W2, name="proj_a", **geo).reshape(b, t, A_W2)
        bq, bk_hm, bv1_hm = _proj_b(xa, g0, mods, w_b, w_b[:, _rope_partner(B_ROPE_W)], cos_t, sin_t, **geo)
        bq = bq.reshape(b, t, B_QK_W)
        lb = lower_bounds[layer]
        lb_par = jnp.stack([jnp.log(lb[0]), jnp.log1p(-lb[0]), jnp.log(lb[1]), jnp.log1p(-lb[1])])
        cq, ci, lf, cg = _proj_hgrn(xa, g0, mods, w[:, OFF_C:OFF_C + C_W].astype(bf16),
                                    lb_par.reshape(4, 1, C_K_W), **geo)
        gates = _proj_plain(xa, g0, mods, w[:, OFF_GATE:].astype(bf16), name="gate_proj", **geo)

        sink32 = attn_sink[layer].astype(f32)
        o_a = _window_attention(proj_a, sink32, seq=s)

        lp = diff_lambda[layer].astype(f32)
        lam = (jnp.exp(jnp.sum(lp[0] * lp[1])) - jnp.exp(jnp.sum(lp[2] * lp[3])) + lam_init).reshape(1).astype(f32)
        o_b = _diff_attention(lam, bq, bk_hm, bv1_hm, tq=FLASH_TQ, tk=FLASH_TK, q_block_offset=0,
                              n_q_blocks=s // FLASH_TQ, k_block_offset=0, nk=t // FLASH_TK)
        o_bx = _diff_attention(lam, bq, bk_hm, bv1_hm, tq=l, tk=l, q_block_offset=s // l, n_q_blocks=1,
                               k_block_offset=s // l, nk=1)
        o_b = jnp.concatenate([o_b, o_bx], axis=1)

        o_cf, o_cb = _hgrn_scan(cq.reshape(b, t, C_K_W), ci.reshape(b, t, C_V_W), lf.reshape(2, b, t, C_K_W),
                                n_ctx_blocks=l // HGRN_TB)

        col = jnp.arange(A_Q_W)
        head_of_col = ((col % LANES) // HEAD_DIM) * A_GROUP + col // LANES
        w_br = w_branch[layer].at[0].set(w_branch[layer][0][head_of_col * HEAD_DIM + col % HEAD_DIM])
        xa, h_moe, h_rows = _merge(o_a.reshape(b * t, A_Q_W), o_b.reshape(b * t, B_V_W), o_cf.reshape(b * t, C_V_W),
                           o_cb.reshape(b * t, C_V_W), cg,
                           gates, xa, ng, mods, diff_norm_w[layer].reshape(1, B_VD).astype(f32),
                           hgrn_norm_w[layer].reshape(1, C_DV).astype(f32),
                           w_br.astype(bf16), w_out[layer].astype(bf16),
                           diff_out_scale=1 - lam_init, **geo)

        xa = _moe_ffn(h_moe, h_rows.reshape(2 * b * t, dm), xa, ng, mods, w_router[layer], b_router[layer],
                      w_gate_up, b_gate_up, w_down, b_down, layer=layer, **geo)
    return xa.reshape(b, t, dm)[:, :s]
```

```python
import functools
import math

import jax
import jax.numpy as jnp
from jax import lax
from jax.experimental import pallas as pl
from jax.experimental.pallas import tpu as pltpu

D_MODEL = 1024
DEPTH = 2
GRID_W = 64
HEAD_DIM = 64
ROPE_THETA = 10000.0
RMS_EPS = 1e-6
NEG_INF = -1e30
BLOCK = 128
A_HQ = 8
A_HKV = 2
A_GROUP = A_HQ // A_HKV
WINDOW = 128
B_H = 4
B_HD = HEAD_DIM
B_VD = 2 * HEAD_DIM
C_H = 4
C_DK = 128
C_DV = 128
N_BRANCH = 3
BRANCH_W = 512
A_Q_W = A_HQ * HEAD_DIM
A_KV_W = A_HKV * HEAD_DIM
B_QK_W = B_H * 2 * B_HD
B_V_W = B_H * B_VD
C_K_W = C_H * C_DK
C_V_W = C_H * C_DV
GATE_W = N_BRANCH * D_MODEL
N_EXPERTS = 32
TOP_K = 4
EXPERT_FF = D_MODEL
SWIGLU_LIMIT = 7.0
SWIGLU_ALPHA = 1.702

A_W = A_Q_W + 2 * A_KV_W
A_ROPE_W = A_Q_W + A_KV_W
B_W = 2 * B_QK_W + B_V_W
B_ROPE_W = 2 * B_QK_W
C_W = 3 * C_K_W + 2 * C_V_W
OFF_A = 0
OFF_B = OFF_A + A_W
OFF_C = OFF_B + B_W
OFF_GATE = OFF_C + C_W

V7X_VMEM_LIMIT_BYTES = 56 * 1024 * 1024
LANES = 128
LOG2_E = math.log2(math.e)

MM_TM = 512
HALF_TM = MM_TM // 2
FLASH_TQ = 2048
FLASH_TK = 768
HGRN_TB = 256
HGRN_SUB = 16
HGRN_UNROLL = 4
MOE_TM = 512


def _cparams(sem):
    return pltpu.CompilerParams(dimension_semantics=sem, vmem_limit_bytes=V7X_VMEM_LIMIT_BYTES)


def _rms_rows(x, g):
    return (x * lax.rsqrt(jnp.mean(x * x, axis=-1, keepdims=True) + RMS_EPS)) * g


def _modulated_tile(x_ref, g, mod_a, mod_b, shift_slot, scale_slot):
    out = []
    for half, mod in enumerate((mod_a, mod_b)):
        x = x_ref[half * HALF_TM:(half + 1) * HALF_TM, :]
        out.append(_rms_rows(x, g * (1.0 + mod[scale_slot])) + mod[shift_slot])
    return jnp.concatenate(out, axis=0)


def _residual_tile(x_ref, y, g, mod_a, mod_b, gate_slot):
    out = []
    for half, mod in enumerate((mod_a, mod_b)):
        rows = slice(half * HALF_TM, (half + 1) * HALF_TM)
        out.append(x_ref[rows, :] + _rms_rows(y[rows, :], g * mod[gate_slot]))
    return out


def _mod_specs(t_rows, seq):
    halves_per_batch = t_rows // HALF_TM
    lat_halves = seq // HALF_TM

    def row(u):
        return (u // halves_per_batch) * 2 + jnp.where(u % halves_per_batch >= lat_halves, 1, 0)

    blk = (1, 6, 1, D_MODEL)
    return [pl.BlockSpec(blk, lambda i: (row(2 * i), 0, 0, 0)),
            pl.BlockSpec(blk, lambda i: (row(2 * i + 1), 0, 0, 0))]


def _proj_rope_kernel(x_ref, g_ref, ma_ref, mb_ref, w_ref, wsw_ref, cos_ref, sin_ref, o_ref, *, rope_w):
    h = _modulated_tile(x_ref, g_ref[0], ma_ref[0], mb_ref[0], 0, 1).astype(jnp.bfloat16)
    z = jnp.dot(h, w_ref[...], preferred_element_type=jnp.float32)
    zs = jnp.dot(h, wsw_ref[...], preferred_element_type=jnp.float32)
    reps = rope_w // LANES
    rot = z[:, :rope_w] * jnp.tile(cos_ref[...], (1, reps)) + zs * jnp.tile(sin_ref[...], (1, reps))
    o_ref[:, :rope_w] = rot.astype(o_ref.dtype)
    o_ref[:, rope_w:] = z[:, rope_w:].astype(o_ref.dtype)


def _proj_b_kernel(x_ref, g_ref, ma_ref, mb_ref, w_ref, wsw_ref, cos_ref, sin_ref, q_ref, k_ref, v1_ref):
    h = _modulated_tile(x_ref, g_ref[0], ma_ref[0], mb_ref[0], 0, 1).astype(jnp.bfloat16)
    z = jnp.dot(h, w_ref[...], preferred_element_type=jnp.float32)
    zs = jnp.dot(h, wsw_ref[...], preferred_element_type=jnp.float32)
    reps = B_ROPE_W // LANES
    rot = z[:, :B_ROPE_W] * jnp.tile(cos_ref[...], (1, reps)) + zs * jnp.tile(sin_ref[...], (1, reps))
    q_ref[...] = rot[:, :B_QK_W].astype(q_ref.dtype)
    for hh in range(B_H):
        k_ref[hh] = rot[:, B_QK_W + hh * 2 * B_HD:B_QK_W + (hh + 1) * 2 * B_HD].astype(k_ref.dtype)
        v1_ref[hh, :, :B_VD] = z[:, B_ROPE_W + hh * B_VD:B_ROPE_W + (hh + 1) * B_VD].astype(v1_ref.dtype)
        v1_ref[hh, :, B_VD:] = jnp.ones((MM_TM, B_VD), v1_ref.dtype)


def _proj_plain_kernel(x_ref, g_ref, ma_ref, mb_ref, w_ref, o_ref):
    h = _modulated_tile(x_ref, g_ref[0], ma_ref[0], mb_ref[0], 0, 1).astype(jnp.bfloat16)
    o_ref[...] = jnp.dot(h, w_ref[...], preferred_element_type=jnp.float32).astype(o_ref.dtype)


def _log_forget(z, log_lb, log_1m_lb):
    log_sig = jnp.minimum(z, 0.0) - jnp.log(1.0 + jnp.exp(-jnp.abs(z)))
    c = log_1m_lb + log_sig
    return jnp.maximum(log_lb, c) + jnp.log(1.0 + jnp.exp(-jnp.abs(log_lb - c)))


def _proj_hgrn_kernel(x_ref, g_ref, ma_ref, mb_ref, w_ref, lb_ref, q_ref, v_ref, lf_ref, cg_ref):
    h = _modulated_tile(x_ref, g_ref[0], ma_ref[0], mb_ref[0], 0, 1).astype(jnp.bfloat16)
    z = jnp.dot(h, w_ref[...], preferred_element_type=jnp.float32)
    zq = z[:, :C_K_W]
    q_ref[...] = zq * jax.nn.sigmoid(zq)
    v_ref[...] = z[:, C_K_W:C_K_W + C_V_W]
    off = C_K_W + C_V_W
    lf_ref[0] = _log_forget(z[:, off:off + C_K_W], lb_ref[0], lb_ref[1])
    lf_ref[1] = _log_forget(z[:, off + C_K_W:off + 2 * C_K_W], lb_ref[2], lb_ref[3])
    cg_ref[...] = z[:, off + 2 * C_K_W:].astype(cg_ref.dtype)


def _proj_common_specs(t_rows, seq):
    row = lambda i: (i, 0)
    return [pl.BlockSpec((MM_TM, D_MODEL), row),
            pl.BlockSpec((1, 1, D_MODEL), lambda i: (0, 0, 0))] + _mod_specs(t_rows, seq)


def _proj_rope(x, g, mods, w, wsw, cos, sin, *, t_rows, seq, rope_w, name):
    m = x.shape[0]
    n = w.shape[1]
    row = lambda i: (i, 0)
    whole = lambda i: (0, 0)
    return pl.pallas_call(
        functools.partial(_proj_rope_kernel, rope_w=rope_w),
        grid=(m // MM_TM,),
        in_specs=_proj_common_specs(t_rows, seq) + [
            pl.BlockSpec((D_MODEL, n), whole), pl.BlockSpec((D_MODEL, rope_w), whole),
            pl.BlockSpec((MM_TM, LANES), row), pl.BlockSpec((MM_TM, LANES), row)],
        out_specs=pl.BlockSpec((MM_TM, n), row),
        out_shape=jax.ShapeDtypeStruct((m, n), jnp.bfloat16),
        compiler_params=_cparams(("arbitrary",)),
        name=name,
    )(x, g, mods, mods, w, wsw, cos, sin)


def _proj_b(x, g, mods, w, wsw, cos, sin, *, t_rows, seq):
    m = x.shape[0]
    row = lambda i: (i, 0)
    whole = lambda i: (0, 0)
    heads = lambda i: (0, i, 0)
    bf16 = jnp.bfloat16
    return pl.pallas_call(
        _proj_b_kernel,
        grid=(m // MM_TM,),
        in_specs=_proj_common_specs(t_rows, seq) + [
            pl.BlockSpec((D_MODEL, B_W), whole), pl.BlockSpec((D_MODEL, B_ROPE_W), whole),
            pl.BlockSpec((MM_TM, LANES), row), pl.BlockSpec((MM_TM, LANES), row)],
        out_specs=[pl.BlockSpec((MM_TM, B_QK_W), row), pl.BlockSpec((B_H, MM_TM, 2 * B_HD), heads),
                   pl.BlockSpec((B_H, MM_TM, 2 * B_VD), heads)],
        out_shape=[jax.ShapeDtypeStruct((m, B_QK_W), bf16), jax.ShapeDtypeStruct((B_H, m, 2 * B_HD), bf16),
                   jax.ShapeDtypeStruct((B_H, m, 2 * B_VD), bf16)],
        compiler_params=_cparams(("arbitrary",)),
        name="proj_b",
    )(x, g, mods, mods, w, wsw, cos, sin)


def _proj_plain(x, g, mods, w, *, t_rows, seq, name):
    m = x.shape[0]
    n = w.shape[1]
    row = lambda i: (i, 0)
    return pl.pallas_call(
        _proj_plain_kernel,
        grid=(m // MM_TM,),
        in_specs=_proj_common_specs(t_rows, seq) + [pl.BlockSpec((D_MODEL, n), lambda i: (0, 0))],
        out_specs=pl.BlockSpec((MM_TM, n), row),
        out_shape=jax.ShapeDtypeStruct((m, n), jnp.bfloat16),
        compiler_params=_cparams(("arbitrary",)),
        name=name,
    )(x, g, mods, mods, w)


def _proj_hgrn(x, g, mods, w, lb_par, *, t_rows, seq):
    m = x.shape[0]
    row = lambda i: (i, 0)
    f32 = jnp.float32
    return pl.pallas_call(
        _proj_hgrn_kernel,
        grid=(m // MM_TM,),
        in_specs=_proj_common_specs(t_rows, seq) + [
            pl.BlockSpec((D_MODEL, C_W), lambda i: (0, 0)),
            pl.BlockSpec((4, 1, C_K_W), lambda i: (0, 0, 0))],
        out_specs=[pl.BlockSpec((MM_TM, C_K_W), row), pl.BlockSpec((MM_TM, C_V_W), row),
                   pl.BlockSpec((2, MM_TM, C_K_W), lambda i: (0, i, 0)), pl.BlockSpec((MM_TM, C_V_W), row)],
        out_shape=[jax.ShapeDtypeStruct((m, C_K_W), f32), jax.ShapeDtypeStruct((m, C_V_W), f32),
                   jax.ShapeDtypeStruct((2, m, C_K_W), f32), jax.ShapeDtypeStruct((m, C_V_W), jnp.bfloat16)],
        compiler_params=_cparams(("arbitrary",)),
        name="hgrn_proj",
    )(x, g, mods, mods, w, lb_par)


def _mm_kernel(x_ref, w_ref, o_ref, *, precision):
    o_ref[...] = jnp.dot(x_ref[...], w_ref[...], preferred_element_type=jnp.float32,
                         precision=precision).astype(o_ref.dtype)


def _matmul(x, w, *, tm, tn, out_dtype, precision=None, name):
    m, k = x.shape
    n = w.shape[1]
    assert m % tm == 0 and n % tn == 0, (m, tm, n, tn)
    return pl.pallas_call(
        functools.partial(_mm_kernel, precision=precision),
        grid=(n // tn, m // tm),
        in_specs=[pl.BlockSpec((tm, k), lambda j, i: (i, 0)),
                  pl.BlockSpec((k, tn), lambda j, i: (0, j))],
        out_specs=pl.BlockSpec((tm, tn), lambda j, i: (i, j)),
        out_shape=jax.ShapeDtypeStruct((m, n), out_dtype),
        compiler_params=_cparams(("arbitrary", "arbitrary")),
        name=name,
    )(x, w)


A_Q8_W = A_HQ * LANES
A_W2 = A_Q8_W + 2 * A_KV_W
A_ROPE_W2 = A_Q8_W + A_KV_W
A_K_BLK = A_Q8_W // A_KV_W
A_V_BLK = A_K_BLK + 1
A_STEP_BLOCKS = 2


def _attn_a_kernel(sink_ref, q_ref, kp_ref, kc_ref, kn_ref, vp_ref, vc_ref, vn_ref, kx_ref, vx_ref,
                   o_ref, *, seq):
    n = pl.program_id(1)
    k_loc = jnp.concatenate([kp_ref[0], kc_ref[0], kn_ref[0]], axis=0)
    v_loc = jnp.concatenate([vp_ref[0], vc_ref[0], vn_ref[0]], axis=0)
    n_keys = 3 * BLOCK + kx_ref.shape[1]
    qi = lax.broadcasted_iota(jnp.int32, (BLOCK, n_keys), 0)
    kj = lax.broadcasted_iota(jnp.int32, (BLOCK, n_keys), 1)
    lane = lax.broadcasted_iota(jnp.int32, (BLOCK, LANES), 1)
    for j in range(A_STEP_BLOCKS):
        blk = n * A_STEP_BLOCKS + j
        rows = slice(j * BLOCK, (j + 1) * BLOCK)
        k_all = jnp.concatenate([k_loc[j * BLOCK:(j + 3) * BLOCK, :], kx_ref[0]], axis=0)
        v_all = jnp.concatenate([v_loc[j * BLOCK:(j + 3) * BLOCK, :], vx_ref[0]], axis=0)
        kpos = blk * BLOCK + kj - BLOCK
        local_ok = (jnp.abs(kj - BLOCK - qi) <= WINDOW) & (kpos >= 0) & (kpos < seq) & (blk * BLOCK < seq)
        mask = local_ok | (kj >= 3 * BLOCK)
        q8 = jnp.concatenate([q_ref[0, rows, hq * LANES:(hq + 1) * LANES] for hq in range(A_HQ)], axis=0)
        s = lax.dot_general(q8, k_all, (((1,), (1,)), ((), ())), preferred_element_type=jnp.float32)
        p_blocks, inv_l = [], []
        for hq in range(A_HQ):
            sh = jnp.where(mask, s[hq * BLOCK:(hq + 1) * BLOCK, :], NEG_INF)
            sk = sink_ref[hq]
            m = jnp.maximum(jnp.max(sh, axis=-1, keepdims=True), sk)
            p = jnp.exp(sh - m)
            inv_l.append(1.0 / (jnp.sum(p, axis=-1, keepdims=True) + jnp.exp(sk - m)))
            p_blocks.append(p.astype(v_all.dtype))
        o = jnp.dot(jnp.concatenate(p_blocks, axis=0), v_all, preferred_element_type=jnp.float32)
        for g in range(A_GROUP):
            lo = o[g * BLOCK:(g + 1) * BLOCK, :] * inv_l[g]
            hi = o[(A_GROUP + g) * BLOCK:(A_GROUP + g + 1) * BLOCK, :] * inv_l[A_GROUP + g]
            o_ref[0, rows, g * LANES:(g + 1) * LANES] = jnp.where(lane < HEAD_DIM, lo, hi).astype(o_ref.dtype)


def _window_attention(proj_a, sink, *, seq):
    b, t, _ = proj_a.shape
    l = t - seq
    step = A_STEP_BLOCKS * BLOCK
    nb = seq // BLOCK
    ns = seq // step
    ctx_blk = seq // l
    q_map = lambda bi, n, *_: (bi, n, 0)
    edge = lambda col, fn: pl.BlockSpec((1, BLOCK, A_KV_W), lambda bi, n, *_: (bi, fn(n), col))
    mid = lambda col: pl.BlockSpec((1, step, A_KV_W), lambda bi, n, *_: (bi, jnp.minimum(n, ns - 1), col))
    prev = lambda n: jnp.clip(n * A_STEP_BLOCKS - 1, 0, nb - 1)
    nxt = lambda n: jnp.minimum((n + 1) * A_STEP_BLOCKS, nb - 1)
    ctx = lambda col: pl.BlockSpec((1, l, A_KV_W), lambda bi, n, *_: (bi, ctx_blk, col))
    return pl.pallas_call(
        functools.partial(_attn_a_kernel, seq=seq),
        grid_spec=pltpu.PrefetchScalarGridSpec(
            num_scalar_prefetch=1, grid=(b, t // step),
            in_specs=[pl.BlockSpec((1, step, A_Q8_W), q_map),
                      edge(A_K_BLK, prev), mid(A_K_BLK), edge(A_K_BLK, nxt),
                      edge(A_V_BLK, prev), mid(A_V_BLK), edge(A_V_BLK, nxt),
                      ctx(A_K_BLK), ctx(A_V_BLK)],
            out_specs=pl.BlockSpec((1, step, A_Q_W), q_map)),
        out_shape=jax.ShapeDtypeStruct((b, t, A_Q_W), jnp.bfloat16),
        compiler_params=_cparams(("arbitrary", "arbitrary")),
        name="window_gqa",
    )(sink, *([proj_a] * 9))


def _diff_flash_kernel(lam_ref, q_ref, k_ref, v_ref, o_ref, q2_sc, sa_sc, sb_sc, m_sc, acc_sc, *, tq, tk, nk):
    u = pl.program_id(2)
    new_q = u % nk == 0

    @pl.when(u == 0)
    def _():
        m_sc[...] = jnp.full_like(m_sc, NEG_INF)
        acc_sc[...] = jnp.zeros_like(acc_sc)
        sb_sc[...] = jnp.full_like(sb_sc, NEG_INF)

    @pl.when(new_q)
    def _():
        q = q_ref[0]
        lane = lax.broadcasted_iota(jnp.int32, q.shape, 1)
        q2_sc[:tq, :] = jnp.where(lane < B_HD, q, jnp.zeros_like(q))
        q2_sc[tq:, :] = jnp.where(lane >= B_HD, q, jnp.zeros_like(q))

    def step(s_new_ref, s_old_ref):
        m_prev = m_sc[...]
        m_new = jnp.maximum(m_prev, jnp.max(s_old_ref[...], axis=-1, keepdims=True))
        alpha = jnp.exp2(m_prev - m_new)
        p = jnp.exp2(s_old_ref[...] - jnp.tile(m_new, (1, tk // LANES))).astype(jnp.bfloat16)
        acc_sc[...] = jnp.tile(alpha, (1, 2)) * acc_sc[...] + jnp.dot(
            p, v_ref[0], preferred_element_type=jnp.float32)
        m_sc[...] = m_new
        s_new_ref[...] = lax.dot_general(q2_sc[...], k_ref[0], (((1,), (1,)), ((), ())),
                                         preferred_element_type=jnp.float32)

    @pl.when(u % 2 == 0)
    def _():
        step(sa_sc, sb_sc)

    @pl.when(u % 2 == 1)
    def _():
        step(sb_sc, sa_sc)

    @pl.when(new_q & (u > 0))
    def _():
        acc = acc_sc[...]
        o = acc[:, :B_VD] / acc[:, B_VD:]
        o_ref[0] = (o[:tq, :] - lam_ref[0] * o[tq:, :]).astype(o_ref.dtype)
        m_sc[...] = jnp.full_like(m_sc, NEG_INF)
        acc_sc[...] = jnp.zeros_like(acc_sc)


def _diff_attention(lam, q, k_hm, v1_hm, *, tq, tk, q_block_offset, n_q_blocks, k_block_offset, nk):
    b, t, _ = q.shape
    kb = t // tk
    n_steps = n_q_blocks * nk + 1
    k_blk = lambda bi, u: bi * kb + k_block_offset + u % nk
    in_specs = [pl.BlockSpec((1, tq, 2 * B_HD),
                             lambda bi, h, u, *_: (bi, jnp.minimum(u // nk, n_q_blocks - 1) + q_block_offset, h)),
                pl.BlockSpec((1, tk, 2 * B_HD), lambda bi, h, u, *_: (h, k_blk(bi, u), 0)),
                pl.BlockSpec((1, tk, 2 * B_VD), lambda bi, h, u, *_: (h, k_blk(bi, jnp.maximum(u - 1, 0)), 0))]
    return pl.pallas_call(
        functools.partial(_diff_flash_kernel, tq=tq, tk=tk, nk=nk),
        grid_spec=pltpu.PrefetchScalarGridSpec(
            num_scalar_prefetch=1, grid=(b, B_H, n_steps),
            in_specs=in_specs,
            out_specs=pl.BlockSpec((1, tq, B_VD), lambda bi, h, u, *_: (bi, jnp.maximum(u - 1, 0) // nk, h)),
            scratch_shapes=[pltpu.VMEM((2 * tq, 2 * B_HD), jnp.bfloat16),
                            pltpu.VMEM((2 * tq, tk), jnp.float32),
                            pltpu.VMEM((2 * tq, tk), jnp.float32),
                            pltpu.VMEM((2 * tq, LANES), jnp.float32),
                            pltpu.VMEM((2 * tq, 2 * B_VD), jnp.float32)]),
        out_shape=jax.ShapeDtypeStruct((b, n_q_blocks * tq, B_V_W), jnp.float32),
        compiler_params=_cparams(("arbitrary", "arbitrary", "arbitrary")),
        name="diff_flash",
    )(lam, q, k_hm, v1_hm)


def _hgrn_sub_chunk(q_ref, v_ref, lf_ref, o_ref, st_ref, r0, tri, rev):
    half = HGRN_SUB // 2
    row8 = lax.broadcasted_iota(jnp.int32, (half, C_DK), 0)
    lf = lf_ref[pl.ds(r0, HGRN_SUB), :] * LOG2_E
    lf_a = lf.astype(jnp.bfloat16)
    rem = lf - lf_a.astype(jnp.float32)
    lf_b = rem.astype(jnp.bfloat16)
    lf_c = (rem - lf_b.astype(jnp.float32)).astype(jnp.bfloat16)
    bl = (jnp.dot(tri, lf_a, preferred_element_type=jnp.float32)
          + jnp.dot(tri, lf_b, preferred_element_type=jnp.float32)
          + jnp.dot(tri, lf_c, preferred_element_type=jnp.float32))
    q = q_ref[pl.ds(r0, HGRN_SUB), :]
    v = v_ref[pl.ds(r0, HGRN_SUB), :]
    last = 0 if rev else HGRN_SUB - 1
    for hh in range(C_H):
        sl = slice(hh * C_DK, (hh + 1) * C_DK)
        lfh, blh, qh, vh = lf[:, sl], bl[:, sl], q[:, sl], v[:, sl]
        tot = blh[last:last + 1, :]
        kh = 1.0 - jnp.exp2(lfh)
        qt = qh * jnp.exp2(blh)
        kt = kh * jnp.exp2(tot - blh)
        st = st_ref[hh]
        o_sub = lax.dot_general(qt.astype(jnp.bfloat16), st.astype(jnp.bfloat16),
                                (((1,), (1,)), ((), ())), preferred_element_type=jnp.float32)
        parts = [o_sub[:half, :], o_sub[half:, :]]
        for ss in range(HGRN_SUB):
            ks = kh[ss:ss + 1, :]
            bs = blh[ss:ss + 1, :]
            vs = vh[ss:ss + 1, :]
            for p in range(2):
                lo = p * half
                if rev:
                    reached = lo <= ss
                    full = lo + half - 1 <= ss
                    ok = row8 + lo <= ss
                else:
                    reached = lo + half - 1 >= ss
                    full = lo >= ss
                    ok = row8 + lo >= ss
                if not reached:
                    continue
                diff = blh[lo:lo + half, :] - bs
                if not full:
                    diff = jnp.where(ok, diff, NEG_INF)
                col = jnp.sum((qh[lo:lo + half, :] * ks) * jnp.exp2(diff), axis=1, keepdims=True)
                parts[p] = parts[p] + col * vs
        o_ref[pl.ds(r0, half), sl] = parts[0]
        o_ref[pl.ds(r0 + half, half), sl] = parts[1]
        upd = lax.dot_general(vh.astype(jnp.bfloat16), kt.astype(jnp.bfloat16),
                              (((0,), (0,)), ((), ())), preferred_element_type=jnp.float32)
        st_ref[hh] = jnp.exp2(tot) * st + upd


def _hgrn_kernel(qf_ref, vf_ref, lff_ref, qb_ref, vb_ref, lfb_ref, of_ref, ob_ref, st_ref, *, tb):
    t = pl.program_id(1)
    nsub = tb // HGRN_SUB

    @pl.when(t == 0)
    def _():
        st_ref[...] = jnp.zeros_like(st_ref)

    rr = lax.broadcasted_iota(jnp.int32, (HGRN_SUB, HGRN_SUB), 0)
    cc = lax.broadcasted_iota(jnp.int32, (HGRN_SUB, HGRN_SUB), 1)
    tri_f = (cc <= rr).astype(jnp.bfloat16)
    tri_b = (cc >= rr).astype(jnp.bfloat16)

    def body(i, carry):
        rf = pl.multiple_of(i * HGRN_SUB, HGRN_SUB)
        rb = pl.multiple_of((nsub - 1 - i) * HGRN_SUB, HGRN_SUB)
        _hgrn_sub_chunk(qf_ref.at[0], vf_ref.at[0], lff_ref.at[0, 0], of_ref.at[0], st_ref.at[0], rf, tri_f, False)
        _hgrn_sub_chunk(qb_ref.at[0], vb_ref.at[0], lfb_ref.at[0, 0], ob_ref.at[0], st_ref.at[1], rb, tri_b, True)
        return carry

    lax.fori_loop(0, nsub, body, 0, unroll=HGRN_UNROLL)


def _hgrn_scan(q, v, lf, *, n_ctx_blocks):
    b, t, w = q.shape
    tb = HGRN_TB
    nblk = t // tb
    nlat = nblk - n_ctx_blocks

    def fwd_block(ti):
        return jnp.where(ti < n_ctx_blocks, nlat + ti, ti - n_ctx_blocks)

    def bwd_block(ti):
        return nblk - 1 - ti

    rows = lambda blk: pl.BlockSpec((1, tb, w), lambda bi, ti: (bi, blk(ti), 0))
    gate = lambda di, blk: pl.BlockSpec((1, 1, tb, w), lambda bi, ti: (di, bi, blk(ti), 0))
    out = jax.ShapeDtypeStruct((b, t, w), jnp.float32)
    return pl.pallas_call(
        functools.partial(_hgrn_kernel, tb=tb),
        grid=(b, nblk),
        in_specs=[rows(fwd_block), rows(fwd_block), gate(0, fwd_block),
                  rows(bwd_block), rows(bwd_block), gate(1, bwd_block)],
        out_specs=[rows(fwd_block), rows(bwd_block)],
        out_shape=[out, out],
        scratch_shapes=[pltpu.VMEM((2, C_H, C_DV, C_DK), jnp.float32)],
        compiler_params=_cparams(("arbitrary", "arbitrary")),
        name="hgrn2_scan",
    )(q, v, lf, q, v, lf)


def _head_rms(x, w, n_heads, width):
    out = []
    for hh in range(n_heads):
        blk = x[:, hh * width:(hh + 1) * width]
        out.append(_rms_rows(blk, w))
    return jnp.concatenate(out, axis=1)


def _merge_kernel(oa_ref, ob_ref, ocf_ref, ocb_ref, cg_ref, gt_ref, x_ref, ng_ref, ma_ref, mb_ref, dw_ref, hw_ref,
                  wbr_ref, wo_ref, xo_ref, ho_ref, hb_ref, *, diff_out_scale):
    ob = (_head_rms(ob_ref[...], dw_ref[...], B_H, B_VD) * diff_out_scale).astype(jnp.bfloat16)
    cg = cg_ref[...].astype(jnp.float32)
    oc = (_head_rms(ocf_ref[...] + ocb_ref[...], hw_ref[...], C_H, C_DV) * (cg * jax.nn.sigmoid(cg)))
    acc = None
    for i, o in enumerate((oa_ref[...], ob, oc.astype(jnp.bfloat16))):
        z = jnp.dot(o, wbr_ref[i], preferred_element_type=jnp.float32)
        g = jax.nn.sigmoid(gt_ref[:, i * D_MODEL:(i + 1) * D_MODEL].astype(jnp.float32))
        acc = g * z if acc is None else acc + g * z
    y = jnp.dot(acc.astype(jnp.bfloat16), wo_ref[...], preferred_element_type=jnp.float32)
    ma, mb = ma_ref[0], mb_ref[0]
    for half, (xn, mod) in enumerate(zip(_residual_tile(x_ref, y, ng_ref[1], ma, mb, 2), (ma, mb))):
        rows = slice(half * HALF_TM, (half + 1) * HALF_TM)
        xo_ref[rows, :] = xn
        h_next = _rms_rows(xn, ng_ref[2] * (1.0 + mod[4])) + mod[3]
        ho_ref[rows, :] = h_next
        hb_ref[0, rows, :] = h_next.astype(hb_ref.dtype)
    hb_ref[1] = jnp.zeros((MM_TM, D_MODEL), hb_ref.dtype)


def _merge(oa, ob, ocf, ocb, cg, gates, x, norm_g, mods, diff_w, hgrn_w, w_br, w_o, *, t_rows, seq,
           diff_out_scale):
    m = x.shape[0]
    tm = MM_TM
    row = lambda i: (i, 0)
    f32 = jnp.float32
    return pl.pallas_call(
        functools.partial(_merge_kernel, diff_out_scale=diff_out_scale),
        grid=(m // tm,),
        in_specs=[pl.BlockSpec((tm, BRANCH_W), row), pl.BlockSpec((tm, BRANCH_W), row),
                  pl.BlockSpec((tm, BRANCH_W), row), pl.BlockSpec((tm, BRANCH_W), row),
                  pl.BlockSpec((tm, BRANCH_W), row),
                  pl.BlockSpec((tm, GATE_W), row), pl.BlockSpec((tm, D_MODEL), row),
                  pl.BlockSpec((4, 1, D_MODEL), lambda i: (0, 0, 0))] + _mod_specs(t_rows, seq) + [
                  pl.BlockSpec((1, B_VD), lambda i: (0, 0)), pl.BlockSpec((1, C_DV), lambda i: (0, 0)),
                  pl.BlockSpec((N_BRANCH, BRANCH_W, D_MODEL), lambda i: (0, 0, 0)),
                  pl.BlockSpec((D_MODEL, D_MODEL), lambda i: (0, 0))],
        out_specs=[pl.BlockSpec((tm, D_MODEL), row), pl.BlockSpec((tm, D_MODEL), row),
                   pl.BlockSpec((2, tm, D_MODEL), lambda i: (0, i, 0))],
        out_shape=[jax.ShapeDtypeStruct((m, D_MODEL), f32), jax.ShapeDtypeStruct((m, D_MODEL), f32),
                   jax.ShapeDtypeStruct((2, m, D_MODEL), jnp.bfloat16)],
        compiler_params=_cparams(("arbitrary",)),
        name="branch_merge",
    )(oa, ob, ocf, ocb, cg, gates, x, norm_g, mods, mods, diff_w, hgrn_w, w_br, w_o)


def _moe_kernel(blk_e_ref, n_used_ref, x_ref, wgu_ref, bgu_ref, wdn_ref, bdn_ref, y_ref, wgu_sc, wdn_sc):
    i = pl.program_id(0)
    n_used = n_used_ref[0]
    e = blk_e_ref[i]
    e_prev = blk_e_ref[jnp.maximum(i - 1, 0)]

    @pl.when((i == 0) | (e != e_prev))
    def _():
        wgu_sc[...] = wgu_ref[0, 0].astype(jnp.bfloat16)
        wdn_sc[...] = wdn_ref[0, 0].astype(jnp.bfloat16)

    @pl.when(i < n_used)
    def _():
        gu = jnp.dot(x_ref[...], wgu_sc[...], preferred_element_type=jnp.float32) + bgu_ref[0, 0]
        g = jnp.minimum(gu[:, :EXPERT_FF], SWIGLU_LIMIT)
        u = jnp.clip(gu[:, EXPERT_FF:], -SWIGLU_LIMIT, SWIGLU_LIMIT)
        act = (u + 1.0) * (g * jax.nn.sigmoid(SWIGLU_ALPHA * g))
        y = jnp.dot(act.astype(jnp.bfloat16), wdn_sc[...], preferred_element_type=jnp.float32) + bdn_ref[0, 0]
        y_ref[...] = y.astype(y_ref.dtype)

    @pl.when(i >= n_used)
    def _():
        y_ref[...] = jnp.zeros_like(y_ref)


def _moe_experts(blk_e, n_used, xs, w_gu, b_gu, w_dn, b_dn, *, layer):
    r, dm = xs.shape
    tm = MOE_TM
    n_blk = blk_e.shape[0]
    return pl.pallas_call(
        _moe_kernel,
        grid_spec=pltpu.PrefetchScalarGridSpec(
            num_scalar_prefetch=2, grid=(n_blk,),
            in_specs=[pl.BlockSpec((tm, dm), lambda i, be, nu: (i, 0)),
                      pl.BlockSpec((1, 1, dm, 2 * EXPERT_FF), lambda i, be, nu: (layer, be[i], 0, 0)),
                      pl.BlockSpec((1, 1, 1, 2 * EXPERT_FF), lambda i, be, nu: (layer, be[i], 0, 0)),
                      pl.BlockSpec((1, 1, EXPERT_FF, dm), lambda i, be, nu: (layer, be[i], 0, 0)),
                      pl.BlockSpec((1, 1, 1, dm), lambda i, be, nu: (layer, be[i], 0, 0))],
            out_specs=pl.BlockSpec((tm, dm), lambda i, be, nu: (i, 0)),
            scratch_shapes=[pltpu.VMEM((dm, 2 * EXPERT_FF), jnp.bfloat16),
                            pltpu.VMEM((EXPERT_FF, dm), jnp.bfloat16)]),
        out_shape=jax.ShapeDtypeStruct((r, dm), jnp.bfloat16),
        compiler_params=_cparams(("arbitrary",)),
        name="moe_experts",
    )(blk_e, n_used, xs, w_gu, b_gu.reshape(DEPTH, N_EXPERTS, 1, -1), w_dn,
      b_dn.reshape(DEPTH, N_EXPERTS, 1, -1))


def _combine_kernel(*refs):
    yk_refs = refs[:TOP_K]
    gt_ref, x_ref, ng_ref, ma_ref, mb_ref, xo_ref = refs[TOP_K:]
    gt = gt_ref[...]
    f = None
    for k in range(TOP_K):
        term = gt[:, k:k + 1] * yk_refs[k][...].astype(jnp.float32)
        f = term if f is None else f + term
    for half, xn in enumerate(_residual_tile(x_ref, f, ng_ref[3], ma_ref[0], mb_ref[0], 5)):
        xo_ref[half * HALF_TM:(half + 1) * HALF_TM, :] = xn


def _moe_combine(yk, gates, x, norm_g, mods, *, t_rows, seq):
    m = x.shape[0]
    tm = MM_TM
    n_tiles = m // tm
    row = lambda i: (i, 0)
    choice = lambda k: pl.BlockSpec((tm, D_MODEL), lambda i: (k * n_tiles + i, 0))
    return pl.pallas_call(
        _combine_kernel,
        grid=(n_tiles,),
        in_specs=[choice(k) for k in range(TOP_K)] + [
            pl.BlockSpec((tm, TOP_K), row), pl.BlockSpec((tm, D_MODEL), row),
            pl.BlockSpec((4, 1, D_MODEL), lambda i: (0, 0, 0))] + _mod_specs(t_rows, seq),
        out_specs=pl.BlockSpec((tm, D_MODEL), row),
        out_shape=jax.ShapeDtypeStruct((m, D_MODEL), jnp.float32),
        compiler_params=_cparams(("arbitrary",)),
        name="moe_combine",
    )(*([yk] * TOP_K), gates, x, norm_g, mods, mods)


def _route_kernel(h_ref, wr_ref, br_ref, e_ref, g_ref, rk_ref, cnt_ref, run_sc):
    i = pl.program_id(0)
    tm = h_ref.shape[0]

    @pl.when(i == 0)
    def _():
        run_sc[...] = jnp.zeros_like(run_sc)

    vals = jnp.dot(h_ref[...], wr_ref[...], preferred_element_type=jnp.float32,
                   precision=lax.Precision.HIGHEST) + br_ref[...]
    lane = lax.broadcasted_iota(jnp.int32, (tm, LANES), 1)
    top_v, top_e, picked = [], [], []
    for _ in range(TOP_K):
        m = jnp.max(vals, axis=-1, keepdims=True)
        idx = jnp.min(jnp.where(vals == m, lane, LANES), axis=-1, keepdims=True)
        sel = lane == idx
        top_v.append(m)
        top_e.append(idx)
        picked.append(sel)
        vals = jnp.where(sel, -jnp.inf, vals)
    ex = [jnp.exp(v - top_v[0]) for v in top_v]
    inv_den = 1.0 / (ex[0] + ex[1] + ex[2] + ex[3])
    any_pick = picked[0] | picked[1] | picked[2] | picked[3]
    rr = lax.broadcasted_iota(jnp.int32, (tm, tm), 0)
    cc = lax.broadcasted_iota(jnp.int32, (tm, tm), 1)
    earlier = (cc < rr).astype(jnp.bfloat16)
    before = jnp.dot(earlier, any_pick.astype(jnp.bfloat16), preferred_element_type=jnp.float32) + run_sc[...]
    e_out = jnp.zeros((tm, LANES), jnp.int32)
    g_out = jnp.zeros((tm, LANES), jnp.float32)
    r_out = jnp.zeros((tm, LANES), jnp.int32)
    for k in range(TOP_K):
        rank = jnp.sum(jnp.where(picked[k], before, 0.0), axis=-1, keepdims=True).astype(jnp.int32)
        e_out = jnp.where(lane == k, top_e[k], e_out)
        g_out = jnp.where(lane == k, ex[k] * inv_den, g_out)
        r_out = jnp.where(lane == k, rank, r_out)
    e_ref[...] = e_out
    g_ref[...] = g_out
    rk_ref[...] = r_out
    run_sc[...] += jnp.sum(any_pick.astype(jnp.float32), axis=0, keepdims=True)
    cnt_ref[...] = run_sc[...]


def _route(h, w_router, b_router):
    n, dm = h.shape
    tm = MM_TM
    w_r = jnp.zeros((dm, LANES), jnp.float32).at[:, :N_EXPERTS].set(w_router)
    b_r = jnp.full((1, LANES), NEG_INF, jnp.float32).at[0, :N_EXPERTS].set(b_router)
    row = lambda i: (i, 0)
    whole = lambda i: (0, 0)
    e, g, rk, cnt = pl.pallas_call(
        _route_kernel,
        grid=(n // tm,),
        in_specs=[pl.BlockSpec((tm, dm), row), pl.BlockSpec((dm, LANES), whole), pl.BlockSpec((1, LANES), whole)],
        out_specs=[pl.BlockSpec((tm, LANES), row), pl.BlockSpec((tm, LANES), row), pl.BlockSpec((tm, LANES), row),
                   pl.BlockSpec((1, LANES), whole)],
        out_shape=[jax.ShapeDtypeStruct((n, LANES), jnp.int32), jax.ShapeDtypeStruct((n, LANES), jnp.float32),
                   jax.ShapeDtypeStruct((n, LANES), jnp.int32), jax.ShapeDtypeStruct((1, LANES), jnp.float32)],
        scratch_shapes=[pltpu.VMEM((1, LANES), jnp.float32)],
        compiler_params=_cparams(("arbitrary",)),
        name="router",
    )(h, w_r, b_r)
    return e[:, :TOP_K], g[:, :TOP_K], rk[:, :TOP_K], cnt[0, :N_EXPERTS].astype(jnp.int32)


def _moe_ffn(h, h_rows, x, norm_g, mods, w_router, b_router, w_gu, b_gu, w_dn, b_dn, *, layer, t_rows, seq):
    n, dm = h.shape
    nk = n * TOP_K
    top_e, gates, rank, counts = _route(h, w_router, b_router)
    flat_e = top_e.reshape(-1)
    padded = (counts + MOE_TM - 1) // MOE_TM * MOE_TM
    pad_end = jnp.cumsum(padded)
    pad_start = pad_end - padded
    grp_start = jnp.cumsum(counts) - counts
    dest = pad_start[flat_e] + rank.reshape(-1)
    n_blk = (nk + MOE_TM - 1) // MOE_TM + N_EXPERTS
    rows = n_blk * MOE_TM
    blk_start = jnp.arange(n_blk, dtype=jnp.int32) * MOE_TM
    blk_e = jnp.minimum(jnp.sum((pad_end[None, :] <= blk_start[:, None]).astype(jnp.int32), axis=1),
                        N_EXPERTS - 1)
    n_used = (pad_end[-1] // MOE_TM).astype(jnp.int32).reshape(1)
    order = jnp.argsort(flat_e)
    blk_shift = (grp_start - pad_start)[blk_e]
    sorted_pos = (jnp.arange(rows, dtype=jnp.int32).reshape(n_blk, MOE_TM) + blk_shift[:, None]).reshape(-1)
    row_tok = (order[jnp.clip(sorted_pos, 0, nk - 1)] // TOP_K).astype(jnp.int32)
    row_tok = jnp.where(sorted_pos < nk, row_tok, jnp.arange(rows, dtype=jnp.int32) % n)
    xs = h_rows[row_tok]
    y = _moe_experts(blk_e, n_used, xs, w_gu, b_gu, w_dn, b_dn, layer=layer)
    yk = y[dest.reshape(n, TOP_K).T.reshape(-1)]
    return _moe_combine(yk, gates, x, norm_g, mods, t_rows=t_rows, seq=seq)


def _rope_tables(rows, n_ctx, batch):
    row = jnp.repeat(jnp.arange(rows, dtype=jnp.float32), GRID_W)
    col = jnp.tile(jnp.arange(GRID_W, dtype=jnp.float32), rows)
    n_freq = HEAD_DIM // 4
    inv = ROPE_THETA ** (-jnp.arange(n_freq, dtype=jnp.float32) / n_freq)
    ang_r = row[:, None] * inv
    ang_c = col[:, None] * inv
    cos_h = jnp.concatenate([jnp.cos(ang_r), jnp.cos(ang_r), jnp.cos(ang_c), jnp.cos(ang_c)], axis=1)
    sin_h = jnp.concatenate([-jnp.sin(ang_r), jnp.sin(ang_r), -jnp.sin(ang_c), jnp.sin(ang_c)], axis=1)
    cos_t = jnp.concatenate([cos_h, jnp.ones((n_ctx, HEAD_DIM), jnp.float32)], axis=0)
    sin_t = jnp.concatenate([sin_h, jnp.zeros((n_ctx, HEAD_DIM), jnp.float32)], axis=0)
    reps = (batch, LANES // HEAD_DIM)
    return jnp.tile(cos_t, reps), jnp.tile(sin_t, reps)


def _rope_partner(n_cols):
    idx = jnp.arange(n_cols)
    quarter = HEAD_DIM // 4
    return jnp.where((idx % (2 * quarter)) < quarter, idx + quarter, idx - quarter)


def kernel(x, c, ctx, c_ctx, w_mod, b_mod, norm_g, w_in, attn_sink, diff_lambda, diff_norm_w,
           hgrn_lb_logits, hgrn_norm_w, w_branch, w_out, w_router, b_router, w_gate_up, b_gate_up,
           w_down, b_down):
    b, s, dm = x.shape
    l = ctx.shape[1]
    t = s + l
    f32 = jnp.float32
    bf16 = jnp.bfloat16
    cos_t, sin_t = _rope_tables(s // GRID_W, l, b)
    lb_cum = jnp.cumsum(jax.nn.softmax(hgrn_lb_logits.astype(f32), axis=0), axis=0)
    lower_bounds = lb_cum - lb_cum[0:1]
    cond_rows = 16
    cond = jnp.zeros((cond_rows, dm), f32).at[:b].set(jax.nn.silu(c)).at[b].set(jax.nn.silu(c_ctx))
    scale = HEAD_DIM ** -0.5
    xa = jnp.concatenate([x, ctx], axis=1).reshape(b * t, dm)
    h_moe = None
    for layer in range(DEPTH):
        lam_init = 0.8 - 0.6 * math.exp(-0.3 * layer)
        mod_all = _matmul(cond, w_mod[layer], tm=cond_rows, tn=3 * dm, out_dtype=f32,
                          precision=lax.Precision.HIGHEST, name="adaln") + b_mod[layer]
        mod = mod_all[:b].reshape(b, 1, 6, 1, dm)
        mod_c = jnp.broadcast_to(mod_all[b:b + 1].reshape(1, 1, 6, 1, dm), (b, 1, 6, 1, dm))
        mods = jnp.concatenate([mod, mod_c], axis=1).reshape(2 * b, 6, 1, dm)
        ng = norm_g[layer].reshape(4, 1, dm)
        g0 = ng[0:1]
        geo = dict(t_rows=t, seq=s)

        w = w_in[layer]
        w_aq = (w[:, OFF_A:OFF_A + A_Q_W] * scale).reshape(dm, A_HKV, A_GROUP, HEAD_DIM)
        pad = jnp.zeros((dm, A_GROUP, HEAD_DIM), f32)
        w_aq8 = jnp.concatenate([jnp.concatenate([w_aq[:, 0], pad], axis=-1),
                                 jnp.concatenate([pad, w_aq[:, 1]], axis=-1)], axis=1).reshape(dm, A_Q8_W)
        w_a = jnp.concatenate([w_aq8, w[:, OFF_A + A_Q_W:OFF_A + A_W]], axis=1).astype(bf16)
        w_b = w[:, OFF_B:OFF_B + B_W].at[:, :B_QK_W].multiply(scale * LOG2_E).astype(bf16)
        proj_a = _proj_rope(xa, g0, mods, w_a, w_a[:, _rope_partner(A_ROPE_W2)], cos_t, sin_t,
                            rope_w=A_ROPE_W2, name="proj_a", **geo).reshape(b, t, A_W2)
        bq, bk_hm, bv1_hm = _proj_b(xa, g0, mods, w_b, w_b[:, _rope_partner(B_ROPE_W)], cos_t, sin_t, **geo)
        bq = bq.reshape(b, t, B_QK_W)
        lb = lower_bounds[layer]
        lb_par = jnp.stack([jnp.log(lb[0]), jnp.log1p(-lb[0]), jnp.log(lb[1]), jnp.log1p(-lb[1])])
        cq, ci, lf, cg = _proj_hgrn(xa, g0, mods, w[:, OFF_C:OFF_C + C_W].astype(bf16),
                                    lb_par.reshape(4, 1, C_K_W), **geo)
        gates = _proj_plain(xa, g0, mods, w[:, OFF_GATE:].astype(bf16), name="gate_proj", **geo)

        sink32 = attn_sink[layer].astype(f32)
        o_a = _window_attention(proj_a, sink32, seq=s)

        lp = diff_lambda[layer].astype(f32)
        lam = (jnp.exp(jnp.sum(lp[0] * lp[1])) - jnp.exp(jnp.sum(lp[2] * lp[3])) + lam_init).reshape(1).astype(f32)
        o_b = _diff_attention(lam, bq, bk_hm, bv1_hm, tq=FLASH_TQ, tk=FLASH_TK, q_block_offset=0,
                              n_q_blocks=s // FLASH_TQ, k_block_offset=0, nk=t // FLASH_TK)
        o_bx = _diff_attention(lam, bq, bk_hm, bv1_hm, tq=l, tk=l, q_block_offset=s // l, n_q_blocks=1,
                               k_block_offset=s // l, nk=1)
        o_b = jnp.concatenate([o_b, o_bx], axis=1)

        o_cf, o_cb = _hgrn_scan(cq.reshape(b, t, C_K_W), ci.reshape(b, t, C_V_W), lf.reshape(2, b, t, C_K_W),
                                n_ctx_blocks=l // HGRN_TB)

        col = jnp.arange(A_Q_W)
        head_of_col = ((col % LANES) // HEAD_DIM) * A_GROUP + col // LANES
        w_br = w_branch[layer].at[0].set(w_branch[layer][0][head_of_col * HEAD_DIM + col % HEAD_DIM])
        xa, h_moe, h_rows = _merge(o_a.reshape(b * t, A_Q_W), o_b.reshape(b * t, B_V_W), o_cf.reshape(b * t, C_V_W),
                           o_cb.reshape(b * t, C_V_W), cg,
                           gates, xa, ng, mods, diff_norm_w[layer].reshape(1, B_VD).astype(f32),
                           hgrn_norm_w[layer].reshape(1, C_DV).astype(f32),
                           w_br.astype(bf16), w_out[layer].astype(bf16),
                           diff_out_scale=1 - lam_init, **geo)

        xa = _moe_ffn(h_moe, h_rows.reshape(2 * b * t, dm), xa, ng, mods, w_router[layer], b_router[layer],
                      w_gate_up, b_gate_up, w_down, b_down, layer=layer, **geo)
    return xa.reshape(b, t, dm)[:, :s]
```

```python
import functools
import math

import jax
import jax.numpy as jnp
from jax import lax
from jax.experimental import pallas as pl
from jax.experimental.pallas import tpu as pltpu

D_MODEL = 1024
DEPTH = 2
GRID_W = 64
HEAD_DIM = 64
ROPE_THETA = 10000.0
RMS_EPS = 1e-6
NEG_INF = -1e30
BLOCK = 128
A_HQ = 8
A_HKV = 2
A_GROUP = A_HQ // A_HKV
WINDOW = 128
B_H = 4
B_HD = HEAD_DIM
B_VD = 2 * HEAD_DIM
C_H = 4
C_DK = 128
C_DV = 128
N_BRANCH = 3
BRANCH_W = 512
A_Q_W = A_HQ * HEAD_DIM
A_KV_W = A_HKV * HEAD_DIM
B_QK_W = B_H * 2 * B_HD
B_V_W = B_H * B_VD
C_K_W = C_H * C_DK
C_V_W = C_H * C_DV
GATE_W = N_BRANCH * D_MODEL
N_EXPERTS = 32
TOP_K = 4
EXPERT_FF = D_MODEL
SWIGLU_LIMIT = 7.0
SWIGLU_ALPHA = 1.702

A_W = A_Q_W + 2 * A_KV_W
A_ROPE_W = A_Q_W + A_KV_W
B_W = 2 * B_QK_W + B_V_W
B_ROPE_W = 2 * B_QK_W
C_W = 3 * C_K_W + 2 * C_V_W
OFF_A = 0
OFF_B = OFF_A + A_W
OFF_C = OFF_B + B_W
OFF_GATE = OFF_C + C_W

V7X_VMEM_LIMIT_BYTES = 56 * 1024 * 1024
LANES = 128
LOG2_E = math.log2(math.e)

MM_TM = 512
HALF_TM = MM_TM // 2
FLASH_TQ = 2048
FLASH_TK = 768
HGRN_TB = 256
HGRN_SUB = 16
HGRN_UNROLL = 4
MOE_TM = 512


def _cparams(sem):
    return pltpu.CompilerParams(dimension_semantics=sem, vmem_limit_bytes=V7X_VMEM_LIMIT_BYTES)


def _rms_rows(x, g):
    return (x * lax.rsqrt(jnp.mean(x * x, axis=-1, keepdims=True) + RMS_EPS)) * g


def _modulated_tile(x_ref, g, mod_a, mod_b, shift_slot, scale_slot):
    out = []
    for half, mod in enumerate((mod_a, mod_b)):
        x = x_ref[half * HALF_TM:(half + 1) * HALF_TM, :]
        out.append(_rms_rows(x, g) * (1.0 + mod[scale_slot]) + mod[shift_slot])
    return jnp.concatenate(out, axis=0)


def _residual_tile(x_ref, y, g, mod_a, mod_b, gate_slot):
    out = []
    for half, mod in enumerate((mod_a, mod_b)):
        rows = slice(half * HALF_TM, (half + 1) * HALF_TM)
        out.append(x_ref[rows, :] + mod[gate_slot] * _rms_rows(y[rows, :], g))
    return out


def _mod_specs(t_rows, seq):
    halves_per_batch = t_rows // HALF_TM
    lat_halves = seq // HALF_TM

    def row(u):
        return (u // halves_per_batch) * 2 + jnp.where(u % halves_per_batch >= lat_halves, 1, 0)

    blk = (1, 6, 1, D_MODEL)
    return [pl.BlockSpec(blk, lambda i: (row(2 * i), 0, 0, 0)),
            pl.BlockSpec(blk, lambda i: (row(2 * i + 1), 0, 0, 0))]


def _rope_lane_block(z_blk, cos, sin, first):
    quarter = HEAD_DIM // 4
    partner = jnp.where(first, pltpu.roll(z_blk, LANES - quarter, 1), pltpu.roll(z_blk, quarter, 1))
    return z_blk * cos + partner * sin


def _rope_first_mask(rows):
    lane = lax.broadcasted_iota(jnp.int32, (rows, LANES), 1)
    return (lane % (HEAD_DIM // 2)) < (HEAD_DIM // 4)


def _proj_rope_kernel(x_ref, g_ref, ma_ref, mb_ref, w_ref, cos_ref, sin_ref, o_ref, *, rope_w):
    h = _modulated_tile(x_ref, g_ref[0], ma_ref[0], mb_ref[0], 0, 1).astype(jnp.bfloat16)
    z = jnp.dot(h, w_ref[...], preferred_element_type=jnp.float32)
    cos, sin, first = cos_ref[...], sin_ref[...], _rope_first_mask(MM_TM)
    for c in range(rope_w // LANES):
        cols = slice(c * LANES, (c + 1) * LANES)
        o_ref[:, cols] = _rope_lane_block(z[:, cols], cos, sin, first).astype(o_ref.dtype)
    o_ref[:, rope_w:] = z[:, rope_w:].astype(o_ref.dtype)


def _proj_b_kernel(x_ref, g_ref, ma_ref, mb_ref, w_ref, cos_ref, sin_ref, q_ref, k_ref, v1_ref):
    h = _modulated_tile(x_ref, g_ref[0], ma_ref[0], mb_ref[0], 0, 1).astype(jnp.bfloat16)
    z = jnp.dot(h, w_ref[...], preferred_element_type=jnp.float32)
    cos, sin, first = cos_ref[...], sin_ref[...], _rope_first_mask(MM_TM)
    for c in range(B_QK_W // LANES):
        cols = slice(c * LANES, (c + 1) * LANES)
        q_ref[:, cols] = _rope_lane_block(z[:, cols], cos, sin, first).astype(q_ref.dtype)
    for hh in range(B_H):
        kcols = slice(B_QK_W + hh * 2 * B_HD, B_QK_W + (hh + 1) * 2 * B_HD)
        k_ref[hh] = _rope_lane_block(z[:, kcols], cos, sin, first).astype(k_ref.dtype)
        v1_ref[hh, :, :B_VD] = z[:, B_ROPE_W + hh * B_VD:B_ROPE_W + (hh + 1) * B_VD].astype(v1_ref.dtype)
        v1_ref[hh, :, B_VD:] = jnp.ones((MM_TM, B_VD), v1_ref.dtype)


def _proj_plain_kernel(x_ref, g_ref, ma_ref, mb_ref, w_ref, o_ref):
    h = _modulated_tile(x_ref, g_ref[0], ma_ref[0], mb_ref[0], 0, 1).astype(jnp.bfloat16)
    o_ref[...] = jnp.dot(h, w_ref[...], preferred_element_type=jnp.float32).astype(o_ref.dtype)


def _log_forget(z, log_lb, log_1m_lb):
    log_sig = jnp.minimum(z, 0.0) - jnp.log(1.0 + jnp.exp(-jnp.abs(z)))
    c = log_1m_lb + log_sig
    return jnp.maximum(log_lb, c) + jnp.log(1.0 + jnp.exp(-jnp.abs(log_lb - c)))


def _proj_hgrn_kernel(x_ref, g_ref, ma_ref, mb_ref, w_ref, lb_ref, q_ref, v_ref, lf_ref, cg_ref):
    h = _modulated_tile(x_ref, g_ref[0], ma_ref[0], mb_ref[0], 0, 1).astype(jnp.bfloat16)
    z = jnp.dot(h, w_ref[...], preferred_element_type=jnp.float32)
    zq = z[:, :C_K_W]
    q_ref[...] = zq * jax.nn.sigmoid(zq)
    v_ref[...] = z[:, C_K_W:C_K_W + C_V_W]
    off = C_K_W + C_V_W
    lf_ref[0] = _log_forget(z[:, off:off + C_K_W], lb_ref[0], lb_ref[1])
    lf_ref[1] = _log_forget(z[:, off + C_K_W:off + 2 * C_K_W], lb_ref[2], lb_ref[3])
    cg_ref[...] = z[:, off + 2 * C_K_W:].astype(cg_ref.dtype)


def _proj_common_specs(t_rows, seq):
    row = lambda i: (i, 0)
    return [pl.BlockSpec((MM_TM, D_MODEL), row),
            pl.BlockSpec((1, 1, D_MODEL), lambda i: (0, 0, 0))] + _mod_specs(t_rows, seq)


def _proj_rope(x, g, mods, w, cos, sin, *, t_rows, seq, rope_w, name):
    m = x.shape[0]
    n = w.shape[1]
    row = lambda i: (i, 0)
    whole = lambda i: (0, 0)
    return pl.pallas_call(
        functools.partial(_proj_rope_kernel, rope_w=rope_w),
        grid=(m // MM_TM,),
        in_specs=_proj_common_specs(t_rows, seq) + [
            pl.BlockSpec((D_MODEL, n), whole),
            pl.BlockSpec((MM_TM, LANES), row), pl.BlockSpec((MM_TM, LANES), row)],
        out_specs=pl.BlockSpec((MM_TM, n), row),
        out_shape=jax.ShapeDtypeStruct((m, n), jnp.bfloat16),
        compiler_params=_cparams(("arbitrary",)),
        name=name,
    )(x, g, mods, mods, w, cos, sin)


def _proj_b(x, g, mods, w, cos, sin, *, t_rows, seq):
    m = x.shape[0]
    row = lambda i: (i, 0)
    whole = lambda i: (0, 0)
    heads = lambda i: (0, i, 0)
    bf16 = jnp.bfloat16
    return pl.pallas_call(
        _proj_b_kernel,
        grid=(m // MM_TM,),
        in_specs=_proj_common_specs(t_rows, seq) + [
            pl.BlockSpec((D_MODEL, B_W), whole),
            pl.BlockSpec((MM_TM, LANES), row), pl.BlockSpec((MM_TM, LANES), row)],
        out_specs=[pl.BlockSpec((MM_TM, B_QK_W), row), pl.BlockSpec((B_H, MM_TM, 2 * B_HD), heads),
                   pl.BlockSpec((B_H, MM_TM, 2 * B_VD), heads)],
        out_shape=[jax.ShapeDtypeStruct((m, B_QK_W), bf16), jax.ShapeDtypeStruct((B_H, m, 2 * B_HD), bf16),
                   jax.ShapeDtypeStruct((B_H, m, 2 * B_VD), bf16)],
        compiler_params=_cparams(("arbitrary",)),
        name="proj_b",
    )(x, g, mods, mods, w, cos, sin)


def _proj_plain(x, g, mods, w, *, t_rows, seq, name):
    m = x.shape[0]
    n = w.shape[1]
    row = lambda i: (i, 0)
    return pl.pallas_call(
        _proj_plain_kernel,
        grid=(m // MM_TM,),
        in_specs=_proj_common_specs(t_rows, seq) + [pl.BlockSpec((D_MODEL, n), lambda i: (0, 0))],
        out_specs=pl.BlockSpec((MM_TM, n), row),
        out_shape=jax.ShapeDtypeStruct((m, n), jnp.bfloat16),
        compiler_params=_cparams(("arbitrary",)),
        name=name,
    )(x, g, mods, mods, w)


def _proj_hgrn(x, g, mods, w, lb_par, *, t_rows, seq):
    m = x.shape[0]
    row = lambda i: (i, 0)
    f32 = jnp.float32
    return pl.pallas_call(
        _proj_hgrn_kernel,
        grid=(m // MM_TM,),
        in_specs=_proj_common_specs(t_rows, seq) + [
            pl.BlockSpec((D_MODEL, C_W), lambda i: (0, 0)),
            pl.BlockSpec((4, 1, C_K_W), lambda i: (0, 0, 0))],
        out_specs=[pl.BlockSpec((MM_TM, C_K_W), row), pl.BlockSpec((MM_TM, C_V_W), row),
                   pl.BlockSpec((2, MM_TM, C_K_W), lambda i: (0, i, 0)), pl.BlockSpec((MM_TM, C_V_W), row)],
        out_shape=[jax.ShapeDtypeStruct((m, C_K_W), f32), jax.ShapeDtypeStruct((m, C_V_W), f32),
                   jax.ShapeDtypeStruct((2, m, C_K_W), f32), jax.ShapeDtypeStruct((m, C_V_W), jnp.bfloat16)],
        compiler_params=_cparams(("arbitrary",)),
        name="hgrn_proj",
    )(x, g, mods, mods, w, lb_par)


def _mm_kernel(x_ref, w_ref, o_ref, *, precision):
    o_ref[...] = jnp.dot(x_ref[...], w_ref[...], preferred_element_type=jnp.float32,
                         precision=precision).astype(o_ref.dtype)


def _matmul(x, w, *, tm, tn, out_dtype, precision=None, name):
    m, k = x.shape
    n = w.shape[1]
    assert m % tm == 0 and n % tn == 0, (m, tm, n, tn)
    return pl.pallas_call(
        functools.partial(_mm_kernel, precision=precision),
        grid=(n // tn, m // tm),
        in_specs=[pl.BlockSpec((tm, k), lambda j, i: (i, 0)),
                  pl.BlockSpec((k, tn), lambda j, i: (0, j))],
        out_specs=pl.BlockSpec((tm, tn), lambda j, i: (i, j)),
        out_shape=jax.ShapeDtypeStruct((m, n), out_dtype),
        compiler_params=_cparams(("arbitrary", "arbitrary")),
        name=name,
    )(x, w)


A_Q8_W = A_HQ * LANES
A_W2 = A_Q8_W + 2 * A_KV_W
A_ROPE_W2 = A_Q8_W + A_KV_W
A_K_BLK = A_Q8_W // A_KV_W
A_V_BLK = A_K_BLK + 1
A_STEP_BLOCKS = 2


def _attn_a_kernel(sink_ref, q_ref, kp_ref, kc_ref, kn_ref, vp_ref, vc_ref, vn_ref, kx_ref, vx_ref,
                   o_ref, *, seq):
    n = pl.program_id(1)
    k_loc = jnp.concatenate([kp_ref[0], kc_ref[0], kn_ref[0]], axis=0)
    v_loc = jnp.concatenate([vp_ref[0], vc_ref[0], vn_ref[0]], axis=0)
    n_keys = 3 * BLOCK + kx_ref.shape[1]
    qi = lax.broadcasted_iota(jnp.int32, (BLOCK, n_keys), 0)
    kj = lax.broadcasted_iota(jnp.int32, (BLOCK, n_keys), 1)
    lane = lax.broadcasted_iota(jnp.int32, (BLOCK, LANES), 1)
    for j in range(A_STEP_BLOCKS):
        blk = n * A_STEP_BLOCKS + j
        rows = slice(j * BLOCK, (j + 1) * BLOCK)
        k_all = jnp.concatenate([k_loc[j * BLOCK:(j + 3) * BLOCK, :], kx_ref[0]], axis=0)
        v_all = jnp.concatenate([v_loc[j * BLOCK:(j + 3) * BLOCK, :], vx_ref[0]], axis=0)
        kpos = blk * BLOCK + kj - BLOCK
        local_ok = (jnp.abs(kj - BLOCK - qi) <= WINDOW) & (kpos >= 0) & (kpos < seq) & (blk * BLOCK < seq)
        mask = local_ok | (kj >= 3 * BLOCK)
        q8 = jnp.concatenate([q_ref[0, rows, hq * LANES:(hq + 1) * LANES] for hq in range(A_HQ)], axis=0)
        s = lax.dot_general(q8, k_all, (((1,), (1,)), ((), ())), preferred_element_type=jnp.float32)
        p_blocks, inv_l = [], []
        for hq in range(A_HQ):
            sh = jnp.where(mask, s[hq * BLOCK:(hq + 1) * BLOCK, :], NEG_INF)
            sk = sink_ref[hq]
            m = jnp.maximum(jnp.max(sh, axis=-1, keepdims=True), sk)
            p = jnp.exp(sh - m)
            inv_l.append(1.0 / (jnp.sum(p, axis=-1, keepdims=True) + jnp.exp(sk - m)))
            p_blocks.append(p.astype(v_all.dtype))
        o = jnp.dot(jnp.concatenate(p_blocks, axis=0), v_all, preferred_element_type=jnp.float32)
        for g in range(A_GROUP):
            lo = o[g * BLOCK:(g + 1) * BLOCK, :] * inv_l[g]
            hi = o[(A_GROUP + g) * BLOCK:(A_GROUP + g + 1) * BLOCK, :] * inv_l[A_GROUP + g]
            o_ref[0, rows, g * LANES:(g + 1) * LANES] = jnp.where(lane < HEAD_DIM, lo, hi).astype(o_ref.dtype)


def _window_attention(proj_a, sink, *, seq):
    b, t, _ = proj_a.shape
    l = t - seq
    step = A_STEP_BLOCKS * BLOCK
    nb = seq // BLOCK
    ns = seq // step
    ctx_blk = seq // l
    q_map = lambda bi, n, *_: (bi, n, 0)
    edge = lambda col, fn: pl.BlockSpec((1, BLOCK, A_KV_W), lambda bi, n, *_: (bi, fn(n), col))
    mid = lambda col: pl.BlockSpec((1, step, A_KV_W), lambda bi, n, *_: (bi, jnp.minimum(n, ns - 1), col))
    prev = lambda n: jnp.clip(n * A_STEP_BLOCKS - 1, 0, nb - 1)
    nxt = lambda n: jnp.minimum((n + 1) * A_STEP_BLOCKS, nb - 1)
    ctx = lambda col: pl.BlockSpec((1, l, A_KV_W), lambda bi, n, *_: (bi, ctx_blk, col))
    return pl.pallas_call(
        functools.partial(_attn_a_kernel, seq=seq),
        grid_spec=pltpu.PrefetchScalarGridSpec(
            num_scalar_prefetch=1, grid=(b, t // step),
            in_specs=[pl.BlockSpec((1, step, A_Q8_W), q_map),
                      edge(A_K_BLK, prev), mid(A_K_BLK), edge(A_K_BLK, nxt),
                      edge(A_V_BLK, prev), mid(A_V_BLK), edge(A_V_BLK, nxt),
                      ctx(A_K_BLK), ctx(A_V_BLK)],
            out_specs=pl.BlockSpec((1, step, A_Q_W), q_map)),
        out_shape=jax.ShapeDtypeStruct((b, t, A_Q_W), jnp.bfloat16),
        compiler_params=_cparams(("arbitrary", "arbitrary")),
        name="window_gqa",
    )(sink, *([proj_a] * 9))


def _diff_flash_kernel(lam_ref, q_ref, k_ref, v_ref, o_ref, q2_sc, sa_sc, sb_sc, m_sc, acc_sc, *, tq, tk, nk):
    u = pl.program_id(2)
    new_q = u % nk == 0

    @pl.when(u == 0)
    def _():
        m_sc[...] = jnp.full_like(m_sc, NEG_INF)
        acc_sc[...] = jnp.zeros_like(acc_sc)
        sb_sc[...] = jnp.full_like(sb_sc, NEG_INF)

    @pl.when(new_q)
    def _():
        q = q_ref[0]
        lane = lax.broadcasted_iota(jnp.int32, q.shape, 1)
        q2_sc[:tq, :] = jnp.where(lane < B_HD, q, jnp.zeros_like(q))
        q2_sc[tq:, :] = jnp.where(lane >= B_HD, q, jnp.zeros_like(q))

    def step(s_new_ref, s_old_ref):
        m_prev = m_sc[...]
        m_new = jnp.maximum(m_prev, jnp.max(s_old_ref[...], axis=-1, keepdims=True))
        alpha = jnp.exp2(m_prev - m_new)
        p = jnp.exp2(s_old_ref[...] - jnp.tile(m_new, (1, tk // LANES))).astype(jnp.bfloat16)
        acc_sc[...] = jnp.tile(alpha, (1, 2)) * acc_sc[...] + jnp.dot(
            p, v_ref[0], preferred_element_type=jnp.float32)
        m_sc[...] = m_new
        s_new_ref[...] = lax.dot_general(q2_sc[...], k_ref[0], (((1,), (1,)), ((), ())),
                                         preferred_element_type=jnp.float32)

    @pl.when(u % 2 == 0)
    def _():
        step(sa_sc, sb_sc)

    @pl.when(u % 2 == 1)
    def _():
        step(sb_sc, sa_sc)

    @pl.when(new_q & (u > 0))
    def _():
        acc = acc_sc[...]
        o = acc[:, :B_VD] / acc[:, B_VD:]
        o_ref[0] = (o[:tq, :] - lam_ref[0] * o[tq:, :]).astype(o_ref.dtype)
        m_sc[...] = jnp.full_like(m_sc, NEG_INF)
        acc_sc[...] = jnp.zeros_like(acc_sc)


def _diff_attention(lam, q, k_hm, v1_hm, *, tq, tk, q_block_offset, n_q_blocks, k_block_offset, nk):
    b, t, _ = q.shape
    kb = t // tk
    n_steps = n_q_blocks * nk + 1
    k_blk = lambda bi, u: bi * kb + k_block_offset + u % nk
    in_specs = [pl.BlockSpec((1, tq, 2 * B_HD),
                             lambda bi, h, u, *_: (bi, jnp.minimum(u // nk, n_q_blocks - 1) + q_block_offset, h)),
                pl.BlockSpec((1, tk, 2 * B_HD), lambda bi, h, u, *_: (h, k_blk(bi, u), 0)),
                pl.BlockSpec((1, tk, 2 * B_VD), lambda bi, h, u, *_: (h, k_blk(bi, jnp.maximum(u - 1, 0)), 0))]
    return pl.pallas_call(
        functools.partial(_diff_flash_kernel, tq=tq, tk=tk, nk=nk),
        grid_spec=pltpu.PrefetchScalarGridSpec(
            num_scalar_prefetch=1, grid=(b, B_H, n_steps),
            in_specs=in_specs,
            out_specs=pl.BlockSpec((1, tq, B_VD), lambda bi, h, u, *_: (bi, jnp.maximum(u - 1, 0) // nk, h)),
            scratch_shapes=[pltpu.VMEM((2 * tq, 2 * B_HD), jnp.bfloat16),
                            pltpu.VMEM((2 * tq, tk), jnp.float32),
                            pltpu.VMEM((2 * tq, tk), jnp.float32),
                            pltpu.VMEM((2 * tq, LANES), jnp.float32),
                            pltpu.VMEM((2 * tq, 2 * B_VD), jnp.float32)]),
        out_shape=jax.ShapeDtypeStruct((b, n_q_blocks * tq, B_V_W), jnp.float32),
        compiler_params=_cparams(("arbitrary", "arbitrary", "arbitrary")),
        name="diff_flash",
    )(lam, q, k_hm, v1_hm)


def _hgrn_sub_chunk(q_ref, v_ref, lf_ref, o_ref, st_ref, r0, tri, rev):
    half = HGRN_SUB // 2
    row8 = lax.broadcasted_iota(jnp.int32, (half, C_DK), 0)
    lf = lf_ref[pl.ds(r0, HGRN_SUB), :] * LOG2_E
    lf_a = lf.astype(jnp.bfloat16)
    rem = lf - lf_a.astype(jnp.float32)
    lf_b = rem.astype(jnp.bfloat16)
    lf_c = (rem - lf_b.astype(jnp.float32)).astype(jnp.bfloat16)
    bl = (jnp.dot(tri, lf_a, preferred_element_type=jnp.float32)
          + jnp.dot(tri, lf_b, preferred_element_type=jnp.float32)
          + jnp.dot(tri, lf_c, preferred_element_type=jnp.float32))
    q = q_ref[pl.ds(r0, HGRN_SUB), :]
    v = v_ref[pl.ds(r0, HGRN_SUB), :]
    last = 0 if rev else HGRN_SUB - 1
    for hh in range(C_H):
        sl = slice(hh * C_DK, (hh + 1) * C_DK)
        lfh, blh, qh, vh = lf[:, sl], bl[:, sl], q[:, sl], v[:, sl]
        tot = blh[last:last + 1, :]
        kh = 1.0 - jnp.exp2(lfh)
        qt = qh * jnp.exp2(blh)
        kt = kh * jnp.exp2(tot - blh)
        st = st_ref[hh]
        o_sub = lax.dot_general(qt.astype(jnp.bfloat16), st.astype(jnp.bfloat16),
                                (((1,), (1,)), ((), ())), preferred_element_type=jnp.float32)
        parts = [o_sub[:half, :], o_sub[half:, :]]
        for ss in range(HGRN_SUB):
            ks = kh[ss:ss + 1, :]
            bs = blh[ss:ss + 1, :]
            vs = vh[ss:ss + 1, :]
            for p in range(2):
                lo = p * half
                if rev:
                    reached = lo <= ss
                    full = lo + half - 1 <= ss
                    ok = row8 + lo <= ss
                else:
                    reached = lo + half - 1 >= ss
                    full = lo >= ss
                    ok = row8 + lo >= ss
                if not reached:
                    continue
                diff = blh[lo:lo + half, :] - bs
                if not full:
                    diff = jnp.where(ok, diff, NEG_INF)
                col = jnp.sum((qh[lo:lo + half, :] * ks) * jnp.exp2(diff), axis=1, keepdims=True)
                parts[p] = parts[p] + col * vs
        o_ref[pl.ds(r0, half), sl] = parts[0]
        o_ref[pl.ds(r0 + half, half), sl] = parts[1]
        upd = lax.dot_general(vh.astype(jnp.bfloat16), kt.astype(jnp.bfloat16),
                              (((0,), (0,)), ((), ())), preferred_element_type=jnp.float32)
        st_ref[hh] = jnp.exp2(tot) * st + upd


def _hgrn_kernel(qf_ref, vf_ref, lff_ref, qb_ref, vb_ref, lfb_ref, of_ref, ob_ref, st_ref, *, tb):
    t = pl.program_id(1)
    nsub = tb // HGRN_SUB

    @pl.when(t == 0)
    def _():
        st_ref[...] = jnp.zeros_like(st_ref)

    rr = lax.broadcasted_iota(jnp.int32, (HGRN_SUB, HGRN_SUB), 0)
    cc = lax.broadcasted_iota(jnp.int32, (HGRN_SUB, HGRN_SUB), 1)
    tri_f = (cc <= rr).astype(jnp.bfloat16)
    tri_b = (cc >= rr).astype(jnp.bfloat16)

    def body(i, carry):
        rf = pl.multiple_of(i * HGRN_SUB, HGRN_SUB)
        rb = pl.multiple_of((nsub - 1 - i) * HGRN_SUB, HGRN_SUB)
        _hgrn_sub_chunk(qf_ref.at[0], vf_ref.at[0], lff_ref.at[0, 0], of_ref.at[0], st_ref.at[0], rf, tri_f, False)
        _hgrn_sub_chunk(qb_ref.at[0], vb_ref.at[0], lfb_ref.at[0, 0], ob_ref.at[0], st_ref.at[1], rb, tri_b, True)
        return carry

    lax.fori_loop(0, nsub, body, 0, unroll=HGRN_UNROLL)


def _hgrn_scan(q, v, lf, *, n_ctx_blocks):
    b, t, w = q.shape
    tb = HGRN_TB
    nblk = t // tb
    nlat = nblk - n_ctx_blocks

    def fwd_block(ti):
        return jnp.where(ti < n_ctx_blocks, nlat + ti, ti - n_ctx_blocks)

    def bwd_block(ti):
        return nblk - 1 - ti

    rows = lambda blk: pl.BlockSpec((1, tb, w), lambda bi, ti: (bi, blk(ti), 0))
    gate = lambda di, blk: pl.BlockSpec((1, 1, tb, w), lambda bi, ti: (di, bi, blk(ti), 0))
    out = jax.ShapeDtypeStruct((b, t, w), jnp.float32)
    return pl.pallas_call(
        functools.partial(_hgrn_kernel, tb=tb),
        grid=(b, nblk),
        in_specs=[rows(fwd_block), rows(fwd_block), gate(0, fwd_block),
                  rows(bwd_block), rows(bwd_block), gate(1, bwd_block)],
        out_specs=[rows(fwd_block), rows(bwd_block)],
        out_shape=[out, out],
        scratch_shapes=[pltpu.VMEM((2, C_H, C_DV, C_DK), jnp.float32)],
        compiler_params=_cparams(("arbitrary", "arbitrary")),
        name="hgrn2_scan",
    )(q, v, lf, q, v, lf)


def _head_rms(x, w, n_heads, width):
    out = []
    for hh in range(n_heads):
        blk = x[:, hh * width:(hh + 1) * width]
        out.append(_rms_rows(blk, w))
    return jnp.concatenate(out, axis=1)


def _merge_kernel(oa_ref, ob_ref, ocf_ref, ocb_ref, cg_ref, gt_ref, x_ref, ng_ref, ma_ref, mb_ref, dw_ref, hw_ref,
                  wbr_ref, wo_ref, xo_ref, ho_ref, hb_ref, *, diff_out_scale):
    ob = (_head_rms(ob_ref[...], dw_ref[...], B_H, B_VD) * diff_out_scale).astype(jnp.bfloat16)
    cg = cg_ref[...].astype(jnp.float32)
    oc = (_head_rms(ocf_ref[...] + ocb_ref[...], hw_ref[...], C_H, C_DV) * (cg * jax.nn.sigmoid(cg)))
    acc = None
    for i, o in enumerate((oa_ref[...], ob, oc.astype(jnp.bfloat16))):
        z = jnp.dot(o, wbr_ref[i], preferred_element_type=jnp.float32)
        g = jax.nn.sigmoid(gt_ref[:, i * D_MODEL:(i + 1) * D_MODEL].astype(jnp.float32))
        acc = g * z if acc is None else acc + g * z
    y = jnp.dot(acc.astype(jnp.bfloat16), wo_ref[...], preferred_element_type=jnp.float32)
    ma, mb = ma_ref[0], mb_ref[0]
    for half, (xn, mod) in enumerate(zip(_residual_tile(x_ref, y, ng_ref[1], ma, mb, 2), (ma, mb))):
        rows = slice(half * HALF_TM, (half + 1) * HALF_TM)
        xo_ref[rows, :] = xn
        h_next = _rms_rows(xn, ng_ref[2]) * (1.0 + mod[4]) + mod[3]
        ho_ref[rows, :] = h_next
        hb_ref[0, rows, :] = h_next.astype(hb_ref.dtype)
    hb_ref[1] = jnp.zeros((MM_TM, D_MODEL), hb_ref.dtype)


def _merge(oa, ob, ocf, ocb, cg, gates, x, norm_g, mods, diff_w, hgrn_w, w_br, w_o, *, t_rows, seq,
           diff_out_scale):
    m = x.shape[0]
    tm = MM_TM
    row = lambda i: (i, 0)
    f32 = jnp.float32
    return pl.pallas_call(
        functools.partial(_merge_kernel, diff_out_scale=diff_out_scale),
        grid=(m // tm,),
        in_specs=[pl.BlockSpec((tm, BRANCH_W), row), pl.BlockSpec((tm, BRANCH_W), row),
                  pl.BlockSpec((tm, BRANCH_W), row), pl.BlockSpec((tm, BRANCH_W), row),
                  pl.BlockSpec((tm, BRANCH_W), row),
                  pl.BlockSpec((tm, GATE_W), row), pl.BlockSpec((tm, D_MODEL), row),
                  pl.BlockSpec((4, 1, D_MODEL), lambda i: (0, 0, 0))] + _mod_specs(t_rows, seq) + [
                  pl.BlockSpec((1, B_VD), lambda i: (0, 0)), pl.BlockSpec((1, C_DV), lambda i: (0, 0)),
                  pl.BlockSpec((N_BRANCH, BRANCH_W, D_MODEL), lambda i: (0, 0, 0)),
                  pl.BlockSpec((D_MODEL, D_MODEL), lambda i: (0, 0))],
        out_specs=[pl.BlockSpec((tm, D_MODEL), row), pl.BlockSpec((tm, D_MODEL), row),
                   pl.BlockSpec((2, tm, D_MODEL), lambda i: (0, i, 0))],
        out_shape=[jax.ShapeDtypeStruct((m, D_MODEL), f32), jax.ShapeDtypeStruct((m, D_MODEL), f32),
                   jax.ShapeDtypeStruct((2, m, D_MODEL), jnp.bfloat16)],
        compiler_params=_cparams(("arbitrary",)),
        name="branch_merge",
    )(oa, ob, ocf, ocb, cg, gates, x, norm_g, mods, mods, diff_w, hgrn_w, w_br, w_o)


def _moe_kernel(blk_e_ref, n_used_ref, x_ref, wgu_ref, bgu_ref, wdn_ref, bdn_ref, y_ref, wgu_sc, wdn_sc):
    i = pl.program_id(0)
    n_used = n_used_ref[0]
    e = blk_e_ref[i]
    e_prev = blk_e_ref[jnp.maximum(i - 1, 0)]

    @pl.when((i == 0) | (e != e_prev))
    def _():
        wgu_sc[...] = wgu_ref[0, 0].astype(jnp.bfloat16)
        wdn_sc[...] = wdn_ref[0, 0].astype(jnp.bfloat16)

    @pl.when(i < n_used)
    def _():
        gu = jnp.dot(x_ref[...], wgu_sc[...], preferred_element_type=jnp.float32) + bgu_ref[0, 0]
        g = jnp.minimum(gu[:, :EXPERT_FF], SWIGLU_LIMIT)
        u = jnp.clip(gu[:, EXPERT_FF:], -SWIGLU_LIMIT, SWIGLU_LIMIT)
        act = (u + 1.0) * (g * jax.nn.sigmoid(SWIGLU_ALPHA * g))
        y = jnp.dot(act.astype(jnp.bfloat16), wdn_sc[...], preferred_element_type=jnp.float32) + bdn_ref[0, 0]
        y_ref[...] = y.astype(y_ref.dtype)

    @pl.when(i >= n_used)
    def _():
        y_ref[...] = jnp.zeros_like(y_ref)


def _moe_experts(blk_e, n_used, xs, w_gu, b_gu, w_dn, b_dn, *, layer):
    r, dm = xs.shape
    tm = MOE_TM
    n_blk = blk_e.shape[0]
    return pl.pallas_call(
        _moe_kernel,
        grid_spec=pltpu.PrefetchScalarGridSpec(
            num_scalar_prefetch=2, grid=(n_blk,),
            in_specs=[pl.BlockSpec((tm, dm), lambda i, be, nu: (i, 0)),
                      pl.BlockSpec((1, 1, dm, 2 * EXPERT_FF), lambda i, be, nu: (layer, be[i], 0, 0)),
                      pl.BlockSpec((1, 1, 1, 2 * EXPERT_FF), lambda i, be, nu: (layer, be[i], 0, 0)),
                      pl.BlockSpec((1, 1, EXPERT_FF, dm), lambda i, be, nu: (layer, be[i], 0, 0)),
                      pl.BlockSpec((1, 1, 1, dm), lambda i, be, nu: (layer, be[i], 0, 0))],
            out_specs=pl.BlockSpec((tm, dm), lambda i, be, nu: (i, 0)),
            scratch_shapes=[pltpu.VMEM((dm, 2 * EXPERT_FF), jnp.bfloat16),
                            pltpu.VMEM((EXPERT_FF, dm), jnp.bfloat16)]),
        out_shape=jax.ShapeDtypeStruct((r, dm), jnp.bfloat16),
        compiler_params=_cparams(("arbitrary",)),
        name="moe_experts",
    )(blk_e, n_used, xs, w_gu, b_gu.reshape(DEPTH, N_EXPERTS, 1, -1), w_dn,
      b_dn.reshape(DEPTH, N_EXPERTS, 1, -1))


def _combine_kernel(*refs):
    yk_refs = refs[:TOP_K]
    gt_ref, x_ref, ng_ref, ma_ref, mb_ref, xo_ref = refs[TOP_K:]
    gt = gt_ref[...]
    f = None
    for k in range(TOP_K):
        term = gt[:, k:k + 1] * yk_refs[k][...].astype(jnp.float32)
        f = term if f is None else f + term
    for half, xn in enumerate(_residual_tile(x_ref, f, ng_ref[3], ma_ref[0], mb_ref[0], 5)):
        xo_ref[half * HALF_TM:(half + 1) * HALF_TM, :] = xn


def _moe_combine(yk, gates, x, norm_g, mods, *, t_rows, seq):
    m = x.shape[0]
    tm = MM_TM
    n_tiles = m // tm
    row = lambda i: (i, 0)
    choice = lambda k: pl.BlockSpec((tm, D_MODEL), lambda i: (k * n_tiles + i, 0))
    return pl.pallas_call(
        _combine_kernel,
        grid=(n_tiles,),
        in_specs=[choice(k) for k in range(TOP_K)] + [
            pl.BlockSpec((tm, TOP_K), row), pl.BlockSpec((tm, D_MODEL), row),
            pl.BlockSpec((4, 1, D_MODEL), lambda i: (0, 0, 0))] + _mod_specs(t_rows, seq),
        out_specs=pl.BlockSpec((tm, D_MODEL), row),
        out_shape=jax.ShapeDtypeStruct((m, D_MODEL), jnp.float32),
        compiler_params=_cparams(("arbitrary",)),
        name="moe_combine",
    )(*([yk] * TOP_K), gates, x, norm_g, mods, mods)


def _route_kernel(h_ref, wr_ref, br_ref, e_ref, g_ref, rk_ref, cnt_ref, run_sc):
    i = pl.program_id(0)
    tm = h_ref.shape[0]

    @pl.when(i == 0)
    def _():
        run_sc[...] = jnp.zeros_like(run_sc)

    vals = jnp.dot(h_ref[...], wr_ref[...], preferred_element_type=jnp.float32,
                   precision=lax.Precision.HIGHEST) + br_ref[...]
    lane = lax.broadcasted_iota(jnp.int32, (tm, LANES), 1)
    top_v, top_e, picked = [], [], []
    for _ in range(TOP_K):
        m = jnp.max(vals, axis=-1, keepdims=True)
        idx = jnp.min(jnp.where(vals == m, lane, LANES), axis=-1, keepdims=True)
        sel = lane == idx
        top_v.append(m)
        top_e.append(idx)
        picked.append(sel)
        vals = jnp.where(sel, -jnp.inf, vals)
    ex = [jnp.exp(v - top_v[0]) for v in top_v]
    inv_den = 1.0 / (ex[0] + ex[1] + ex[2] + ex[3])
    any_pick = picked[0] | picked[1] | picked[2] | picked[3]
    rr = lax.broadcasted_iota(jnp.int32, (tm, tm), 0)
    cc = lax.broadcasted_iota(jnp.int32, (tm, tm), 1)
    earlier = (cc < rr).astype(jnp.bfloat16)
    before = jnp.dot(earlier, any_pick.astype(jnp.bfloat16), preferred_element_type=jnp.float32) + run_sc[...]
    e_out = jnp.zeros((tm, LANES), jnp.int32)
    g_out = jnp.zeros((tm, LANES), jnp.float32)
    r_out = jnp.zeros((tm, LANES), jnp.int32)
    for k in range(TOP_K):
        rank = jnp.sum(jnp.where(picked[k], before, 0.0), axis=-1, keepdims=True).astype(jnp.int32)
        e_out = jnp.where(lane == k, top_e[k], e_out)
        g_out = jnp.where(lane == k, ex[k] * inv_den, g_out)
        r_out = jnp.where(lane == k, rank, r_out)
    e_ref[...] = e_out
    g_ref[...] = g_out
    rk_ref[...] = r_out
    run_sc[...] += jnp.sum(any_pick.astype(jnp.float32), axis=0, keepdims=True)
    cnt_ref[...] = run_sc[...]


def _route(h, w_router, b_router):
    n, dm = h.shape
    tm = MM_TM
    w_r = jnp.zeros((dm, LANES), jnp.float32).at[:, :N_EXPERTS].set(w_router)
    b_r = jnp.full((1, LANES), NEG_INF, jnp.float32).at[0, :N_EXPERTS].set(b_router)
    row = lambda i: (i, 0)
    whole = lambda i: (0, 0)
    e, g, rk, cnt = pl.pallas_call(
        _route_kernel,
        grid=(n // tm,),
        in_specs=[pl.BlockSpec((tm, dm), row), pl.BlockSpec((dm, LANES), whole), pl.BlockSpec((1, LANES), whole)],
        out_specs=[pl.BlockSpec((tm, LANES), row), pl.BlockSpec((tm, LANES), row), pl.BlockSpec((tm, LANES), row),
                   pl.BlockSpec((1, LANES), whole)],
        out_shape=[jax.ShapeDtypeStruct((n, LANES), jnp.int32), jax.ShapeDtypeStruct((n, LANES), jnp.float32),
                   jax.ShapeDtypeStruct((n, LANES), jnp.int32), jax.ShapeDtypeStruct((1, LANES), jnp.float32)],
        scratch_shapes=[pltpu.VMEM((1, LANES), jnp.float32)],
        compiler_params=_cparams(("arbitrary",)),
        name="router",
    )(h, w_r, b_r)
    return e[:, :TOP_K], g[:, :TOP_K], rk[:, :TOP_K], cnt[0, :N_EXPERTS].astype(jnp.int32)


def _moe_ffn(h, h_rows, x, norm_g, mods, w_router, b_router, w_gu, b_gu, w_dn, b_dn, *, layer, t_rows, seq):
    n, dm = h.shape
    nk = n * TOP_K
    top_e, gates, rank, counts = _route(h, w_router, b_router)
    flat_e = top_e.reshape(-1)
    padded = (counts + MOE_TM - 1) // MOE_TM * MOE_TM
    pad_end = jnp.cumsum(padded)
    pad_start = pad_end - padded
    grp_start = jnp.cumsum(counts) - counts
    dest = (pad_start[top_e] + rank).reshape(-1)
    n_blk = (nk + MOE_TM - 1) // MOE_TM + N_EXPERTS
    rows = n_blk * MOE_TM
    blk_start = jnp.arange(n_blk, dtype=jnp.int32) * MOE_TM
    blk_e = jnp.minimum(jnp.sum((pad_end[None, :] <= blk_start[:, None]).astype(jnp.int32), axis=1),
                        N_EXPERTS - 1)
    n_used = (pad_end[-1] // MOE_TM).astype(jnp.int32).reshape(1)
    order = jnp.argsort(flat_e)
    blk_shift = (grp_start - pad_start)[blk_e]
    sorted_pos = (jnp.arange(rows, dtype=jnp.int32).reshape(n_blk, MOE_TM) + blk_shift[:, None]).reshape(-1)
    row_tok = (order[jnp.clip(sorted_pos, 0, nk - 1)] // TOP_K).astype(jnp.int32)
    row_tok = jnp.where(sorted_pos < nk, row_tok, jnp.arange(rows, dtype=jnp.int32) % n)
    xs = h_rows[row_tok]
    y = _moe_experts(blk_e, n_used, xs, w_gu, b_gu, w_dn, b_dn, layer=layer)
    yk = y[dest.reshape(n, TOP_K).T.reshape(-1)]
    return _moe_combine(yk, gates, x, norm_g, mods, t_rows=t_rows, seq=seq)


def _rope_tables(rows, n_ctx, batch):
    row = jnp.repeat(jnp.arange(rows, dtype=jnp.float32), GRID_W)
    col = jnp.tile(jnp.arange(GRID_W, dtype=jnp.float32), rows)
    n_freq = HEAD_DIM // 4
    inv = ROPE_THETA ** (-jnp.arange(n_freq, dtype=jnp.float32) / n_freq)
    ang_r = row[:, None] * inv
    ang_c = col[:, None] * inv
    cos_h = jnp.concatenate([jnp.cos(ang_r), jnp.cos(ang_r), jnp.cos(ang_c), jnp.cos(ang_c)], axis=1)
    sin_h = jnp.concatenate([-jnp.sin(ang_r), jnp.sin(ang_r), -jnp.sin(ang_c), jnp.sin(ang_c)], axis=1)
    cos_t = jnp.concatenate([cos_h, jnp.ones((n_ctx, HEAD_DIM), jnp.float32)], axis=0)
    sin_t = jnp.concatenate([sin_h, jnp.zeros((n_ctx, HEAD_DIM), jnp.float32)], axis=0)
    reps = (batch, LANES // HEAD_DIM)
    return jnp.tile(cos_t, reps), jnp.tile(sin_t, reps)


def kernel(x, c, ctx, c_ctx, w_mod, b_mod, norm_g, w_in, attn_sink, diff_lambda, diff_norm_w,
           hgrn_lb_logits, hgrn_norm_w, w_branch, w_out, w_router, b_router, w_gate_up, b_gate_up,
           w_down, b_down):
    b, s, dm = x.shape
    l = ctx.shape[1]
    t = s + l
    f32 = jnp.float32
    bf16 = jnp.bfloat16
    cos_t, sin_t = _rope_tables(s // GRID_W, l, b)
    lb_cum = jnp.cumsum(jax.nn.softmax(hgrn_lb_logits.astype(f32), axis=0), axis=0)
    lower_bounds = lb_cum - lb_cum[0:1]
    cond_rows = 16
    cond = jnp.zeros((cond_rows, dm), f32).at[:b].set(jax.nn.silu(c)).at[b].set(jax.nn.silu(c_ctx))
    scale = HEAD_DIM ** -0.5
    xa = jnp.concatenate([x, ctx], axis=1).reshape(b * t, dm)
    h_moe = None
    for layer in range(DEPTH):
        lam_init = 0.8 - 0.6 * math.exp(-0.3 * layer)
        mod_all = _matmul(cond, w_mod[layer], tm=cond_rows, tn=3 * dm, out_dtype=f32,
                          precision=lax.Precision.HIGHEST, name="adaln") + b_mod[layer]
        mod = mod_all[:b].reshape(b, 1, 6, 1, dm)
        mod_c = jnp.broadcast_to(mod_all[b:b + 1].reshape(1, 1, 6, 1, dm), (b, 1, 6, 1, dm))
        mods = jnp.concatenate([mod, mod_c], axis=1).reshape(2 * b, 6, 1, dm)
        ng = norm_g[layer].reshape(4, 1, dm)
        g0 = ng[0:1]
        geo = dict(t_rows=t, seq=s)

        w = w_in[layer]
        w_aq = (w[:, OFF_A:OFF_A + A_Q_W] * scale).reshape(dm, A_HKV, A_GROUP, HEAD_DIM)
        pad = jnp.zeros((dm, A_GROUP, HEAD_DIM), f32)
        w_aq8 = jnp.concatenate([jnp.concatenate([w_aq[:, 0], pad], axis=-1),
                                 jnp.concatenate([pad, w_aq[:, 1]], axis=-1)], axis=1).reshape(dm, A_Q8_W)
        w_a = jnp.concatenate([w_aq8, w[:, OFF_A + A_Q_W:OFF_A + A_W]], axis=1).astype(bf16)
        w_b = w[:, OFF_B:OFF_B + B_W].at[:, :B_QK_W].multiply(scale * LOG2_E).astype(bf16)
        proj_a = _proj_rope(xa, g0, mods, w_a, cos_t, sin_t,
                            rope_w=A_ROPE_W2, name="proj_a", **geo).reshape(b, t, A_W2)
        bq, bk_hm, bv1_hm = _proj_b(xa, g0, mods, w_b, cos_t, sin_t, **geo)
        bq = bq.reshape(b, t, B_QK_W)
        lb = lower_bounds[layer]
        lb_par = jnp.stack([jnp.log(lb[0]), jnp.log1p(-lb[0]), jnp.log(lb[1]), jnp.log1p(-lb[1])])
        cq, ci, lf, cg = _proj_hgrn(xa, g0, mods, w[:, OFF_C:OFF_C + C_W].astype(bf16),
                                    lb_par.reshape(4, 1, C_K_W), **geo)
        gates = _proj_plain(xa, g0, mods, w[:, OFF_GATE:].astype(bf16), name="gate_proj", **geo)

        sink32 = attn_sink[layer].astype(f32)
        o_a = _window_attention(proj_a, sink32, seq=s)

        lp = diff_lambda[layer].astype(f32)
        lam = (jnp.exp(jnp.sum(lp[0] * lp[1])) - jnp.exp(jnp.sum(lp[2] * lp[3])) + lam_init).reshape(1).astype(f32)
        o_b = _diff_attention(lam, bq, bk_hm, bv1_hm, tq=FLASH_TQ, tk=FLASH_TK, q_block_offset=0,
                              n_q_blocks=s // FLASH_TQ, k_block_offset=0, nk=t // FLASH_TK)
        o_bx = _diff_attention(lam, bq, bk_hm, bv1_hm, tq=l, tk=l, q_block_offset=s // l, n_q_blocks=1,
                               k_block_offset=s // l, nk=1)
        o_b = jnp.concatenate([o_b, o_bx], axis=1)

        o_cf, o_cb = _hgrn_scan(cq.reshape(b, t, C_K_W), ci.reshape(b, t, C_V_W), lf.reshape(2, b, t, C_K_W),
                                n_ctx_blocks=l // HGRN_TB)

        col = jnp.arange(A_Q_W)
        head_of_col = ((col % LANES) // HEAD_DIM) * A_GROUP + col // LANES
        w_br = w_branch[layer].at[0].set(w_branch[layer][0][head_of_col * HEAD_DIM + col % HEAD_DIM])
        xa, h_moe, h_rows = _merge(o_a.reshape(b * t, A_Q_W), o_b.reshape(b * t, B_V_W), o_cf.reshape(b * t, C_V_W),
                           o_cb.reshape(b * t, C_V_W), cg,
                           gates, xa, ng, mods, diff_norm_w[layer].reshape(1, B_VD).astype(f32),
                           hgrn_norm_w[layer].reshape(1, C_DV).astype(f32),
                           w_br.astype(bf16), w_out[layer].astype(bf16),
                           diff_out_scale=1 - lam_init, **geo)

        xa = _moe_ffn(h_moe, h_rows.reshape(2 * b * t, dm), xa, ng, mods, w_router[layer], b_router[layer],
                      w_gate_up, b_gate_up, w_down, b_down, layer=layer, **geo)
    return xa.reshape(b, t, dm)[:, :s]
```

```python
import functools
import math

import jax
import jax.numpy as jnp
from jax import lax
from jax.experimental import pallas as pl
from jax.experimental.pallas import tpu as pltpu

D_MODEL = 1024
DEPTH = 2
GRID_W = 64
HEAD_DIM = 64
ROPE_THETA = 10000.0
RMS_EPS = 1e-6
NEG_INF = -1e30
BLOCK = 128
A_HQ = 8
A_HKV = 2
A_GROUP = A_HQ // A_HKV
WINDOW = 128
B_H = 4
B_HD = HEAD_DIM
B_VD = 2 * HEAD_DIM
C_H = 4
C_DK = 128
C_DV = 128
N_BRANCH = 3
BRANCH_W = 512
A_Q_W = A_HQ * HEAD_DIM
A_KV_W = A_HKV * HEAD_DIM
B_QK_W = B_H * 2 * B_HD
B_V_W = B_H * B_VD
C_K_W = C_H * C_DK
C_V_W = C_H * C_DV
GATE_W = N_BRANCH * D_MODEL
N_EXPERTS = 32
TOP_K = 4
EXPERT_FF = D_MODEL
SWIGLU_LIMIT = 7.0
SWIGLU_ALPHA = 1.702

A_W = A_Q_W + 2 * A_KV_W
A_ROPE_W = A_Q_W + A_KV_W
B_W = 2 * B_QK_W + B_V_W
B_ROPE_W = 2 * B_QK_W
C_W = 3 * C_K_W + 2 * C_V_W
OFF_A = 0
OFF_B = OFF_A + A_W
OFF_C = OFF_B + B_W
OFF_GATE = OFF_C + C_W

V7X_VMEM_LIMIT_BYTES = 56 * 1024 * 1024
LANES = 128
LOG2_E = math.log2(math.e)

MM_TM = 512
HALF_TM = MM_TM // 2
FLASH_TQ = 2048
FLASH_TK = 768
HGRN_TB = 256
HGRN_SUB = 16
HGRN_UNROLL = 4
MOE_TM = 512


def _cparams(sem):
    return pltpu.CompilerParams(dimension_semantics=sem, vmem_limit_bytes=V7X_VMEM_LIMIT_BYTES)


def _rms_rows(x, g):
    return (x * lax.rsqrt(jnp.mean(x * x, axis=-1, keepdims=True) + RMS_EPS)) * g


def _modulated_tile(x_ref, g, mod_a, mod_b, shift_slot, scale_slot):
    out = []
    for half, mod in enumerate((mod_a, mod_b)):
        x = x_ref[half * HALF_TM:(half + 1) * HALF_TM, :]
        out.append(_rms_rows(x, g) * (1.0 + mod[scale_slot]) + mod[shift_slot])
    return jnp.concatenate(out, axis=0)


def _residual_tile(x_ref, y, g, mod_a, mod_b, gate_slot):
    out = []
    for half, mod in enumerate((mod_a, mod_b)):
        rows = slice(half * HALF_TM, (half + 1) * HALF_TM)
        out.append(x_ref[rows, :] + mod[gate_slot] * _rms_rows(y[rows, :], g))
    return out


def _mod_specs(t_rows, seq):
    halves_per_batch = t_rows // HALF_TM
    lat_halves = seq // HALF_TM

    def row(u):
        return (u // halves_per_batch) * 2 + jnp.where(u % halves_per_batch >= lat_halves, 1, 0)

    blk = (1, 6, 1, D_MODEL)
    return [pl.BlockSpec(blk, lambda i: (row(2 * i), 0, 0, 0)),
            pl.BlockSpec(blk, lambda i: (row(2 * i + 1), 0, 0, 0))]


def _rope_lane_block(z_blk, cos, sin, first):
    quarter = HEAD_DIM // 4
    partner = jnp.where(first, pltpu.roll(z_blk, LANES - quarter, 1), pltpu.roll(z_blk, quarter, 1))
    return z_blk * cos + partner * sin


def _rope_first_mask(rows):
    lane = lax.broadcasted_iota(jnp.int32, (rows, LANES), 1)
    return (lane % (HEAD_DIM // 2)) < (HEAD_DIM // 4)


def _proj_ab_kernel(x_ref, g_ref, ma_ref, mb_ref, wa_ref, wb_ref, cos_ref, sin_ref,
                    oa_ref, q_ref, k_ref, v1_ref):
    h = _modulated_tile(x_ref, g_ref[0], ma_ref[0], mb_ref[0], 0, 1).astype(jnp.bfloat16)
    cos, sin, first = cos_ref[...], sin_ref[...], _rope_first_mask(MM_TM)
    za = jnp.dot(h, wa_ref[...], preferred_element_type=jnp.float32)
    for c in range(A_ROPE_W2 // LANES):
        cols = slice(c * LANES, (c + 1) * LANES)
        oa_ref[:, cols] = _rope_lane_block(za[:, cols], cos, sin, first).astype(oa_ref.dtype)
    oa_ref[:, A_ROPE_W2:] = za[:, A_ROPE_W2:].astype(oa_ref.dtype)
    z = jnp.dot(h, wb_ref[...], preferred_element_type=jnp.float32)
    for c in range(B_QK_W // LANES):
        cols = slice(c * LANES, (c + 1) * LANES)
        q_ref[:, cols] = _rope_lane_block(z[:, cols], cos, sin, first).astype(q_ref.dtype)
    for hh in range(B_H):
        kcols = slice(B_QK_W + hh * 2 * B_HD, B_QK_W + (hh + 1) * 2 * B_HD)
        k_ref[hh] = _rope_lane_block(z[:, kcols], cos, sin, first).astype(k_ref.dtype)
        v1_ref[hh, :, :B_VD] = z[:, B_ROPE_W + hh * B_VD:B_ROPE_W + (hh + 1) * B_VD].astype(v1_ref.dtype)
        v1_ref[hh, :, B_VD:] = jnp.ones((MM_TM, B_VD), v1_ref.dtype)


def _proj_plain_kernel(x_ref, g_ref, ma_ref, mb_ref, w_ref, o_ref):
    h = _modulated_tile(x_ref, g_ref[0], ma_ref[0], mb_ref[0], 0, 1).astype(jnp.bfloat16)
    o_ref[...] = jnp.dot(h, w_ref[...], preferred_element_type=jnp.float32).astype(o_ref.dtype)


def _log_forget(z, log_lb, log_1m_lb):
    log_sig = jnp.minimum(z, 0.0) - jnp.log(1.0 + jnp.exp(-jnp.abs(z)))
    c = log_1m_lb + log_sig
    return jnp.maximum(log_lb, c) + jnp.log(1.0 + jnp.exp(-jnp.abs(log_lb - c)))


def _proj_hgrn_kernel(x_ref, g_ref, ma_ref, mb_ref, w_ref, lb_ref, q_ref, v_ref, lf_ref, cg_ref):
    h = _modulated_tile(x_ref, g_ref[0], ma_ref[0], mb_ref[0], 0, 1).astype(jnp.bfloat16)
    z = jnp.dot(h, w_ref[...], preferred_element_type=jnp.float32)
    zq = z[:, :C_K_W]
    q_ref[...] = zq * jax.nn.sigmoid(zq)
    v_ref[...] = z[:, C_K_W:C_K_W + C_V_W]
    off = C_K_W + C_V_W
    lf_ref[0] = _log_forget(z[:, off:off + C_K_W], lb_ref[0], lb_ref[1])
    lf_ref[1] = _log_forget(z[:, off + C_K_W:off + 2 * C_K_W], lb_ref[2], lb_ref[3])
    cg_ref[...] = z[:, off + 2 * C_K_W:].astype(cg_ref.dtype)


def _proj_common_specs(t_rows, seq):
    row = lambda i: (i, 0)
    return [pl.BlockSpec((MM_TM, D_MODEL), row),
            pl.BlockSpec((1, 1, D_MODEL), lambda i: (0, 0, 0))] + _mod_specs(t_rows, seq)


def _proj_ab(x, g, mods, w_a, w_b, cos, sin, *, t_rows, seq):
    m = x.shape[0]
    row = lambda i: (i, 0)
    whole = lambda i: (0, 0)
    heads = lambda i: (0, i, 0)
    bf16 = jnp.bfloat16
    return pl.pallas_call(
        _proj_ab_kernel,
        grid=(m // MM_TM,),
        in_specs=_proj_common_specs(t_rows, seq) + [
            pl.BlockSpec((D_MODEL, A_W2), whole), pl.BlockSpec((D_MODEL, B_W), whole),
            pl.BlockSpec((MM_TM, LANES), row), pl.BlockSpec((MM_TM, LANES), row)],
        out_specs=[pl.BlockSpec((MM_TM, A_W2), row), pl.BlockSpec((MM_TM, B_QK_W), row),
                   pl.BlockSpec((B_H, MM_TM, 2 * B_HD), heads), pl.BlockSpec((B_H, MM_TM, 2 * B_VD), heads)],
        out_shape=[jax.ShapeDtypeStruct((m, A_W2), bf16), jax.ShapeDtypeStruct((m, B_QK_W), bf16),
                   jax.ShapeDtypeStruct((B_H, m, 2 * B_HD), bf16), jax.ShapeDtypeStruct((B_H, m, 2 * B_VD), bf16)],
        compiler_params=_cparams(("arbitrary",)),
        name="proj_ab",
    )(x, g, mods, mods, w_a, w_b, cos, sin)


def _proj_plain(x, g, mods, w, *, t_rows, seq, name):
    m = x.shape[0]
    n = w.shape[1]
    row = lambda i: (i, 0)
    return pl.pallas_call(
        _proj_plain_kernel,
        grid=(m // MM_TM,),
        in_specs=_proj_common_specs(t_rows, seq) + [pl.BlockSpec((D_MODEL, n), lambda i: (0, 0))],
        out_specs=pl.BlockSpec((MM_TM, n), row),
        out_shape=jax.ShapeDtypeStruct((m, n), jnp.bfloat16),
        compiler_params=_cparams(("arbitrary",)),
        name=name,
    )(x, g, mods, mods, w)


def _proj_hgrn(x, g, mods, w, lb_par, *, t_rows, seq):
    m = x.shape[0]
    row = lambda i: (i, 0)
    f32 = jnp.float32
    return pl.pallas_call(
        _proj_hgrn_kernel,
        grid=(m // MM_TM,),
        in_specs=_proj_common_specs(t_rows, seq) + [
            pl.BlockSpec((D_MODEL, C_W), lambda i: (0, 0)),
            pl.BlockSpec((4, 1, C_K_W), lambda i: (0, 0, 0))],
        out_specs=[pl.BlockSpec((MM_TM, C_K_W), row), pl.BlockSpec((MM_TM, C_V_W), row),
                   pl.BlockSpec((2, MM_TM, C_K_W), lambda i: (0, i, 0)), pl.BlockSpec((MM_TM, C_V_W), row)],
        out_shape=[jax.ShapeDtypeStruct((m, C_K_W), f32), jax.ShapeDtypeStruct((m, C_V_W), f32),
                   jax.ShapeDtypeStruct((2, m, C_K_W), f32), jax.ShapeDtypeStruct((m, C_V_W), jnp.bfloat16)],
        compiler_params=_cparams(("arbitrary",)),
        name="hgrn_proj",
    )(x, g, mods, mods, w, lb_par)


def _mm_kernel(x_ref, w_ref, o_ref, *, precision):
    o_ref[...] = jnp.dot(x_ref[...], w_ref[...], preferred_element_type=jnp.float32,
                         precision=precision).astype(o_ref.dtype)


def _matmul(x, w, *, tm, tn, out_dtype, precision=None, name):
    m, k = x.shape
    n = w.shape[1]
    assert m % tm == 0 and n % tn == 0, (m, tm, n, tn)
    return pl.pallas_call(
        functools.partial(_mm_kernel, precision=precision),
        grid=(n // tn, m // tm),
        in_specs=[pl.BlockSpec((tm, k), lambda j, i: (i, 0)),
                  pl.BlockSpec((k, tn), lambda j, i: (0, j))],
        out_specs=pl.BlockSpec((tm, tn), lambda j, i: (i, j)),
        out_shape=jax.ShapeDtypeStruct((m, n), out_dtype),
        compiler_params=_cparams(("arbitrary", "arbitrary")),
        name=name,
    )(x, w)


A_Q8_W = A_HQ * LANES
A_W2 = A_Q8_W + 2 * A_KV_W
A_ROPE_W2 = A_Q8_W + A_KV_W
A_K_BLK = A_Q8_W // A_KV_W
A_V_BLK = A_K_BLK + 1
A_STEP_BLOCKS = 2


def _attn_a_kernel(sink_ref, q_ref, kp_ref, kc_ref, kn_ref, vp_ref, vc_ref, vn_ref, kx_ref, vx_ref,
                   o_ref, *, seq):
    n = pl.program_id(1)
    k_loc = jnp.concatenate([kp_ref[0], kc_ref[0], kn_ref[0]], axis=0)
    v_loc = jnp.concatenate([vp_ref[0], vc_ref[0], vn_ref[0]], axis=0)
    n_keys = 3 * BLOCK + kx_ref.shape[1]
    qi = lax.broadcasted_iota(jnp.int32, (BLOCK, n_keys), 0)
    kj = lax.broadcasted_iota(jnp.int32, (BLOCK, n_keys), 1)
    lane = lax.broadcasted_iota(jnp.int32, (BLOCK, LANES), 1)
    for j in range(A_STEP_BLOCKS):
        blk = n * A_STEP_BLOCKS + j
        rows = slice(j * BLOCK, (j + 1) * BLOCK)
        k_all = jnp.concatenate([k_loc[j * BLOCK:(j + 3) * BLOCK, :], kx_ref[0]], axis=0)
        v_all = jnp.concatenate([v_loc[j * BLOCK:(j + 3) * BLOCK, :], vx_ref[0]], axis=0)
        kpos = blk * BLOCK + kj - BLOCK
        local_ok = (jnp.abs(kj - BLOCK - qi) <= WINDOW) & (kpos >= 0) & (kpos < seq) & (blk * BLOCK < seq)
        mask = local_ok | (kj >= 3 * BLOCK)
        q8 = jnp.concatenate([q_ref[0, rows, hq * LANES:(hq + 1) * LANES] for hq in range(A_HQ)], axis=0)
        s = lax.dot_general(q8, k_all, (((1,), (1,)), ((), ())), preferred_element_type=jnp.float32)
        p_blocks, inv_l = [], []
        for hq in range(A_HQ):
            sh = jnp.where(mask, s[hq * BLOCK:(hq + 1) * BLOCK, :], NEG_INF)
            sk = sink_ref[hq]
            m = jnp.maximum(jnp.max(sh, axis=-1, keepdims=True), sk)
            p = jnp.exp(sh - m)
            inv_l.append(1.0 / (jnp.sum(p, axis=-1, keepdims=True) + jnp.exp(sk - m)))
            p_blocks.append(p.astype(v_all.dtype))
        o = jnp.dot(jnp.concatenate(p_blocks, axis=0), v_all, preferred_element_type=jnp.float32)
        for g in range(A_GROUP):
            lo = o[g * BLOCK:(g + 1) * BLOCK, :] * inv_l[g]
            hi = o[(A_GROUP + g) * BLOCK:(A_GROUP + g + 1) * BLOCK, :] * inv_l[A_GROUP + g]
            o_ref[0, rows, g * LANES:(g + 1) * LANES] = jnp.where(lane < HEAD_DIM, lo, hi).astype(o_ref.dtype)


def _window_attention(proj_a, sink, *, seq):
    b, t, _ = proj_a.shape
    l = t - seq
    step = A_STEP_BLOCKS * BLOCK
    nb = seq // BLOCK
    ns = seq // step
    ctx_blk = seq // l
    q_map = lambda bi, n, *_: (bi, n, 0)
    edge = lambda col, fn: pl.BlockSpec((1, BLOCK, A_KV_W), lambda bi, n, *_: (bi, fn(n), col))
    mid = lambda col: pl.BlockSpec((1, step, A_KV_W), lambda bi, n, *_: (bi, jnp.minimum(n, ns - 1), col))
    prev = lambda n: jnp.clip(n * A_STEP_BLOCKS - 1, 0, nb - 1)
    nxt = lambda n: jnp.minimum((n + 1) * A_STEP_BLOCKS, nb - 1)
    ctx = lambda col: pl.BlockSpec((1, l, A_KV_W), lambda bi, n, *_: (bi, ctx_blk, col))
    return pl.pallas_call(
        functools.partial(_attn_a_kernel, seq=seq),
        grid_spec=pltpu.PrefetchScalarGridSpec(
            num_scalar_prefetch=1, grid=(b, t // step),
            in_specs=[pl.BlockSpec((1, step, A_Q8_W), q_map),
                      edge(A_K_BLK, prev), mid(A_K_BLK), edge(A_K_BLK, nxt),
                      edge(A_V_BLK, prev), mid(A_V_BLK), edge(A_V_BLK, nxt),
                      ctx(A_K_BLK), ctx(A_V_BLK)],
            out_specs=pl.BlockSpec((1, step, A_Q_W), q_map)),
        out_shape=jax.ShapeDtypeStruct((b, t, A_Q_W), jnp.bfloat16),
        compiler_params=_cparams(("arbitrary", "arbitrary")),
        name="window_gqa",
    )(sink, *([proj_a] * 9))


def _diff_flash_kernel(lam_ref, q_ref, k_ref, v_ref, o_ref, q2_sc, sa_sc, sb_sc, m_sc, acc_sc, *, tq, tk, nk):
    u = pl.program_id(2)
    new_q = u % nk == 0

    @pl.when(u == 0)
    def _():
        m_sc[...] = jnp.full_like(m_sc, NEG_INF)
        acc_sc[...] = jnp.zeros_like(acc_sc)
        sb_sc[...] = jnp.full_like(sb_sc, NEG_INF)

    @pl.when(new_q)
    def _():
        q = q_ref[0]
        lane = lax.broadcasted_iota(jnp.int32, q.shape, 1)
        q2_sc[:tq, :] = jnp.where(lane < B_HD, q, jnp.zeros_like(q))
        q2_sc[tq:, :] = jnp.where(lane >= B_HD, q, jnp.zeros_like(q))

    def step(s_new_ref, s_old_ref):
        m_prev = m_sc[...]
        m_new = jnp.maximum(m_prev, jnp.max(s_old_ref[...], axis=-1, keepdims=True))
        alpha = jnp.exp2(m_prev - m_new)
        p = jnp.exp2(s_old_ref[...] - jnp.tile(m_new, (1, tk // LANES))).astype(jnp.bfloat16)
        acc_sc[...] = jnp.tile(alpha, (1, 2)) * acc_sc[...] + jnp.dot(
            p, v_ref[0], preferred_element_type=jnp.float32)
        m_sc[...] = m_new
        s_new_ref[...] = lax.dot_general(q2_sc[...], k_ref[0], (((1,), (1,)), ((), ())),
                                         preferred_element_type=jnp.float32)

    @pl.when(u % 2 == 0)
    def _():
        step(sa_sc, sb_sc)

    @pl.when(u % 2 == 1)
    def _():
        step(sb_sc, sa_sc)

    @pl.when(new_q & (u > 0))
    def _():
        acc = acc_sc[...]
        o = acc[:, :B_VD] / acc[:, B_VD:]
        o_ref[0] = (o[:tq, :] - lam_ref[0] * o[tq:, :]).astype(o_ref.dtype)
        m_sc[...] = jnp.full_like(m_sc, NEG_INF)
        acc_sc[...] = jnp.zeros_like(acc_sc)


def _diff_attention(lam, q, k_hm, v1_hm, *, tq, tk, q_block_offset, n_q_blocks, k_block_offset, nk):
    b, t, _ = q.shape
    kb = t // tk
    n_steps = n_q_blocks * nk + 1
    k_blk = lambda bi, u: bi * kb + k_block_offset + u % nk
    in_specs = [pl.BlockSpec((1, tq, 2 * B_HD),
                             lambda bi, h, u, *_: (bi, jnp.minimum(u // nk, n_q_blocks - 1) + q_block_offset, h)),
                pl.BlockSpec((1, tk, 2 * B_HD), lambda bi, h, u, *_: (h, k_blk(bi, u), 0)),
                pl.BlockSpec((1, tk, 2 * B_VD), lambda bi, h, u, *_: (h, k_blk(bi, jnp.maximum(u - 1, 0)), 0))]
    return pl.pallas_call(
        functools.partial(_diff_flash_kernel, tq=tq, tk=tk, nk=nk),
        grid_spec=pltpu.PrefetchScalarGridSpec(
            num_scalar_prefetch=1, grid=(b, B_H, n_steps),
            in_specs=in_specs,
            out_specs=pl.BlockSpec((1, tq, B_VD), lambda bi, h, u, *_: (bi, jnp.maximum(u - 1, 0) // nk, h)),
            scratch_shapes=[pltpu.VMEM((2 * tq, 2 * B_HD), jnp.bfloat16),
                            pltpu.VMEM((2 * tq, tk), jnp.float32),
                            pltpu.VMEM((2 * tq, tk), jnp.float32),
                            pltpu.VMEM((2 * tq, LANES), jnp.float32),
                            pltpu.VMEM((2 * tq, 2 * B_VD), jnp.float32)]),
        out_shape=jax.ShapeDtypeStruct((b, n_q_blocks * tq, B_V_W), jnp.float32),
        compiler_params=_cparams(("arbitrary", "arbitrary", "arbitrary")),
        name="diff_flash",
    )(lam, q, k_hm, v1_hm)


def _hgrn_sub_chunk(q_ref, v_ref, lf_ref, o_ref, st_ref, r0, tri, rev):
    half = HGRN_SUB // 2
    row8 = lax.broadcasted_iota(jnp.int32, (half, C_DK), 0)
    lf = lf_ref[pl.ds(r0, HGRN_SUB), :] * LOG2_E
    lf_a = lf.astype(jnp.bfloat16)
    rem = lf - lf_a.astype(jnp.float32)
    lf_b = rem.astype(jnp.bfloat16)
    lf_c = (rem - lf_b.astype(jnp.float32)).astype(jnp.bfloat16)
    bl = (jnp.dot(tri, lf_a, preferred_element_type=jnp.float32)
          + jnp.dot(tri, lf_b, preferred_element_type=jnp.float32)
          + jnp.dot(tri, lf_c, preferred_element_type=jnp.float32))
    q = q_ref[pl.ds(r0, HGRN_SUB), :]
    v = v_ref[pl.ds(r0, HGRN_SUB), :]
    last = 0 if rev else HGRN_SUB - 1
    for hh in range(C_H):
        sl = slice(hh * C_DK, (hh + 1) * C_DK)
        lfh, blh, qh, vh = lf[:, sl], bl[:, sl], q[:, sl], v[:, sl]
        tot = blh[last:last + 1, :]
        kh = 1.0 - jnp.exp2(lfh)
        qt = qh * jnp.exp2(blh)
        kt = kh * jnp.exp2(tot - blh)
        st = st_ref[hh]
        o_sub = lax.dot_general(qt.astype(jnp.bfloat16), st.astype(jnp.bfloat16),
                                (((1,), (1,)), ((), ())), preferred_element_type=jnp.float32)
        parts = [o_sub[:half, :], o_sub[half:, :]]
        for ss in range(HGRN_SUB):
            ks = kh[ss:ss + 1, :]
            bs = blh[ss:ss + 1, :]
            vs = vh[ss:ss + 1, :]
            for p in range(2):
                lo = p * half
                if rev:
                    reached = lo <= ss
                    full = lo + half - 1 <= ss
                    ok = row8 + lo <= ss
                else:
                    reached = lo + half - 1 >= ss
                    full = lo >= ss
                    ok = row8 + lo >= ss
                if not reached:
                    continue
                diff = blh[lo:lo + half, :] - bs
                if not full:
                    diff = jnp.where(ok, diff, NEG_INF)
                col = jnp.sum((qh[lo:lo + half, :] * ks) * jnp.exp2(diff), axis=1, keepdims=True)
                parts[p] = parts[p] + col * vs
        o_ref[pl.ds(r0, half), sl] = parts[0]
        o_ref[pl.ds(r0 + half, half), sl] = parts[1]
        upd = lax.dot_general(vh.astype(jnp.bfloat16), kt.astype(jnp.bfloat16),
                              (((0,), (0,)), ((), ())), preferred_element_type=jnp.float32)
        st_ref[hh] = jnp.exp2(tot) * st + upd


def _hgrn_kernel(qf_ref, vf_ref, lff_ref, qb_ref, vb_ref, lfb_ref, of_ref, ob_ref, st_ref, *, tb):
    t = pl.program_id(1)
    nsub = tb // HGRN_SUB

    @pl.when(t == 0)
    def _():
        st_ref[...] = jnp.zeros_like(st_ref)

    rr = lax.broadcasted_iota(jnp.int32, (HGRN_SUB, HGRN_SUB), 0)
    cc = lax.broadcasted_iota(jnp.int32, (HGRN_SUB, HGRN_SUB), 1)
    tri_f = (cc <= rr).astype(jnp.bfloat16)
    tri_b = (cc >= rr).astype(jnp.bfloat16)

    def body(i, carry):
        rf = pl.multiple_of(i * HGRN_SUB, HGRN_SUB)
        rb = pl.multiple_of((nsub - 1 - i) * HGRN_SUB, HGRN_SUB)
        _hgrn_sub_chunk(qf_ref.at[0], vf_ref.at[0], lff_ref.at[0, 0], of_ref.at[0], st_ref.at[0], rf, tri_f, False)
        _hgrn_sub_chunk(qb_ref.at[0], vb_ref.at[0], lfb_ref.at[0, 0], ob_ref.at[0], st_ref.at[1], rb, tri_b, True)
        return carry

    lax.fori_loop(0, nsub, body, 0, unroll=HGRN_UNROLL)


def _hgrn_scan(q, v, lf, *, n_ctx_blocks):
    b, t, w = q.shape
    tb = HGRN_TB
    nblk = t // tb
    nlat = nblk - n_ctx_blocks

    def fwd_block(ti):
        return jnp.where(ti < n_ctx_blocks, nlat + ti, ti - n_ctx_blocks)

    def bwd_block(ti):
        return nblk - 1 - ti

    rows = lambda blk: pl.BlockSpec((1, tb, w), lambda bi, ti: (bi, blk(ti), 0))
    gate = lambda di, blk: pl.BlockSpec((1, 1, tb, w), lambda bi, ti: (di, bi, blk(ti), 0))
    out = jax.ShapeDtypeStruct((b, t, w), jnp.float32)
    return pl.pallas_call(
        functools.partial(_hgrn_kernel, tb=tb),
        grid=(b, nblk),
        in_specs=[rows(fwd_block), rows(fwd_block), gate(0, fwd_block),
                  rows(bwd_block), rows(bwd_block), gate(1, bwd_block)],
        out_specs=[rows(fwd_block), rows(bwd_block)],
        out_shape=[out, out],
        scratch_shapes=[pltpu.VMEM((2, C_H, C_DV, C_DK), jnp.float32)],
        compiler_params=_cparams(("arbitrary", "arbitrary")),
        name="hgrn2_scan",
    )(q, v, lf, q, v, lf)


def _head_rms(x, w, n_heads, width):
    out = []
    for hh in range(n_heads):
        blk = x[:, hh * width:(hh + 1) * width]
        out.append(_rms_rows(blk, w))
    return jnp.concatenate(out, axis=1)


def _merge_kernel(oa_ref, ob_ref, ocf_ref, ocb_ref, cg_ref, gt_ref, x_ref, ng_ref, ma_ref, mb_ref, dw_ref, hw_ref,
                  wbr_ref, wo_ref, xo_ref, ho_ref, hb_ref, *, diff_out_scale):
    ob = (_head_rms(ob_ref[...], dw_ref[...], B_H, B_VD) * diff_out_scale).astype(jnp.bfloat16)
    cg = cg_ref[...].astype(jnp.float32)
    oc = (_head_rms(ocf_ref[...] + ocb_ref[...], hw_ref[...], C_H, C_DV) * (cg * jax.nn.sigmoid(cg)))
    acc = None
    for i, o in enumerate((oa_ref[...], ob, oc.astype(jnp.bfloat16))):
        z = jnp.dot(o, wbr_ref[i], preferred_element_type=jnp.float32)
        g = jax.nn.sigmoid(gt_ref[:, i * D_MODEL:(i + 1) * D_MODEL].astype(jnp.float32))
        acc = g * z if acc is None else acc + g * z
    y = jnp.dot(acc.astype(jnp.bfloat16), wo_ref[...], preferred_element_type=jnp.float32)
    ma, mb = ma_ref[0], mb_ref[0]
    for half, (xn, mod) in enumerate(zip(_residual_tile(x_ref, y, ng_ref[1], ma, mb, 2), (ma, mb))):
        rows = slice(half * HALF_TM, (half + 1) * HALF_TM)
        xo_ref[rows, :] = xn
        h_next = _rms_rows(xn, ng_ref[2]) * (1.0 + mod[4]) + mod[3]
        ho_ref[rows, :] = h_next
        hb_ref[0, rows, :] = h_next.astype(hb_ref.dtype)
    hb_ref[1] = jnp.zeros((MM_TM, D_MODEL), hb_ref.dtype)


def _merge(oa, ob, ocf, ocb, cg, gates, x, norm_g, mods, diff_w, hgrn_w, w_br, w_o, *, t_rows, seq,
           diff_out_scale):
    m = x.shape[0]
    tm = MM_TM
    row = lambda i: (i, 0)
    f32 = jnp.float32
    return pl.pallas_call(
        functools.partial(_merge_kernel, diff_out_scale=diff_out_scale),
        grid=(m // tm,),
        in_specs=[pl.BlockSpec((tm, BRANCH_W), row), pl.BlockSpec((tm, BRANCH_W), row),
                  pl.BlockSpec((tm, BRANCH_W), row), pl.BlockSpec((tm, BRANCH_W), row),
                  pl.BlockSpec((tm, BRANCH_W), row),
                  pl.BlockSpec((tm, GATE_W), row), pl.BlockSpec((tm, D_MODEL), row),
                  pl.BlockSpec((4, 1, D_MODEL), lambda i: (0, 0, 0))] + _mod_specs(t_rows, seq) + [
                  pl.BlockSpec((1, B_VD), lambda i: (0, 0)), pl.BlockSpec((1, C_DV), lambda i: (0, 0)),
                  pl.BlockSpec((N_BRANCH, BRANCH_W, D_MODEL), lambda i: (0, 0, 0)),
                  pl.BlockSpec((D_MODEL, D_MODEL), lambda i: (0, 0))],
        out_specs=[pl.BlockSpec((tm, D_MODEL), row), pl.BlockSpec((tm, D_MODEL), row),
                   pl.BlockSpec((2, tm, D_MODEL), lambda i: (0, i, 0))],
        out_shape=[jax.ShapeDtypeStruct((m, D_MODEL), f32), jax.ShapeDtypeStruct((m, D_MODEL), f32),
                   jax.ShapeDtypeStruct((2, m, D_MODEL), jnp.bfloat16)],
        compiler_params=_cparams(("arbitrary",)),
        name="branch_merge",
    )(oa, ob, ocf, ocb, cg, gates, x, norm_g, mods, mods, diff_w, hgrn_w, w_br, w_o)


def _moe_kernel(blk_e_ref, n_used_ref, x_ref, wgu_ref, bgu_ref, wdn_ref, bdn_ref, y_ref, wgu_sc, wdn_sc):
    i = pl.program_id(0)
    n_used = n_used_ref[0]
    e = blk_e_ref[i]
    e_prev = blk_e_ref[jnp.maximum(i - 1, 0)]

    @pl.when((i == 0) | (e != e_prev))
    def _():
        wgu_sc[...] = wgu_ref[0, 0].astype(jnp.bfloat16)
        wdn_sc[...] = wdn_ref[0, 0].astype(jnp.bfloat16)

    @pl.when(i < n_used)
    def _():
        gu = jnp.dot(x_ref[...], wgu_sc[...], preferred_element_type=jnp.float32) + bgu_ref[0, 0]
        g = jnp.minimum(gu[:, :EXPERT_FF], SWIGLU_LIMIT)
        u = jnp.clip(gu[:, EXPERT_FF:], -SWIGLU_LIMIT, SWIGLU_LIMIT)
        act = (u + 1.0) * (g * jax.nn.sigmoid(SWIGLU_ALPHA * g))
        y = jnp.dot(act.astype(jnp.bfloat16), wdn_sc[...], preferred_element_type=jnp.float32) + bdn_ref[0, 0]
        y_ref[...] = y.astype(y_ref.dtype)

    @pl.when(i >= n_used)
    def _():
        y_ref[...] = jnp.zeros_like(y_ref)


def _moe_experts(blk_e, n_used, xs, w_gu, b_gu, w_dn, b_dn, *, layer):
    r, dm = xs.shape
    tm = MOE_TM
    n_blk = blk_e.shape[0]
    return pl.pallas_call(
        _moe_kernel,
        grid_spec=pltpu.PrefetchScalarGridSpec(
            num_scalar_prefetch=2, grid=(n_blk,),
            in_specs=[pl.BlockSpec((tm, dm), lambda i, be, nu: (i, 0)),
                      pl.BlockSpec((1, 1, dm, 2 * EXPERT_FF), lambda i, be, nu: (layer, be[i], 0, 0)),
                      pl.BlockSpec((1, 1, 1, 2 * EXPERT_FF), lambda i, be, nu: (layer, be[i], 0, 0)),
                      pl.BlockSpec((1, 1, EXPERT_FF, dm), lambda i, be, nu: (layer, be[i], 0, 0)),
                      pl.BlockSpec((1, 1, 1, dm), lambda i, be, nu: (layer, be[i], 0, 0))],
            out_specs=pl.BlockSpec((tm, dm), lambda i, be, nu: (i, 0)),
            scratch_shapes=[pltpu.VMEM((dm, 2 * EXPERT_FF), jnp.bfloat16),
                            pltpu.VMEM((EXPERT_FF, dm), jnp.bfloat16)]),
        out_shape=jax.ShapeDtypeStruct((r, dm), jnp.bfloat16),
        compiler_params=_cparams(("arbitrary",)),
        name="moe_experts",
    )(blk_e, n_used, xs, w_gu, b_gu.reshape(DEPTH, N_EXPERTS, 1, -1), w_dn,
      b_dn.reshape(DEPTH, N_EXPERTS, 1, -1))


def _combine_kernel(*refs):
    yk_refs = refs[:TOP_K]
    gt_ref, x_ref, ng_ref, ma_ref, mb_ref, xo_ref = refs[TOP_K:]
    gt = gt_ref[...]
    f = None
    for k in range(TOP_K):
        term = gt[:, k:k + 1] * yk_refs[k][...].astype(jnp.float32)
        f = term if f is None else f + term
    for half, xn in enumerate(_residual_tile(x_ref, f, ng_ref[3], ma_ref[0], mb_ref[0], 5)):
        xo_ref[half * HALF_TM:(half + 1) * HALF_TM, :] = xn


def _moe_combine(yk, gates, x, norm_g, mods, *, t_rows, seq):
    m = x.shape[0]
    tm = MM_TM
    n_tiles = m // tm
    row = lambda i: (i, 0)
    choice = lambda k: pl.BlockSpec((tm, D_MODEL), lambda i: (k * n_tiles + i, 0))
    return pl.pallas_call(
        _combine_kernel,
        grid=(n_tiles,),
        in_specs=[choice(k) for k in range(TOP_K)] + [
            pl.BlockSpec((tm, TOP_K), row), pl.BlockSpec((tm, D_MODEL), row),
            pl.BlockSpec((4, 1, D_MODEL), lambda i: (0, 0, 0))] + _mod_specs(t_rows, seq),
        out_specs=pl.BlockSpec((tm, D_MODEL), row),
        out_shape=jax.ShapeDtypeStruct((m, D_MODEL), jnp.float32),
        compiler_params=_cparams(("arbitrary",)),
        name="moe_combine",
    )(*([yk] * TOP_K), gates, x, norm_g, mods, mods)


def _route_kernel(h_ref, wr_ref, br_ref, e_ref, g_ref, rk_ref, cnt_ref, run_sc):
    i = pl.program_id(0)
    tm = h_ref.shape[0]

    @pl.when(i == 0)
    def _():
        run_sc[...] = jnp.zeros_like(run_sc)

    vals = jnp.dot(h_ref[...], wr_ref[...], preferred_element_type=jnp.float32,
                   precision=lax.Precision.HIGHEST) + br_ref[...]
    lane = lax.broadcasted_iota(jnp.int32, (tm, LANES), 1)
    top_v, top_e, picked = [], [], []
    for _ in range(TOP_K):
        m = jnp.max(vals, axis=-1, keepdims=True)
        idx = jnp.min(jnp.where(vals == m, lane, LANES), axis=-1, keepdims=True)
        sel = lane == idx
        top_v.append(m)
        top_e.append(idx)
        picked.append(sel)
        vals = jnp.where(sel, -jnp.inf, vals)
    ex = [jnp.exp(v - top_v[0]) for v in top_v]
    inv_den = 1.0 / (ex[0] + ex[1] + ex[2] + ex[3])
    any_pick = picked[0] | picked[1] | picked[2] | picked[3]
    rr = lax.broadcasted_iota(jnp.int32, (tm, tm), 0)
    cc = lax.broadcasted_iota(jnp.int32, (tm, tm), 1)
    earlier = (cc < rr).astype(jnp.bfloat16)
    before = jnp.dot(earlier, any_pick.astype(jnp.bfloat16), preferred_element_type=jnp.float32) + run_sc[...]
    e_out = jnp.zeros((tm, LANES), jnp.int32)
    g_out = jnp.zeros((tm, LANES), jnp.float32)
    r_out = jnp.zeros((tm, LANES), jnp.int32)
    for k in range(TOP_K):
        rank = jnp.sum(jnp.where(picked[k], before, 0.0), axis=-1, keepdims=True).astype(jnp.int32)
        e_out = jnp.where(lane == k, top_e[k], e_out)
        g_out = jnp.where(lane == k, ex[k] * inv_den, g_out)
        r_out = jnp.where(lane == k, rank, r_out)
    e_ref[...] = e_out
    g_ref[...] = g_out
    rk_ref[...] = r_out
    run_sc[...] += jnp.sum(any_pick.astype(jnp.float32), axis=0, keepdims=True)
    cnt_ref[...] = run_sc[...]


def _route(h, w_router, b_router):
    n, dm = h.shape
    tm = MM_TM
    w_r = jnp.zeros((dm, LANES), jnp.float32).at[:, :N_EXPERTS].set(w_router)
    b_r = jnp.full((1, LANES), NEG_INF, jnp.float32).at[0, :N_EXPERTS].set(b_router)
    row = lambda i: (i, 0)
    whole = lambda i: (0, 0)
    e, g, rk, cnt = pl.pallas_call(
        _route_kernel,
        grid=(n // tm,),
        in_specs=[pl.BlockSpec((tm, dm), row), pl.BlockSpec((dm, LANES), whole), pl.BlockSpec((1, LANES), whole)],
        out_specs=[pl.BlockSpec((tm, LANES), row), pl.BlockSpec((tm, LANES), row), pl.BlockSpec((tm, LANES), row),
                   pl.BlockSpec((1, LANES), whole)],
        out_shape=[jax.ShapeDtypeStruct((n, LANES), jnp.int32), jax.ShapeDtypeStruct((n, LANES), jnp.float32),
                   jax.ShapeDtypeStruct((n, LANES), jnp.int32), jax.ShapeDtypeStruct((1, LANES), jnp.float32)],
        scratch_shapes=[pltpu.VMEM((1, LANES), jnp.float32)],
        compiler_params=_cparams(("arbitrary",)),
        name="router",
    )(h, w_r, b_r)
    return e[:, :TOP_K], g[:, :TOP_K], rk[:, :TOP_K], cnt[0, :N_EXPERTS].astype(jnp.int32)


def _moe_ffn(h, h_rows, x, norm_g, mods, w_router, b_router, w_gu, b_gu, w_dn, b_dn, *, layer, t_rows, seq):
    n, dm = h.shape
    nk = n * TOP_K
    top_e, gates, rank, counts = _route(h, w_router, b_router)
    flat_e = top_e.reshape(-1)
    padded = (counts + MOE_TM - 1) // MOE_TM * MOE_TM
    pad_end = jnp.cumsum(padded)
    pad_start = pad_end - padded
    grp_start = jnp.cumsum(counts) - counts
    dest = (pad_start[top_e] + rank).reshape(-1)
    n_blk = (nk + MOE_TM - 1) // MOE_TM + N_EXPERTS
    rows = n_blk * MOE_TM
    blk_start = jnp.arange(n_blk, dtype=jnp.int32) * MOE_TM
    blk_e = jnp.minimum(jnp.sum((pad_end[None, :] <= blk_start[:, None]).astype(jnp.int32), axis=1),
                        N_EXPERTS - 1)
    n_used = (pad_end[-1] // MOE_TM).astype(jnp.int32).reshape(1)
    order = jnp.argsort(flat_e)
    blk_shift = (grp_start - pad_start)[blk_e]
    sorted_pos = (jnp.arange(rows, dtype=jnp.int32).reshape(n_blk, MOE_TM) + blk_shift[:, None]).reshape(-1)
    row_tok = (order[jnp.clip(sorted_pos, 0, nk - 1)] // TOP_K).astype(jnp.int32)
    row_tok = jnp.where(sorted_pos < nk, row_tok, jnp.arange(rows, dtype=jnp.int32) % n)
    xs = h_rows[row_tok]
    y = _moe_experts(blk_e, n_used, xs, w_gu, b_gu, w_dn, b_dn, layer=layer)
    yk = y[dest.reshape(n, TOP_K).T.reshape(-1)]
    return _moe_combine(yk, gates, x, norm_g, mods, t_rows=t_rows, seq=seq)


def _rope_tables(rows, n_ctx, batch):
    row = jnp.repeat(jnp.arange(rows, dtype=jnp.float32), GRID_W)
    col = jnp.tile(jnp.arange(GRID_W, dtype=jnp.float32), rows)
    n_freq = HEAD_DIM // 4
    inv = ROPE_THETA ** (-jnp.arange(n_freq, dtype=jnp.float32) / n_freq)
    ang_r = row[:, None] * inv
    ang_c = col[:, None] * inv
    cos_h = jnp.concatenate([jnp.cos(ang_r), jnp.cos(ang_r), jnp.cos(ang_c), jnp.cos(ang_c)], axis=1)
    sin_h = jnp.concatenate([-jnp.sin(ang_r), jnp.sin(ang_r), -jnp.sin(ang_c), jnp.sin(ang_c)], axis=1)
    cos_t = jnp.concatenate([cos_h, jnp.ones((n_ctx, HEAD_DIM), jnp.float32)], axis=0)
    sin_t = jnp.concatenate([sin_h, jnp.zeros((n_ctx, HEAD_DIM), jnp.float32)], axis=0)
    reps = (batch, LANES // HEAD_DIM)
    return jnp.tile(cos_t, reps), jnp.tile(sin_t, reps)


def kernel(x, c, ctx, c_ctx, w_mod, b_mod, norm_g, w_in, attn_sink, diff_lambda, diff_norm_w,
           hgrn_lb_logits, hgrn_norm_w, w_branch, w_out, w_router, b_router, w_gate_up, b_gate_up,
           w_down, b_down):
    b, s, dm = x.shape
    l = ctx.shape[1]
    t = s + l
    f32 = jnp.float32
    bf16 = jnp.bfloat16
    cos_t, sin_t = _rope_tables(s // GRID_W, l, b)
    lb_cum = jnp.cumsum(jax.nn.softmax(hgrn_lb_logits.astype(f32), axis=0), axis=0)
    lower_bounds = lb_cum - lb_cum[0:1]
    cond_rows = 16
    cond = jnp.zeros((cond_rows, dm), f32).at[:b].set(jax.nn.silu(c)).at[b].set(jax.nn.silu(c_ctx))
    scale = HEAD_DIM ** -0.5
    xa = jnp.concatenate([x, ctx], axis=1).reshape(b * t, dm)
    h_moe = None
    for layer in range(DEPTH):
        lam_init = 0.8 - 0.6 * math.exp(-0.3 * layer)
        mod_all = _matmul(cond, w_mod[layer], tm=cond_rows, tn=3 * dm, out_dtype=f32,
                          precision=lax.Precision.HIGHEST, name="adaln") + b_mod[layer]
        mod = mod_all[:b].reshape(b, 1, 6, 1, dm)
        mod_c = jnp.broadcast_to(mod_all[b:b + 1].reshape(1, 1, 6, 1, dm), (b, 1, 6, 1, dm))
        mods = jnp.concatenate([mod, mod_c], axis=1).reshape(2 * b, 6, 1, dm)
        ng = norm_g[layer].reshape(4, 1, dm)
        g0 = ng[0:1]
        geo = dict(t_rows=t, seq=s)

        w = w_in[layer]
        w_aq = (w[:, OFF_A:OFF_A + A_Q_W] * scale).reshape(dm, A_HKV, A_GROUP, HEAD_DIM)
        pad = jnp.zeros((dm, A_GROUP, HEAD_DIM), f32)
        w_aq8 = jnp.concatenate([jnp.concatenate([w_aq[:, 0], pad], axis=-1),
                                 jnp.concatenate([pad, w_aq[:, 1]], axis=-1)], axis=1).reshape(dm, A_Q8_W)
        w_a = jnp.concatenate([w_aq8, w[:, OFF_A + A_Q_W:OFF_A + A_W]], axis=1).astype(bf16)
        w_b = w[:, OFF_B:OFF_B + B_W].at[:, :B_QK_W].multiply(scale * LOG2_E).astype(bf16)
        proj_a, bq, bk_hm, bv1_hm = _proj_ab(xa, g0, mods, w_a, w_b, cos_t, sin_t, **geo)
        proj_a = proj_a.reshape(b, t, A_W2)
        bq = bq.reshape(b, t, B_QK_W)
        lb = lower_bounds[layer]
        lb_par = jnp.stack([jnp.log(lb[0]), jnp.log1p(-lb[0]), jnp.log(lb[1]), jnp.log1p(-lb[1])])
        cq, ci, lf, cg = _proj_hgrn(xa, g0, mods, w[:, OFF_C:OFF_C + C_W].astype(bf16),
                                    lb_par.reshape(4, 1, C_K_W), **geo)
        gates = _proj_plain(xa, g0, mods, w[:, OFF_GATE:].astype(bf16), name="gate_proj", **geo)

        sink32 = attn_sink[layer].astype(f32)
        o_a = _window_attention(proj_a, sink32, seq=s)

        lp = diff_lambda[layer].astype(f32)
        lam = (jnp.exp(jnp.sum(lp[0] * lp[1])) - jnp.exp(jnp.sum(lp[2] * lp[3])) + lam_init).reshape(1).astype(f32)
        o_b = _diff_attention(lam, bq, bk_hm, bv1_hm, tq=FLASH_TQ, tk=FLASH_TK, q_block_offset=0,
                              n_q_blocks=s // FLASH_TQ, k_block_offset=0, nk=t // FLASH_TK)
        o_bx = _diff_attention(lam, bq, bk_hm, bv1_hm, tq=l, tk=l, q_block_offset=s // l, n_q_blocks=1,
                               k_block_offset=s // l, nk=1)
        o_b = jnp.concatenate([o_b, o_bx], axis=1)

        o_cf, o_cb = _hgrn_scan(cq.reshape(b, t, C_K_W), ci.reshape(b, t, C_V_W), lf.reshape(2, b, t, C_K_W),
                                n_ctx_blocks=l // HGRN_TB)

        col = jnp.arange(A_Q_W)
        head_of_col = ((col % LANES) // HEAD_DIM) * A_GROUP + col // LANES
        w_br = w_branch[layer].at[0].set(w_branch[layer][0][head_of_col * HEAD_DIM + col % HEAD_DIM])
        xa, h_moe, h_rows = _merge(o_a.reshape(b * t, A_Q_W), o_b.reshape(b * t, B_V_W), o_cf.reshape(b * t, C_V_W),
                           o_cb.reshape(b * t, C_V_W), cg,
                           gates, xa, ng, mods, diff_norm_w[layer].reshape(1, B_VD).astype(f32),
                           hgrn_norm_w[layer].reshape(1, C_DV).astype(f32),
                           w_br.astype(bf16), w_out[layer].astype(bf16),
                           diff_out_scale=1 - lam_init, **geo)

        xa = _moe_ffn(h_moe, h_rows.reshape(2 * b * t, dm), xa, ng, mods, w_router[layer], b_router[layer],
                      w_gate_up, b_gate_up, w_down, b_down, layer=layer, **geo)
    return xa.reshape(b, t, dm)[:, :s]
```

```python
import functools
import math

import jax
import jax.numpy as jnp
from jax import lax
from jax.experimental import pallas as pl
from jax.experimental.pallas import tpu as pltpu

D_MODEL = 1024
DEPTH = 2
GRID_W = 64
HEAD_DIM = 64
ROPE_THETA = 10000.0
RMS_EPS = 1e-6
NEG_INF = -1e30
BLOCK = 128
A_HQ = 8
A_HKV = 2
A_GROUP = A_HQ // A_HKV
WINDOW = 128
B_H = 4
B_HD = HEAD_DIM
B_VD = 2 * HEAD_DIM
C_H = 4
C_DK = 128
C_DV = 128
N_BRANCH = 3
BRANCH_W = 512
A_Q_W = A_HQ * HEAD_DIM
A_KV_W = A_HKV * HEAD_DIM
B_QK_W = B_H * 2 * B_HD
B_V_W = B_H * B_VD
C_K_W = C_H * C_DK
C_V_W = C_H * C_DV
GATE_W = N_BRANCH * D_MODEL
N_EXPERTS = 32
TOP_K = 4
EXPERT_FF = D_MODEL
SWIGLU_LIMIT = 7.0
SWIGLU_ALPHA = 1.702

A_W = A_Q_W + 2 * A_KV_W
A_ROPE_W = A_Q_W + A_KV_W
B_W = 2 * B_QK_W + B_V_W
B_ROPE_W = 2 * B_QK_W
C_W = 3 * C_K_W + 2 * C_V_W
OFF_A = 0
OFF_B = OFF_A + A_W
OFF_C = OFF_B + B_W
OFF_GATE = OFF_C + C_W

V7X_VMEM_LIMIT_BYTES = 56 * 1024 * 1024
LANES = 128
LOG2_E = math.log2(math.e)

MM_TM = 512
HALF_TM = MM_TM // 2
FLASH_TQ = 2048
FLASH_TK = 768
HGRN_TB = 256
HGRN_SUB = 16
HGRN_UNROLL = 4
MOE_TM = 512


def _cparams(sem):
    return pltpu.CompilerParams(dimension_semantics=sem, vmem_limit_bytes=V7X_VMEM_LIMIT_BYTES)


def _rms_rows(x, g):
    return (x * lax.rsqrt(jnp.mean(x * x, axis=-1, keepdims=True) + RMS_EPS)) * g


def _modulated_tile(x_ref, g, mod_a, mod_b, shift_slot, scale_slot):
    out = []
    for half, mod in enumerate((mod_a, mod_b)):
        x = x_ref[half * HALF_TM:(half + 1) * HALF_TM, :]
        out.append(_rms_rows(x, g) * (1.0 + mod[scale_slot]) + mod[shift_slot])
    return jnp.concatenate(out, axis=0)


def _residual_tile(x_ref, y, g, mod_a, mod_b, gate_slot):
    out = []
    for half, mod in enumerate((mod_a, mod_b)):
        rows = slice(half * HALF_TM, (half + 1) * HALF_TM)
        out.append(x_ref[rows, :] + mod[gate_slot] * _rms_rows(y[rows, :], g))
    return out


def _mod_specs(t_rows, seq):
    halves_per_batch = t_rows // HALF_TM
    lat_halves = seq // HALF_TM

    def row(u):
        return (u // halves_per_batch) * 2 + jnp.where(u % halves_per_batch >= lat_halves, 1, 0)

    blk = (1, 6, 1, D_MODEL)
    return [pl.BlockSpec(blk, lambda i: (row(2 * i), 0, 0, 0)),
            pl.BlockSpec(blk, lambda i: (row(2 * i + 1), 0, 0, 0))]


def _rope_lane_block(z_blk, cos, sin, first):
    quarter = HEAD_DIM // 4
    partner = jnp.where(first, pltpu.roll(z_blk, LANES - quarter, 1), pltpu.roll(z_blk, quarter, 1))
    return z_blk * cos + partner * sin


def _rope_first_mask(rows):
    lane = lax.broadcasted_iota(jnp.int32, (rows, LANES), 1)
    return (lane % (HEAD_DIM // 2)) < (HEAD_DIM // 4)


def _proj_ab_kernel(x_ref, g_ref, ma_ref, mb_ref, wa_ref, wb_ref, cos_ref, sin_ref,
                    oa_ref, q_ref, k_ref, v1_ref):
    h = _modulated_tile(x_ref, g_ref[0], ma_ref[0], mb_ref[0], 0, 1).astype(jnp.bfloat16)
    cos, sin, first = cos_ref[...], sin_ref[...], _rope_first_mask(MM_TM)
    za = jnp.dot(h, wa_ref[...], preferred_element_type=jnp.float32)
    for c in range(A_ROPE_W2 // LANES):
        cols = slice(c * LANES, (c + 1) * LANES)
        oa_ref[:, cols] = _rope_lane_block(za[:, cols], cos, sin, first).astype(oa_ref.dtype)
    oa_ref[:, A_ROPE_W2:] = za[:, A_ROPE_W2:].astype(oa_ref.dtype)
    z = jnp.dot(h, wb_ref[...], preferred_element_type=jnp.float32)
    for c in range(B_QK_W // LANES):
        cols = slice(c * LANES, (c + 1) * LANES)
        q_ref[:, cols] = _rope_lane_block(z[:, cols], cos, sin, first).astype(q_ref.dtype)
    for hh in range(B_H):
        kcols = slice(B_QK_W + hh * 2 * B_HD, B_QK_W + (hh + 1) * 2 * B_HD)
        k_ref[hh] = _rope_lane_block(z[:, kcols], cos, sin, first).astype(k_ref.dtype)
        v1_ref[hh, :, :B_VD] = z[:, B_ROPE_W + hh * B_VD:B_ROPE_W + (hh + 1) * B_VD].astype(v1_ref.dtype)
        v1_ref[hh, :, B_VD:] = jnp.ones((MM_TM, B_VD), v1_ref.dtype)


def _log_forget(z, log_lb, log_1m_lb):
    log_sig = jnp.minimum(z, 0.0) - jnp.log(1.0 + jnp.exp(-jnp.abs(z)))
    c = log_1m_lb + log_sig
    return jnp.maximum(log_lb, c) + jnp.log(1.0 + jnp.exp(-jnp.abs(log_lb - c)))


def _proj_hgrn_kernel(x_ref, g_ref, ma_ref, mb_ref, w_ref, wg_ref, lb_ref, q_ref, v_ref, lf_ref, cg_ref, gt_ref):
    h = _modulated_tile(x_ref, g_ref[0], ma_ref[0], mb_ref[0], 0, 1).astype(jnp.bfloat16)
    z = jnp.dot(h, w_ref[...], preferred_element_type=jnp.float32)
    zq = z[:, :C_K_W]
    q_ref[...] = zq * jax.nn.sigmoid(zq)
    v_ref[...] = z[:, C_K_W:C_K_W + C_V_W]
    off = C_K_W + C_V_W
    lf_ref[0] = _log_forget(z[:, off:off + C_K_W], lb_ref[0], lb_ref[1])
    lf_ref[1] = _log_forget(z[:, off + C_K_W:off + 2 * C_K_W], lb_ref[2], lb_ref[3])
    cg_ref[...] = z[:, off + 2 * C_K_W:].astype(cg_ref.dtype)
    gt_ref[...] = jnp.dot(h, wg_ref[...], preferred_element_type=jnp.float32).astype(gt_ref.dtype)


def _proj_common_specs(t_rows, seq):
    row = lambda i: (i, 0)
    return [pl.BlockSpec((MM_TM, D_MODEL), row),
            pl.BlockSpec((1, 1, D_MODEL), lambda i: (0, 0, 0))] + _mod_specs(t_rows, seq)


def _proj_ab(x, g, mods, w_a, w_b, cos, sin, *, t_rows, seq):
    m = x.shape[0]
    row = lambda i: (i, 0)
    whole = lambda i: (0, 0)
    heads = lambda i: (0, i, 0)
    bf16 = jnp.bfloat16
    return pl.pallas_call(
        _proj_ab_kernel,
        grid=(m // MM_TM,),
        in_specs=_proj_common_specs(t_rows, seq) + [
            pl.BlockSpec((D_MODEL, A_W2), whole), pl.BlockSpec((D_MODEL, B_W), whole),
            pl.BlockSpec((MM_TM, LANES), row), pl.BlockSpec((MM_TM, LANES), row)],
        out_specs=[pl.BlockSpec((MM_TM, A_W2), row), pl.BlockSpec((MM_TM, B_QK_W), row),
                   pl.BlockSpec((B_H, MM_TM, 2 * B_HD), heads), pl.BlockSpec((B_H, MM_TM, 2 * B_VD), heads)],
        out_shape=[jax.ShapeDtypeStruct((m, A_W2), bf16), jax.ShapeDtypeStruct((m, B_QK_W), bf16),
                   jax.ShapeDtypeStruct((B_H, m, 2 * B_HD), bf16), jax.ShapeDtypeStruct((B_H, m, 2 * B_VD), bf16)],
        compiler_params=_cparams(("arbitrary",)),
        name="proj_ab",
    )(x, g, mods, mods, w_a, w_b, cos, sin)


def _proj_hgrn(x, g, mods, w, w_gate, lb_par, *, t_rows, seq):
    m = x.shape[0]
    row = lambda i: (i, 0)
    f32 = jnp.float32
    resident = lambda shape: pl.BlockSpec(shape, lambda i: (0, 0), pipeline_mode=pl.Buffered(1))
    return pl.pallas_call(
        _proj_hgrn_kernel,
        grid=(m // MM_TM,),
        in_specs=_proj_common_specs(t_rows, seq) + [
            resident((D_MODEL, C_W)), resident((D_MODEL, GATE_W)),
            pl.BlockSpec((4, 1, C_K_W), lambda i: (0, 0, 0))],
        out_specs=[pl.BlockSpec((MM_TM, C_K_W), row), pl.BlockSpec((MM_TM, C_V_W), row),
                   pl.BlockSpec((2, MM_TM, C_K_W), lambda i: (0, i, 0)), pl.BlockSpec((MM_TM, C_V_W), row),
                   pl.BlockSpec((MM_TM, GATE_W), row)],
        out_shape=[jax.ShapeDtypeStruct((m, C_K_W), f32), jax.ShapeDtypeStruct((m, C_V_W), f32),
                   jax.ShapeDtypeStruct((2, m, C_K_W), f32), jax.ShapeDtypeStruct((m, C_V_W), jnp.bfloat16),
                   jax.ShapeDtypeStruct((m, GATE_W), jnp.bfloat16)],
        compiler_params=_cparams(("arbitrary",)),
        name="hgrn_gate_proj",
    )(x, g, mods, mods, w, w_gate, lb_par)


def _mm_kernel(x_ref, w_ref, o_ref, *, precision):
    o_ref[...] = jnp.dot(x_ref[...], w_ref[...], preferred_element_type=jnp.float32,
                         precision=precision).astype(o_ref.dtype)


def _matmul(x, w, *, tm, tn, out_dtype, precision=None, name):
    m, k = x.shape
    n = w.shape[1]
    assert m % tm == 0 and n % tn == 0, (m, tm, n, tn)
    return pl.pallas_call(
        functools.partial(_mm_kernel, precision=precision),
        grid=(n // tn, m // tm),
        in_specs=[pl.BlockSpec((tm, k), lambda j, i: (i, 0)),
                  pl.BlockSpec((k, tn), lambda j, i: (0, j))],
        out_specs=pl.BlockSpec((tm, tn), lambda j, i: (i, j)),
        out_shape=jax.ShapeDtypeStruct((m, n), out_dtype),
        compiler_params=_cparams(("arbitrary", "arbitrary")),
        name=name,
    )(x, w)


A_Q8_W = A_HQ * LANES
A_W2 = A_Q8_W + 2 * A_KV_W
A_ROPE_W2 = A_Q8_W + A_KV_W
A_K_BLK = A_Q8_W // A_KV_W
A_V_BLK = A_K_BLK + 1
A_STEP_BLOCKS = 2


def _attn_a_kernel(sink_ref, q_ref, kp_ref, kc_ref, kn_ref, vp_ref, vc_ref, vn_ref, kx_ref, vx_ref,
                   o_ref, *, seq):
    n = pl.program_id(1)
    k_loc = jnp.concatenate([kp_ref[0], kc_ref[0], kn_ref[0]], axis=0)
    v_loc = jnp.concatenate([vp_ref[0], vc_ref[0], vn_ref[0]], axis=0)
    n_keys = 3 * BLOCK + kx_ref.shape[1]
    qi = lax.broadcasted_iota(jnp.int32, (BLOCK, n_keys), 0)
    kj = lax.broadcasted_iota(jnp.int32, (BLOCK, n_keys), 1)
    lane = lax.broadcasted_iota(jnp.int32, (BLOCK, LANES), 1)
    for j in range(A_STEP_BLOCKS):
        blk = n * A_STEP_BLOCKS + j
        rows = slice(j * BLOCK, (j + 1) * BLOCK)
        k_all = jnp.concatenate([k_loc[j * BLOCK:(j + 3) * BLOCK, :], kx_ref[0]], axis=0)
        v_all = jnp.concatenate([v_loc[j * BLOCK:(j + 3) * BLOCK, :], vx_ref[0]], axis=0)
        kpos = blk * BLOCK + kj - BLOCK
        local_ok = (jnp.abs(kj - BLOCK - qi) <= WINDOW) & (kpos >= 0) & (kpos < seq) & (blk * BLOCK < seq)
        mask = local_ok | (kj >= 3 * BLOCK)
        q8 = jnp.concatenate([q_ref[0, rows, hq * LANES:(hq + 1) * LANES] for hq in range(A_HQ)], axis=0)
        s = lax.dot_general(q8, k_all, (((1,), (1,)), ((), ())), preferred_element_type=jnp.float32)
        p_blocks, inv_l = [], []
        for hq in range(A_HQ):
            sh = jnp.where(mask, s[hq * BLOCK:(hq + 1) * BLOCK, :], NEG_INF)
            sk = sink_ref[hq]
            m = jnp.maximum(jnp.max(sh, axis=-1, keepdims=True), sk)
            p = jnp.exp(sh - m)
            inv_l.append(1.0 / (jnp.sum(p, axis=-1, keepdims=True) + jnp.exp(sk - m)))
            p_blocks.append(p.astype(v_all.dtype))
        o = jnp.dot(jnp.concatenate(p_blocks, axis=0), v_all, preferred_element_type=jnp.float32)
        for g in range(A_GROUP):
            lo = o[g * BLOCK:(g + 1) * BLOCK, :] * inv_l[g]
            hi = o[(A_GROUP + g) * BLOCK:(A_GROUP + g + 1) * BLOCK, :] * inv_l[A_GROUP + g]
            o_ref[0, rows, g * LANES:(g + 1) * LANES] = jnp.where(lane < HEAD_DIM, lo, hi).astype(o_ref.dtype)


def _window_attention(proj_a, sink, *, seq):
    b, t, _ = proj_a.shape
    l = t - seq
    step = A_STEP_BLOCKS * BLOCK
    nb = seq // BLOCK
    ns = seq // step
    ctx_blk = seq // l
    q_map = lambda bi, n, *_: (bi, n, 0)
    edge = lambda col, fn: pl.BlockSpec((1, BLOCK, A_KV_W), lambda bi, n, *_: (bi, fn(n), col))
    mid = lambda col: pl.BlockSpec((1, step, A_KV_W), lambda bi, n, *_: (bi, jnp.minimum(n, ns - 1), col))
    prev = lambda n: jnp.clip(n * A_STEP_BLOCKS - 1, 0, nb - 1)
    nxt = lambda n: jnp.minimum((n + 1) * A_STEP_BLOCKS, nb - 1)
    ctx = lambda col: pl.BlockSpec((1, l, A_KV_W), lambda bi, n, *_: (bi, ctx_blk, col))
    return pl.pallas_call(
        functools.partial(_attn_a_kernel, seq=seq),
        grid_spec=pltpu.PrefetchScalarGridSpec(
            num_scalar_prefetch=1, grid=(b, t // step),
            in_specs=[pl.BlockSpec((1, step, A_Q8_W), q_map),
                      edge(A_K_BLK, prev), mid(A_K_BLK), edge(A_K_BLK, nxt),
                      edge(A_V_BLK, prev), mid(A_V_BLK), edge(A_V_BLK, nxt),
                      ctx(A_K_BLK), ctx(A_V_BLK)],
            out_specs=pl.BlockSpec((1, step, A_Q_W), q_map)),
        out_shape=jax.ShapeDtypeStruct((b, t, A_Q_W), jnp.bfloat16),
        compiler_params=_cparams(("arbitrary", "arbitrary")),
        name="window_gqa",
    )(sink, *([proj_a] * 9))


def _diff_flash_kernel(lam_ref, q_ref, k_ref, v_ref, o_ref, q2_sc, sa_sc, sb_sc, m_sc, acc_sc, *, tq, tk, nk):
    u = pl.program_id(2)
    new_q = u % nk == 0

    @pl.when(u == 0)
    def _():
        m_sc[...] = jnp.full_like(m_sc, NEG_INF)
        acc_sc[...] = jnp.zeros_like(acc_sc)
        sb_sc[...] = jnp.full_like(sb_sc, NEG_INF)

    @pl.when(new_q)
    def _():
        q = q_ref[0]
        lane = lax.broadcasted_iota(jnp.int32, q.shape, 1)
        q2_sc[:tq, :] = jnp.where(lane < B_HD, q, jnp.zeros_like(q))
        q2_sc[tq:, :] = jnp.where(lane >= B_HD, q, jnp.zeros_like(q))

    def step(s_new_ref, s_old_ref):
        m_prev = m_sc[...]
        m_new = jnp.maximum(m_prev, jnp.max(s_old_ref[...], axis=-1, keepdims=True))
        alpha = jnp.exp2(m_prev - m_new)
        p = jnp.exp2(s_old_ref[...] - jnp.tile(m_new, (1, tk // LANES))).astype(jnp.bfloat16)
        acc_sc[...] = jnp.tile(alpha, (1, 2)) * acc_sc[...] + jnp.dot(
            p, v_ref[0], preferred_element_type=jnp.float32)
        m_sc[...] = m_new
        s_new_ref[...] = lax.dot_general(q2_sc[...], k_ref[0], (((1,), (1,)), ((), ())),
                                         preferred_element_type=jnp.float32)

    @pl.when(u % 2 == 0)
    def _():
        step(sa_sc, sb_sc)

    @pl.when(u % 2 == 1)
    def _():
        step(sb_sc, sa_sc)

    @pl.when(new_q & (u > 0))
    def _():
        acc = acc_sc[...]
        o = acc[:, :B_VD] / acc[:, B_VD:]
        o_ref[0] = (o[:tq, :] - lam_ref[0] * o[tq:, :]).astype(o_ref.dtype)
        m_sc[...] = jnp.full_like(m_sc, NEG_INF)
        acc_sc[...] = jnp.zeros_like(acc_sc)


def _diff_attention(lam, q, k_hm, v1_hm, *, tq, tk, q_block_offset, n_q_blocks, k_block_offset, nk):
    b, t, _ = q.shape
    kb = t // tk
    n_steps = n_q_blocks * nk + 1
    k_blk = lambda bi, u: bi * kb + k_block_offset + u % nk
    in_specs = [pl.BlockSpec((1, tq, 2 * B_HD),
                             lambda bi, h, u, *_: (bi, jnp.minimum(u // nk, n_q_blocks - 1) + q_block_offset, h)),
                pl.BlockSpec((1, tk, 2 * B_HD), lambda bi, h, u, *_: (h, k_blk(bi, u), 0)),
                pl.BlockSpec((1, tk, 2 * B_VD), lambda bi, h, u, *_: (h, k_blk(bi, jnp.maximum(u - 1, 0)), 0))]
    return pl.pallas_call(
        functools.partial(_diff_flash_kernel, tq=tq, tk=tk, nk=nk),
        grid_spec=pltpu.PrefetchScalarGridSpec(
            num_scalar_prefetch=1, grid=(b, B_H, n_steps),
            in_specs=in_specs,
            out_specs=pl.BlockSpec((1, tq, B_VD), lambda bi, h, u, *_: (bi, jnp.maximum(u - 1, 0) // nk, h)),
            scratch_shapes=[pltpu.VMEM((2 * tq, 2 * B_HD), jnp.bfloat16),
                            pltpu.VMEM((2 * tq, tk), jnp.float32),
                            pltpu.VMEM((2 * tq, tk), jnp.float32),
                            pltpu.VMEM((2 * tq, LANES), jnp.float32),
                            pltpu.VMEM((2 * tq, 2 * B_VD), jnp.float32)]),
        out_shape=jax.ShapeDtypeStruct((b, n_q_blocks * tq, B_V_W), jnp.float32),
        compiler_params=_cparams(("arbitrary", "arbitrary", "arbitrary")),
        name="diff_flash",
    )(lam, q, k_hm, v1_hm)


def _hgrn_sub_chunk(q_ref, v_ref, lf_ref, o_ref, st_ref, r0, tri, rev):
    half = HGRN_SUB // 2
    row8 = lax.broadcasted_iota(jnp.int32, (half, C_DK), 0)
    lf = lf_ref[pl.ds(r0, HGRN_SUB), :] * LOG2_E
    lf_a = lf.astype(jnp.bfloat16)
    rem = lf - lf_a.astype(jnp.float32)
    lf_b = rem.astype(jnp.bfloat16)
    lf_c = (rem - lf_b.astype(jnp.float32)).astype(jnp.bfloat16)
    bl = (jnp.dot(tri, lf_a, preferred_element_type=jnp.float32)
          + jnp.dot(tri, lf_b, preferred_element_type=jnp.float32)
          + jnp.dot(tri, lf_c, preferred_element_type=jnp.float32))
    q = q_ref[pl.ds(r0, HGRN_SUB), :]
    v = v_ref[pl.ds(r0, HGRN_SUB), :]
    last = 0 if rev else HGRN_SUB - 1
    for hh in range(C_H):
        sl = slice(hh * C_DK, (hh + 1) * C_DK)
        lfh, blh, qh, vh = lf[:, sl], bl[:, sl], q[:, sl], v[:, sl]
        tot = blh[last:last + 1, :]
        kh = 1.0 - jnp.exp2(lfh)
        qt = qh * jnp.exp2(blh)
        kt = kh * jnp.exp2(tot - blh)
        st = st_ref[hh]
        o_sub = lax.dot_general(qt.astype(jnp.bfloat16), st.astype(jnp.bfloat16),
                                (((1,), (1,)), ((), ())), preferred_element_type=jnp.float32)
        parts = [o_sub[:half, :], o_sub[half:, :]]
        for ss in range(HGRN_SUB):
            ks = kh[ss:ss + 1, :]
            bs = blh[ss:ss + 1, :]
            vs = vh[ss:ss + 1, :]
            for p in range(2):
                lo = p * half
                if rev:
                    reached = lo <= ss
                    full = lo + half - 1 <= ss
                    ok = row8 + lo <= ss
                else:
                    reached = lo + half - 1 >= ss
                    full = lo >= ss
                    ok = row8 + lo >= ss
                if not reached:
                    continue
                diff = blh[lo:lo + half, :] - bs
                if not full:
                    diff = jnp.where(ok, diff, NEG_INF)
                col = jnp.sum((qh[lo:lo + half, :] * ks) * jnp.exp2(diff), axis=1, keepdims=True)
                parts[p] = parts[p] + col * vs
        o_ref[pl.ds(r0, half), sl] = parts[0]
        o_ref[pl.ds(r0 + half, half), sl] = parts[1]
        upd = lax.dot_general(vh.astype(jnp.bfloat16), kt.astype(jnp.bfloat16),
                              (((0,), (0,)), ((), ())), preferred_element_type=jnp.float32)
        st_ref[hh] = jnp.exp2(tot) * st + upd


def _hgrn_kernel(qf_ref, vf_ref, lff_ref, qb_ref, vb_ref, lfb_ref, of_ref, ob_ref, st_ref, *, tb):
    t = pl.program_id(1)
    nsub = tb // HGRN_SUB

    @pl.when(t == 0)
    def _():
        st_ref[...] = jnp.zeros_like(st_ref)

    rr = lax.broadcasted_iota(jnp.int32, (HGRN_SUB, HGRN_SUB), 0)
    cc = lax.broadcasted_iota(jnp.int32, (HGRN_SUB, HGRN_SUB), 1)
    tri_f = (cc <= rr).astype(jnp.bfloat16)
    tri_b = (cc >= rr).astype(jnp.bfloat16)

    def body(i, carry):
        rf = pl.multiple_of(i * HGRN_SUB, HGRN_SUB)
        rb = pl.multiple_of((nsub - 1 - i) * HGRN_SUB, HGRN_SUB)
        _hgrn_sub_chunk(qf_ref.at[0], vf_ref.at[0], lff_ref.at[0, 0], of_ref.at[0], st_ref.at[0], rf, tri_f, False)
        _hgrn_sub_chunk(qb_ref.at[0], vb_ref.at[0], lfb_ref.at[0, 0], ob_ref.at[0], st_ref.at[1], rb, tri_b, True)
        return carry

    lax.fori_loop(0, nsub, body, 0, unroll=HGRN_UNROLL)


def _hgrn_scan(q, v, lf, *, n_ctx_blocks):
    b, t, w = q.shape
    tb = HGRN_TB
    nblk = t // tb
    nlat = nblk - n_ctx_blocks

    def fwd_block(ti):
        return jnp.where(ti < n_ctx_blocks, nlat + ti, ti - n_ctx_blocks)

    def bwd_block(ti):
        return nblk - 1 - ti

    rows = lambda blk: pl.BlockSpec((1, tb, w), lambda bi, ti: (bi, blk(ti), 0))
    gate = lambda di, blk: pl.BlockSpec((1, 1, tb, w), lambda bi, ti: (di, bi, blk(ti), 0))
    out = jax.ShapeDtypeStruct((b, t, w), jnp.float32)
    return pl.pallas_call(
        functools.partial(_hgrn_kernel, tb=tb),
        grid=(b, nblk),
        in_specs=[rows(fwd_block), rows(fwd_block), gate(0, fwd_block),
                  rows(bwd_block), rows(bwd_block), gate(1, bwd_block)],
        out_specs=[rows(fwd_block), rows(bwd_block)],
        out_shape=[out, out],
        scratch_shapes=[pltpu.VMEM((2, C_H, C_DV, C_DK), jnp.float32)],
        compiler_params=_cparams(("arbitrary", "arbitrary")),
        name="hgrn2_scan",
    )(q, v, lf, q, v, lf)


def _head_rms(x, w, n_heads, width):
    out = []
    for hh in range(n_heads):
        blk = x[:, hh * width:(hh + 1) * width]
        out.append(_rms_rows(blk, w))
    return jnp.concatenate(out, axis=1)


def _merge_kernel(oa_ref, ob_ref, ocf_ref, ocb_ref, cg_ref, gt_ref, x_ref, ng_ref, ma_ref, mb_ref, dw_ref, hw_ref,
                  wbr_ref, wo_ref, xo_ref, ho_ref, hb_ref, *, diff_out_scale):
    ob = (_head_rms(ob_ref[...], dw_ref[...], B_H, B_VD) * diff_out_scale).astype(jnp.bfloat16)
    cg = cg_ref[...].astype(jnp.float32)
    oc = (_head_rms(ocf_ref[...] + ocb_ref[...], hw_ref[...], C_H, C_DV) * (cg * jax.nn.sigmoid(cg)))
    acc = None
    for i, o in enumerate((oa_ref[...], ob, oc.astype(jnp.bfloat16))):
        z = jnp.dot(o, wbr_ref[i], preferred_element_type=jnp.float32)
        g = jax.nn.sigmoid(gt_ref[:, i * D_MODEL:(i + 1) * D_MODEL].astype(jnp.float32))
        acc = g * z if acc is None else acc + g * z
    y = jnp.dot(acc.astype(jnp.bfloat16), wo_ref[...], preferred_element_type=jnp.float32)
    ma, mb = ma_ref[0], mb_ref[0]
    for half, (xn, mod) in enumerate(zip(_residual_tile(x_ref, y, ng_ref[1], ma, mb, 2), (ma, mb))):
        rows = slice(half * HALF_TM, (half + 1) * HALF_TM)
        xo_ref[rows, :] = xn
        h_next = _rms_rows(xn, ng_ref[2]) * (1.0 + mod[4]) + mod[3]
        ho_ref[rows, :] = h_next
        hb_ref[0, rows, :] = h_next.astype(hb_ref.dtype)
    hb_ref[1] = jnp.zeros((MM_TM, D_MODEL), hb_ref.dtype)


def _merge(oa, ob, ocf, ocb, cg, gates, x, norm_g, mods, diff_w, hgrn_w, w_br, w_o, *, t_rows, seq,
           diff_out_scale):
    m = x.shape[0]
    tm = MM_TM
    row = lambda i: (i, 0)
    f32 = jnp.float32
    return pl.pallas_call(
        functools.partial(_merge_kernel, diff_out_scale=diff_out_scale),
        grid=(m // tm,),
        in_specs=[pl.BlockSpec((tm, BRANCH_W), row), pl.BlockSpec((tm, BRANCH_W), row),
                  pl.BlockSpec((tm, BRANCH_W), row), pl.BlockSpec((tm, BRANCH_W), row),
                  pl.BlockSpec((tm, BRANCH_W), row),
                  pl.BlockSpec((tm, GATE_W), row), pl.BlockSpec((tm, D_MODEL), row),
                  pl.BlockSpec((4, 1, D_MODEL), lambda i: (0, 0, 0))] + _mod_specs(t_rows, seq) + [
                  pl.BlockSpec((1, B_VD), lambda i: (0, 0)), pl.BlockSpec((1, C_DV), lambda i: (0, 0)),
                  pl.BlockSpec((N_BRANCH, BRANCH_W, D_MODEL), lambda i: (0, 0, 0)),
                  pl.BlockSpec((D_MODEL, D_MODEL), lambda i: (0, 0))],
        out_specs=[pl.BlockSpec((tm, D_MODEL), row), pl.BlockSpec((tm, D_MODEL), row),
                   pl.BlockSpec((2, tm, D_MODEL), lambda i: (0, i, 0))],
        out_shape=[jax.ShapeDtypeStruct((m, D_MODEL), f32), jax.ShapeDtypeStruct((m, D_MODEL), f32),
                   jax.ShapeDtypeStruct((2, m, D_MODEL), jnp.bfloat16)],
        compiler_params=_cparams(("arbitrary",)),
        name="branch_merge",
    )(oa, ob, ocf, ocb, cg, gates, x, norm_g, mods, mods, diff_w, hgrn_w, w_br, w_o)


def _moe_kernel(blk_e_ref, n_used_ref, x_ref, wgu_ref, bgu_ref, wdn_ref, bdn_ref, y_ref, wgu_sc, wdn_sc):
    i = pl.program_id(0)
    n_used = n_used_ref[0]
    e = blk_e_ref[i]
    e_prev = blk_e_ref[jnp.maximum(i - 1, 0)]

    @pl.when((i == 0) | (e != e_prev))
    def _():
        wgu_sc[...] = wgu_ref[0, 0].astype(jnp.bfloat16)
        wdn_sc[...] = wdn_ref[0, 0].astype(jnp.bfloat16)

    @pl.when(i < n_used)
    def _():
        gu = jnp.dot(x_ref[...], wgu_sc[...], preferred_element_type=jnp.float32) + bgu_ref[0, 0]
        g = jnp.minimum(gu[:, :EXPERT_FF], SWIGLU_LIMIT)
        u = jnp.clip(gu[:, EXPERT_FF:], -SWIGLU_LIMIT, SWIGLU_LIMIT)
        act = (u + 1.0) * (g * jax.nn.sigmoid(SWIGLU_ALPHA * g))
        y = jnp.dot(act.astype(jnp.bfloat16), wdn_sc[...], preferred_element_type=jnp.float32) + bdn_ref[0, 0]
        y_ref[...] = y.astype(y_ref.dtype)

    @pl.when(i >= n_used)
    def _():
        y_ref[...] = jnp.zeros_like(y_ref)


def _moe_experts(blk_e, n_used, xs, w_gu, b_gu, w_dn, b_dn, *, layer):
    r, dm = xs.shape
    tm = MOE_TM
    n_blk = blk_e.shape[0]
    return pl.pallas_call(
        _moe_kernel,
        grid_spec=pltpu.PrefetchScalarGridSpec(
            num_scalar_prefetch=2, grid=(n_blk,),
            in_specs=[pl.BlockSpec((tm, dm), lambda i, be, nu: (i, 0)),
                      pl.BlockSpec((1, 1, dm, 2 * EXPERT_FF), lambda i, be, nu: (layer, be[i], 0, 0)),
                      pl.BlockSpec((1, 1, 1, 2 * EXPERT_FF), lambda i, be, nu: (layer, be[i], 0, 0)),
                      pl.BlockSpec((1, 1, EXPERT_FF, dm), lambda i, be, nu: (layer, be[i], 0, 0)),
                      pl.BlockSpec((1, 1, 1, dm), lambda i, be, nu: (layer, be[i], 0, 0))],
            out_specs=pl.BlockSpec((tm, dm), lambda i, be, nu: (i, 0)),
            scratch_shapes=[pltpu.VMEM((dm, 2 * EXPERT_FF), jnp.bfloat16),
                            pltpu.VMEM((EXPERT_FF, dm), jnp.bfloat16)]),
        out_shape=jax.ShapeDtypeStruct((r, dm), jnp.bfloat16),
        compiler_params=_cparams(("arbitrary",)),
        name="moe_experts",
    )(blk_e, n_used, xs, w_gu, b_gu.reshape(DEPTH, N_EXPERTS, 1, -1), w_dn,
      b_dn.reshape(DEPTH, N_EXPERTS, 1, -1))


def _combine_kernel(*refs):
    yk_refs = refs[:TOP_K]
    gt_ref, x_ref, ng_ref, ma_ref, mb_ref, xo_ref = refs[TOP_K:]
    gt = gt_ref[...]
    f = None
    for k in range(TOP_K):
        term = gt[:, k:k + 1] * yk_refs[k][...].astype(jnp.float32)
        f = term if f is None else f + term
    for half, xn in enumerate(_residual_tile(x_ref, f, ng_ref[3], ma_ref[0], mb_ref[0], 5)):
        xo_ref[half * HALF_TM:(half + 1) * HALF_TM, :] = xn


def _moe_combine(yk, gates, x, norm_g, mods, *, t_rows, seq):
    m = x.shape[0]
    tm = MM_TM
    n_tiles = m // tm
    row = lambda i: (i, 0)
    choice = lambda k: pl.BlockSpec((tm, D_MODEL), lambda i: (k * n_tiles + i, 0))
    return pl.pallas_call(
        _combine_kernel,
        grid=(n_tiles,),
        in_specs=[choice(k) for k in range(TOP_K)] + [
            pl.BlockSpec((tm, TOP_K), row), pl.BlockSpec((tm, D_MODEL), row),
            pl.BlockSpec((4, 1, D_MODEL), lambda i: (0, 0, 0))] + _mod_specs(t_rows, seq),
        out_specs=pl.BlockSpec((tm, D_MODEL), row),
        out_shape=jax.ShapeDtypeStruct((m, D_MODEL), jnp.float32),
        compiler_params=_cparams(("arbitrary",)),
        name="moe_combine",
    )(*([yk] * TOP_K), gates, x, norm_g, mods, mods)


def _route_kernel(h_ref, wr_ref, br_ref, e_ref, g_ref, rk_ref, cnt_ref, run_sc):
    i = pl.program_id(0)
    tm = h_ref.shape[0]

    @pl.when(i == 0)
    def _():
        run_sc[...] = jnp.zeros_like(run_sc)

    vals = jnp.dot(h_ref[...], wr_ref[...], preferred_element_type=jnp.float32,
                   precision=lax.Precision.HIGHEST) + br_ref[...]
    lane = lax.broadcasted_iota(jnp.int32, (tm, LANES), 1)
    top_v, top_e, picked = [], [], []
    for _ in range(TOP_K):
        m = jnp.max(vals, axis=-1, keepdims=True)
        idx = jnp.min(jnp.where(vals == m, lane, LANES), axis=-1, keepdims=True)
        sel = lane == idx
        top_v.append(m)
        top_e.append(idx)
        picked.append(sel)
        vals = jnp.where(sel, -jnp.inf, vals)
    ex = [jnp.exp(v - top_v[0]) for v in top_v]
    inv_den = 1.0 / (ex[0] + ex[1] + ex[2] + ex[3])
    any_pick = picked[0] | picked[1] | picked[2] | picked[3]
    rr = lax.broadcasted_iota(jnp.int32, (tm, tm), 0)
    cc = lax.broadcasted_iota(jnp.int32, (tm, tm), 1)
    earlier = (cc < rr).astype(jnp.bfloat16)
    before = jnp.dot(earlier, any_pick.astype(jnp.bfloat16), preferred_element_type=jnp.float32) + run_sc[...]
    e_out = jnp.zeros((tm, LANES), jnp.int32)
    g_out = jnp.zeros((tm, LANES), jnp.float32)
    r_out = jnp.zeros((tm, LANES), jnp.int32)
    for k in range(TOP_K):
        rank = jnp.sum(jnp.where(picked[k], before, 0.0), axis=-1, keepdims=True).astype(jnp.int32)
        e_out = jnp.where(lane == k, top_e[k], e_out)
        g_out = jnp.where(lane == k, ex[k] * inv_den, g_out)
        r_out = jnp.where(lane == k, rank, r_out)
    e_ref[...] = e_out
    g_ref[...] = g_out
    rk_ref[...] = r_out
    run_sc[...] += jnp.sum(any_pick.astype(jnp.float32), axis=0, keepdims=True)
    cnt_ref[...] = run_sc[...]


def _route(h, w_router, b_router):
    n, dm = h.shape
    tm = MM_TM
    w_r = jnp.zeros((dm, LANES), jnp.float32).at[:, :N_EXPERTS].set(w_router)
    b_r = jnp.full((1, LANES), NEG_INF, jnp.float32).at[0, :N_EXPERTS].set(b_router)
    row = lambda i: (i, 0)
    whole = lambda i: (0, 0)
    e, g, rk, cnt = pl.pallas_call(
        _route_kernel,
        grid=(n // tm,),
        in_specs=[pl.BlockSpec((tm, dm), row), pl.BlockSpec((dm, LANES), whole), pl.BlockSpec((1, LANES), whole)],
        out_specs=[pl.BlockSpec((tm, LANES), row), pl.BlockSpec((tm, LANES), row), pl.BlockSpec((tm, LANES), row),
                   pl.BlockSpec((1, LANES), whole)],
        out_shape=[jax.ShapeDtypeStruct((n, LANES), jnp.int32), jax.ShapeDtypeStruct((n, LANES), jnp.float32),
                   jax.ShapeDtypeStruct((n, LANES), jnp.int32), jax.ShapeDtypeStruct((1, LANES), jnp.float32)],
        scratch_shapes=[pltpu.VMEM((1, LANES), jnp.float32)],
        compiler_params=_cparams(("arbitrary",)),
        name="router",
    )(h, w_r, b_r)
    return e[:, :TOP_K], g[:, :TOP_K], rk[:, :TOP_K], cnt[0, :N_EXPERTS].astype(jnp.int32)


def _moe_ffn(h, h_rows, x, norm_g, mods, w_router, b_router, w_gu, b_gu, w_dn, b_dn, *, layer, t_rows, seq):
    n, dm = h.shape
    nk = n * TOP_K
    top_e, gates, rank, counts = _route(h, w_router, b_router)
    flat_e = top_e.reshape(-1)
    padded = (counts + MOE_TM - 1) // MOE_TM * MOE_TM
    pad_end = jnp.cumsum(padded)
    pad_start = pad_end - padded
    grp_start = jnp.cumsum(counts) - counts
    dest = (pad_start[top_e] + rank).reshape(-1)
    n_blk = (nk + MOE_TM - 1) // MOE_TM + N_EXPERTS
    rows = n_blk * MOE_TM
    blk_start = jnp.arange(n_blk, dtype=jnp.int32) * MOE_TM
    blk_e = jnp.minimum(jnp.sum((pad_end[None, :] <= blk_start[:, None]).astype(jnp.int32), axis=1),
                        N_EXPERTS - 1)
    n_used = (pad_end[-1] // MOE_TM).astype(jnp.int32).reshape(1)
    order = jnp.argsort(flat_e)
    blk_shift = (grp_start - pad_start)[blk_e]
    sorted_pos = (jnp.arange(rows, dtype=jnp.int32).reshape(n_blk, MOE_TM) + blk_shift[:, None]).reshape(-1)
    row_tok = (order[jnp.clip(sorted_pos, 0, nk - 1)] // TOP_K).astype(jnp.int32)
    row_tok = jnp.where(sorted_pos < nk, row_tok, jnp.arange(rows, dtype=jnp.int32) % n)
    xs = h_rows[row_tok]
    y = _moe_experts(blk_e, n_used, xs, w_gu, b_gu, w_dn, b_dn, layer=layer)
    yk = y[dest.reshape(n, TOP_K).T.reshape(-1)]
    return _moe_combine(yk, gates, x, norm_g, mods, t_rows=t_rows, seq=seq)


def _rope_tables(rows, n_ctx, batch):
    row = jnp.repeat(jnp.arange(rows, dtype=jnp.float32), GRID_W)
    col = jnp.tile(jnp.arange(GRID_W, dtype=jnp.float32), rows)
    n_freq = HEAD_DIM // 4
    inv = ROPE_THETA ** (-jnp.arange(n_freq, dtype=jnp.float32) / n_freq)
    ang_r = row[:, None] * inv
    ang_c = col[:, None] * inv
    cos_h = jnp.concatenate([jnp.cos(ang_r), jnp.cos(ang_r), jnp.cos(ang_c), jnp.cos(ang_c)], axis=1)
    sin_h = jnp.concatenate([-jnp.sin(ang_r), jnp.sin(ang_r), -jnp.sin(ang_c), jnp.sin(ang_c)], axis=1)
    cos_t = jnp.concatenate([cos_h, jnp.ones((n_ctx, HEAD_DIM), jnp.float32)], axis=0)
    sin_t = jnp.concatenate([sin_h, jnp.zeros((n_ctx, HEAD_DIM), jnp.float32)], axis=0)
    reps = (batch, LANES // HEAD_DIM)
    return jnp.tile(cos_t, reps), jnp.tile(sin_t, reps)


def kernel(x, c, ctx, c_ctx, w_mod, b_mod, norm_g, w_in, attn_sink, diff_lambda, diff_norm_w,
           hgrn_lb_logits, hgrn_norm_w, w_branch, w_out, w_router, b_router, w_gate_up, b_gate_up,
           w_down, b_down):
    b, s, dm = x.shape
    l = ctx.shape[1]
    t = s + l
    f32 = jnp.float32
    bf16 = jnp.bfloat16
    cos_t, sin_t = _rope_tables(s // GRID_W, l, b)
    lb_cum = jnp.cumsum(jax.nn.softmax(hgrn_lb_logits.astype(f32), axis=0), axis=0)
    lower_bounds = lb_cum - lb_cum[0:1]
    cond_rows = 16
    cond = jnp.zeros((cond_rows, dm), f32).at[:b].set(jax.nn.silu(c)).at[b].set(jax.nn.silu(c_ctx))
    scale = HEAD_DIM ** -0.5
    xa = jnp.concatenate([x, ctx], axis=1).reshape(b * t, dm)
    h_moe = None
    for layer in range(DEPTH):
        lam_init = 0.8 - 0.6 * math.exp(-0.3 * layer)
        mod_all = _matmul(cond, w_mod[layer], tm=cond_rows, tn=3 * dm, out_dtype=f32,
                          precision=lax.Precision.HIGHEST, name="adaln") + b_mod[layer]
        mod = mod_all[:b].reshape(b, 1, 6, 1, dm)
        mod_c = jnp.broadcast_to(mod_all[b:b + 1].reshape(1, 1, 6, 1, dm), (b, 1, 6, 1, dm))
        mods = jnp.concatenate([mod, mod_c], axis=1).reshape(2 * b, 6, 1, dm)
        ng = norm_g[layer].reshape(4, 1, dm)
        g0 = ng[0:1]
        geo = dict(t_rows=t, seq=s)

        w = w_in[layer]
        w_aq = (w[:, OFF_A:OFF_A + A_Q_W] * scale).reshape(dm, A_HKV, A_GROUP, HEAD_DIM)
        pad = jnp.zeros((dm, A_GROUP, HEAD_DIM), f32)
        w_aq8 = jnp.concatenate([jnp.concatenate([w_aq[:, 0], pad], axis=-1),
                                 jnp.concatenate([pad, w_aq[:, 1]], axis=-1)], axis=1).reshape(dm, A_Q8_W)
        w_a = jnp.concatenate([w_aq8, w[:, OFF_A + A_Q_W:OFF_A + A_W]], axis=1).astype(bf16)
        w_b = w[:, OFF_B:OFF_B + B_W].at[:, :B_QK_W].multiply(scale * LOG2_E).astype(bf16)
        proj_a, bq, bk_hm, bv1_hm = _proj_ab(xa, g0, mods, w_a, w_b, cos_t, sin_t, **geo)
        proj_a = proj_a.reshape(b, t, A_W2)
        bq = bq.reshape(b, t, B_QK_W)
        lb = lower_bounds[layer]
        lb_par = jnp.stack([jnp.log(lb[0]), jnp.log1p(-lb[0]), jnp.log(lb[1]), jnp.log1p(-lb[1])])
        cq, ci, lf, cg, gates = _proj_hgrn(xa, g0, mods, w[:, OFF_C:OFF_C + C_W].astype(bf16),
                                           w[:, OFF_GATE:].astype(bf16), lb_par.reshape(4, 1, C_K_W), **geo)

        sink32 = attn_sink[layer].astype(f32)
        o_a = _window_attention(proj_a, sink32, seq=s)

        lp = diff_lambda[layer].astype(f32)
        lam = (jnp.exp(jnp.sum(lp[0] * lp[1])) - jnp.exp(jnp.sum(lp[2] * lp[3])) + lam_init).reshape(1).astype(f32)
        o_b = _diff_attention(lam, bq, bk_hm, bv1_hm, tq=FLASH_TQ, tk=FLASH_TK, q_block_offset=0,
                              n_q_blocks=s // FLASH_TQ, k_block_offset=0, nk=t // FLASH_TK)
        o_bx = _diff_attention(lam, bq, bk_hm, bv1_hm, tq=l, tk=l, q_block_offset=s // l, n_q_blocks=1,
                               k_block_offset=s // l, nk=1)
        o_b = jnp.concatenate([o_b, o_bx], axis=1)

        o_cf, o_cb = _hgrn_scan(cq.reshape(b, t, C_K_W), ci.reshape(b, t, C_V_W), lf.reshape(2, b, t, C_K_W),
                                n_ctx_blocks=l // HGRN_TB)

        col = jnp.arange(A_Q_W)
        head_of_col = ((col % LANES) // HEAD_DIM) * A_GROUP + col // LANES
        w_br = w_branch[layer].at[0].set(w_branch[layer][0][head_of_col * HEAD_DIM + col % HEAD_DIM])
        xa, h_moe, h_rows = _merge(o_a.reshape(b * t, A_Q_W), o_b.reshape(b * t, B_V_W), o_cf.reshape(b * t, C_V_W),
                           o_cb.reshape(b * t, C_V_W), cg,
                           gates, xa, ng, mods, diff_norm_w[layer].reshape(1, B_VD).astype(f32),
                           hgrn_norm_w[layer].reshape(1, C_DV).astype(f32),
                           w_br.astype(bf16), w_out[layer].astype(bf16),
                           diff_out_scale=1 - lam_init, **geo)

        xa = _moe_ffn(h_moe, h_rows.reshape(2 * b * t, dm), xa, ng, mods, w_router[layer], b_router[layer],
                      w_gate_up, b_gate_up, w_down, b_down, layer=layer, **geo)
    return xa.reshape(b, t, dm)[:, :s]
```
